```python
import math
import jax
import jax.numpy as jnp
from jax import lax
import numpy as np

D_MODEL = 1024
BATCH = 2
SEQ = 8192
DEPTH = 4
DEC_BATCH = 32
DEC_SEQ = 1
PAST_LEN = 8192
PAGE_SIZE = 128

N_HEADS = 16
HEAD_DIM = 64
N_KV_HEADS = 4
Q_PER_KV = N_HEADS // N_KV_HEADS
IDX_HEADS = 8
IDX_DIM = 64
IDX_SCALE = (IDX_HEADS * IDX_DIM) ** -0.5
TOPK_MAX = 256
ROPE_THETA = 500000.0
ROPE_FRACTION = 4
Q_BLOCK = 128
ATTN_SIZES = (N_HEADS * HEAD_DIM, N_KV_HEADS * HEAD_DIM, N_KV_HEADS * HEAD_DIM,
              IDX_HEADS * IDX_DIM, IDX_DIM, IDX_HEADS)
ATTN_PROJ = sum(ATTN_SIZES)
ATTN_SPLITS = tuple(int(s) for s in np.cumsum(ATTN_SIZES)[:-1])
D_RNN = D_MODEL
RG_BLOCKS = 8
RG_BLOCK_W = D_RNN // RG_BLOCKS
CONV_W = 4
RG_C = 8.0
N_GROUPS = 4
EXPERTS_PER_GROUP = 8
N_EXPERTS = N_GROUPS * EXPERTS_PER_GROUP
TOPK_IN_GROUP = 2
D_EXPERT = 256
MOE_BLOCK = 128
N_ATTN = (DEPTH + 1) // 2
N_REC = DEPTH // 2
EPS = 1e-6

kernel_name = 'dsa_rglru_hier_moe_adaln_step'


def rmsnorm(x, g):
    xf = x.astype(jnp.float32)
    xf = xf * lax.rsqrt(jnp.mean(xf * xf, axis=-1, keepdims=True) + EPS)
    return (xf * g.astype(jnp.float32)).astype(x.dtype)


def modulate(x, g, shift, scale):
    return rmsnorm(x, g) * (1 + scale[:, None, :]) + shift[:, None, :]


def adaln(c, w, b):
    return jnp.split(jax.nn.silu(c) @ w + b, 6, axis=-1)


def rope_partial(x, pos):
    d = x.shape[-1]
    rot = d // ROPE_FRACTION
    half = rot // 2
    inv = jnp.exp(-math.log(ROPE_THETA) * jnp.arange(half, dtype=jnp.float32) * (2.0 / rot))
    ang = pos.astype(jnp.float32)[:, None] * inv[None, :]
    cos = jnp.cos(ang)[None, :, None, :].astype(x.dtype)
    sin = jnp.sin(ang)[None, :, None, :].astype(x.dtype)
    x1, x2, rest = x[..., :half], x[..., half:rot], x[..., rot:]
    return jnp.concatenate([x1 * cos - x2 * sin, x1 * sin + x2 * cos, rest], axis=-1)


def gather_rows(arr, idx):
    return jax.vmap(lambda a, i: a[i])(arr, idx)


def attn_project(h, w_in, pos):
    B, T, _ = h.shape
    q, k, v, qi, ki, wi = jnp.split(h @ w_in, ATTN_SPLITS, axis=-1)
    q = rope_partial(q.reshape(B, T, N_HEADS, HEAD_DIM), pos)
    k = rope_partial(k.reshape(B, T, N_KV_HEADS, HEAD_DIM), pos)
    v = v.reshape(B, T, N_KV_HEADS, HEAD_DIM)
    qi = rope_partial(qi.reshape(B, T, IDX_HEADS, IDX_DIM), pos)
    ki = rope_partial(ki[:, :, None, :], pos)[:, :, 0]
    return q, k, v, qi, ki, wi


def indexer_scores(qi, wi, ki, q_pos, k_pos):
    dots = jnp.einsum('bqhd,bsd->bqhs', qi.astype(jnp.float32), ki.astype(jnp.float32))
    s = jnp.einsum('bqhs,bqh->bqs', jax.nn.relu(dots), wi.astype(jnp.float32) * IDX_SCALE)
    return jnp.where(k_pos[None, None, :] <= q_pos[None, :, None], s, -jnp.inf)


def sparse_attend(q, k_sel, v_sel, valid):
    B, Q = q.shape[:2]
    qg = q.reshape(B, Q, N_KV_HEADS, Q_PER_KV, HEAD_DIM)
    s = jnp.einsum('bqngd,bqknd->bqngk', qg, k_sel).astype(jnp.float32) * (HEAD_DIM ** -0.5)
    s = jnp.where(valid[:, :, None, None, :], s, -jnp.inf)
    p = jax.nn.softmax(s, axis=-1).astype(v_sel.dtype)
    o = jnp.einsum('bqngk,bqknd->bqngd', p, v_sel)
    return o.reshape(B, Q, N_HEADS * HEAD_DIM)


def dsa_prompt(q, k, v, qi, ki, wi):
    B, T = q.shape[:2]
    topk = min(TOPK_MAX, T // 4)
    nblk = T // Q_BLOCK
    k_pos = jnp.arange(T)

    def block(args):
        qb, qib, wib, start = args
        q_pos = start + jnp.arange(Q_BLOCK)
        sc = indexer_scores(qib, wib, ki, q_pos, k_pos)
        _, idx = lax.top_k(sc, topk)
        valid = idx <= q_pos[None, :, None]
        return sparse_attend(qb, gather_rows(k, idx), gather_rows(v, idx), valid)

    def to_blocks(a):
        return a.reshape(B, nblk, Q_BLOCK, *a.shape[2:]).swapaxes(0, 1)

    out = lax.map(block, (to_blocks(q), to_blocks(qi), to_blocks(wi),
                          jnp.arange(nblk) * Q_BLOCK))
    return out.swapaxes(0, 1).reshape(B, T, N_HEADS * HEAD_DIM)


def dsa_sample(q, k_new, v_new, qi, ki_new, wi, cache_k_l, cache_v_l, cache_ki_l, page_table):
    DB, DQ = q.shape[:2]
    past = page_table.shape[1] * PAGE_SIZE
    L = past + DQ
    topk = min(TOPK_MAX, L // 4)
    ki_past = cache_ki_l[page_table].reshape(DB, past, IDX_DIM)
    ki_all = jnp.concatenate([ki_past, ki_new.astype(ki_past.dtype)], axis=1)
    q_pos = past + jnp.arange(DQ)
    sc = indexer_scores(qi, wi, ki_all, q_pos, jnp.arange(L))
    _, idx = lax.top_k(sc, topk)
    valid = idx <= q_pos[None, :, None]
    is_new = (idx >= past)[..., None, None]
    pidx = jnp.minimum(idx, past - 1)
    phys = gather_rows(page_table, pidx // PAGE_SIZE)
    off = pidx % PAGE_SIZE
    nidx = jnp.clip(idx - past, 0, DQ - 1)
    k_sel = jnp.where(is_new, gather_rows(k_new, nidx), cache_k_l[phys, off])
    v_sel = jnp.where(is_new, gather_rows(v_new, nidx), cache_v_l[phys, off])
    return sparse_attend(q, k_sel, v_sel, valid)


def rglru_block(h, conv_buf, h0, w_in, conv_w, conv_b, w_a, b_a, w_x, b_x, lam, w_out):
    B, T, _ = h.shape
    gate_in, xb = jnp.split(h @ w_in, 2, axis=-1)
    gate = jax.nn.gelu(gate_in)
    xcat = jnp.concatenate([conv_buf.astype(xb.dtype), xb], axis=1)
    xc = conv_b + sum(xcat[:, j:j + T] * conv_w[j] for j in range(CONV_W))
    new_buf = xcat[:, T:]
    xcb = xc.reshape(B, T, RG_BLOCKS, RG_BLOCK_W)
    r = jax.nn.sigmoid(jnp.einsum('btnc,ncd->btnd', xcb, w_a).reshape(B, T, D_RNN) + b_a)
    i = jax.nn.sigmoid(jnp.einsum('btnc,ncd->btnd', xcb, w_x).reshape(B, T, D_RNN) + b_x)
    log_a = -RG_C * r.astype(jnp.float32) * jax.nn.softplus(-lam.astype(jnp.float32))
    a = jnp.exp(log_a)
    u = jnp.sqrt(-jnp.expm1(2.0 * log_a)) * (i * xc).astype(jnp.float32)

    def step(hc, au):
        hc = au[0] * hc + au[1]
        return hc, hc

    hT, hs = lax.scan(step, h0.astype(jnp.float32), (a.swapaxes(0, 1), u.swapaxes(0, 1)))
    y = hs.swapaxes(0, 1).astype(h.dtype) * gate
    return y @ w_out, hT.astype(h.dtype), new_buf


def dispatch_experts(x, eidx, ew, w1, w2):
    N, K = eidx.shape
    D = x.shape[-1]
    A = N * K
    blk = max(8, min(MOE_BLOCK, A // N_EXPERTS))
    nb = -(-A // blk) + N_EXPERTS
    flat_e = eidx.reshape(A)
    flat_t = jnp.repeat(jnp.arange(N), K)
    flat_w = ew.reshape(A)
    order = jnp.argsort(flat_e)
    se, st, sw = flat_e[order], flat_t[order], flat_w[order]
    counts = jnp.bincount(flat_e, length=N_EXPERTS)
    start = jnp.cumsum(counts) - counts
    padded = (counts + blk - 1) // blk * blk
    pend = jnp.cumsum(padded)
    pstart = pend - padded
    dest = pstart[se] + (jnp.arange(A) - start[se])
    row_tok = jnp.full((nb * blk,), N, jnp.int32).at[dest].set(st)
    blk_expert = jnp.minimum(jnp.searchsorted(pend, jnp.arange(nb) * blk, side='right'), N_EXPERTS - 1)
    x_pad = jnp.concatenate([x, jnp.zeros((1, D), x.dtype)], axis=0)
    xb = x_pad[row_tok].reshape(nb, blk, D)

    def run(args):
        xi, e = args
        g, u = jnp.split(xi @ w1[e], 2, axis=-1)
        return (jax.nn.silu(g) * u) @ w2[e]

    yb = lax.map(run, (xb, blk_expert)).reshape(nb * blk, D)
    return jnp.zeros((N, D), x.dtype).at[st].add(yb[dest] * sw[:, None])


def hier_moe(h, wg, bg, we, be, w1, w2):
    B, T, D = h.shape
    x = h.reshape(B * T, D)
    g_logits = (x @ wg + bg).astype(jnp.float32)
    grp = jnp.argmax(g_logits, axis=-1)
    g_w = jnp.take_along_axis(jax.nn.softmax(g_logits, axis=-1), grp[:, None], axis=-1)
    e_logits = (x @ we + be).astype(jnp.float32).reshape(B * T, N_GROUPS, EXPERTS_PER_GROUP)
    e_in = jnp.take_along_axis(e_logits, grp[:, None, None], axis=1)[:, 0]
    top_w, top_i = lax.top_k(jax.nn.softmax(e_in, axis=-1), TOPK_IN_GROUP)
    top_w = top_w / jnp.sum(top_w, axis=-1, keepdims=True) * g_w
    eidx = grp[:, None] * EXPERTS_PER_GROUP + top_i
    y = dispatch_experts(x, eidx, top_w.astype(x.dtype), w1, w2)
    return y.reshape(B, T, D)


def setup_inputs(seed: int = 0) -> dict:
    key = jax.random.key(seed)
    ks = iter(jax.random.split(key, 40))
    f32 = jnp.float32

    def nrm(shape, scale):
        return jax.random.normal(next(ks), shape, f32) * scale

    n_pages = PAST_LEN // PAGE_SIZE
    n_pool = (DEC_BATCH * n_pages * 5 + 3) // 4
    page_table = jax.random.permutation(next(ks), n_pool)[:DEC_BATCH * n_pages]
    page_table = page_table.reshape(DEC_BATCH, n_pages).astype(jnp.int32)
    a0 = jax.random.uniform(next(ks), (N_REC, D_RNN), f32, 0.9, 0.999)
    s0 = a0 ** (1.0 / RG_C)
    lam = jnp.log(s0) - jnp.log1p(-s0)
    return {
        'x_prompt': nrm((BATCH, SEQ, D_MODEL), 1.0),
        'x_sample': nrm((DEC_BATCH, DEC_SEQ, D_MODEL), 1.0),
        'cache_k': nrm((N_ATTN, n_pool, PAGE_SIZE, N_KV_HEADS, HEAD_DIM), 1.0),
        'cache_v': nrm((N_ATTN, n_pool, PAGE_SIZE, N_KV_HEADS, HEAD_DIM), 1.0),
        'cache_kidx': nrm((N_ATTN, n_pool, PAGE_SIZE, IDX_DIM), 1.0),
        'state_h': nrm((N_REC, DEC_BATCH, D_RNN), 0.5),
        'state_conv': nrm((N_REC, DEC_BATCH, CONV_W - 1, D_RNN), 1.0),
        'page_table': page_table,
        'c_prompt': nrm((BATCH, D_MODEL), 1.0),
        'c_sample': nrm((DEC_BATCH, D_MODEL), 1.0),
        'ada_w': nrm((DEPTH, D_MODEL, 6 * D_MODEL), 0.5 * D_MODEL ** -0.5),
        'ada_b': nrm((DEPTH, 6 * D_MODEL), 0.02),
        'norm_mix_g': 1.0 + nrm((DEPTH, D_MODEL), 0.02),
        'norm_ffn_g': 1.0 + nrm((DEPTH, D_MODEL), 0.02),
        'norm_final_g': 1.0 + nrm((D_MODEL,), 0.02),
        'attn_w_in': nrm((N_ATTN, D_MODEL, ATTN_PROJ), D_MODEL ** -0.5),
        'attn_w_out': nrm((N_ATTN, N_HEADS * HEAD_DIM, D_MODEL), (N_HEADS * HEAD_DIM) ** -0.5),
        'rec_w_in': nrm((N_REC, D_MODEL, 2 * D_RNN), D_MODEL ** -0.5),
        'rec_conv_w': nrm((N_REC, CONV_W, D_RNN), CONV_W ** -0.5),
        'rec_conv_b': nrm((N_REC, D_RNN), 0.02),
        'rec_w_a': nrm((N_REC, RG_BLOCKS, RG_BLOCK_W, RG_BLOCK_W), RG_BLOCK_W ** -0.5),
        'rec_b_a': nrm((N_REC, D_RNN), 0.02),
        'rec_w_x': nrm((N_REC, RG_BLOCKS, RG_BLOCK_W, RG_BLOCK_W), RG_BLOCK_W ** -0.5),
        'rec_b_x': nrm((N_REC, D_RNN), 0.02),
        'rec_lambda': lam,
        'rec_w_out': nrm((N_REC, D_RNN, D_MODEL), D_RNN ** -0.5),
        'moe_w_group': nrm((DEPTH, D_MODEL, N_GROUPS), D_MODEL ** -0.5),
        'moe_b_group': nrm((DEPTH, N_GROUPS), 0.01),
        'moe_w_expert': nrm((DEPTH, D_MODEL, N_EXPERTS), D_MODEL ** -0.5),
        'moe_b_expert': nrm((DEPTH, N_EXPERTS), 0.01),
        'moe_w1': nrm((DEPTH, N_EXPERTS, D_MODEL, 2 * D_EXPERT), D_MODEL ** -0.5),
        'moe_w2': nrm((DEPTH, N_EXPERTS, D_EXPERT, D_MODEL), D_EXPERT ** -0.5),
    }


def reference(x_prompt, x_sample, cache_k, cache_v, cache_kidx, state_h, state_conv, page_table,
              c_prompt, c_sample, ada_w, ada_b, norm_mix_g, norm_ffn_g, norm_final_g,
              attn_w_in, attn_w_out, rec_w_in, rec_conv_w, rec_conv_b, rec_w_a, rec_b_a,
              rec_w_x, rec_b_x, rec_lambda, rec_w_out, moe_w_group, moe_b_group,
              moe_w_expert, moe_b_expert, moe_w1, moe_w2):
    xp, xs = x_prompt, x_sample
    pos_p = jnp.arange(xp.shape[1])
    pos_s = PAST_LEN + jnp.arange(xs.shape[1])
    k_p, v_p, ki_p, h_p, cv_p = [], [], [], [], []
    k_s, v_s, ki_s, h_s, cv_s = [], [], [], [], []
    for l in range(DEPTH):
        sh1p, sc1p, g1p, sh2p, sc2p, g2p = adaln(c_prompt, ada_w[l], ada_b[l])
        sh1s, sc1s, g1s, sh2s, sc2s, g2s = adaln(c_sample, ada_w[l], ada_b[l])
        hp = modulate(xp, norm_mix_g[l], sh1p, sc1p)
        hs = modulate(xs, norm_mix_g[l], sh1s, sc1s)
        if l % 2 == 0:
            a = l // 2
            q, k, v, qi, ki, wi = attn_project(hp, attn_w_in[a], pos_p)
            op = dsa_prompt(q, k, v, qi, ki, wi) @ attn_w_out[a]
            k_p.append(k)
            v_p.append(v)
            ki_p.append(ki)
            q, k, v, qi, ki, wi = attn_project(hs, attn_w_in[a], pos_s)
            os_ = dsa_sample(q, k, v, qi, ki, wi, cache_k[a], cache_v[a], cache_kidx[a],
                             page_table) @ attn_w_out[a]
            k_s.append(k)
            v_s.append(v)
            ki_s.append(ki)
        else:
            r = l // 2
            rw = (rec_w_in[r], rec_conv_w[r], rec_conv_b[r], rec_w_a[r], rec_b_a[r],
                  rec_w_x[r], rec_b_x[r], rec_lambda[r], rec_w_out[r])
            B = xp.shape[0]
            op, hT, buf = rglru_block(hp, jnp.zeros((B, CONV_W - 1, D_RNN), hp.dtype),
                                      jnp.zeros((B, D_RNN), hp.dtype), *rw)
            h_p.append(hT)
            cv_p.append(buf)
            os_, hT, buf = rglru_block(hs, state_conv[r], state_h[r], *rw)
            h_s.append(hT)
            cv_s.append(buf)
        xp = xp + g1p[:, None, :] * op
        xs = xs + g1s[:, None, :] * os_
        mw = (moe_w_group[l], moe_b_group[l], moe_w_expert[l], moe_b_expert[l], moe_w1[l], moe_w2[l])
        xp = xp + g2p[:, None, :] * hier_moe(modulate(xp, norm_ffn_g[l], sh2p, sc2p), *mw)
        xs = xs + g2s[:, None, :] * hier_moe(modulate(xs, norm_ffn_g[l], sh2s, sc2s), *mw)
    y_prompt = rmsnorm(xp, norm_final_g)
    y_sample = rmsnorm(xs, norm_final_g)
    return (y_prompt, y_sample,
            jnp.stack(k_p), jnp.stack(v_p), jnp.stack(ki_p), jnp.stack(h_p), jnp.stack(cv_p),
            jnp.stack(k_s), jnp.stack(v_s), jnp.stack(ki_s), jnp.stack(h_s), jnp.stack(cv_s))
```

```python
import functools
import math

import jax
import jax.numpy as jnp
from jax import lax
from jax.experimental import pallas as pl
from jax.experimental.pallas import tpu as pltpu

F32 = jnp.float32
BF16 = jnp.bfloat16
I32 = jnp.int32

D_MODEL = 1024
DEPTH = 4
PAGE_SIZE = 128
N_HEADS = 16
HEAD_DIM = 64
N_KV_HEADS = 4
Q_PER_KV = N_HEADS // N_KV_HEADS
IDX_HEADS = 8
IDX_DIM = 64
IDX_SCALE = (IDX_HEADS * IDX_DIM) ** -0.5
TOPK_MAX = 256
ROPE_THETA = 500000.0
ROPE_FRACTION = 4
Q_WIDTH = N_HEADS * HEAD_DIM
KV_WIDTH = N_KV_HEADS * HEAD_DIM
QI_WIDTH = IDX_HEADS * IDX_DIM
ATTN_PROJ = Q_WIDTH + 2 * KV_WIDTH + QI_WIDTH + IDX_DIM + IDX_HEADS
D_RNN = D_MODEL
RG_BLOCKS = 8
RG_BLOCK_W = D_RNN // RG_BLOCKS
CONV_W = 4
RG_C = 8.0
N_GROUPS = 4
EXPERTS_PER_GROUP = 8
N_EXPERTS = N_GROUPS * EXPERTS_PER_GROUP
D_EXPERT = 256
EPS = 1e-6

LANES = 128
SUBLANES = 8
VMEM_LIMIT_BYTES = 56 * 1024 * 1024

ATTN_PROJ_PAD = -(-ATTN_PROJ // LANES) * LANES
ROUTER_PAD = LANES
Q_TILE = LANES
KEY_CHUNK = 512
ROW_TILE = 512
MOE_ROW_TILE = 1024
SCAN_CHUNK = 512
INT_MIN = -(2 ** 31)
INT_MAX = 2 ** 31 - 1
NEG = -1e30


def _cparams(n_axes):
    return pltpu.CompilerParams(dimension_semantics=("arbitrary",) * n_axes,
                                vmem_limit_bytes=VMEM_LIMIT_BYTES)


def _modulated_norm(x, g, shift, scale):
    xn = x * lax.rsqrt(jnp.mean(x * x, axis=-1, keepdims=True) + EPS)
    return (xn * g) * (1.0 + scale) + shift


def _rope_group(xs, cos, sa, sb):
    return xs * cos + pltpu.roll(xs, LANES - 8, 1) * sa + pltpu.roll(xs, 8, 1) * sb


def _float_order_key(s):
    bits = lax.bitcast_convert_type(s, I32)
    return bits ^ ((bits >> 31) & INT_MAX)


def _mod_spec(arr, tm):
    if arr.shape[1] == 1:
        return pl.BlockSpec((1, 1, arr.shape[2]), lambda b, i, *_: (b, 0, 0))
    return pl.BlockSpec((1, tm, arr.shape[2]), lambda b, i, *_: (b, i, 0))


def _adaln_kernel(c_ref, w_ref, b_ref, o_ref):
    c = c_ref[...]
    s = (c * jax.nn.sigmoid(c)).astype(BF16)
    o_ref[0] = jnp.dot(s, w_ref[0].astype(BF16), preferred_element_type=F32) + b_ref[0]


def _adaln(c_all, ada_w, ada_b):
    depth, d, n = ada_w.shape
    rows = c_all.shape[0]
    tn = 1536
    return pl.pallas_call(
        _adaln_kernel,
        grid=(depth, n // tn),
        in_specs=[pl.BlockSpec((rows, d), lambda l, j: (0, 0)),
                  pl.BlockSpec((1, d, tn), lambda l, j: (l, 0, j)),
                  pl.BlockSpec((1, 1, tn), lambda l, j: (l, 0, j))],
        out_specs=pl.BlockSpec((1, rows, tn), lambda l, j: (l, 0, j)),
        out_shape=jax.ShapeDtypeStruct((depth, rows, n), F32),
        compiler_params=_cparams(2),
        name="adaln",
    )(c_all, ada_w, ada_b.reshape(depth, 1, n))


def _rope_tables(pos):
    rot = HEAD_DIM // ROPE_FRACTION
    half = rot // 2
    inv = jnp.exp(-math.log(ROPE_THETA) * jnp.arange(half, dtype=F32) * (2.0 / rot))
    ang = pos.astype(F32)[:, None] * inv[None, :]
    cos, sin = jnp.cos(ang), jnp.sin(ang)
    r = pos.shape[0]
    z = lambda w: jnp.zeros((r, w), F32)
    cos64 = jnp.concatenate([cos, cos, jnp.ones((r, HEAD_DIM - rot), F32)], axis=1)
    sa64 = jnp.concatenate([-sin, z(HEAD_DIM - half)], axis=1)
    sb64 = jnp.concatenate([z(half), sin, z(HEAD_DIM - rot)], axis=1)
    two = lambda t: jnp.concatenate([t, t], axis=1)
    return two(cos64), two(sa64), two(sb64)


def _attn_proj_prompt_kernel(x_ref, g_ref, sh_ref, sc_ref, w_ref, cos_ref, sa_ref, sb_ref,
                             k_ref, v_ref, ki_ref, qT_ref, qiT_ref, wiT_ref, kb_ref, vT_ref, kib_ref):
    h = _modulated_norm(x_ref[0], g_ref[...], sh_ref[0], sc_ref[0]).astype(BF16)
    acc = jnp.dot(h, w_ref[...], preferred_element_type=F32)
    cos, sa, sb = cos_ref[...], sa_ref[...], sb_ref[...]
    n_qb = acc.shape[0] // Q_TILE
    grp = lambda off, j: acc[:, off + j * LANES: off + (j + 1) * LANES]

    for j in range(Q_WIDTH // LANES):
        qjT = (_rope_group(grp(0, j), cos, sa, sb) * (HEAD_DIM ** -0.5)).T.astype(BF16)
        for hh in range(2):
            n, g = divmod(2 * j + hh, Q_PER_KV)
            for jb in range(n_qb):
                qT_ref[0, jb, n, :, g * Q_TILE:(g + 1) * Q_TILE] = (
                    qjT[hh * HEAD_DIM:(hh + 1) * HEAD_DIM, jb * Q_TILE:(jb + 1) * Q_TILE])
    for j in range(KV_WIDTH // LANES):
        kj = _rope_group(grp(Q_WIDTH, j), cos, sa, sb)
        k_ref[0, :, j * LANES:(j + 1) * LANES] = kj
        for hh in range(2):
            kb_ref[0, 2 * j + hh] = kj[:, hh * HEAD_DIM:(hh + 1) * HEAD_DIM].astype(BF16)
    for j in range(KV_WIDTH // LANES):
        vj = grp(Q_WIDTH + KV_WIDTH, j)
        v_ref[0, :, j * LANES:(j + 1) * LANES] = vj
        vT_ref[0, 0, j * LANES:(j + 1) * LANES, :] = vj.T.astype(BF16)
    off_qi = Q_WIDTH + 2 * KV_WIDTH
    for j in range(QI_WIDTH // LANES):
        qijT = _rope_group(grp(off_qi, j), cos, sa, sb).T.astype(BF16)
        for hh in range(2):
            head = 2 * j + hh
            for jb in range(n_qb):
                qiT_ref[0, jb, :, head * Q_TILE:(head + 1) * Q_TILE] = (
                    qijT[hh * IDX_DIM:(hh + 1) * IDX_DIM, jb * Q_TILE:(jb + 1) * Q_TILE])
    last = grp(off_qi + QI_WIDTH, 0)
    kir = _rope_group(last, cos, sa, sb)[:, :IDX_DIM]
    ki_ref[0] = kir
    kib_ref[0] = kir.astype(BF16)
    lastT = last.T
    for jb in range(n_qb):
        wiT_ref[0, jb] = lastT[IDX_DIM:IDX_DIM + IDX_HEADS, jb * Q_TILE:(jb + 1) * Q_TILE]


def _attn_proj_prompt(x, g, shift, scale, w_pad, tables):
    b, t, d = x.shape
    tm = ROW_TILE
    nq = t // Q_TILE
    grid = (b, t // tm)
    row = lambda width: pl.BlockSpec((1, tm, width), lambda bi, i: (bi, i, 0))
    tab = pl.BlockSpec((tm, LANES), lambda bi, i: (i, 0))
    out_shape = (
        jax.ShapeDtypeStruct((b, t, KV_WIDTH), F32),
        jax.ShapeDtypeStruct((b, t, KV_WIDTH), F32),
        jax.ShapeDtypeStruct((b, t, IDX_DIM), F32),
        jax.ShapeDtypeStruct((b, nq, N_KV_HEADS, HEAD_DIM, Q_PER_KV * Q_TILE), BF16),
        jax.ShapeDtypeStruct((b, nq, IDX_DIM, IDX_HEADS * Q_TILE), BF16),
        jax.ShapeDtypeStruct((b, nq, IDX_HEADS, Q_TILE), F32),
        jax.ShapeDtypeStruct((b, N_KV_HEADS, t, HEAD_DIM), BF16),
        jax.ShapeDtypeStruct((b, t // tm, KV_WIDTH, tm), BF16),
        jax.ShapeDtypeStruct((b, t, IDX_DIM), BF16),
    )
    nqb = tm // Q_TILE
    out_specs = (
        row(KV_WIDTH), row(KV_WIDTH), row(IDX_DIM),
        pl.BlockSpec((1, nqb, N_KV_HEADS, HEAD_DIM, Q_PER_KV * Q_TILE), lambda bi, i: (bi, i, 0, 0, 0)),
        pl.BlockSpec((1, nqb, IDX_DIM, IDX_HEADS * Q_TILE), lambda bi, i: (bi, i, 0, 0)),
        pl.BlockSpec((1, nqb, IDX_HEADS, Q_TILE), lambda bi, i: (bi, i, 0, 0)),
        pl.BlockSpec((1, N_KV_HEADS, tm, HEAD_DIM), lambda bi, i: (bi, 0, i, 0)),
        pl.BlockSpec((1, 1, KV_WIDTH, tm), lambda bi, i: (bi, i, 0, 0)),
        row(IDX_DIM),
    )
    return pl.pallas_call(
        _attn_proj_prompt_kernel,
        grid=grid,
        in_specs=[row(d), pl.BlockSpec((1, d), lambda bi, i: (0, 0)),
                  _mod_spec(shift, tm), _mod_spec(scale, tm),
                  pl.BlockSpec(w_pad.shape, lambda bi, i: (0, 0)), tab, tab, tab],
        out_specs=out_specs,
        out_shape=out_shape,
        compiler_params=_cparams(2),
        name="attn_proj_prompt",
    )(x, g, shift, scale, w_pad, *tables)


def _attn_proj_sample_kernel(x_ref, g_ref, sh_ref, sc_ref, w_ref, cos_ref, sa_ref, sb_ref,
                             q_ref, k_ref, v_ref, qi_ref, ki_ref, wi_ref):
    h = _modulated_norm(x_ref[0], g_ref[...], sh_ref[0], sc_ref[0]).astype(BF16)
    acc = jnp.dot(h, w_ref[...], preferred_element_type=F32)
    cos, sa, sb = cos_ref[...], sa_ref[...], sb_ref[...]
    grp = lambda off, j: acc[:, off + j * LANES: off + (j + 1) * LANES]
    for j in range(Q_WIDTH // LANES):
        q_ref[:, j * LANES:(j + 1) * LANES] = _rope_group(grp(0, j), cos, sa, sb) * (HEAD_DIM ** -0.5)
    for j in range(KV_WIDTH // LANES):
        k_ref[:, j * LANES:(j + 1) * LANES] = _rope_group(grp(Q_WIDTH, j), cos, sa, sb)
        v_ref[:, j * LANES:(j + 1) * LANES] = grp(Q_WIDTH + KV_WIDTH, j)
    off_qi = Q_WIDTH + 2 * KV_WIDTH
    for j in range(QI_WIDTH // LANES):
        qi_ref[:, j * LANES:(j + 1) * LANES] = _rope_group(grp(off_qi, j), cos, sa, sb)
    last = grp(off_qi + QI_WIDTH, 0)
    ki_ref[...] = _rope_group(last, cos, sa, sb)[:, :IDX_DIM]
    wi_ref[...] = last[:, IDX_DIM:IDX_DIM + IDX_HEADS]


def _attn_proj_sample(x, g, shift, scale, w_pad, tables):
    _, rows, d = x.shape
    full = lambda shape: pl.BlockSpec(shape, lambda i: (0,) * len(shape))
    widths = (Q_WIDTH, KV_WIDTH, KV_WIDTH, QI_WIDTH, IDX_DIM, IDX_HEADS)
    return pl.pallas_call(
        _attn_proj_sample_kernel,
        grid=(1,),
        in_specs=[full((1, rows, d)), full((1, d)), full((1, rows, d)), full((1, rows, d)),
                  full(w_pad.shape), full((rows, LANES)), full((rows, LANES)), full((rows, LANES))],
        out_specs=tuple(full((rows, w)) for w in widths),
        out_shape=tuple(jax.ShapeDtypeStruct((rows, w), F32) for w in widths),
        compiler_params=_cparams(1),
        name="attn_proj_sample",
    )(x, g, shift, scale, w_pad, *tables)


def _dsa_prompt_kernel(qT_ref, qiT_ref, wiT_ref, kb_ref, vT_ref, kib_ref, o_ref,
                       keys_ref, acc_ref, m_ref, l_ref, j_ref, *, topk, idx_bits):
    kc = KEY_CHUNK
    qb = pl.program_id(1)
    n_chunks = (qb * Q_TILE + Q_TILE + kc - 1) // kc
    rows = lax.broadcasted_iota(I32, (kc, Q_TILE), 0)
    q_pos = qb * Q_TILE + lax.broadcasted_iota(I32, (kc, Q_TILE), 1)

    w_heads = wiT_ref[0, 0] * IDX_SCALE
    qiT = qiT_ref[0, 0]

    def score_chunk(c, carry):
        off = pl.multiple_of(c * kc, kc)
        dots = jnp.dot(kib_ref[0, pl.ds(off, kc), :], qiT, preferred_element_type=F32)
        s = jnp.zeros((kc, Q_TILE), F32)
        for h in range(IDX_HEADS):
            s = s + jnp.maximum(dots[:, h * Q_TILE:(h + 1) * Q_TILE], 0.0) * w_heads[h:h + 1, :]
        keys_ref[pl.ds(off, kc), :] = jnp.where(rows + off <= q_pos, _float_order_key(s), INT_MIN)
        return carry

    lax.fori_loop(0, n_chunks, score_chunk, 0)

    def count(pred):
        def body(c, cnt):
            off = pl.multiple_of(c * kc, kc)
            hit = jnp.where(pred(keys_ref[pl.ds(off, kc), :], rows + off), 1, 0)
            return cnt + hit.reshape(kc // SUBLANES, SUBLANES, Q_TILE).sum(axis=0)
        cnt = lax.fori_loop(0, n_chunks, body, jnp.zeros((SUBLANES, Q_TILE), I32))
        return cnt.sum(axis=0, keepdims=True)

    def thr_bit(i, thr):
        cand = thr + lax.shift_left(jnp.int32(1), 31 - i)
        return jnp.where(count(lambda k, s: k >= cand) >= topk, cand, thr)

    thr = lax.fori_loop(0, 32, thr_bit, jnp.full((1, Q_TILE), INT_MIN, I32))
    n_ge = count(lambda k, s: k >= thr)
    need = topk - count(lambda k, s: k > thr)

    j_ref[...] = jnp.full((1, Q_TILE), INT_MAX, I32)
    tie_split = jnp.max(jnp.where((n_ge > topk) & (thr > INT_MIN), 1, 0))

    @pl.when(tie_split > 0)
    def _():
        def idx_bit(i, j0):
            cand = j0 + lax.shift_left(jnp.int32(1), idx_bits - 1 - i)
            taken_before = count(lambda k, s: (k == thr) & (s < cand))
            return jnp.where(taken_before < need, cand, j0)
        j_ref[...] = lax.fori_loop(0, idx_bits, idx_bit, jnp.zeros((1, Q_TILE), I32))

    j_sel = j_ref[...]

    m_ref[...] = jnp.full(m_ref.shape, NEG, F32)
    l_ref[...] = jnp.zeros(l_ref.shape, F32)
    acc_ref[...] = jnp.zeros(acc_ref.shape, F32)

    def attend_chunk(c, carry):
        off = pl.multiple_of(c * kc, kc)
        k = keys_ref[pl.ds(off, kc), :]
        s_idx = rows + off
        sel = ((k > thr) | ((k == thr) & (s_idx <= j_sel))) & (s_idx <= q_pos)
        bias = jnp.where(sel, 0.0, NEG)
        bias = jnp.concatenate([bias] * Q_PER_KV, axis=1)
        for n in range(N_KV_HEADS):
            s = jnp.dot(kb_ref[0, n, pl.ds(off, kc), :], qT_ref[0, 0, n], preferred_element_type=F32) + bias
            m_prev = m_ref[n]
            m_new = jnp.maximum(m_prev, s.max(axis=0, keepdims=True))
            alpha = jnp.exp(m_prev - m_new)
            p = jnp.exp(s - m_new)
            l_ref[n] = alpha * l_ref[n] + p.sum(axis=0, keepdims=True)
            pv = jnp.dot(vT_ref[0, c, n * HEAD_DIM:(n + 1) * HEAD_DIM, :], p.astype(BF16),
                         preferred_element_type=F32)
            acc_ref[n] = alpha * acc_ref[n] + pv
            m_ref[n] = m_new
        return carry

    lax.fori_loop(0, n_chunks, attend_chunk, 0)

    for n in range(N_KV_HEADS):
        on = acc_ref[n] / l_ref[n]
        for gp in range(Q_PER_KV // 2):
            pair = jnp.concatenate([on[:, (2 * gp) * Q_TILE:(2 * gp + 1) * Q_TILE],
                                    on[:, (2 * gp + 1) * Q_TILE:(2 * gp + 2) * Q_TILE]], axis=0)
            col = (n * Q_PER_KV // 2 + gp) * LANES
            o_ref[0, :, col:col + LANES] = pair.T.astype(BF16)


def _dsa_prompt(qT, qiT, wiT, kb, vT, kib):
    b, nq = qT.shape[:2]
    t = kb.shape[2]
    topk = min(TOPK_MAX, t // 4)
    idx_bits = max(1, (t - 1).bit_length())
    per_q = lambda shape: pl.BlockSpec((1, 1) + shape, lambda bi, qi: (bi, qi) + (0,) * len(shape))
    per_b = lambda shape: pl.BlockSpec((1,) + shape, lambda bi, qi: (bi,) + (0,) * len(shape))
    return pl.pallas_call(
        functools.partial(_dsa_prompt_kernel, topk=topk, idx_bits=idx_bits),
        grid=(b, nq),
        in_specs=[per_q(qT.shape[2:]), per_q(qiT.shape[2:]), per_q(wiT.shape[2:]),
                  per_b(kb.shape[1:]), per_b(vT.shape[1:]), per_b(kib.shape[1:])],
        out_specs=pl.BlockSpec((1, Q_TILE, Q_WIDTH), lambda bi, qi: (bi, qi, 0)),
        out_shape=jax.ShapeDtypeStruct((b, t, Q_WIDTH), BF16),
        scratch_shapes=[pltpu.VMEM((t, Q_TILE), I32),
                        pltpu.VMEM((N_KV_HEADS, HEAD_DIM, Q_PER_KV * Q_TILE), F32),
                        pltpu.VMEM((N_KV_HEADS, 1, Q_PER_KV * Q_TILE), F32),
                        pltpu.VMEM((N_KV_HEADS, 1, Q_PER_KV * Q_TILE), F32),
                        pltpu.VMEM((1, Q_TILE), I32)],
        compiler_params=_cparams(2),
        name="dsa_prompt",
    )(qT, qiT, wiT, kb, vT, kib)


def _dsa_sample_select_kernel(pt_ref, qi_ref, w_ref, kidx_ref, kin_ref, bias_ref, keys_ref,
                              *, n_pages, topk, idx_bits):
    p = pl.program_id(1)
    rows_pad = keys_ref.shape[0]
    qi = qi_ref[0]
    w = w_ref[0] * IDX_SCALE
    dots = lax.dot_general(qi, kidx_ref[0].astype(BF16), (((1,), (1,)), ((), ())),
                           preferred_element_type=F32)
    s = jnp.sum(jnp.maximum(dots, 0.0) * w, axis=0, keepdims=True)
    keys_ref[pl.ds(p, 1), :] = _float_order_key(s)

    @pl.when(p == n_pages - 1)
    def _():
        lane1 = lax.broadcasted_iota(I32, (1, LANES), 1)
        kn = kin_ref[0].astype(BF16).astype(F32)
        dn = jnp.sum(qi.astype(F32) * kn, axis=1, keepdims=True)
        sn = jnp.sum(jnp.maximum(dn, 0.0) * w, axis=0, keepdims=True)
        keys_ref[n_pages:n_pages + 1, :] = jnp.where(lane1 == 0, _float_order_key(sn), INT_MIN)
        keys_ref[n_pages + 1:, :] = jnp.full((rows_pad - n_pages - 1, LANES), INT_MIN, I32)

        keys = keys_ref[...]
        row = lax.broadcasted_iota(I32, keys.shape, 0)
        lane = lax.broadcasted_iota(I32, keys.shape, 1)
        idx = row * LANES + lane
        real = (row < n_pages) | ((row == n_pages) & (lane == 0))
        count = lambda hit: jnp.sum(jnp.where(hit, 1, 0), keepdims=True)

        def thr_bit(i, thr):
            cand = thr + lax.shift_left(jnp.int32(1), 31 - i)
            return jnp.where(count(keys >= cand) >= topk, cand, thr)

        thr = lax.fori_loop(0, 32, thr_bit, jnp.full((1, 1), INT_MIN, I32))
        need = topk - count(keys > thr)

        def idx_bit(i, j0):
            cand = j0 + lax.shift_left(jnp.int32(1), idx_bits - 1 - i)
            return jnp.where(count((keys == thr) & (idx < cand)) < need, cand, j0)

        j_sel = lax.fori_loop(0, idx_bits, idx_bit, jnp.zeros((1, 1), I32))
        sel = ((keys > thr) | ((keys == thr) & (idx <= j_sel))) & real
        bias_ref[0] = jnp.where(sel, 0.0, NEG)


def _dsa_sample_select(page_table, qi3, w3, cache_kidx, ki_new, layer, n_pool):
    db, n_pages = page_table.shape
    rows_pad = -(-(n_pages + 1) // SUBLANES) * SUBLANES
    n_keys = n_pages * PAGE_SIZE + 1
    topk = min(TOPK_MAX, n_keys // 4)
    idx_bits = max(1, (n_keys - 1).bit_length())
    grid_spec = pltpu.PrefetchScalarGridSpec(
        num_scalar_prefetch=1,
        grid=(db, n_pages),
        in_specs=[pl.BlockSpec((1, IDX_HEADS, IDX_DIM), lambda b, p, pt: (b, 0, 0)),
                  pl.BlockSpec((1, IDX_HEADS, 1), lambda b, p, pt: (b, 0, 0)),
                  pl.BlockSpec((1, PAGE_SIZE, IDX_DIM), lambda b, p, pt: (layer * n_pool + pt[b, p], 0, 0)),
                  pl.BlockSpec((1, 1, IDX_DIM), lambda b, p, pt: (b, 0, 0))],
        out_specs=pl.BlockSpec((1, rows_pad, LANES), lambda b, p, pt: (b, 0, 0)),
        scratch_shapes=[pltpu.VMEM((rows_pad, LANES), I32)],
    )
    return pl.pallas_call(
        functools.partial(_dsa_sample_select_kernel, n_pages=n_pages, topk=topk, idx_bits=idx_bits),
        grid_spec=grid_spec,
        out_shape=jax.ShapeDtypeStruct((db, rows_pad, LANES), F32),
        compiler_params=_cparams(2),
        name="dsa_sample_select",
    )(page_table, qi3, w3, cache_kidx, ki_new)


def _dsa_sample_attend_kernel(pt_ref, q_ref, bias_ref, k_ref, v_ref, kn_ref, vn_ref, o_ref,
                              acc_ref, m_ref, l_ref, *, n_pages):
    p = pl.program_id(1)

    @pl.when(p == 0)
    def _():
        m_ref[...] = jnp.full(m_ref.shape, NEG, F32)
        l_ref[...] = jnp.zeros(l_ref.shape, F32)
        acc_ref[...] = jnp.zeros(acc_ref.shape, F32)

    q = q_ref[0]
    s = lax.dot_general(q, k_ref[0].astype(BF16), (((1,), (1,)), ((), ())),
                        preferred_element_type=F32) + bias_ref[0, pl.ds(p, 1), :]
    m_prev = m_ref[...]
    m_new = jnp.maximum(m_prev, s.max(axis=1, keepdims=True))
    alpha = jnp.exp(m_prev - m_new)
    pr = jnp.exp(s - m_new)
    l_ref[...] = alpha * l_ref[...] + pr.sum(axis=1, keepdims=True)
    acc_ref[...] = alpha * acc_ref[...] + jnp.dot(pr.astype(BF16), v_ref[0].astype(BF16),
                                                  preferred_element_type=F32)
    m_ref[...] = m_new

    @pl.when(p == n_pages - 1)
    def _():
        kn = kn_ref[0].astype(BF16).astype(F32)
        vn = vn_ref[0].astype(BF16).astype(F32)
        sn = jnp.sum(q.astype(F32) * kn, axis=1, keepdims=True) + bias_ref[0, n_pages:n_pages + 1, 0:1]
        m_fin = jnp.maximum(m_new, sn)
        a_fin = jnp.exp(m_new - m_fin)
        pn = jnp.exp(sn - m_fin)
        l_fin = a_fin * l_ref[...] + pn
        acc = a_fin * acc_ref[...] + pn.astype(BF16).astype(F32) * vn
        out = acc / l_fin
        head_grp = lax.broadcasted_iota(I32, out.shape, 0) // Q_PER_KV
        lane_grp = lax.broadcasted_iota(I32, out.shape, 1) // HEAD_DIM
        out = jnp.where(head_grp == lane_grp, out, 0.0)
        o = out[:, 0:HEAD_DIM]
        for n in range(1, N_KV_HEADS):
            o = o + out[:, n * HEAD_DIM:(n + 1) * HEAD_DIM]
        o_ref[0] = o


def _dsa_sample_attend(page_table, q_bd, bias, cache_k, cache_v, k_new, v_new, layer, n_pool):
    db, n_pages = page_table.shape
    page = lambda b, p, pt: (layer * n_pool + pt[b, p], 0, 0)
    per_b = lambda shape: pl.BlockSpec((1,) + shape, lambda b, p, pt: (b,) + (0,) * len(shape))
    grid_spec = pltpu.PrefetchScalarGridSpec(
        num_scalar_prefetch=1,
        grid=(db, n_pages),
        in_specs=[per_b(q_bd.shape[1:]), per_b(bias.shape[1:]),
                  pl.BlockSpec((1, PAGE_SIZE, KV_WIDTH), page),
                  pl.BlockSpec((1, PAGE_SIZE, KV_WIDTH), page),
                  per_b((1, KV_WIDTH)), per_b((1, KV_WIDTH))],
        out_specs=per_b((N_HEADS, HEAD_DIM)),
        scratch_shapes=[pltpu.VMEM((N_HEADS, KV_WIDTH), F32),
                        pltpu.VMEM((N_HEADS, 1), F32),
                        pltpu.VMEM((N_HEADS, 1), F32)],
    )
    return pl.pallas_call(
        functools.partial(_dsa_sample_attend_kernel, n_pages=n_pages),
        grid_spec=grid_spec,
        out_shape=jax.ShapeDtypeStruct((db, N_HEADS, HEAD_DIM), F32),
        compiler_params=_cparams(2),
        name="dsa_sample_attend",
    )(page_table, q_bd, bias, cache_k, cache_v, k_new, v_new)


def _out_proj_kernel(a_ref, w_ref, x_ref, gt_ref, o_ref):
    y = jnp.dot(a_ref[0], w_ref[...], preferred_element_type=F32)
    o_ref[0] = x_ref[0] + gt_ref[0] * y


def _out_proj(a, w, x, gate):
    b, t, d = x.shape
    tm = min(ROW_TILE, t)
    row = lambda width: pl.BlockSpec((1, tm, width), lambda bi, i: (bi, i, 0))
    return pl.pallas_call(
        _out_proj_kernel,
        grid=(b, t // tm),
        in_specs=[row(a.shape[2]), pl.BlockSpec(w.shape, lambda bi, i: (0, 0)), row(d), _mod_spec(gate, tm)],
        out_specs=row(d),
        out_shape=jax.ShapeDtypeStruct((b, t, d), F32),
        compiler_params=_cparams(2),
        name="out_proj",
    )(a, w, x, gate)


def _gelu_tanh(x):
    return x * (0.5 * (1.0 + jnp.tanh(math.sqrt(2.0 / math.pi) * (x + 0.044715 * (x * x * x)))))


def _rec_proj_kernel(x_ref, g_ref, sh_ref, sc_ref, w_ref, gate_ref, xb_ref):
    h = _modulated_norm(x_ref[0], g_ref[...], sh_ref[0], sc_ref[0]).astype(BF16)
    acc = jnp.dot(h, w_ref[...], preferred_element_type=F32)
    gate_ref[0] = _gelu_tanh(acc[:, :D_RNN])
    xb_ref[0] = acc[:, D_RNN:]


def _rec_proj(x, g, shift, scale, w):
    b, t, d = x.shape
    tm = min(ROW_TILE, t)
    row = lambda width: pl.BlockSpec((1, tm, width), lambda bi, i: (bi, i, 0))
    return pl.pallas_call(
        _rec_proj_kernel,
        grid=(b, t // tm),
        in_specs=[row(d), pl.BlockSpec((1, d), lambda bi, i: (0, 0)), _mod_spec(shift, tm),
                  _mod_spec(scale, tm), pl.BlockSpec(w.shape, lambda bi, i: (0, 0))],
        out_specs=(row(D_RNN), row(D_RNN)),
        out_shape=(jax.ShapeDtypeStruct((b, t, D_RNN), F32),) * 2,
        compiler_params=_cparams(2),
        name="rec_proj",
    )(x, g, shift, scale, w)


def _rglru_gates(xc, wa_ref, ba, wx_ref, bx, lam):
    xcb = xc.astype(BF16)
    blk = lambda w_ref: jnp.concatenate(
        [jnp.dot(xcb[:, n * RG_BLOCK_W:(n + 1) * RG_BLOCK_W], w_ref[n], preferred_element_type=F32)
         for n in range(RG_BLOCKS)], axis=1)
    r = jax.nn.sigmoid(blk(wa_ref) + ba)
    i = jax.nn.sigmoid(blk(wx_ref) + bx)
    neg_lam = -lam
    softplus = jnp.maximum(neg_lam, 0.0) + jnp.log1p(jnp.exp(-jnp.abs(neg_lam)))
    log_a = -RG_C * r * softplus
    a = jnp.exp(log_a)
    u = jnp.sqrt(1.0 - a * a) * (i * xc)
    return a, u


def _rglru_prompt_kernel(gate_ref, xb_ref, cw_ref, cb_ref, wa_ref, ba_ref, wx_ref, bx_ref, lam_ref,
                         y_ref, h_ref, conv_ref, xcat_ref, a_ref, u_ref, hs_ref, hc_ref):
    tc = xb_ref.shape[1]
    c = pl.program_id(1)
    pad = SUBLANES

    @pl.when(c == 0)
    def _():
        xcat_ref[0:pad, :] = jnp.zeros((pad, D_RNN), F32)
        hc_ref[...] = jnp.zeros(hc_ref.shape, F32)

    @pl.when(c > 0)
    def _():
        xcat_ref[0:pad, :] = xcat_ref[tc:tc + pad, :]

    xcat_ref[pad:pad + tc, :] = xb_ref[0]
    cw = cw_ref[...]
    xc = xcat_ref[pad - 3:pad - 3 + tc, :] * cw[0:1, :]
    for j in range(1, CONV_W):
        xc = xc + xcat_ref[pad - 3 + j:pad - 3 + j + tc, :] * cw[j:j + 1, :]
    xc = cb_ref[...] + xc
    a, u = _rglru_gates(xc, wa_ref, ba_ref[...], wx_ref, bx_ref[...], lam_ref[...])
    a_ref[...] = a
    u_ref[...] = u

    def step(t, h):
        h = a_ref[pl.ds(t, 1), :] * h + u_ref[pl.ds(t, 1), :]
        hs_ref[pl.ds(t, 1), :] = h
        return h

    h = lax.fori_loop(0, tc, step, hc_ref[...], unroll=8)
    hc_ref[...] = h
    y_ref[0] = (hs_ref[...] * gate_ref[0]).astype(BF16)
    h_ref[0] = h
    conv_ref[0] = xcat_ref[pad + tc - (CONV_W - 1):pad + tc, :]


def _rglru_prompt(gate, xb, cw, cb, wa, ba, wx, bx, lam):
    b, t, d = xb.shape
    tc = min(SCAN_CHUNK, t)
    row = pl.BlockSpec((1, tc, d), lambda bi, i: (bi, i, 0))
    vec = pl.BlockSpec((1, d), lambda bi, i: (0, 0))
    full = lambda shape: pl.BlockSpec(shape, lambda bi, i: (0,) * len(shape))
    return pl.pallas_call(
        _rglru_prompt_kernel,
        grid=(b, t // tc),
        in_specs=[row, row, full(cw.shape), vec, full(wa.shape), vec, full(wx.shape), vec, vec],
        out_specs=(row, pl.BlockSpec((1, 1, d), lambda bi, i: (bi, 0, 0)),
                   pl.BlockSpec((1, CONV_W - 1, d), lambda bi, i: (bi, 0, 0))),
        out_shape=(jax.ShapeDtypeStruct((b, t, d), BF16), jax.ShapeDtypeStruct((b, 1, d), F32),
                   jax.ShapeDtypeStruct((b, CONV_W - 1, d), F32)),
        scratch_shapes=[pltpu.VMEM((tc + 2 * SUBLANES, d), F32), pltpu.VMEM((tc, d), F32),
                        pltpu.VMEM((tc, d), F32), pltpu.VMEM((tc, d), F32), pltpu.VMEM((1, d), F32)],
        compiler_params=_cparams(2),
        name="rglru_prompt",
    )(gate, xb, cw, cb, wa, ba, wx, bx, lam)


def _rglru_sample_kernel(gate_ref, xb_ref, buf_ref, h0_ref, cw_ref, cb_ref, wa_ref, ba_ref, wx_ref, bx_ref,
                         lam_ref, y_ref, h_ref, nbuf_ref):
    cw = cw_ref[...]
    xb = xb_ref[...]
    xc = buf_ref[0] * cw[0:1, :]
    for j in range(1, CONV_W - 1):
        xc = xc + buf_ref[j] * cw[j:j + 1, :]
    xc = cb_ref[...] + (xc + xb * cw[CONV_W - 1:CONV_W, :])
    a, u = _rglru_gates(xc, wa_ref, ba_ref[...], wx_ref, bx_ref[...], lam_ref[...])
    h = a * h0_ref[...] + u
    h_ref[...] = h
    y_ref[...] = (h * gate_ref[...]).astype(BF16)
    for j in range(CONV_W - 2):
        nbuf_ref[j] = buf_ref[j + 1]
    nbuf_ref[CONV_W - 2] = xb


def _rglru_sample(gate, xb, buf, h0, cw, cb, wa, ba, wx, bx, lam):
    rows, d = xb.shape
    full = lambda a: pl.BlockSpec(a.shape, lambda i: (0,) * a.ndim)
    args = (gate, xb, buf, h0, cw, cb, wa, ba, wx, bx, lam)
    return pl.pallas_call(
        _rglru_sample_kernel,
        grid=(1,),
        in_specs=[full(a) for a in args],
        out_specs=(full(xb), full(xb), full(buf)),
        out_shape=(jax.ShapeDtypeStruct((rows, d), BF16), jax.ShapeDtypeStruct((rows, d), F32),
                   jax.ShapeDtypeStruct(buf.shape, F32)),
        compiler_params=_cparams(1),
        name="rglru_sample",
    )(*args)


def _route(logits):
    lane = lax.broadcasted_iota(I32, logits.shape, 1)

    def first_max(v):
        m = jnp.max(v, axis=1, keepdims=True)
        return m, jnp.min(jnp.where(v == m, lane, LANES), axis=1, keepdims=True)

    is_grp = lane < N_GROUPS
    gm, grp = first_max(jnp.where(is_grp, logits, -jnp.inf))
    g_w = 1.0 / jnp.sum(jnp.where(is_grp, jnp.exp(logits - gm), 0.0), axis=1, keepdims=True)
    lo = N_GROUPS + EXPERTS_PER_GROUP * grp
    el = jnp.where((lane >= lo) & (lane < lo + EXPERTS_PER_GROUP), logits, -jnp.inf)
    m1, i1 = first_max(el)
    m2, i2 = first_max(jnp.where(lane == i1, -jnp.inf, el))
    e2 = jnp.exp(m2 - m1)
    den = 1.0 + e2
    return i1 - N_GROUPS, i2 - N_GROUPS, (1.0 / den) * g_w, (e2 / den) * g_w


def _moe_kernel(x_ref, g_ref, sh_ref, sc_ref, gt_ref, wr_ref, br_ref, w1_ref, w2_ref, gf_ref, o_ref,
                h_ref, acc_ref, e0_ref, e1_ref, c0_ref, c1_ref, *, final_norm):
    e = pl.program_id(2)

    @pl.when(e == 0)
    def _():
        h = _modulated_norm(x_ref[0], g_ref[...], sh_ref[0], sc_ref[0])
        h_ref[...] = h.astype(BF16)
        logits = jnp.dot(h_ref[...], wr_ref[...], preferred_element_type=F32) + br_ref[...]
        e0_ref[...], e1_ref[...], c0_ref[...], c1_ref[...] = _route(logits)
        acc_ref[...] = jnp.zeros(acc_ref.shape, F32)

    gu = jnp.dot(h_ref[...], w1_ref[0].astype(BF16), preferred_element_type=F32)
    gp, up = gu[:, :D_EXPERT], gu[:, D_EXPERT:]
    act = ((gp * jax.nn.sigmoid(gp)) * up).astype(BF16)
    y = jnp.dot(act, w2_ref[0].astype(BF16), preferred_element_type=F32)
    wt = jnp.where(e0_ref[...] == e, c0_ref[...], 0.0) + jnp.where(e1_ref[...] == e, c1_ref[...], 0.0)
    acc_ref[...] += y * wt

    @pl.when(e == N_EXPERTS - 1)
    def _():
        out = x_ref[0] + gt_ref[0] * acc_ref[...]
        if final_norm:
            out = (out * lax.rsqrt(jnp.mean(out * out, axis=-1, keepdims=True) + EPS)) * gf_ref[...]
        o_ref[0] = out


def _moe(x, g, shift, scale, gate, wr, br, w1, w2, layer, gf, final_norm):
    b, t, d = x.shape
    tm = min(MOE_ROW_TILE, t)
    row = pl.BlockSpec((1, tm, d), lambda bi, i, e: (bi, i, 0))
    vec = lambda width: pl.BlockSpec((1, width), lambda bi, i, e: (0, 0))
    col = lambda dt: pltpu.VMEM((tm, 1), dt)
    return pl.pallas_call(
        functools.partial(_moe_kernel, final_norm=final_norm),
        grid=(b, t // tm, N_EXPERTS),
        in_specs=[row, vec(d), _mod_spec(shift, tm), _mod_spec(scale, tm), _mod_spec(gate, tm),
                  pl.BlockSpec(wr.shape, lambda bi, i, e: (0, 0)), vec(ROUTER_PAD),
                  pl.BlockSpec((1, d, 2 * D_EXPERT), lambda bi, i, e: (layer * N_EXPERTS + e, 0, 0)),
                  pl.BlockSpec((1, D_EXPERT, d), lambda bi, i, e: (layer * N_EXPERTS + e, 0, 0)),
                  vec(d)],
        out_specs=row,
        out_shape=jax.ShapeDtypeStruct((b, t, d), F32),
        scratch_shapes=[pltpu.VMEM((tm, d), BF16), pltpu.VMEM((tm, d), F32),
                        col(I32), col(I32), col(F32), col(F32)],
        compiler_params=_cparams(3),
        name="moe",
    )(x, g, shift, scale, gate, wr, br, w1, w2, gf)


def kernel(x_prompt, x_sample, cache_k, cache_v, cache_kidx, state_h, state_conv, page_table,
           c_prompt, c_sample, ada_w, ada_b, norm_mix_g, norm_ffn_g, norm_final_g,
           attn_w_in, attn_w_out, rec_w_in, rec_conv_w, rec_conv_b, rec_w_a, rec_b_a,
           rec_w_x, rec_b_x, rec_lambda, rec_w_out, moe_w_group, moe_b_group,
           moe_w_expert, moe_b_expert, moe_w1, moe_w2):
    b, t, d = x_prompt.shape
    db = x_sample.shape[0]
    n_pages = page_table.shape[1]
    past = n_pages * PAGE_SIZE
    n_pool = cache_k.shape[1]

    n_cond = b + db
    cond_rows = -(-n_cond // SUBLANES) * SUBLANES
    c_all = jnp.concatenate([c_prompt, c_sample, jnp.zeros((cond_rows - n_cond, d), F32)], axis=0)
    mod = _adaln(c_all, ada_w, ada_b)

    tables_p = _rope_tables(jnp.arange(t, dtype=I32))
    tables_s = _rope_tables(jnp.full((db,), past, I32))
    cache_k2 = cache_k.reshape(-1, PAGE_SIZE, KV_WIDTH)
    cache_v2 = cache_v.reshape(-1, PAGE_SIZE, KV_WIDTH)
    cache_ki2 = cache_kidx.reshape(-1, PAGE_SIZE, IDX_DIM)
    moe_w1f = moe_w1.reshape(-1, d, 2 * D_EXPERT)
    moe_w2f = moe_w2.reshape(-1, D_EXPERT, d)
    row_vec = lambda v: v.reshape(1, -1)
    head_grp = jnp.arange(N_HEADS)[:, None] // Q_PER_KV == jnp.arange(KV_WIDTH)[None, :] // HEAD_DIM

    xp = x_prompt
    xs = x_sample.reshape(1, db, d)
    outs = {name: [] for name in ("k_p", "v_p", "ki_p", "h_p", "cv_p", "k_s", "v_s", "ki_s", "h_s", "cv_s")}
    for l in range(DEPTH):
        parts = [mod[l, :, i * d:(i + 1) * d] for i in range(6)]
        sh1p, sc1p, g1p, sh2p, sc2p, g2p = [m[:b].reshape(b, 1, d) for m in parts]
        sh1s, sc1s, g1s, sh2s, sc2s, g2s = [m[b:n_cond].reshape(1, db, d) for m in parts]
        g_mix = row_vec(norm_mix_g[l])
        if l % 2 == 0:
            a = l // 2
            w_in = jnp.pad(attn_w_in[a], ((0, 0), (0, ATTN_PROJ_PAD - ATTN_PROJ))).astype(BF16)
            w_out = attn_w_out[a].astype(BF16)
            k, v, ki, qT, qiT, wiT, kb, vT, kib = _attn_proj_prompt(xp, g_mix, sh1p, sc1p, w_in, tables_p)
            op = _dsa_prompt(qT, qiT, wiT, kb, vT, kib)
            outs["k_p"].append(k.reshape(b, t, N_KV_HEADS, HEAD_DIM))
            outs["v_p"].append(v.reshape(b, t, N_KV_HEADS, HEAD_DIM))
            outs["ki_p"].append(ki)
            xp = _out_proj(op, w_out, xp, g1p)

            q, k, v, qi, ki, wi = _attn_proj_sample(xs, g_mix, sh1s, sc1s, w_in, tables_s)
            bias = _dsa_sample_select(page_table, qi.reshape(db, IDX_HEADS, IDX_DIM).astype(BF16),
                                      wi.reshape(db, IDX_HEADS, 1), cache_ki2,
                                      ki.reshape(db, 1, IDX_DIM), a, n_pool)
            q_bd = jnp.where(head_grp[None], jnp.tile(q.reshape(db, N_HEADS, HEAD_DIM), (1, 1, N_KV_HEADS)),
                             0.0).astype(BF16)
            os_ = _dsa_sample_attend(page_table, q_bd, bias, cache_k2, cache_v2,
                                     k.reshape(db, 1, KV_WIDTH), v.reshape(db, 1, KV_WIDTH), a, n_pool)
            outs["k_s"].append(k.reshape(db, 1, N_KV_HEADS, HEAD_DIM))
            outs["v_s"].append(v.reshape(db, 1, N_KV_HEADS, HEAD_DIM))
            outs["ki_s"].append(ki.reshape(db, 1, IDX_DIM))
            xs = _out_proj(os_.reshape(1, db, Q_WIDTH).astype(BF16), w_out, xs, g1s)
        else:
            r = l // 2
            w_in = rec_w_in[r].astype(BF16)
            w_out = rec_w_out[r].astype(BF16)
            rec = (rec_conv_w[r], row_vec(rec_conv_b[r]), rec_w_a[r].astype(BF16), row_vec(rec_b_a[r]),
                   rec_w_x[r].astype(BF16), row_vec(rec_b_x[r]), row_vec(rec_lambda[r]))
            gate, xb = _rec_proj(xp, g_mix, sh1p, sc1p, w_in)
            y, h_t, buf = _rglru_prompt(gate, xb, *rec)
            outs["h_p"].append(h_t.reshape(b, D_RNN))
            outs["cv_p"].append(buf)
            xp = _out_proj(y, w_out, xp, g1p)

            gate, xb = _rec_proj(xs, g_mix, sh1s, sc1s, w_in)
            y, h_t, buf = _rglru_sample(gate[0], xb[0], state_conv[r].swapaxes(0, 1), state_h[r], *rec)
            outs["h_s"].append(h_t)
            outs["cv_s"].append(buf.swapaxes(0, 1))
            xs = _out_proj(y.reshape(1, db, D_RNN), w_out, xs, g1s)

        wr = jnp.concatenate([moe_w_group[l], moe_w_expert[l],
                              jnp.zeros((d, ROUTER_PAD - N_GROUPS - N_EXPERTS), F32)], axis=1).astype(BF16)
        br = jnp.concatenate([moe_b_group[l], moe_b_expert[l],
                              jnp.zeros((ROUTER_PAD - N_GROUPS - N_EXPERTS,), F32)]).reshape(1, ROUTER_PAD)
        last = l == DEPTH - 1
        moe_args = (wr, br, moe_w1f, moe_w2f, l, row_vec(norm_final_g), last)
        xp = _moe(xp, row_vec(norm_ffn_g[l]), sh2p, sc2p, g2p, *moe_args)
        xs = _moe(xs, row_vec(norm_ffn_g[l]), sh2s, sc2s, g2s, *moe_args)

    st = lambda name: jnp.stack(outs[name])
    return (xp, xs.reshape(db, 1, d), st("k_p"), st("v_p"), st("ki_p"), st("h_p"), st("cv_p"),
            st("k_s"), st("v_s"), st("ki_s"), st("h_s"), st("cv_s"))
```

```python
import functools
import math

import jax
import jax.numpy as jnp
from jax import lax
from jax.experimental import pallas as pl
from jax.experimental.pallas import tpu as pltpu

F32 = jnp.float32
BF16 = jnp.bfloat16
I32 = jnp.int32

D_MODEL = 1024
DEPTH = 4
PAGE_SIZE = 128
N_HEADS = 16
HEAD_DIM = 64
N_KV_HEADS = 4
Q_PER_KV = N_HEADS // N_KV_HEADS
IDX_HEADS = 8
IDX_DIM = 64
IDX_SCALE = (IDX_HEADS * IDX_DIM) ** -0.5
TOPK_MAX = 256
ROPE_THETA = 500000.0
ROPE_FRACTION = 4
Q_WIDTH = N_HEADS * HEAD_DIM
KV_WIDTH = N_KV_HEADS * HEAD_DIM
QI_WIDTH = IDX_HEADS * IDX_DIM
ATTN_PROJ = Q_WIDTH + 2 * KV_WIDTH + QI_WIDTH + IDX_DIM + IDX_HEADS
D_RNN = D_MODEL
RG_BLOCKS = 8
RG_BLOCK_W = D_RNN // RG_BLOCKS
CONV_W = 4
RG_C = 8.0
N_GROUPS = 4
EXPERTS_PER_GROUP = 8
N_EXPERTS = N_GROUPS * EXPERTS_PER_GROUP
D_EXPERT = 256
EPS = 1e-6

LANES = 128
SUBLANES = 8
VMEM_LIMIT_BYTES = 56 * 1024 * 1024

ATTN_PROJ_PAD = -(-ATTN_PROJ // LANES) * LANES
ROUTER_PAD = LANES
Q_TILE = LANES
KEY_CHUNK = 512
ROW_TILE = 512
MOE_ROW_TILE = 1024
SCAN_CHUNK = 512
SELECT_PAGES_PER_STEP = 16
ATTEND_PAGES_PER_STEP = 8
INT_MIN = -(2 ** 31)
INT_MAX = 2 ** 31 - 1
NEG = -1e30


def _cparams(n_axes):
    return pltpu.CompilerParams(dimension_semantics=("arbitrary",) * n_axes,
                                vmem_limit_bytes=VMEM_LIMIT_BYTES)


def _modulated_norm(x, g, shift, scale):
    xn = x * lax.rsqrt(jnp.mean(x * x, axis=-1, keepdims=True) + EPS)
    return (xn * g) * (1.0 + scale) + shift


def _rope_group(xs, cos, sa, sb):
    return xs * cos + pltpu.roll(xs, LANES - 8, 1) * sa + pltpu.roll(xs, 8, 1) * sb


def _float_order_key(s):
    bits = lax.bitcast_convert_type(s, I32)
    return bits ^ ((bits >> 31) & INT_MAX)


def _mod_spec(arr, tm):
    if arr.shape[1] == 1:
        return pl.BlockSpec((1, 1, arr.shape[2]), lambda b, i, *_: (b, 0, 0))
    return pl.BlockSpec((1, tm, arr.shape[2]), lambda b, i, *_: (b, i, 0))


def _adaln_kernel(c_ref, w_ref, b_ref, o_ref):
    c = c_ref[...]
    s = (c * jax.nn.sigmoid(c)).astype(BF16)
    o_ref[0] = jnp.dot(s, w_ref[0].astype(BF16), preferred_element_type=F32) + b_ref[0]


def _adaln(c_all, ada_w, ada_b):
    depth, d, n = ada_w.shape
    rows = c_all.shape[0]
    tn = 1536
    return pl.pallas_call(
        _adaln_kernel,
        grid=(depth, n // tn),
        in_specs=[pl.BlockSpec((rows, d), lambda l, j: (0, 0)),
                  pl.BlockSpec((1, d, tn), lambda l, j: (l, 0, j)),
                  pl.BlockSpec((1, 1, tn), lambda l, j: (l, 0, j))],
        out_specs=pl.BlockSpec((1, rows, tn), lambda l, j: (l, 0, j)),
        out_shape=jax.ShapeDtypeStruct((depth, rows, n), F32),
        compiler_params=_cparams(2),
        name="adaln",
    )(c_all, ada_w, ada_b.reshape(depth, 1, n))


def _rope_tables(pos):
    rot = HEAD_DIM // ROPE_FRACTION
    half = rot // 2
    inv = jnp.exp(-math.log(ROPE_THETA) * jnp.arange(half, dtype=F32) * (2.0 / rot))
    ang = pos.astype(F32)[:, None] * inv[None, :]
    cos, sin = jnp.cos(ang), jnp.sin(ang)
    r = pos.shape[0]
    z = lambda w: jnp.zeros((r, w), F32)
    cos64 = jnp.concatenate([cos, cos, jnp.ones((r, HEAD_DIM - rot), F32)], axis=1)
    sa64 = jnp.concatenate([-sin, z(HEAD_DIM - half)], axis=1)
    sb64 = jnp.concatenate([z(half), sin, z(HEAD_DIM - rot)], axis=1)
    two = lambda t: jnp.concatenate([t, t], axis=1)
    return two(cos64), two(sa64), two(sb64)


def _attn_proj_prompt_kernel(x_ref, g_ref, sh_ref, sc_ref, w_ref, cos_ref, sa_ref, sb_ref,
                             kT_ref, vT_ref, kiT_ref, qT_ref, qiT_ref, wiT_ref, kb_ref, vTb_ref, kib_ref):
    h = _modulated_norm(x_ref[0], g_ref[...], sh_ref[0], sc_ref[0]).astype(BF16)
    acc = jnp.dot(h, w_ref[...], preferred_element_type=F32)
    cos, sa, sb = cos_ref[...], sa_ref[...], sb_ref[...]
    n_qb = acc.shape[0] // Q_TILE
    grp = lambda off, j: acc[:, off + j * LANES: off + (j + 1) * LANES]

    for j in range(Q_WIDTH // LANES):
        qjT = (_rope_group(grp(0, j), cos, sa, sb) * (HEAD_DIM ** -0.5)).T.astype(BF16)
        for hh in range(2):
            n, g = divmod(2 * j + hh, Q_PER_KV)
            for jb in range(n_qb):
                qT_ref[0, jb, n, :, g * Q_TILE:(g + 1) * Q_TILE] = (
                    qjT[hh * HEAD_DIM:(hh + 1) * HEAD_DIM, jb * Q_TILE:(jb + 1) * Q_TILE])
    for j in range(KV_WIDTH // LANES):
        kj = _rope_group(grp(Q_WIDTH, j), cos, sa, sb)
        kT_ref[0, j * LANES:(j + 1) * LANES, :] = kj.T
        for hh in range(2):
            kb_ref[0, 2 * j + hh] = kj[:, hh * HEAD_DIM:(hh + 1) * HEAD_DIM].astype(BF16)
    for j in range(KV_WIDTH // LANES):
        vjT = grp(Q_WIDTH + KV_WIDTH, j).T
        vT_ref[0, j * LANES:(j + 1) * LANES, :] = vjT
        vTb_ref[0, 0, j * LANES:(j + 1) * LANES, :] = vjT.astype(BF16)
    off_qi = Q_WIDTH + 2 * KV_WIDTH
    for j in range(QI_WIDTH // LANES):
        qijT = _rope_group(grp(off_qi, j), cos, sa, sb).T.astype(BF16)
        for hh in range(2):
            head = 2 * j + hh
            for jb in range(n_qb):
                qiT_ref[0, jb, :, head * Q_TILE:(head + 1) * Q_TILE] = (
                    qijT[hh * IDX_DIM:(hh + 1) * IDX_DIM, jb * Q_TILE:(jb + 1) * Q_TILE])
    last = grp(off_qi + QI_WIDTH, 0)
    kir = _rope_group(last, cos, sa, sb)
    kiT_ref[0] = kir.T[:IDX_DIM, :]
    kib_ref[0] = kir[:, :IDX_DIM].astype(BF16)
    lastT = last.T
    for jb in range(n_qb):
        wiT_ref[0, jb] = lastT[IDX_DIM:IDX_DIM + IDX_HEADS, jb * Q_TILE:(jb + 1) * Q_TILE]


def _attn_proj_prompt(x, g, shift, scale, w_pad, tables):
    b, t, d = x.shape
    tm = ROW_TILE
    nq = t // Q_TILE
    grid = (b, t // tm)
    row = lambda width: pl.BlockSpec((1, tm, width), lambda bi, i: (bi, i, 0))
    col = lambda width: pl.BlockSpec((1, width, tm), lambda bi, i: (bi, 0, i))
    tab = pl.BlockSpec((tm, LANES), lambda bi, i: (i, 0))
    out_shape = (
        jax.ShapeDtypeStruct((b, KV_WIDTH, t), F32),
        jax.ShapeDtypeStruct((b, KV_WIDTH, t), F32),
        jax.ShapeDtypeStruct((b, IDX_DIM, t), F32),
        jax.ShapeDtypeStruct((b, nq, N_KV_HEADS, HEAD_DIM, Q_PER_KV * Q_TILE), BF16),
        jax.ShapeDtypeStruct((b, nq, IDX_DIM, IDX_HEADS * Q_TILE), BF16),
        jax.ShapeDtypeStruct((b, nq, IDX_HEADS, Q_TILE), F32),
        jax.ShapeDtypeStruct((b, N_KV_HEADS, t, HEAD_DIM), BF16),
        jax.ShapeDtypeStruct((b, t // tm, KV_WIDTH, tm), BF16),
        jax.ShapeDtypeStruct((b, t, IDX_DIM), BF16),
    )
    nqb = tm // Q_TILE
    out_specs = (
        col(KV_WIDTH), col(KV_WIDTH), col(IDX_DIM),
        pl.BlockSpec((1, nqb, N_KV_HEADS, HEAD_DIM, Q_PER_KV * Q_TILE), lambda bi, i: (bi, i, 0, 0, 0)),
        pl.BlockSpec((1, nqb, IDX_DIM, IDX_HEADS * Q_TILE), lambda bi, i: (bi, i, 0, 0)),
        pl.BlockSpec((1, nqb, IDX_HEADS, Q_TILE), lambda bi, i: (bi, i, 0, 0)),
        pl.BlockSpec((1, N_KV_HEADS, tm, HEAD_DIM), lambda bi, i: (bi, 0, i, 0)),
        pl.BlockSpec((1, 1, KV_WIDTH, tm), lambda bi, i: (bi, i, 0, 0)),
        row(IDX_DIM),
    )
    return pl.pallas_call(
        _attn_proj_prompt_kernel,
        grid=grid,
        in_specs=[row(d), pl.BlockSpec((1, d), lambda bi, i: (0, 0)),
                  _mod_spec(shift, tm), _mod_spec(scale, tm),
                  pl.BlockSpec(w_pad.shape, lambda bi, i: (0, 0)), tab, tab, tab],
        out_specs=out_specs,
        out_shape=out_shape,
        compiler_params=_cparams(2),
        name="attn_proj_prompt",
    )(x, g, shift, scale, w_pad, *tables)


def _attn_proj_sample_kernel(x_ref, g_ref, sh_ref, sc_ref, w_ref, cos_ref, sa_ref, sb_ref,
                             q_ref, k_ref, v_ref, qi_ref, ki_ref, wi_ref):
    h = _modulated_norm(x_ref[0], g_ref[...], sh_ref[0], sc_ref[0]).astype(BF16)
    acc = jnp.dot(h, w_ref[...], preferred_element_type=F32)
    cos, sa, sb = cos_ref[...], sa_ref[...], sb_ref[...]
    grp = lambda off, j: acc[:, off + j * LANES: off + (j + 1) * LANES]
    for j in range(Q_WIDTH // LANES):
        q_ref[:, j * LANES:(j + 1) * LANES] = _rope_group(grp(0, j), cos, sa, sb) * (HEAD_DIM ** -0.5)
    for j in range(KV_WIDTH // LANES):
        k_ref[:, j * LANES:(j + 1) * LANES] = _rope_group(grp(Q_WIDTH, j), cos, sa, sb)
        v_ref[:, j * LANES:(j + 1) * LANES] = grp(Q_WIDTH + KV_WIDTH, j)
    off_qi = Q_WIDTH + 2 * KV_WIDTH
    for j in range(QI_WIDTH // LANES):
        qi_ref[:, j * LANES:(j + 1) * LANES] = _rope_group(grp(off_qi, j), cos, sa, sb)
    last = grp(off_qi + QI_WIDTH, 0)
    ki_ref[...] = _rope_group(last, cos, sa, sb)[:, :IDX_DIM]
    wi_ref[...] = last[:, IDX_DIM:IDX_DIM + IDX_HEADS]


def _attn_proj_sample(x, g, shift, scale, w_pad, tables):
    _, rows, d = x.shape
    full = lambda shape: pl.BlockSpec(shape, lambda i: (0,) * len(shape))
    widths = (Q_WIDTH, KV_WIDTH, KV_WIDTH, QI_WIDTH, IDX_DIM, IDX_HEADS)
    return pl.pallas_call(
        _attn_proj_sample_kernel,
        grid=(1,),
        in_specs=[full((1, rows, d)), full((1, d)), full((1, rows, d)), full((1, rows, d)),
                  full(w_pad.shape), full((rows, LANES)), full((rows, LANES)), full((rows, LANES))],
        out_specs=tuple(full((rows, w)) for w in widths),
        out_shape=tuple(jax.ShapeDtypeStruct((rows, w), F32) for w in widths),
        compiler_params=_cparams(1),
        name="attn_proj_sample",
    )(x, g, shift, scale, w_pad, *tables)


def _dsa_prompt_kernel(qT_ref, qiT_ref, wiT_ref, kb_ref, vT_ref, kib_ref, o_ref,
                       keys_ref, acc_ref, m_ref, l_ref, j_ref, *, topk, idx_bits):
    kc = KEY_CHUNK
    qb = pl.program_id(1)
    n_chunks = (qb * Q_TILE + Q_TILE + kc - 1) // kc
    rows = lax.broadcasted_iota(I32, (kc, Q_TILE), 0)
    q_pos = qb * Q_TILE + lax.broadcasted_iota(I32, (kc, Q_TILE), 1)

    w_heads = wiT_ref[0, 0] * IDX_SCALE
    qiT = qiT_ref[0, 0]

    def score_chunk(c, carry):
        off = pl.multiple_of(c * kc, kc)
        dots = jnp.dot(kib_ref[0, pl.ds(off, kc), :], qiT, preferred_element_type=F32)
        s = jnp.zeros((kc, Q_TILE), F32)
        for h in range(IDX_HEADS):
            s = s + jnp.maximum(dots[:, h * Q_TILE:(h + 1) * Q_TILE], 0.0) * w_heads[h:h + 1, :]
        keys_ref[pl.ds(off, kc), :] = jnp.where(rows + off <= q_pos, _float_order_key(s), INT_MIN)
        return carry

    lax.fori_loop(0, n_chunks, score_chunk, 0)

    def count(pred):
        def body(c, cnt):
            off = pl.multiple_of(c * kc, kc)
            hit = jnp.where(pred(keys_ref[pl.ds(off, kc), :], rows + off), 1, 0)
            return cnt + hit.reshape(kc // SUBLANES, SUBLANES, Q_TILE).sum(axis=0)
        cnt = lax.fori_loop(0, n_chunks, body, jnp.zeros((SUBLANES, Q_TILE), I32))
        return cnt.sum(axis=0, keepdims=True)

    def thr_open(state):
        i, _, n_ge = state
        return (i < 32) & (jnp.max(jnp.where(n_ge != topk, 1, 0)) > 0)

    def thr_bit(state):
        i, thr, n_ge = state
        cand = thr + lax.shift_left(jnp.int32(1), 31 - i)
        n_cand = count(lambda k, s: k >= cand)
        ok = n_cand >= topk
        return i + 1, jnp.where(ok, cand, thr), jnp.where(ok, n_cand, n_ge)

    n_all = jnp.full((1, Q_TILE), 1, I32) * (n_chunks * kc)
    _, thr, n_ge = lax.while_loop(thr_open, thr_bit,
                                  (jnp.int32(0), jnp.full((1, Q_TILE), INT_MIN, I32), n_all))

    j_ref[...] = jnp.full((1, Q_TILE), INT_MAX, I32)
    tie_split = jnp.max(jnp.where((n_ge > topk) & (thr > INT_MIN), 1, 0))

    @pl.when(tie_split > 0)
    def _():
        need = topk - count(lambda k, s: k > thr)

        def idx_bit(i, j0):
            cand = j0 + lax.shift_left(jnp.int32(1), idx_bits - 1 - i)
            taken_before = count(lambda k, s: (k == thr) & (s < cand))
            return jnp.where(taken_before < need, cand, j0)
        j_ref[...] = lax.fori_loop(0, idx_bits, idx_bit, jnp.zeros((1, Q_TILE), I32))

    j_sel = j_ref[...]

    m_ref[...] = jnp.full(m_ref.shape, NEG, F32)
    l_ref[...] = jnp.zeros(l_ref.shape, F32)
    acc_ref[...] = jnp.zeros(acc_ref.shape, F32)

    def attend_chunk(c, carry):
        off = pl.multiple_of(c * kc, kc)
        k = keys_ref[pl.ds(off, kc), :]
        s_idx = rows + off
        sel = ((k > thr) | ((k == thr) & (s_idx <= j_sel))) & (s_idx <= q_pos)
        bias = jnp.where(sel, 0.0, NEG)
        bias = jnp.concatenate([bias] * Q_PER_KV, axis=1)
        for n in range(N_KV_HEADS):
            s = jnp.dot(kb_ref[0, n, pl.ds(off, kc), :], qT_ref[0, 0, n], preferred_element_type=F32) + bias
            m_prev = m_ref[n]
            m_new = jnp.maximum(m_prev, s.max(axis=0, keepdims=True))
            alpha = jnp.exp(m_prev - m_new)
            p = jnp.exp(s - m_new)
            l_ref[n] = alpha * l_ref[n] + p.sum(axis=0, keepdims=True)
            pv = jnp.dot(vT_ref[0, c, n * HEAD_DIM:(n + 1) * HEAD_DIM, :], p.astype(BF16),
                         preferred_element_type=F32)
            acc_ref[n] = alpha * acc_ref[n] + pv
            m_ref[n] = m_new
        return carry

    lax.fori_loop(0, n_chunks, attend_chunk, 0)

    for n in range(N_KV_HEADS):
        on = acc_ref[n] / l_ref[n]
        for gp in range(Q_PER_KV // 2):
            pair = jnp.concatenate([on[:, (2 * gp) * Q_TILE:(2 * gp + 1) * Q_TILE],
                                    on[:, (2 * gp + 1) * Q_TILE:(2 * gp + 2) * Q_TILE]], axis=0)
            col = (n * Q_PER_KV // 2 + gp) * LANES
            o_ref[0, :, col:col + LANES] = pair.T.astype(BF16)


def _dsa_prompt(qT, qiT, wiT, kb, vT, kib):
    b, nq = qT.shape[:2]
    t = kb.shape[2]
    topk = min(TOPK_MAX, t // 4)
    idx_bits = max(1, (t - 1).bit_length())
    per_q = lambda shape: pl.BlockSpec((1, 1) + shape, lambda bi, qi: (bi, qi) + (0,) * len(shape))
    per_b = lambda shape: pl.BlockSpec((1,) + shape, lambda bi, qi: (bi,) + (0,) * len(shape))
    return pl.pallas_call(
        functools.partial(_dsa_prompt_kernel, topk=topk, idx_bits=idx_bits),
        grid=(b, nq),
        in_specs=[per_q(qT.shape[2:]), per_q(qiT.shape[2:]), per_q(wiT.shape[2:]),
                  per_b(kb.shape[1:]), per_b(vT.shape[1:]), per_b(kib.shape[1:])],
        out_specs=pl.BlockSpec((1, Q_TILE, Q_WIDTH), lambda bi, qi: (bi, qi, 0)),
        out_shape=jax.ShapeDtypeStruct((b, t, Q_WIDTH), BF16),
        scratch_shapes=[pltpu.VMEM((t, Q_TILE), I32),
                        pltpu.VMEM((N_KV_HEADS, HEAD_DIM, Q_PER_KV * Q_TILE), F32),
                        pltpu.VMEM((N_KV_HEADS, 1, Q_PER_KV * Q_TILE), F32),
                        pltpu.VMEM((N_KV_HEADS, 1, Q_PER_KV * Q_TILE), F32),
                        pltpu.VMEM((1, Q_TILE), I32)],
        compiler_params=_cparams(2),
        name="dsa_prompt",
    )(qT, qiT, wiT, kb, vT, kib)


def _dsa_sample_select_kernel(pt_ref, qi_ref, w_ref, *refs, n_pages, pg, topk, idx_bits):
    kiT_refs, (kin_ref, bias_ref, keys_ref) = refs[:pg], refs[pg:]
    g = pl.program_id(1)
    rows_pad = keys_ref.shape[0]
    qi = qi_ref[0]
    w = w_ref[0] * IDX_SCALE
    kiT = jnp.concatenate([r[0].astype(BF16) for r in kiT_refs], axis=1)
    dots = jnp.dot(qi, kiT, preferred_element_type=F32)
    page_keys = _float_order_key(jnp.sum(jnp.maximum(dots, 0.0) * w, axis=0, keepdims=True))
    for j in range(pg):
        keys_ref[pl.ds(g * pg + j, 1), :] = page_keys[:, j * PAGE_SIZE:(j + 1) * PAGE_SIZE]

    @pl.when(g == n_pages // pg - 1)
    def _():
        lane1 = lax.broadcasted_iota(I32, (1, LANES), 1)
        kn = kin_ref[0].astype(BF16).astype(F32)
        dn = jnp.sum(qi.astype(F32) * kn, axis=1, keepdims=True)
        sn = jnp.sum(jnp.maximum(dn, 0.0) * w, axis=0, keepdims=True)
        keys_ref[n_pages:n_pages + 1, :] = jnp.where(lane1 == 0, _float_order_key(sn), INT_MIN)
        keys_ref[n_pages + 1:, :] = jnp.full((rows_pad - n_pages - 1, LANES), INT_MIN, I32)

        keys = keys_ref[...]
        row = lax.broadcasted_iota(I32, keys.shape, 0)
        lane = lax.broadcasted_iota(I32, keys.shape, 1)
        idx = row * LANES + lane
        real = (row < n_pages) | ((row == n_pages) & (lane == 0))
        count = lambda hit: jnp.sum(jnp.where(hit, 1, 0), keepdims=True)

        def thr_bit(i, thr):
            cand = thr + lax.shift_left(jnp.int32(1), 31 - i)
            return jnp.where(count(keys >= cand) >= topk, cand, thr)

        thr = lax.fori_loop(0, 32, thr_bit, jnp.full((1, 1), INT_MIN, I32))
        need = topk - count(keys > thr)

        def idx_bit(i, j0):
            cand = j0 + lax.shift_left(jnp.int32(1), idx_bits - 1 - i)
            return jnp.where(count((keys == thr) & (idx < cand)) < need, cand, j0)

        j_sel = lax.fori_loop(0, idx_bits, idx_bit, jnp.zeros((1, 1), I32))
        sel = ((keys > thr) | ((keys == thr) & (idx <= j_sel))) & real
        bias_ref[0] = jnp.where(sel, 0.0, NEG)


def _page_specs(block, layer, n_pool, pg):
    def spec(j):
        return pl.BlockSpec(block, lambda b, g, pt: (layer * n_pool + pt[b, g * pg + j], 0, 0))
    return [spec(j) for j in range(pg)]


def _dsa_sample_select(page_table, qi3, w3, cache_kiT, ki_new, layer, n_pool):
    db, n_pages = page_table.shape
    pg = min(SELECT_PAGES_PER_STEP, n_pages)
    rows_pad = -(-(n_pages + 1) // SUBLANES) * SUBLANES
    n_keys = n_pages * PAGE_SIZE + 1
    topk = min(TOPK_MAX, n_keys // 4)
    idx_bits = max(1, (n_keys - 1).bit_length())
    grid_spec = pltpu.PrefetchScalarGridSpec(
        num_scalar_prefetch=1,
        grid=(db, n_pages // pg),
        in_specs=[pl.BlockSpec((1, IDX_HEADS, IDX_DIM), lambda b, g, pt: (b, 0, 0)),
                  pl.BlockSpec((1, IDX_HEADS, 1), lambda b, g, pt: (b, 0, 0))]
                 + _page_specs((1, IDX_DIM, PAGE_SIZE), layer, n_pool, pg)
                 + [pl.BlockSpec((1, 1, IDX_DIM), lambda b, g, pt: (b, 0, 0))],
        out_specs=pl.BlockSpec((1, rows_pad, LANES), lambda b, g, pt: (b, 0, 0)),
        scratch_shapes=[pltpu.VMEM((rows_pad, LANES), I32)],
    )
    return pl.pallas_call(
        functools.partial(_dsa_sample_select_kernel, n_pages=n_pages, pg=pg, topk=topk, idx_bits=idx_bits),
        grid_spec=grid_spec,
        out_shape=jax.ShapeDtypeStruct((db, rows_pad, LANES), F32),
        compiler_params=_cparams(2),
        name="dsa_sample_select",
    )(page_table, qi3, w3, *([cache_kiT] * pg), ki_new)


def _dsa_sample_attend_kernel(pt_ref, q_ref, bias_ref, *refs, n_pages, pg):
    kT_refs, vT_refs = refs[:pg], refs[pg:2 * pg]
    kn_ref, vn_ref, o_ref, acc_ref, m_ref, l_ref = refs[2 * pg:]
    g = pl.program_id(1)

    @pl.when(g == 0)
    def _():
        m_ref[...] = jnp.full(m_ref.shape, NEG, F32)
        l_ref[...] = jnp.zeros(l_ref.shape, F32)
        acc_ref[...] = jnp.zeros(acc_ref.shape, F32)

    q = q_ref[0]
    kT = jnp.concatenate([r[0].astype(BF16) for r in kT_refs], axis=1)
    bias = jnp.concatenate([bias_ref[0, pl.ds(g * pg + j, 1), :] for j in range(pg)], axis=1)
    s = jnp.dot(q, kT, preferred_element_type=F32) + bias
    m_prev = m_ref[...]
    m_new = jnp.maximum(m_prev, s.max(axis=1, keepdims=True))
    alpha = jnp.exp(m_prev - m_new)
    pr = jnp.exp(s - m_new)
    l_ref[...] = alpha * l_ref[...] + pr.sum(axis=1, keepdims=True)
    vT = jnp.concatenate([r[0].astype(BF16) for r in vT_refs], axis=1)
    acc_ref[...] = alpha * acc_ref[...] + lax.dot_general(
        pr.astype(BF16), vT, (((1,), (1,)), ((), ())), preferred_element_type=F32)
    m_ref[...] = m_new

    @pl.when(g == n_pages // pg - 1)
    def _():
        kn = kn_ref[0].astype(BF16).astype(F32)
        vn = vn_ref[0].astype(BF16).astype(F32)
        sn = jnp.sum(q.astype(F32) * kn, axis=1, keepdims=True) + bias_ref[0, n_pages:n_pages + 1, 0:1]
        m_fin = jnp.maximum(m_new, sn)
        a_fin = jnp.exp(m_new - m_fin)
        pn = jnp.exp(sn - m_fin)
        l_fin = a_fin * l_ref[...] + pn
        acc = a_fin * acc_ref[...] + pn.astype(BF16).astype(F32) * vn
        out = acc / l_fin
        head_grp = lax.broadcasted_iota(I32, out.shape, 0) // Q_PER_KV
        lane_grp = lax.broadcasted_iota(I32, out.shape, 1) // HEAD_DIM
        out = jnp.where(head_grp == lane_grp, out, 0.0)
        o = out[:, 0:HEAD_DIM]
        for n in range(1, N_KV_HEADS):
            o = o + out[:, n * HEAD_DIM:(n + 1) * HEAD_DIM]
        o_ref[0] = o


def _dsa_sample_attend(page_table, q_bd, bias, cache_kT, cache_vT, k_new, v_new, layer, n_pool):
    db, n_pages = page_table.shape
    pg = min(ATTEND_PAGES_PER_STEP, n_pages)
    per_b = lambda shape: pl.BlockSpec((1,) + shape, lambda b, g, pt: (b,) + (0,) * len(shape))
    pages = _page_specs((1, KV_WIDTH, PAGE_SIZE), layer, n_pool, pg)
    grid_spec = pltpu.PrefetchScalarGridSpec(
        num_scalar_prefetch=1,
        grid=(db, n_pages // pg),
        in_specs=[per_b(q_bd.shape[1:]), per_b(bias.shape[1:])] + pages + pages
                 + [per_b((1, KV_WIDTH)), per_b((1, KV_WIDTH))],
        out_specs=per_b((N_HEADS, HEAD_DIM)),
        scratch_shapes=[pltpu.VMEM((N_HEADS, KV_WIDTH), F32),
                        pltpu.VMEM((N_HEADS, 1), F32),
                        pltpu.VMEM((N_HEADS, 1), F32)],
    )
    return pl.pallas_call(
        functools.partial(_dsa_sample_attend_kernel, n_pages=n_pages, pg=pg),
        grid_spec=grid_spec,
        out_shape=jax.ShapeDtypeStruct((db, N_HEADS, HEAD_DIM), F32),
        compiler_params=_cparams(2),
        name="dsa_sample_attend",
    )(page_table, q_bd, bias, *([cache_kT] * pg), *([cache_vT] * pg), k_new, v_new)


def _out_proj_kernel(a_ref, w_ref, x_ref, gt_ref, o_ref):
    y = jnp.dot(a_ref[0], w_ref[...], preferred_element_type=F32)
    o_ref[0] = x_ref[0] + gt_ref[0] * y


def _out_proj(a, w, x, gate):
    b, t, d = x.shape
    tm = min(ROW_TILE, t)
    row = lambda width: pl.BlockSpec((1, tm, width), lambda bi, i: (bi, i, 0))
    return pl.pallas_call(
        _out_proj_kernel,
        grid=(b, t // tm),
        in_specs=[row(a.shape[2]), pl.BlockSpec(w.shape, lambda bi, i: (0, 0)), row(d), _mod_spec(gate, tm)],
        out_specs=row(d),
        out_shape=jax.ShapeDtypeStruct((b, t, d), F32),
        compiler_params=_cparams(2),
        name="out_proj",
    )(a, w, x, gate)


def _gelu_tanh(x):
    return x * (0.5 * (1.0 + jnp.tanh(math.sqrt(2.0 / math.pi) * (x + 0.044715 * (x * x * x)))))


def _rec_proj_kernel(x_ref, g_ref, sh_ref, sc_ref, w_ref, gate_ref, xb_ref):
    h = _modulated_norm(x_ref[0], g_ref[...], sh_ref[0], sc_ref[0]).astype(BF16)
    acc = jnp.dot(h, w_ref[...], preferred_element_type=F32)
    gate_ref[0] = _gelu_tanh(acc[:, :D_RNN])
    xb_ref[0] = acc[:, D_RNN:]


def _rec_proj(x, g, shift, scale, w):
    b, t, d = x.shape
    tm = min(ROW_TILE, t)
    row = lambda width: pl.BlockSpec((1, tm, width), lambda bi, i: (bi, i, 0))
    return pl.pallas_call(
        _rec_proj_kernel,
        grid=(b, t // tm),
        in_specs=[row(d), pl.BlockSpec((1, d), lambda bi, i: (0, 0)), _mod_spec(shift, tm),
                  _mod_spec(scale, tm), pl.BlockSpec(w.shape, lambda bi, i: (0, 0))],
        out_specs=(row(D_RNN), row(D_RNN)),
        out_shape=(jax.ShapeDtypeStruct((b, t, D_RNN), F32),) * 2,
        compiler_params=_cparams(2),
        name="rec_proj",
    )(x, g, shift, scale, w)


def _rglru_gates(xc, wa_ref, ba, wx_ref, bx, lam):
    xcb = xc.astype(BF16)
    blk = lambda w_ref: jnp.concatenate(
        [jnp.dot(xcb[:, n * RG_BLOCK_W:(n + 1) * RG_BLOCK_W], w_ref[n], preferred_element_type=F32)
         for n in range(RG_BLOCKS)], axis=1)
    r = jax.nn.sigmoid(blk(wa_ref) + ba)
    i = jax.nn.sigmoid(blk(wx_ref) + bx)
    neg_lam = -lam
    softplus = jnp.maximum(neg_lam, 0.0) + jnp.log1p(jnp.exp(-jnp.abs(neg_lam)))
    log_a = -RG_C * r * softplus
    a = jnp.exp(log_a)
    u = jnp.sqrt(1.0 - a * a) * (i * xc)
    return a, u


def _rglru_prompt_kernel(gate_ref, xb_ref, cw_ref, cb_ref, wa_ref, ba_ref, wx_ref, bx_ref, lam_ref,
                         y_ref, h_ref, conv_ref, xcat_ref, a_ref, u_ref, hs_ref, hc_ref):
    tc = xb_ref.shape[1]
    c = pl.program_id(1)
    pad = SUBLANES

    @pl.when(c == 0)
    def _():
        xcat_ref[0:pad, :] = jnp.zeros((pad, D_RNN), F32)
        hc_ref[...] = jnp.zeros(hc_ref.shape, F32)

    @pl.when(c > 0)
    def _():
        xcat_ref[0:pad, :] = xcat_ref[tc:tc + pad, :]

    xcat_ref[pad:pad + tc, :] = xb_ref[0]
    cw = cw_ref[...]
    xc = xcat_ref[pad - 3:pad - 3 + tc, :] * cw[0:1, :]
    for j in range(1, CONV_W):
        xc = xc + xcat_ref[pad - 3 + j:pad - 3 + j + tc, :] * cw[j:j + 1, :]
    xc = cb_ref[...] + xc
    a, u = _rglru_gates(xc, wa_ref, ba_ref[...], wx_ref, bx_ref[...], lam_ref[...])
    a_ref[...] = a
    u_ref[...] = u

    def step(t, h):
        h = a_ref[pl.ds(t, 1), :] * h + u_ref[pl.ds(t, 1), :]
        hs_ref[pl.ds(t, 1), :] = h
        return h

    h = lax.fori_loop(0, tc, step, hc_ref[...], unroll=8)
    hc_ref[...] = h
    y_ref[0] = (hs_ref[...] * gate_ref[0]).astype(BF16)
    h_ref[0] = h
    conv_ref[0] = xcat_ref[pad + tc - (CONV_W - 1):pad + tc, :]


def _rglru_prompt(gate, xb, cw, cb, wa, ba, wx, bx, lam):
    b, t, d = xb.shape
    tc = min(SCAN_CHUNK, t)
    row = pl.BlockSpec((1, tc, d), lambda bi, i: (bi, i, 0))
    vec = pl.BlockSpec((1, d), lambda bi, i: (0, 0))
    full = lambda shape: pl.BlockSpec(shape, lambda bi, i: (0,) * len(shape))
    return pl.pallas_call(
        _rglru_prompt_kernel,
        grid=(b, t // tc),
        in_specs=[row, row, full(cw.shape), vec, full(wa.shape), vec, full(wx.shape), vec, vec],
        out_specs=(row, pl.BlockSpec((1, 1, d), lambda bi, i: (bi, 0, 0)),
                   pl.BlockSpec((1, CONV_W - 1, d), lambda bi, i: (bi, 0, 0))),
        out_shape=(jax.ShapeDtypeStruct((b, t, d), BF16), jax.ShapeDtypeStruct((b, 1, d), F32),
                   jax.ShapeDtypeStruct((b, CONV_W - 1, d), F32)),
        scratch_shapes=[pltpu.VMEM((tc + 2 * SUBLANES, d), F32), pltpu.VMEM((tc, d), F32),
                        pltpu.VMEM((tc, d), F32), pltpu.VMEM((tc, d), F32), pltpu.VMEM((1, d), F32)],
        compiler_params=_cparams(2),
        name="rglru_prompt",
    )(gate, xb, cw, cb, wa, ba, wx, bx, lam)


def _rglru_sample_kernel(gate_ref, xb_ref, buf_ref, h0_ref, cw_ref, cb_ref, wa_ref, ba_ref, wx_ref, bx_ref,
                         lam_ref, y_ref, h_ref, nbuf_ref):
    cw = cw_ref[...]
    xb = xb_ref[...]
    xc = buf_ref[0] * cw[0:1, :]
    for j in range(1, CONV_W - 1):
        xc = xc + buf_ref[j] * cw[j:j + 1, :]
    xc = cb_ref[...] + (xc + xb * cw[CONV_W - 1:CONV_W, :])
    a, u = _rglru_gates(xc, wa_ref, ba_ref[...], wx_ref, bx_ref[...], lam_ref[...])
    h = a * h0_ref[...] + u
    h_ref[...] = h
    y_ref[...] = (h * gate_ref[...]).astype(BF16)
    for j in range(CONV_W - 2):
        nbuf_ref[j] = buf_ref[j + 1]
    nbuf_ref[CONV_W - 2] = xb


def _rglru_sample(gate, xb, buf, h0, cw, cb, wa, ba, wx, bx, lam):
    rows, d = xb.shape
    full = lambda a: pl.BlockSpec(a.shape, lambda i: (0,) * a.ndim)
    args = (gate, xb, buf, h0, cw, cb, wa, ba, wx, bx, lam)
    return pl.pallas_call(
        _rglru_sample_kernel,
        grid=(1,),
        in_specs=[full(a) for a in args],
        out_specs=(full(xb), full(xb), full(buf)),
        out_shape=(jax.ShapeDtypeStruct((rows, d), BF16), jax.ShapeDtypeStruct((rows, d), F32),
                   jax.ShapeDtypeStruct(buf.shape, F32)),
        compiler_params=_cparams(1),
        name="rglru_sample",
    )(*args)


def _route(logits):
    lane = lax.broadcasted_iota(I32, logits.shape, 1)

    def first_max(v):
        m = jnp.max(v, axis=1, keepdims=True)
        return m, jnp.min(jnp.where(v == m, lane, LANES), axis=1, keepdims=True)

    is_grp = lane < N_GROUPS
    gm, grp = first_max(jnp.where(is_grp, logits, -jnp.inf))
    g_w = 1.0 / jnp.sum(jnp.where(is_grp, jnp.exp(logits - gm), 0.0), axis=1, keepdims=True)
    lo = N_GROUPS + EXPERTS_PER_GROUP * grp
    el = jnp.where((lane >= lo) & (lane < lo + EXPERTS_PER_GROUP), logits, -jnp.inf)
    m1, i1 = first_max(el)
    m2, i2 = first_max(jnp.where(lane == i1, -jnp.inf, el))
    e2 = jnp.exp(m2 - m1)
    den = 1.0 + e2
    return i1 - N_GROUPS, i2 - N_GROUPS, (1.0 / den) * g_w, (e2 / den) * g_w


def _moe_kernel(x_ref, g_ref, sh_ref, sc_ref, gt_ref, wr_ref, br_ref, w1_ref, w2_ref, gf_ref, o_ref,
                h_ref, acc_ref, e0_ref, e1_ref, c0_ref, c1_ref, *, final_norm):
    e = pl.program_id(2)

    @pl.when(e == 0)
    def _():
        h = _modulated_norm(x_ref[0], g_ref[...], sh_ref[0], sc_ref[0])
        h_ref[...] = h.astype(BF16)
        logits = jnp.dot(h_ref[...], wr_ref[...], preferred_element_type=F32) + br_ref[...]
        e0_ref[...], e1_ref[...], c0_ref[...], c1_ref[...] = _route(logits)
        acc_ref[...] = jnp.zeros(acc_ref.shape, F32)

    gu = jnp.dot(h_ref[...], w1_ref[0].astype(BF16), preferred_element_type=F32)
    gp, up = gu[:, :D_EXPERT], gu[:, D_EXPERT:]
    act = ((gp * jax.nn.sigmoid(gp)) * up).astype(BF16)
    y = jnp.dot(act, w2_ref[0].astype(BF16), preferred_element_type=F32)
    wt = jnp.where(e0_ref[...] == e, c0_ref[...], 0.0) + jnp.where(e1_ref[...] == e, c1_ref[...], 0.0)
    acc_ref[...] += y * wt

    @pl.when(e == N_EXPERTS - 1)
    def _():
        out = x_ref[0] + gt_ref[0] * acc_ref[...]
        if final_norm:
            out = (out * lax.rsqrt(jnp.mean(out * out, axis=-1, keepdims=True) + EPS)) * gf_ref[...]
        o_ref[0] = out


def _moe(x, g, shift, scale, gate, wr, br, w1, w2, layer, gf, final_norm):
    b, t, d = x.shape
    tm = min(MOE_ROW_TILE, t)
    row = pl.BlockSpec((1, tm, d), lambda bi, i, e: (bi, i, 0))
    vec = lambda width: pl.BlockSpec((1, width), lambda bi, i, e: (0, 0))
    col = lambda dt: pltpu.VMEM((tm, 1), dt)
    return pl.pallas_call(
        functools.partial(_moe_kernel, final_norm=final_norm),
        grid=(b, t // tm, N_EXPERTS),
        in_specs=[row, vec(d), _mod_spec(shift, tm), _mod_spec(scale, tm), _mod_spec(gate, tm),
                  pl.BlockSpec(wr.shape, lambda bi, i, e: (0, 0)), vec(ROUTER_PAD),
                  pl.BlockSpec((1, d, 2 * D_EXPERT), lambda bi, i, e: (layer * N_EXPERTS + e, 0, 0)),
                  pl.BlockSpec((1, D_EXPERT, d), lambda bi, i, e: (layer * N_EXPERTS + e, 0, 0)),
                  vec(d)],
        out_specs=row,
        out_shape=jax.ShapeDtypeStruct((b, t, d), F32),
        scratch_shapes=[pltpu.VMEM((tm, d), BF16), pltpu.VMEM((tm, d), F32),
                        col(I32), col(I32), col(F32), col(F32)],
        compiler_params=_cparams(3),
        name="moe",
    )(x, g, shift, scale, gate, wr, br, w1, w2, gf)


def kernel(x_prompt, x_sample, cache_k, cache_v, cache_kidx, state_h, state_conv, page_table,
           c_prompt, c_sample, ada_w, ada_b, norm_mix_g, norm_ffn_g, norm_final_g,
           attn_w_in, attn_w_out, rec_w_in, rec_conv_w, rec_conv_b, rec_w_a, rec_b_a,
           rec_w_x, rec_b_x, rec_lambda, rec_w_out, moe_w_group, moe_b_group,
           moe_w_expert, moe_b_expert, moe_w1, moe_w2):
    b, t, d = x_prompt.shape
    db = x_sample.shape[0]
    n_pages = page_table.shape[1]
    past = n_pages * PAGE_SIZE
    n_pool = cache_k.shape[1]

    n_cond = b + db
    cond_rows = -(-n_cond // SUBLANES) * SUBLANES
    c_all = jnp.concatenate([c_prompt, c_sample, jnp.zeros((cond_rows - n_cond, d), F32)], axis=0)
    mod = _adaln(c_all, ada_w, ada_b)

    tables_p = _rope_tables(jnp.arange(t, dtype=I32))
    tables_s = _rope_tables(jnp.full((db,), past, I32))
    cache_kT = cache_k.transpose(0, 1, 3, 4, 2).reshape(-1, KV_WIDTH, PAGE_SIZE)
    cache_vT = cache_v.transpose(0, 1, 3, 4, 2).reshape(-1, KV_WIDTH, PAGE_SIZE)
    cache_kiT = cache_kidx.transpose(0, 1, 3, 2).reshape(-1, IDX_DIM, PAGE_SIZE)
    moe_w1f = moe_w1.reshape(-1, d, 2 * D_EXPERT)
    moe_w2f = moe_w2.reshape(-1, D_EXPERT, d)
    row_vec = lambda v: v.reshape(1, -1)
    head_grp = jnp.arange(N_HEADS)[:, None] // Q_PER_KV == jnp.arange(KV_WIDTH)[None, :] // HEAD_DIM

    xp = x_prompt
    xs = x_sample.reshape(1, db, d)
    outs = {name: [] for name in ("k_p", "v_p", "ki_p", "h_p", "cv_p", "k_s", "v_s", "ki_s", "h_s", "cv_s")}
    for l in range(DEPTH):
        parts = [mod[l, :, i * d:(i + 1) * d] for i in range(6)]
        sh1p, sc1p, g1p, sh2p, sc2p, g2p = [m[:b].reshape(b, 1, d) for m in parts]
        sh1s, sc1s, g1s, sh2s, sc2s, g2s = [m[b:n_cond].reshape(1, db, d) for m in parts]
        g_mix = row_vec(norm_mix_g[l])
        if l % 2 == 0:
            a = l // 2
            w_in = jnp.pad(attn_w_in[a], ((0, 0), (0, ATTN_PROJ_PAD - ATTN_PROJ))).astype(BF16)
            w_out = attn_w_out[a].astype(BF16)
            kT, vT, kiT, qT, qiT, wiT, kb, vTb, kib = _attn_proj_prompt(xp, g_mix, sh1p, sc1p, w_in, tables_p)
            op = _dsa_prompt(qT, qiT, wiT, kb, vTb, kib)
            outs["k_p"].append(kT.reshape(b, N_KV_HEADS, HEAD_DIM, t).transpose(0, 3, 1, 2))
            outs["v_p"].append(vT.reshape(b, N_KV_HEADS, HEAD_DIM, t).transpose(0, 3, 1, 2))
            outs["ki_p"].append(kiT.transpose(0, 2, 1))
            xp = _out_proj(op, w_out, xp, g1p)

            q, k, v, qi, ki, wi = _attn_proj_sample(xs, g_mix, sh1s, sc1s, w_in, tables_s)
            bias = _dsa_sample_select(page_table, qi.reshape(db, IDX_HEADS, IDX_DIM).astype(BF16),
                                      wi.reshape(db, IDX_HEADS, 1), cache_kiT,
                                      ki.reshape(db, 1, IDX_DIM), a, n_pool)
            q_bd = jnp.where(head_grp[None], jnp.tile(q.reshape(db, N_HEADS, HEAD_DIM), (1, 1, N_KV_HEADS)),
                             0.0).astype(BF16)
            os_ = _dsa_sample_attend(page_table, q_bd, bias, cache_kT, cache_vT,
                                     k.reshape(db, 1, KV_WIDTH), v.reshape(db, 1, KV_WIDTH), a, n_pool)
            outs["k_s"].append(k.reshape(db, 1, N_KV_HEADS, HEAD_DIM))
            outs["v_s"].append(v.reshape(db, 1, N_KV_HEADS, HEAD_DIM))
            outs["ki_s"].append(ki.reshape(db, 1, IDX_DIM))
            xs = _out_proj(os_.reshape(1, db, Q_WIDTH).astype(BF16), w_out, xs, g1s)
        else:
            r = l // 2
            w_in = rec_w_in[r].astype(BF16)
            w_out = rec_w_out[r].astype(BF16)
            rec = (rec_conv_w[r], row_vec(rec_conv_b[r]), rec_w_a[r].astype(BF16), row_vec(rec_b_a[r]),
                   rec_w_x[r].astype(BF16), row_vec(rec_b_x[r]), row_vec(rec_lambda[r]))
            gate, xb = _rec_proj(xp, g_mix, sh1p, sc1p, w_in)
            y, h_t, buf = _rglru_prompt(gate, xb, *rec)
            outs["h_p"].append(h_t.reshape(b, D_RNN))
            outs["cv_p"].append(buf)
            xp = _out_proj(y, w_out, xp, g1p)

            gate, xb = _rec_proj(xs, g_mix, sh1s, sc1s, w_in)
            y, h_t, buf = _rglru_sample(gate[0], xb[0], state_conv[r].swapaxes(0, 1), state_h[r], *rec)
            outs["h_s"].append(h_t)
            outs["cv_s"].append(buf.swapaxes(0, 1))
            xs = _out_proj(y.reshape(1, db, D_RNN), w_out, xs, g1s)

        wr = jnp.concatenate([moe_w_group[l], moe_w_expert[l],
                              jnp.zeros((d, ROUTER_PAD - N_GROUPS - N_EXPERTS), F32)], axis=1).astype(BF16)
        br = jnp.concatenate([moe_b_group[l], moe_b_expert[l],
                              jnp.zeros((ROUTER_PAD - N_GROUPS - N_EXPERTS,), F32)]).reshape(1, ROUTER_PAD)
        last = l == DEPTH - 1
        moe_args = (wr, br, moe_w1f, moe_w2f, l, row_vec(norm_final_g), last)
        xp = _moe(xp, row_vec(norm_ffn_g[l]), sh2p, sc2p, g2p, *moe_args)
        xs = _moe(xs, row_vec(norm_ffn_g[l]), sh2s, sc2s, g2s, *moe_args)

    st = lambda name: jnp.stack(outs[name])
    return (xp, xs.reshape(db, 1, d), st("k_p"), st("v_p"), st("ki_p"), st("h_p"), st("cv_p"),
            st("k_s"), st("v_s"), st("ki_s"), st("h_s"), st("cv_s"))
```

```python
import functools
import math

import jax
import jax.numpy as jnp
from jax import lax
from jax.experimental import pallas as pl
from jax.experimental.pallas import tpu as pltpu

F32 = jnp.float32
BF16 = jnp.bfloat16
I32 = jnp.int32

D_MODEL = 1024
DEPTH = 4
PAGE_SIZE = 128
N_HEADS = 16
HEAD_DIM = 64
N_KV_HEADS = 4
Q_PER_KV = N_HEADS // N_KV_HEADS
IDX_HEADS = 8
IDX_DIM = 64
IDX_SCALE = (IDX_HEADS * IDX_DIM) ** -0.5
TOPK_MAX = 256
ROPE_THETA = 500000.0
ROPE_FRACTION = 4
Q_WIDTH = N_HEADS * HEAD_DIM
KV_WIDTH = N_KV_HEADS * HEAD_DIM
QI_WIDTH = IDX_HEADS * IDX_DIM
ATTN_PROJ = Q_WIDTH + 2 * KV_WIDTH + QI_WIDTH + IDX_DIM + IDX_HEADS
D_RNN = D_MODEL
RG_BLOCKS = 8
RG_BLOCK_W = D_RNN // RG_BLOCKS
CONV_W = 4
RG_C = 8.0
N_GROUPS = 4
EXPERTS_PER_GROUP = 8
N_EXPERTS = N_GROUPS * EXPERTS_PER_GROUP
D_EXPERT = 256
EPS = 1e-6

LANES = 128
SUBLANES = 8
VMEM_LIMIT_BYTES = 56 * 1024 * 1024

ATTN_PROJ_PAD = -(-ATTN_PROJ // LANES) * LANES
ROUTER_PAD = LANES
Q_TILE = LANES
KEY_CHUNK = 512
ROW_TILE = 512
MOE_ROW_TILE = 1024
SCAN_CHUNK = 512
SELECT_PAGES_PER_STEP = 16
ATTEND_PAGES_PER_STEP = 8
BF16_SUBLANES = 16
HEAD_AUG = HEAD_DIM + BF16_SUBLANES
BOUND_SLACK = 1.02
DENOM_FLOOR = 2.0 ** -60
DENOM_CEIL = 2.0 ** 60
INT_MIN = -(2 ** 31)
INT_MAX = 2 ** 31 - 1
NEG = -1e30


def _cparams(n_axes):
    return pltpu.CompilerParams(dimension_semantics=("arbitrary",) * n_axes,
                                vmem_limit_bytes=VMEM_LIMIT_BYTES)


def _modulated_norm(x, g, shift, scale):
    xn = x * lax.rsqrt(jnp.mean(x * x, axis=-1, keepdims=True) + EPS)
    return (xn * g) * (1.0 + scale) + shift


def _rope_group(xs, cos, sa, sb):
    return xs * cos + pltpu.roll(xs, LANES - 8, 1) * sa + pltpu.roll(xs, 8, 1) * sb


def _float_order_key(s):
    bits = lax.bitcast_convert_type(s, I32)
    return bits ^ ((bits >> 31) & INT_MAX)


def _mod_spec(arr, tm):
    if arr.shape[1] == 1:
        return pl.BlockSpec((1, 1, arr.shape[2]), lambda b, i, *_: (b, 0, 0))
    return pl.BlockSpec((1, tm, arr.shape[2]), lambda b, i, *_: (b, i, 0))


def _adaln_kernel(c_ref, w_ref, b_ref, o_ref):
    c = c_ref[...]
    s = (c * jax.nn.sigmoid(c)).astype(BF16)
    o_ref[0] = jnp.dot(s, w_ref[0].astype(BF16), preferred_element_type=F32) + b_ref[0]


def _adaln(c_all, ada_w, ada_b):
    depth, d, n = ada_w.shape
    rows = c_all.shape[0]
    tn = 1536
    return pl.pallas_call(
        _adaln_kernel,
        grid=(depth, n // tn),
        in_specs=[pl.BlockSpec((rows, d), lambda l, j: (0, 0)),
                  pl.BlockSpec((1, d, tn), lambda l, j: (l, 0, j)),
                  pl.BlockSpec((1, 1, tn), lambda l, j: (l, 0, j))],
        out_specs=pl.BlockSpec((1, rows, tn), lambda l, j: (l, 0, j)),
        out_shape=jax.ShapeDtypeStruct((depth, rows, n), F32),
        compiler_params=_cparams(2),
        name="adaln",
    )(c_all, ada_w, ada_b.reshape(depth, 1, n))


def _rope_tables(pos):
    rot = HEAD_DIM // ROPE_FRACTION
    half = rot // 2
    inv = jnp.exp(-math.log(ROPE_THETA) * jnp.arange(half, dtype=F32) * (2.0 / rot))
    ang = pos.astype(F32)[:, None] * inv[None, :]
    cos, sin = jnp.cos(ang), jnp.sin(ang)
    r = pos.shape[0]
    z = lambda w: jnp.zeros((r, w), F32)
    cos64 = jnp.concatenate([cos, cos, jnp.ones((r, HEAD_DIM - rot), F32)], axis=1)
    sa64 = jnp.concatenate([-sin, z(HEAD_DIM - half)], axis=1)
    sb64 = jnp.concatenate([z(half), sin, z(HEAD_DIM - rot)], axis=1)
    two = lambda t: jnp.concatenate([t, t], axis=1)
    return two(cos64), two(sa64), two(sb64)


def _attn_proj_prompt_kernel(x_ref, g_ref, sh_ref, sc_ref, w_ref, cos_ref, sa_ref, sb_ref,
                             kT_ref, vT_ref, kiT_ref, qT_ref, qiT_ref, wiT_ref, kb_ref, vTb_ref, kib_ref,
                             kmax_ref):
    h = _modulated_norm(x_ref[0], g_ref[...], sh_ref[0], sc_ref[0]).astype(BF16)
    acc = jnp.dot(h, w_ref[...], preferred_element_type=F32)
    cos, sa, sb = cos_ref[...], sa_ref[...], sb_ref[...]
    n_qb = acc.shape[0] // Q_TILE
    grp = lambda off, j: acc[:, off + j * LANES: off + (j + 1) * LANES]

    tm = acc.shape[0]
    one_hot_row = lambda shape, axis: jnp.where(lax.broadcasted_iota(I32, shape, axis) == 0, 1.0, 0.0).astype(BF16)

    @pl.when(pl.program_id(1) == 0)
    def _():
        kmax_ref[...] = jnp.zeros(kmax_ref.shape, F32)

    for j in range(KV_WIDTH // LANES):
        kj = _rope_group(grp(Q_WIDTH, j), cos, sa, sb)
        kjT = kj.T
        kT_ref[0, j * LANES:(j + 1) * LANES, :] = kjT
        sq = kjT * kjT
        for hh in range(2):
            n = 2 * j + hh
            norm2 = jnp.sum(sq[hh * HEAD_DIM:(hh + 1) * HEAD_DIM, :], axis=0, keepdims=True)
            kmax_ref[n] = jnp.maximum(kmax_ref[n], jnp.max(norm2, axis=1, keepdims=True))
            kb_ref[0, n, :, 0:HEAD_DIM] = kj[:, hh * HEAD_DIM:(hh + 1) * HEAD_DIM].astype(BF16)
            kb_ref[0, n, :, HEAD_DIM:HEAD_AUG] = one_hot_row((tm, BF16_SUBLANES), 1)
    for j in range(Q_WIDTH // LANES):
        qjT = (_rope_group(grp(0, j), cos, sa, sb) * (HEAD_DIM ** -0.5)).T
        sq = qjT * qjT
        qjTb = qjT.astype(BF16)
        for hh in range(2):
            n, g = divmod(2 * j + hh, Q_PER_KV)
            norm2 = jnp.sum(sq[hh * HEAD_DIM:(hh + 1) * HEAD_DIM, :], axis=0, keepdims=True)
            shift = -BOUND_SLACK * jnp.sqrt(norm2 * kmax_ref[n])
            shift = jnp.concatenate([shift, jnp.zeros((BF16_SUBLANES - 1, tm), F32)], axis=0).astype(BF16)
            for jb in range(n_qb):
                qT_ref[0, jb, n, 0:HEAD_DIM, g * Q_TILE:(g + 1) * Q_TILE] = (
                    qjTb[hh * HEAD_DIM:(hh + 1) * HEAD_DIM, jb * Q_TILE:(jb + 1) * Q_TILE])
                qT_ref[0, jb, n, HEAD_DIM:HEAD_AUG, g * Q_TILE:(g + 1) * Q_TILE] = (
                    shift[:, jb * Q_TILE:(jb + 1) * Q_TILE])
    for j in range(KV_WIDTH // LANES):
        vjT = grp(Q_WIDTH + KV_WIDTH, j).T
        vT_ref[0, j * LANES:(j + 1) * LANES, :] = vjT
        for hh in range(2):
            n = 2 * j + hh
            vTb_ref[0, 0, n * HEAD_AUG:n * HEAD_AUG + HEAD_DIM, :] = (
                vjT[hh * HEAD_DIM:(hh + 1) * HEAD_DIM, :].astype(BF16))
            vTb_ref[0, 0, n * HEAD_AUG + HEAD_DIM:(n + 1) * HEAD_AUG, :] = one_hot_row((BF16_SUBLANES, tm), 0)
    off_qi = Q_WIDTH + 2 * KV_WIDTH
    for j in range(QI_WIDTH // LANES):
        qijT = _rope_group(grp(off_qi, j), cos, sa, sb).T.astype(BF16)
        for hh in range(2):
            head = 2 * j + hh
            for jb in range(n_qb):
                qiT_ref[0, jb, :, head * Q_TILE:(head + 1) * Q_TILE] = (
                    qijT[hh * IDX_DIM:(hh + 1) * IDX_DIM, jb * Q_TILE:(jb + 1) * Q_TILE])
    last = grp(off_qi + QI_WIDTH, 0)
    kir = _rope_group(last, cos, sa, sb)
    kiT_ref[0] = kir.T[:IDX_DIM, :]
    kib_ref[0] = kir[:, :IDX_DIM].astype(BF16)
    lastT = last.T
    for jb in range(n_qb):
        wiT_ref[0, jb] = lastT[IDX_DIM:IDX_DIM + IDX_HEADS, jb * Q_TILE:(jb + 1) * Q_TILE]


def _attn_proj_prompt(x, g, shift, scale, w_pad, tables):
    b, t, d = x.shape
    tm = ROW_TILE
    nq = t // Q_TILE
    grid = (b, t // tm)
    row = lambda width: pl.BlockSpec((1, tm, width), lambda bi, i: (bi, i, 0))
    col = lambda width: pl.BlockSpec((1, width, tm), lambda bi, i: (bi, 0, i))
    tab = pl.BlockSpec((tm, LANES), lambda bi, i: (i, 0))
    out_shape = (
        jax.ShapeDtypeStruct((b, KV_WIDTH, t), F32),
        jax.ShapeDtypeStruct((b, KV_WIDTH, t), F32),
        jax.ShapeDtypeStruct((b, IDX_DIM, t), F32),
        jax.ShapeDtypeStruct((b, nq, N_KV_HEADS, HEAD_AUG, Q_PER_KV * Q_TILE), BF16),
        jax.ShapeDtypeStruct((b, nq, IDX_DIM, IDX_HEADS * Q_TILE), BF16),
        jax.ShapeDtypeStruct((b, nq, IDX_HEADS, Q_TILE), F32),
        jax.ShapeDtypeStruct((b, N_KV_HEADS, t, HEAD_AUG), BF16),
        jax.ShapeDtypeStruct((b, t // tm, N_KV_HEADS * HEAD_AUG, tm), BF16),
        jax.ShapeDtypeStruct((b, t, IDX_DIM), BF16),
    )
    nqb = tm // Q_TILE
    out_specs = (
        col(KV_WIDTH), col(KV_WIDTH), col(IDX_DIM),
        pl.BlockSpec((1, nqb, N_KV_HEADS, HEAD_AUG, Q_PER_KV * Q_TILE), lambda bi, i: (bi, i, 0, 0, 0)),
        pl.BlockSpec((1, nqb, IDX_DIM, IDX_HEADS * Q_TILE), lambda bi, i: (bi, i, 0, 0)),
        pl.BlockSpec((1, nqb, IDX_HEADS, Q_TILE), lambda bi, i: (bi, i, 0, 0)),
        pl.BlockSpec((1, N_KV_HEADS, tm, HEAD_AUG), lambda bi, i: (bi, 0, i, 0)),
        pl.BlockSpec((1, 1, N_KV_HEADS * HEAD_AUG, tm), lambda bi, i: (bi, i, 0, 0)),
        row(IDX_DIM),
    )
    return pl.pallas_call(
        _attn_proj_prompt_kernel,
        grid=grid,
        in_specs=[row(d), pl.BlockSpec((1, d), lambda bi, i: (0, 0)),
                  _mod_spec(shift, tm), _mod_spec(scale, tm),
                  pl.BlockSpec(w_pad.shape, lambda bi, i: (0, 0)), tab, tab, tab],
        out_specs=out_specs,
        out_shape=out_shape,
        scratch_shapes=[pltpu.VMEM((N_KV_HEADS, 1, 1), F32)],
        compiler_params=_cparams(2),
        name="attn_proj_prompt",
    )(x, g, shift, scale, w_pad, *tables)


def _attn_proj_sample_kernel(x_ref, g_ref, sh_ref, sc_ref, w_ref, cos_ref, sa_ref, sb_ref,
                             q_ref, k_ref, v_ref, qi_ref, ki_ref, wi_ref):
    h = _modulated_norm(x_ref[0], g_ref[...], sh_ref[0], sc_ref[0]).astype(BF16)
    acc = jnp.dot(h, w_ref[...], preferred_element_type=F32)
    cos, sa, sb = cos_ref[...], sa_ref[...], sb_ref[...]
    grp = lambda off, j: acc[:, off + j * LANES: off + (j + 1) * LANES]
    for j in range(Q_WIDTH // LANES):
        q_ref[:, j * LANES:(j + 1) * LANES] = _rope_group(grp(0, j), cos, sa, sb) * (HEAD_DIM ** -0.5)
    for j in range(KV_WIDTH // LANES):
        k_ref[:, j * LANES:(j + 1) * LANES] = _rope_group(grp(Q_WIDTH, j), cos, sa, sb)
        v_ref[:, j * LANES:(j + 1) * LANES] = grp(Q_WIDTH + KV_WIDTH, j)
    off_qi = Q_WIDTH + 2 * KV_WIDTH
    for j in range(QI_WIDTH // LANES):
        qi_ref[:, j * LANES:(j + 1) * LANES] = _rope_group(grp(off_qi, j), cos, sa, sb)
    last = grp(off_qi + QI_WIDTH, 0)
    ki_ref[...] = _rope_group(last, cos, sa, sb)[:, :IDX_DIM]
    wi_ref[...] = last[:, IDX_DIM:IDX_DIM + IDX_HEADS]


def _attn_proj_sample(x, g, shift, scale, w_pad, tables):
    _, rows, d = x.shape
    full = lambda shape: pl.BlockSpec(shape, lambda i: (0,) * len(shape))
    widths = (Q_WIDTH, KV_WIDTH, KV_WIDTH, QI_WIDTH, IDX_DIM, IDX_HEADS)
    return pl.pallas_call(
        _attn_proj_sample_kernel,
        grid=(1,),
        in_specs=[full((1, rows, d)), full((1, d)), full((1, rows, d)), full((1, rows, d)),
                  full(w_pad.shape), full((rows, LANES)), full((rows, LANES)), full((rows, LANES))],
        out_specs=tuple(full((rows, w)) for w in widths),
        out_shape=tuple(jax.ShapeDtypeStruct((rows, w), F32) for w in widths),
        compiler_params=_cparams(1),
        name="attn_proj_sample",
    )(x, g, shift, scale, w_pad, *tables)


def _dsa_prompt_kernel(qT_ref, qiT_ref, wiT_ref, kb_ref, vT_ref, kib_ref, o_ref,
                       keys_ref, acc_ref, acc2_ref, m_ref, l_ref, j_ref, *, topk, idx_bits):
    kc = KEY_CHUNK
    qb = pl.program_id(1)
    n_chunks = (qb * Q_TILE + Q_TILE + kc - 1) // kc
    rows = lax.broadcasted_iota(I32, (kc, Q_TILE), 0)
    q_pos = qb * Q_TILE + lax.broadcasted_iota(I32, (kc, Q_TILE), 1)

    w_heads = wiT_ref[0, 0] * IDX_SCALE
    qiT = qiT_ref[0, 0]

    def score_chunk(c, carry):
        off = pl.multiple_of(c * kc, kc)
        dots = jnp.dot(kib_ref[0, pl.ds(off, kc), :], qiT, preferred_element_type=F32)
        s = jnp.zeros((kc, Q_TILE), F32)
        for h in range(IDX_HEADS):
            s = s + jnp.maximum(dots[:, h * Q_TILE:(h + 1) * Q_TILE], 0.0) * w_heads[h:h + 1, :]
        keys_ref[pl.ds(off, kc), :] = jnp.where(rows + off <= q_pos, _float_order_key(s), INT_MIN)
        return carry

    lax.fori_loop(0, n_chunks, score_chunk, 0)

    def count(pred):
        def body(c, cnt):
            off = pl.multiple_of(c * kc, kc)
            hit = jnp.where(pred(keys_ref[pl.ds(off, kc), :], rows + off), 1, 0)
            return cnt + hit.reshape(kc // SUBLANES, SUBLANES, Q_TILE).sum(axis=0)
        cnt = lax.fori_loop(0, n_chunks, body, jnp.zeros((SUBLANES, Q_TILE), I32))
        return cnt.sum(axis=0, keepdims=True)

    def thr_bit(i, state):
        thr, n_ge = state
        cand = thr + lax.shift_left(jnp.int32(1), 31 - i)
        n_cand = count(lambda k, s: k >= cand)
        ok = n_cand >= topk
        return jnp.where(ok, cand, thr), jnp.where(ok, n_cand, n_ge)

    n_all = jnp.full((1, Q_TILE), 1, I32) * (n_chunks * kc)
    thr, n_ge = lax.fori_loop(0, 32, thr_bit, (jnp.full((1, Q_TILE), INT_MIN, I32), n_all))

    j_ref[...] = jnp.full((1, Q_TILE), INT_MAX, I32)
    tie_split = jnp.max(jnp.where((n_ge > topk) & (thr > INT_MIN), 1, 0))

    @pl.when(tie_split > 0)
    def _():
        need = topk - count(lambda k, s: k > thr)

        def idx_bit(i, j0):
            cand = j0 + lax.shift_left(jnp.int32(1), idx_bits - 1 - i)
            taken_before = count(lambda k, s: (k == thr) & (s < cand))
            return jnp.where(taken_before < need, cand, j0)
        j_ref[...] = lax.fori_loop(0, idx_bits, idx_bit, jnp.zeros((1, Q_TILE), I32))

    j_sel = j_ref[...]

    def selection_bias(c):
        off = pl.multiple_of(c * kc, kc)
        k = keys_ref[pl.ds(off, kc), :]
        s_idx = rows + off
        sel = ((k > thr) | ((k == thr) & (s_idx <= j_sel))) & (s_idx <= q_pos)
        bias = jnp.where(sel, 0.0, NEG)
        return off, jnp.concatenate([bias] * Q_PER_KV, axis=1)

    acc_ref[...] = jnp.zeros(acc_ref.shape, F32)

    def attend_chunk(c, carry):
        off, bias = selection_bias(c)
        for n in range(N_KV_HEADS):
            s = jnp.dot(kb_ref[0, n, pl.ds(off, kc), :], qT_ref[0, 0, n], preferred_element_type=F32) + bias
            acc_ref[n] += jnp.dot(vT_ref[0, c, n * HEAD_AUG:(n + 1) * HEAD_AUG, :], jnp.exp(s).astype(BF16),
                                  preferred_element_type=F32)
        return carry

    lax.fori_loop(0, n_chunks, attend_chunk, 0)
    denom = jnp.concatenate([acc_ref[n][HEAD_DIM:HEAD_DIM + 1, :] for n in range(N_KV_HEADS)], axis=0)
    healthy = (jnp.min(denom) >= DENOM_FLOOR) & (jnp.max(denom) <= DENOM_CEIL)

    @pl.when(jnp.logical_not(healthy))
    def _():
        m_ref[...] = jnp.full(m_ref.shape, NEG, F32)
        l_ref[...] = jnp.zeros(l_ref.shape, F32)
        acc2_ref[...] = jnp.zeros(acc2_ref.shape, F32)

        def attend_chunk_online(c, carry):
            off, bias = selection_bias(c)
            for n in range(N_KV_HEADS):
                s = jnp.dot(kb_ref[0, n, pl.ds(off, kc), 0:HEAD_DIM], qT_ref[0, 0, n, 0:HEAD_DIM, :],
                            preferred_element_type=F32) + bias
                m_prev = m_ref[n]
                m_new = jnp.maximum(m_prev, s.max(axis=0, keepdims=True))
                alpha = jnp.exp(m_prev - m_new)
                p = jnp.exp(s - m_new)
                l_ref[n] = alpha * l_ref[n] + p.sum(axis=0, keepdims=True)
                pv = jnp.dot(vT_ref[0, c, n * HEAD_AUG:n * HEAD_AUG + HEAD_DIM, :], p.astype(BF16),
                             preferred_element_type=F32)
                acc2_ref[n] = alpha * acc2_ref[n] + pv
                m_ref[n] = m_new
            return carry

        lax.fori_loop(0, n_chunks, attend_chunk_online, 0)
        for n in range(N_KV_HEADS):
            acc_ref[n, 0:HEAD_DIM, :] = acc2_ref[n]
            acc_ref[n, HEAD_DIM:HEAD_DIM + 1, :] = l_ref[n]

    for n in range(N_KV_HEADS):
        on = acc_ref[n, 0:HEAD_DIM, :] / acc_ref[n, HEAD_DIM:HEAD_DIM + 1, :]
        for gp in range(Q_PER_KV // 2):
            pair = jnp.concatenate([on[:, (2 * gp) * Q_TILE:(2 * gp + 1) * Q_TILE],
                                    on[:, (2 * gp + 1) * Q_TILE:(2 * gp + 2) * Q_TILE]], axis=0)
            col = (n * Q_PER_KV // 2 + gp) * LANES
            o_ref[0, :, col:col + LANES] = pair.T.astype(BF16)


def _dsa_prompt(qT, qiT, wiT, kb, vT, kib):
    b, nq = qT.shape[:2]
    t = kb.shape[2]
    topk = min(TOPK_MAX, t // 4)
    idx_bits = max(1, (t - 1).bit_length())
    per_q = lambda shape: pl.BlockSpec((1, 1) + shape, lambda bi, qi: (bi, qi) + (0,) * len(shape))
    per_b = lambda shape: pl.BlockSpec((1,) + shape, lambda bi, qi: (bi,) + (0,) * len(shape))
    return pl.pallas_call(
        functools.partial(_dsa_prompt_kernel, topk=topk, idx_bits=idx_bits),
        grid=(b, nq),
        in_specs=[per_q(qT.shape[2:]), per_q(qiT.shape[2:]), per_q(wiT.shape[2:]),
                  per_b(kb.shape[1:]), per_b(vT.shape[1:]), per_b(kib.shape[1:])],
        out_specs=pl.BlockSpec((1, Q_TILE, Q_WIDTH), lambda bi, qi: (bi, qi, 0)),
        out_shape=jax.ShapeDtypeStruct((b, t, Q_WIDTH), BF16),
        scratch_shapes=[pltpu.VMEM((t, Q_TILE), I32),
                        pltpu.VMEM((N_KV_HEADS, HEAD_AUG, Q_PER_KV * Q_TILE), F32),
                        pltpu.VMEM((N_KV_HEADS, HEAD_DIM, Q_PER_KV * Q_TILE), F32),
                        pltpu.VMEM((N_KV_HEADS, 1, Q_PER_KV * Q_TILE), F32),
                        pltpu.VMEM((N_KV_HEADS, 1, Q_PER_KV * Q_TILE), F32),
                        pltpu.VMEM((1, Q_TILE), I32)],
        compiler_params=_cparams(2),
        name="dsa_prompt",
    )(qT, qiT, wiT, kb, vT, kib)


def _dsa_sample_select_kernel(pt_ref, qi_ref, w_ref, *refs, n_pages, pg, topk, idx_bits):
    kiT_refs, (kin_ref, bias_ref, keys_ref) = refs[:pg], refs[pg:]
    g = pl.program_id(1)
    rows_pad = keys_ref.shape[0]
    qi = qi_ref[0]
    w = w_ref[0] * IDX_SCALE
    kiT = jnp.concatenate([r[0].astype(BF16) for r in kiT_refs], axis=1)
    dots = jnp.dot(qi, kiT, preferred_element_type=F32)
    page_keys = _float_order_key(jnp.sum(jnp.maximum(dots, 0.0) * w, axis=0, keepdims=True))
    for j in range(pg):
        keys_ref[pl.ds(g * pg + j, 1), :] = page_keys[:, j * PAGE_SIZE:(j + 1) * PAGE_SIZE]

    @pl.when(g == n_pages // pg - 1)
    def _():
        lane1 = lax.broadcasted_iota(I32, (1, LANES), 1)
        kn = kin_ref[0].astype(BF16).astype(F32)
        dn = jnp.sum(qi.astype(F32) * kn, axis=1, keepdims=True)
        sn = jnp.sum(jnp.maximum(dn, 0.0) * w, axis=0, keepdims=True)
        keys_ref[n_pages:n_pages + 1, :] = jnp.where(lane1 == 0, _float_order_key(sn), INT_MIN)
        keys_ref[n_pages + 1:, :] = jnp.full((rows_pad - n_pages - 1, LANES), INT_MIN, I32)

        keys = keys_ref[...]
        row = lax.broadcasted_iota(I32, keys.shape, 0)
        lane = lax.broadcasted_iota(I32, keys.shape, 1)
        idx = row * LANES + lane
        real = (row < n_pages) | ((row == n_pages) & (lane == 0))
        count = lambda hit: jnp.sum(jnp.where(hit, 1, 0), keepdims=True)

        def thr_bit(i, thr):
            cand = thr + lax.shift_left(jnp.int32(1), 31 - i)
            return jnp.where(count(keys >= cand) >= topk, cand, thr)

        thr = lax.fori_loop(0, 32, thr_bit, jnp.full((1, 1), INT_MIN, I32))
        need = topk - count(keys > thr)

        def idx_bit(i, j0):
            cand = j0 + lax.shift_left(jnp.int32(1), idx_bits - 1 - i)
            return jnp.where(count((keys == thr) & (idx < cand)) < need, cand, j0)

        j_sel = lax.fori_loop(0, idx_bits, idx_bit, jnp.zeros((1, 1), I32))
        sel = ((keys > thr) | ((keys == thr) & (idx <= j_sel))) & real
        bias_ref[0] = jnp.where(sel, 0.0, NEG)


def _page_specs(block, layer, n_pool, pg):
    def spec(j):
        return pl.BlockSpec(block, lambda b, g, pt: (layer * n_pool + pt[b, g * pg + j], 0, 0))
    return [spec(j) for j in range(pg)]


def _dsa_sample_select(page_table, qi3, w3, cache_kiT, ki_new, layer, n_pool):
    db, n_pages = page_table.shape
    pg = min(SELECT_PAGES_PER_STEP, n_pages)
    rows_pad = -(-(n_pages + 1) // SUBLANES) * SUBLANES
    n_keys = n_pages * PAGE_SIZE + 1
    topk = min(TOPK_MAX, n_keys // 4)
    idx_bits = max(1, (n_keys - 1).bit_length())
    grid_spec = pltpu.PrefetchScalarGridSpec(
        num_scalar_prefetch=1,
        grid=(db, n_pages // pg),
        in_specs=[pl.BlockSpec((1, IDX_HEADS, IDX_DIM), lambda b, g, pt: (b, 0, 0)),
                  pl.BlockSpec((1, IDX_HEADS, 1), lambda b, g, pt: (b, 0, 0))]
                 + _page_specs((1, IDX_DIM, PAGE_SIZE), layer, n_pool, pg)
                 + [pl.BlockSpec((1, 1, IDX_DIM), lambda b, g, pt: (b, 0, 0))],
        out_specs=pl.BlockSpec((1, rows_pad, LANES), lambda b, g, pt: (b, 0, 0)),
        scratch_shapes=[pltpu.VMEM((rows_pad, LANES), I32)],
    )
    return pl.pallas_call(
        functools.partial(_dsa_sample_select_kernel, n_pages=n_pages, pg=pg, topk=topk, idx_bits=idx_bits),
        grid_spec=grid_spec,
        out_shape=jax.ShapeDtypeStruct((db, rows_pad, LANES), F32),
        compiler_params=_cparams(2),
        name="dsa_sample_select",
    )(page_table, qi3, w3, *([cache_kiT] * pg), ki_new)


def _dsa_sample_attend_kernel(pt_ref, q_ref, bias_ref, *refs, n_pages, pg):
    kT_refs, vT_refs = refs[:pg], refs[pg:2 * pg]
    kn_ref, vn_ref, o_ref, acc_ref, m_ref, l_ref = refs[2 * pg:]
    g = pl.program_id(1)

    @pl.when(g == 0)
    def _():
        m_ref[...] = jnp.full(m_ref.shape, NEG, F32)
        l_ref[...] = jnp.zeros(l_ref.shape, F32)
        acc_ref[...] = jnp.zeros(acc_ref.shape, F32)

    q = q_ref[0]
    kT = jnp.concatenate([r[0].astype(BF16) for r in kT_refs], axis=1)
    bias = jnp.concatenate([bias_ref[0, pl.ds(g * pg + j, 1), :] for j in range(pg)], axis=1)
    s = jnp.dot(q, kT, preferred_element_type=F32) + bias
    m_prev = m_ref[...]
    m_new = jnp.maximum(m_prev, s.max(axis=1, keepdims=True))
    alpha = jnp.exp(m_prev - m_new)
    pr = jnp.exp(s - m_new)
    l_ref[...] = alpha * l_ref[...] + pr.sum(axis=1, keepdims=True)
    vT = jnp.concatenate([r[0].astype(BF16) for r in vT_refs], axis=1)
    acc_ref[...] = alpha * acc_ref[...] + lax.dot_general(
        pr.astype(BF16), vT, (((1,), (1,)), ((), ())), preferred_element_type=F32)
    m_ref[...] = m_new

    @pl.when(g == n_pages // pg - 1)
    def _():
        kn = kn_ref[0].astype(BF16).astype(F32)
        vn = vn_ref[0].astype(BF16).astype(F32)
        sn = jnp.sum(q.astype(F32) * kn, axis=1, keepdims=True) + bias_ref[0, n_pages:n_pages + 1, 0:1]
        m_fin = jnp.maximum(m_new, sn)
        a_fin = jnp.exp(m_new - m_fin)
        pn = jnp.exp(sn - m_fin)
        l_fin = a_fin * l_ref[...] + pn
        acc = a_fin * acc_ref[...] + pn.astype(BF16).astype(F32) * vn
        out = acc / l_fin
        head_grp = lax.broadcasted_iota(I32, out.shape, 0) // Q_PER_KV
        lane_grp = lax.broadcasted_iota(I32, out.shape, 1) // HEAD_DIM
        out = jnp.where(head_grp == lane_grp, out, 0.0)
        o = out[:, 0:HEAD_DIM]
        for n in range(1, N_KV_HEADS):
            o = o + out[:, n * HEAD_DIM:(n + 1) * HEAD_DIM]
        o_ref[0] = o


def _dsa_sample_attend(page_table, q_bd, bias, cache_kT, cache_vT, k_new, v_new, layer, n_pool):
    db, n_pages = page_table.shape
    pg = min(ATTEND_PAGES_PER_STEP, n_pages)
    per_b = lambda shape: pl.BlockSpec((1,) + shape, lambda b, g, pt: (b,) + (0,) * len(shape))
    pages = _page_specs((1, KV_WIDTH, PAGE_SIZE), layer, n_pool, pg)
    grid_spec = pltpu.PrefetchScalarGridSpec(
        num_scalar_prefetch=1,
        grid=(db, n_pages // pg),
        in_specs=[per_b(q_bd.shape[1:]), per_b(bias.shape[1:])] + pages + pages
                 + [per_b((1, KV_WIDTH)), per_b((1, KV_WIDTH))],
        out_specs=per_b((N_HEADS, HEAD_DIM)),
        scratch_shapes=[pltpu.VMEM((N_HEADS, KV_WIDTH), F32),
                        pltpu.VMEM((N_HEADS, 1), F32),
                        pltpu.VMEM((N_HEADS, 1), F32)],
    )
    return pl.pallas_call(
        functools.partial(_dsa_sample_attend_kernel, n_pages=n_pages, pg=pg),
        grid_spec=grid_spec,
        out_shape=jax.ShapeDtypeStruct((db, N_HEADS, HEAD_DIM), F32),
        compiler_params=_cparams(2),
        name="dsa_sample_attend",
    )(page_table, q_bd, bias, *([cache_kT] * pg), *([cache_vT] * pg), k_new, v_new)


def _out_proj_kernel(a_ref, w_ref, x_ref, gt_ref, o_ref):
    y = jnp.dot(a_ref[0], w_ref[...], preferred_element_type=F32)
    o_ref[0] = x_ref[0] + gt_ref[0] * y


def _out_proj(a, w, x, gate):
    b, t, d = x.shape
    tm = min(ROW_TILE, t)
    row = lambda width: pl.BlockSpec((1, tm, width), lambda bi, i: (bi, i, 0))
    return pl.pallas_call(
        _out_proj_kernel,
        grid=(b, t // tm),
        in_specs=[row(a.shape[2]), pl.BlockSpec(w.shape, lambda bi, i: (0, 0)), row(d), _mod_spec(gate, tm)],
        out_specs=row(d),
        out_shape=jax.ShapeDtypeStruct((b, t, d), F32),
        compiler_params=_cparams(2),
        name="out_proj",
    )(a, w, x, gate)


def _gelu_tanh(x):
    return x * (0.5 * (1.0 + jnp.tanh(math.sqrt(2.0 / math.pi) * (x + 0.044715 * (x * x * x)))))


def _rec_proj_kernel(x_ref, g_ref, sh_ref, sc_ref, w_ref, gate_ref, xb_ref):
    h = _modulated_norm(x_ref[0], g_ref[...], sh_ref[0], sc_ref[0]).astype(BF16)
    acc = jnp.dot(h, w_ref[...], preferred_element_type=F32)
    gate_ref[0] = _gelu_tanh(acc[:, :D_RNN])
    xb_ref[0] = acc[:, D_RNN:]


def _rec_proj(x, g, shift, scale, w):
    b, t, d = x.shape
    tm = min(ROW_TILE, t)
    row = lambda width: pl.BlockSpec((1, tm, width), lambda bi, i: (bi, i, 0))
    return pl.pallas_call(
        _rec_proj_kernel,
        grid=(b, t // tm),
        in_specs=[row(d), pl.BlockSpec((1, d), lambda bi, i: (0, 0)), _mod_spec(shift, tm),
                  _mod_spec(scale, tm), pl.BlockSpec(w.shape, lambda bi, i: (0, 0))],
        out_specs=(row(D_RNN), row(D_RNN)),
        out_shape=(jax.ShapeDtypeStruct((b, t, D_RNN), F32),) * 2,
        compiler_params=_cparams(2),
        name="rec_proj",
    )(x, g, shift, scale, w)


def _rglru_gates(xc, wa_ref, ba, wx_ref, bx, lam):
    xcb = xc.astype(BF16)
    blk = lambda w_ref: jnp.concatenate(
        [jnp.dot(xcb[:, n * RG_BLOCK_W:(n + 1) * RG_BLOCK_W], w_ref[n], preferred_element_type=F32)
         for n in range(RG_BLOCKS)], axis=1)
    r = jax.nn.sigmoid(blk(wa_ref) + ba)
    i = jax.nn.sigmoid(blk(wx_ref) + bx)
    neg_lam = -lam
    softplus = jnp.maximum(neg_lam, 0.0) + jnp.log1p(jnp.exp(-jnp.abs(neg_lam)))
    log_a = -RG_C * r * softplus
    a = jnp.exp(log_a)
    u = jnp.sqrt(1.0 - a * a) * (i * xc)
    return a, u


def _rglru_prompt_kernel(gate_ref, xb_ref, cw_ref, cb_ref, wa_ref, ba_ref, wx_ref, bx_ref, lam_ref,
                         y_ref, h_ref, conv_ref, xcat_ref, a_ref, u_ref, hs_ref, hc_ref):
    tc = xb_ref.shape[1]
    c = pl.program_id(1)
    pad = SUBLANES

    @pl.when(c == 0)
    def _():
        xcat_ref[0:pad, :] = jnp.zeros((pad, D_RNN), F32)
        hc_ref[...] = jnp.zeros(hc_ref.shape, F32)

    @pl.when(c > 0)
    def _():
        xcat_ref[0:pad, :] = xcat_ref[tc:tc + pad, :]

    xcat_ref[pad:pad + tc, :] = xb_ref[0]
    cw = cw_ref[...]
    xc = xcat_ref[pad - 3:pad - 3 + tc, :] * cw[0:1, :]
    for j in range(1, CONV_W):
        xc = xc + xcat_ref[pad - 3 + j:pad - 3 + j + tc, :] * cw[j:j + 1, :]
    xc = cb_ref[...] + xc
    a, u = _rglru_gates(xc, wa_ref, ba_ref[...], wx_ref, bx_ref[...], lam_ref[...])
    a_ref[...] = a
    u_ref[...] = u

    def step(t, h):
        h = a_ref[pl.ds(t, 1), :] * h + u_ref[pl.ds(t, 1), :]
        hs_ref[pl.ds(t, 1), :] = h
        return h

    h = lax.fori_loop(0, tc, step, hc_ref[...], unroll=8)
    hc_ref[...] = h
    y_ref[0] = (hs_ref[...] * gate_ref[0]).astype(BF16)
    h_ref[0] = h
    conv_ref[0] = xcat_ref[pad + tc - (CONV_W - 1):pad + tc, :]


def _rglru_prompt(gate, xb, cw, cb, wa, ba, wx, bx, lam):
    b, t, d = xb.shape
    tc = min(SCAN_CHUNK, t)
    row = pl.BlockSpec((1, tc, d), lambda bi, i: (bi, i, 0))
    vec = pl.BlockSpec((1, d), lambda bi, i: (0, 0))
    full = lambda shape: pl.BlockSpec(shape, lambda bi, i: (0,) * len(shape))
    return pl.pallas_call(
        _rglru_prompt_kernel,
        grid=(b, t // tc),
        in_specs=[row, row, full(cw.shape), vec, full(wa.shape), vec, full(wx.shape), vec, vec],
        out_specs=(row, pl.BlockSpec((1, 1, d), lambda bi, i: (bi, 0, 0)),
                   pl.BlockSpec((1, CONV_W - 1, d), lambda bi, i: (bi, 0, 0))),
        out_shape=(jax.ShapeDtypeStruct((b, t, d), BF16), jax.ShapeDtypeStruct((b, 1, d), F32),
                   jax.ShapeDtypeStruct((b, CONV_W - 1, d), F32)),
        scratch_shapes=[pltpu.VMEM((tc + 2 * SUBLANES, d), F32), pltpu.VMEM((tc, d), F32),
                        pltpu.VMEM((tc, d), F32), pltpu.VMEM((tc, d), F32), pltpu.VMEM((1, d), F32)],
        compiler_params=_cparams(2),
        name="rglru_prompt",
    )(gate, xb, cw, cb, wa, ba, wx, bx, lam)


def _rglru_sample_kernel(gate_ref, xb_ref, buf_ref, h0_ref, cw_ref, cb_ref, wa_ref, ba_ref, wx_ref, bx_ref,
                         lam_ref, y_ref, h_ref, nbuf_ref):
    cw = cw_ref[...]
    xb = xb_ref[...]
    xc = buf_ref[0] * cw[0:1, :]
    for j in range(1, CONV_W - 1):
        xc = xc + buf_ref[j] * cw[j:j + 1, :]
    xc = cb_ref[...] + (xc + xb * cw[CONV_W - 1:CONV_W, :])
    a, u = _rglru_gates(xc, wa_ref, ba_ref[...], wx_ref, bx_ref[...], lam_ref[...])
    h = a * h0_ref[...] + u
    h_ref[...] = h
    y_ref[...] = (h * gate_ref[...]).astype(BF16)
    for j in range(CONV_W - 2):
        nbuf_ref[j] = buf_ref[j + 1]
    nbuf_ref[CONV_W - 2] = xb


def _rglru_sample(gate, xb, buf, h0, cw, cb, wa, ba, wx, bx, lam):
    rows, d = xb.shape
    full = lambda a: pl.BlockSpec(a.shape, lambda i: (0,) * a.ndim)
    args = (gate, xb, buf, h0, cw, cb, wa, ba, wx, bx, lam)
    return pl.pallas_call(
        _rglru_sample_kernel,
        grid=(1,),
        in_specs=[full(a) for a in args],
        out_specs=(full(xb), full(xb), full(buf)),
        out_shape=(jax.ShapeDtypeStruct((rows, d), BF16), jax.ShapeDtypeStruct((rows, d), F32),
                   jax.ShapeDtypeStruct(buf.shape, F32)),
        compiler_params=_cparams(1),
        name="rglru_sample",
    )(*args)


def _route(logits):
    lane = lax.broadcasted_iota(I32, logits.shape, 1)

    def first_max(v):
        m = jnp.max(v, axis=1, keepdims=True)
        return m, jnp.min(jnp.where(v == m, lane, LANES), axis=1, keepdims=True)

    is_grp = lane < N_GROUPS
    gm, grp = first_max(jnp.where(is_grp, logits, -jnp.inf))
    g_w = 1.0 / jnp.sum(jnp.where(is_grp, jnp.exp(logits - gm), 0.0), axis=1, keepdims=True)
    lo = N_GROUPS + EXPERTS_PER_GROUP * grp
    el = jnp.where((lane >= lo) & (lane < lo + EXPERTS_PER_GROUP), logits, -jnp.inf)
    m1, i1 = first_max(el)
    m2, i2 = first_max(jnp.where(lane == i1, -jnp.inf, el))
    e2 = jnp.exp(m2 - m1)
    den = 1.0 + e2
    return i1 - N_GROUPS, i2 - N_GROUPS, (1.0 / den) * g_w, (e2 / den) * g_w


def _moe_kernel(x_ref, g_ref, sh_ref, sc_ref, gt_ref, wr_ref, br_ref, w1_ref, w2_ref, gf_ref, o_ref,
                h_ref, acc_ref, e0_ref, e1_ref, c0_ref, c1_ref, *, final_norm):
    e = pl.program_id(2)

    @pl.when(e == 0)
    def _():
        h = _modulated_norm(x_ref[0], g_ref[...], sh_ref[0], sc_ref[0])
        h_ref[...] = h.astype(BF16)
        logits = jnp.dot(h_ref[...], wr_ref[...], preferred_element_type=F32) + br_ref[...]
        e0_ref[...], e1_ref[...], c0_ref[...], c1_ref[...] = _route(logits)
        acc_ref[...] = jnp.zeros(acc_ref.shape, F32)

    gu = jnp.dot(h_ref[...], w1_ref[0].astype(BF16), preferred_element_type=F32)
    gp, up = gu[:, :D_EXPERT], gu[:, D_EXPERT:]
    act = ((gp * jax.nn.sigmoid(gp)) * up).astype(BF16)
    y = jnp.dot(act, w2_ref[0].astype(BF16), preferred_element_type=F32)
    wt = jnp.where(e0_ref[...] == e, c0_ref[...], 0.0) + jnp.where(e1_ref[...] == e, c1_ref[...], 0.0)
    acc_ref[...] += y * wt

    @pl.when(e == N_EXPERTS - 1)
    def _():
        out = x_ref[0] + gt_ref[0] * acc_ref[...]
        if final_norm:
            out = (out * lax.rsqrt(jnp.mean(out * out, axis=-1, keepdims=True) + EPS)) * gf_ref[...]
        o_ref[0] = out


def _moe(x, g, shift, scale, gate, wr, br, w1, w2, layer, gf, final_norm):
    b, t, d = x.shape
    tm = min(MOE_ROW_TILE, t)
    row = pl.BlockSpec((1, tm, d), lambda bi, i, e: (bi, i, 0))
    vec = lambda width: pl.BlockSpec((1, width), lambda bi, i, e: (0, 0))
    col = lambda dt: pltpu.VMEM((tm, 1), dt)
    return pl.pallas_call(
        functools.partial(_moe_kernel, final_norm=final_norm),
        grid=(b, t // tm, N_EXPERTS),
        in_specs=[row, vec(d), _mod_spec(shift, tm), _mod_spec(scale, tm), _mod_spec(gate, tm),
                  pl.BlockSpec(wr.shape, lambda bi, i, e: (0, 0)), vec(ROUTER_PAD),
                  pl.BlockSpec((1, d, 2 * D_EXPERT), lambda bi, i, e: (layer * N_EXPERTS + e, 0, 0)),
                  pl.BlockSpec((1, D_EXPERT, d), lambda bi, i, e: (layer * N_EXPERTS + e, 0, 0)),
                  vec(d)],
        out_specs=row,
        out_shape=jax.ShapeDtypeStruct((b, t, d), F32),
        scratch_shapes=[pltpu.VMEM((tm, d), BF16), pltpu.VMEM((tm, d), F32),
                        col(I32), col(I32), col(F32), col(F32)],
        compiler_params=_cparams(3),
        name="moe",
    )(x, g, shift, scale, gate, wr, br, w1, w2, gf)


def kernel(x_prompt, x_sample, cache_k, cache_v, cache_kidx, state_h, state_conv, page_table,
           c_prompt, c_sample, ada_w, ada_b, norm_mix_g, norm_ffn_g, norm_final_g,
           attn_w_in, attn_w_out, rec_w_in, rec_conv_w, rec_conv_b, rec_w_a, rec_b_a,
           rec_w_x, rec_b_x, rec_lambda, rec_w_out, moe_w_group, moe_b_group,
           moe_w_expert, moe_b_expert, moe_w1, moe_w2):
    b, t, d = x_prompt.shape
    db = x_sample.shape[0]
    n_pages = page_table.shape[1]
    past = n_pages * PAGE_SIZE
    n_pool = cache_k.shape[1]

    n_cond = b + db
    cond_rows = -(-n_cond // SUBLANES) * SUBLANES
    c_all = jnp.concatenate([c_prompt, c_sample, jnp.zeros((cond_rows - n_cond, d), F32)], axis=0)
    mod = _adaln(c_all, ada_w, ada_b)

    tables_p = _rope_tables(jnp.arange(t, dtype=I32))
    tables_s = _rope_tables(jnp.full((db,), past, I32))
    cache_kT = cache_k.transpose(0, 1, 3, 4, 2).reshape(-1, KV_WIDTH, PAGE_SIZE)
    cache_vT = cache_v.transpose(0, 1, 3, 4, 2).reshape(-1, KV_WIDTH, PAGE_SIZE)
    cache_kiT = cache_kidx.transpose(0, 1, 3, 2).reshape(-1, IDX_DIM, PAGE_SIZE)
    moe_w1f = moe_w1.reshape(-1, d, 2 * D_EXPERT)
    moe_w2f = moe_w2.reshape(-1, D_EXPERT, d)
    row_vec = lambda v: v.reshape(1, -1)
    head_grp = jnp.arange(N_HEADS)[:, None] // Q_PER_KV == jnp.arange(KV_WIDTH)[None, :] // HEAD_DIM

    xp = x_prompt
    xs = x_sample.reshape(1, db, d)
    outs = {name: [] for name in ("k_p", "v_p", "ki_p", "h_p", "cv_p", "k_s", "v_s", "ki_s", "h_s", "cv_s")}
    for l in range(DEPTH):
        parts = [mod[l, :, i * d:(i + 1) * d] for i in range(6)]
        sh1p, sc1p, g1p, sh2p, sc2p, g2p = [m[:b].reshape(b, 1, d) for m in parts]
        sh1s, sc1s, g1s, sh2s, sc2s, g2s = [m[b:n_cond].reshape(1, db, d) for m in parts]
        g_mix = row_vec(norm_mix_g[l])
        if l % 2 == 0:
            a = l // 2
            w_in = jnp.pad(attn_w_in[a], ((0, 0), (0, ATTN_PROJ_PAD - ATTN_PROJ))).astype(BF16)
            w_out = attn_w_out[a].astype(BF16)
            kT, vT, kiT, qT, qiT, wiT, kb, vTb, kib = _attn_proj_prompt(xp, g_mix, sh1p, sc1p, w_in, tables_p)
            op = _dsa_prompt(qT, qiT, wiT, kb, vTb, kib)
            outs["k_p"].append(kT.reshape(b, N_KV_HEADS, HEAD_DIM, t).transpose(0, 3, 1, 2))
            outs["v_p"].append(vT.reshape(b, N_KV_HEADS, HEAD_DIM, t).transpose(0, 3, 1, 2))
            outs["ki_p"].append(kiT.transpose(0, 2, 1))
            xp = _out_proj(op, w_out, xp, g1p)

            q, k, v, qi, ki, wi = _attn_proj_sample(xs, g_mix, sh1s, sc1s, w_in, tables_s)
            bias = _dsa_sample_select(page_table, qi.reshape(db, IDX_HEADS, IDX_DIM).astype(BF16),
                                      wi.reshape(db, IDX_HEADS, 1), cache_kiT,
                                      ki.reshape(db, 1, IDX_DIM), a, n_pool)
            q_bd = jnp.where(head_grp[None], jnp.tile(q.reshape(db, N_HEADS, HEAD_DIM), (1, 1, N_KV_HEADS)),
                             0.0).astype(BF16)
            os_ = _dsa_sample_attend(page_table, q_bd, bias, cache_kT, cache_vT,
                                     k.reshape(db, 1, KV_WIDTH), v.reshape(db, 1, KV_WIDTH), a, n_pool)
            outs["k_s"].append(k.reshape(db, 1, N_KV_HEADS, HEAD_DIM))
            outs["v_s"].append(v.reshape(db, 1, N_KV_HEADS, HEAD_DIM))
            outs["ki_s"].append(ki.reshape(db, 1, IDX_DIM))
            xs = _out_proj(os_.reshape(1, db, Q_WIDTH).astype(BF16), w_out, xs, g1s)
        else:
            r = l // 2
            w_in = rec_w_in[r].astype(BF16)
            w_out = rec_w_out[r].astype(BF16)
            rec = (rec_conv_w[r], row_vec(rec_conv_b[r]), rec_w_a[r].astype(BF16), row_vec(rec_b_a[r]),
                   rec_w_x[r].astype(BF16), row_vec(rec_b_x[r]), row_vec(rec_lambda[r]))
            gate, xb = _rec_proj(xp, g_mix, sh1p, sc1p, w_in)
            y, h_t, buf = _rglru_prompt(gate, xb, *rec)
            outs["h_p"].append(h_t.reshape(b, D_RNN))
            outs["cv_p"].append(buf)
            xp = _out_proj(y, w_out, xp, g1p)

            gate, xb = _rec_proj(xs, g_mix, sh1s, sc1s, w_in)
            y, h_t, buf = _rglru_sample(gate[0], xb[0], state_conv[r].swapaxes(0, 1), state_h[r], *rec)
            outs["h_s"].append(h_t)
            outs["cv_s"].append(buf.swapaxes(0, 1))
            xs = _out_proj(y.reshape(1, db, D_RNN), w_out, xs, g1s)

        wr = jnp.concatenate([moe_w_group[l], moe_w_expert[l],
                              jnp.zeros((d, ROUTER_PAD - N_GROUPS - N_EXPERTS), F32)], axis=1).astype(BF16)
        br = jnp.concatenate([moe_b_group[l], moe_b_expert[l],
                              jnp.zeros((ROUTER_PAD - N_GROUPS - N_EXPERTS,), F32)]).reshape(1, ROUTER_PAD)
        last = l == DEPTH - 1
        moe_args = (wr, br, moe_w1f, moe_w2f, l, row_vec(norm_final_g), last)
        xp = _moe(xp, row_vec(norm_ffn_g[l]), sh2p, sc2p, g2p, *moe_args)
        xs = _moe(xs, row_vec(norm_ffn_g[l]), sh2s, sc2s, g2s, *moe_args)

    st = lambda name: jnp.stack(outs[name])
    return (xp, xs.reshape(db, 1, d), st("k_p"), st("v_p"), st("ki_p"), st("h_p"), st("cv_p"),
            st("k_s"), st("v_s"), st("ki_s"), st("h_s"), st("cv_s"))
```

```python
import functools
import math

import jax
import jax.numpy as jnp
from jax import lax
from jax.experimental import pallas as pl
from jax.experimental.pallas import tpu as pltpu

F32 = jnp.float32
BF16 = jnp.bfloat16
I32 = jnp.int32

D_MODEL = 1024
DEPTH = 4
PAGE_SIZE = 128
N_HEADS = 16
HEAD_DIM = 64
N_KV_HEADS = 4
Q_PER_KV = N_HEADS // N_KV_HEADS
IDX_HEADS = 8
IDX_DIM = 64
IDX_SCALE = (IDX_HEADS * IDX_DIM) ** -0.5
TOPK_MAX = 256
ROPE_THETA = 500000.0
ROPE_FRACTION = 4
Q_WIDTH = N_HEADS * HEAD_DIM
KV_WIDTH = N_KV_HEADS * HEAD_DIM
QI_WIDTH = IDX_HEADS * IDX_DIM
ATTN_PROJ = Q_WIDTH + 2 * KV_WIDTH + QI_WIDTH + IDX_DIM + IDX_HEADS
D_RNN = D_MODEL
RG_BLOCKS = 8
RG_BLOCK_W = D_RNN // RG_BLOCKS
CONV_W = 4
RG_C = 8.0
N_GROUPS = 4
EXPERTS_PER_GROUP = 8
N_EXPERTS = N_GROUPS * EXPERTS_PER_GROUP
D_EXPERT = 256
EPS = 1e-6

LANES = 128
SUBLANES = 8
VMEM_LIMIT_BYTES = 56 * 1024 * 1024

ATTN_PROJ_PAD = -(-ATTN_PROJ // LANES) * LANES
ROUTER_PAD = LANES
Q_TILE = LANES
KEY_CHUNK = 512
ROW_TILE = 512
MOE_ROW_TILE = 1024
MOE_BLOCK_ROWS = 256
SCAN_CHUNK = 512
SELECT_PAGES_PER_STEP = 16
ATTEND_PAGES_PER_STEP = 8
BF16_SUBLANES = 16
HEAD_AUG = HEAD_DIM + BF16_SUBLANES
BOUND_SLACK = 1.02
DENOM_FLOOR = 2.0 ** -60
DENOM_CEIL = 2.0 ** 60
INT_MIN = -(2 ** 31)
INT_MAX = 2 ** 31 - 1
NEG = -1e30


def _cparams(n_axes):
    return pltpu.CompilerParams(dimension_semantics=("arbitrary",) * n_axes,
                                vmem_limit_bytes=VMEM_LIMIT_BYTES)


def _modulated_norm(x, g, shift, scale):
    xn = x * lax.rsqrt(jnp.mean(x * x, axis=-1, keepdims=True) + EPS)
    return (xn * g) * (1.0 + scale) + shift


def _rope_group(xs, cos, sa, sb):
    return xs * cos + pltpu.roll(xs, LANES - 8, 1) * sa + pltpu.roll(xs, 8, 1) * sb


def _float_order_key(s):
    bits = lax.bitcast_convert_type(s, I32)
    return bits ^ ((bits >> 31) & INT_MAX)


def _mod_spec(arr, tm):
    if arr.shape[1] == 1:
        return pl.BlockSpec((1, 1, arr.shape[2]), lambda b, i, *_: (b, 0, 0))
    return pl.BlockSpec((1, tm, arr.shape[2]), lambda b, i, *_: (b, i, 0))


def _adaln_kernel(c_ref, w_ref, b_ref, o_ref):
    c = c_ref[...]
    s = (c * jax.nn.sigmoid(c)).astype(BF16)
    o_ref[0] = jnp.dot(s, w_ref[0].astype(BF16), preferred_element_type=F32) + b_ref[0]


def _adaln(c_all, ada_w, ada_b):
    depth, d, n = ada_w.shape
    rows = c_all.shape[0]
    tn = 1536
    return pl.pallas_call(
        _adaln_kernel,
        grid=(depth, n // tn),
        in_specs=[pl.BlockSpec((rows, d), lambda l, j: (0, 0)),
                  pl.BlockSpec((1, d, tn), lambda l, j: (l, 0, j)),
                  pl.BlockSpec((1, 1, tn), lambda l, j: (l, 0, j))],
        out_specs=pl.BlockSpec((1, rows, tn), lambda l, j: (l, 0, j)),
        out_shape=jax.ShapeDtypeStruct((depth, rows, n), F32),
        compiler_params=_cparams(2),
        name="adaln",
    )(c_all, ada_w, ada_b.reshape(depth, 1, n))


def _rope_tables(pos):
    rot = HEAD_DIM // ROPE_FRACTION
    half = rot // 2
    inv = jnp.exp(-math.log(ROPE_THETA) * jnp.arange(half, dtype=F32) * (2.0 / rot))
    ang = pos.astype(F32)[:, None] * inv[None, :]
    cos, sin = jnp.cos(ang), jnp.sin(ang)
    r = pos.shape[0]
    z = lambda w: jnp.zeros((r, w), F32)
    cos64 = jnp.concatenate([cos, cos, jnp.ones((r, HEAD_DIM - rot), F32)], axis=1)
    sa64 = jnp.concatenate([-sin, z(HEAD_DIM - half)], axis=1)
    sb64 = jnp.concatenate([z(half), sin, z(HEAD_DIM - rot)], axis=1)
    two = lambda t: jnp.concatenate([t, t], axis=1)
    return two(cos64), two(sa64), two(sb64)


def _attn_proj_prompt_kernel(x_ref, g_ref, sh_ref, sc_ref, w_ref, cos_ref, sa_ref, sb_ref,
                             kT_ref, vT_ref, kiT_ref, qT_ref, qiT_ref, wiT_ref, kb_ref, vTb_ref, kib_ref,
                             kmax_ref):
    h = _modulated_norm(x_ref[0], g_ref[...], sh_ref[0], sc_ref[0]).astype(BF16)
    acc = jnp.dot(h, w_ref[...], preferred_element_type=F32)
    cos, sa, sb = cos_ref[...], sa_ref[...], sb_ref[...]
    n_qb = acc.shape[0] // Q_TILE
    grp = lambda off, j: acc[:, off + j * LANES: off + (j + 1) * LANES]

    tm = acc.shape[0]
    one_hot_row = lambda shape, axis: jnp.where(lax.broadcasted_iota(I32, shape, axis) == 0, 1.0, 0.0).astype(BF16)

    @pl.when(pl.program_id(1) == 0)
    def _():
        kmax_ref[...] = jnp.zeros(kmax_ref.shape, F32)

    for j in range(KV_WIDTH // LANES):
        kj = _rope_group(grp(Q_WIDTH, j), cos, sa, sb)
        kjT = kj.T
        kT_ref[0, j * LANES:(j + 1) * LANES, :] = kjT
        sq = kjT * kjT
        for hh in range(2):
            n = 2 * j + hh
            norm2 = jnp.sum(sq[hh * HEAD_DIM:(hh + 1) * HEAD_DIM, :], axis=0, keepdims=True)
            kmax_ref[n] = jnp.maximum(kmax_ref[n], jnp.max(norm2, axis=1, keepdims=True))
            kb_ref[0, n, :, 0:HEAD_DIM] = kj[:, hh * HEAD_DIM:(hh + 1) * HEAD_DIM].astype(BF16)
            kb_ref[0, n, :, HEAD_DIM:HEAD_AUG] = one_hot_row((tm, BF16_SUBLANES), 1)
    for j in range(Q_WIDTH // LANES):
        qjT = (_rope_group(grp(0, j), cos, sa, sb) * (HEAD_DIM ** -0.5)).T
        sq = qjT * qjT
        qjTb = qjT.astype(BF16)
        for hh in range(2):
            n, g = divmod(2 * j + hh, Q_PER_KV)
            norm2 = jnp.sum(sq[hh * HEAD_DIM:(hh + 1) * HEAD_DIM, :], axis=0, keepdims=True)
            shift = -BOUND_SLACK * jnp.sqrt(norm2 * kmax_ref[n])
            shift = jnp.concatenate([shift, jnp.zeros((BF16_SUBLANES - 1, tm), F32)], axis=0).astype(BF16)
            for jb in range(n_qb):
                qT_ref[0, jb, n, 0:HEAD_DIM, g * Q_TILE:(g + 1) * Q_TILE] = (
                    qjTb[hh * HEAD_DIM:(hh + 1) * HEAD_DIM, jb * Q_TILE:(jb + 1) * Q_TILE])
                qT_ref[0, jb, n, HEAD_DIM:HEAD_AUG, g * Q_TILE:(g + 1) * Q_TILE] = (
                    shift[:, jb * Q_TILE:(jb + 1) * Q_TILE])
    for j in range(KV_WIDTH // LANES):
        vjT = grp(Q_WIDTH + KV_WIDTH, j).T
        vT_ref[0, j * LANES:(j + 1) * LANES, :] = vjT
        for hh in range(2):
            n = 2 * j + hh
            vTb_ref[0, 0, n * HEAD_AUG:n * HEAD_AUG + HEAD_DIM, :] = (
                vjT[hh * HEAD_DIM:(hh + 1) * HEAD_DIM, :].astype(BF16))
            vTb_ref[0, 0, n * HEAD_AUG + HEAD_DIM:(n + 1) * HEAD_AUG, :] = one_hot_row((BF16_SUBLANES, tm), 0)
    off_qi = Q_WIDTH + 2 * KV_WIDTH
    for j in range(QI_WIDTH // LANES):
        qijT = _rope_group(grp(off_qi, j), cos, sa, sb).T.astype(BF16)
        for hh in range(2):
            head = 2 * j + hh
            for jb in range(n_qb):
                qiT_ref[0, jb, :, head * Q_TILE:(head + 1) * Q_TILE] = (
                    qijT[hh * IDX_DIM:(hh + 1) * IDX_DIM, jb * Q_TILE:(jb + 1) * Q_TILE])
    last = grp(off_qi + QI_WIDTH, 0)
    kir = _rope_group(last, cos, sa, sb)
    kiT_ref[0] = kir.T[:IDX_DIM, :]
    kib_ref[0] = kir[:, :IDX_DIM].astype(BF16)
    lastT = last.T
    for jb in range(n_qb):
        wiT_ref[0, jb] = lastT[IDX_DIM:IDX_DIM + IDX_HEADS, jb * Q_TILE:(jb + 1) * Q_TILE]


def _attn_proj_prompt(x, g, shift, scale, w_pad, tables):
    b, t, d = x.shape
    tm = ROW_TILE
    nq = t // Q_TILE
    grid = (b, t // tm)
    row = lambda width: pl.BlockSpec((1, tm, width), lambda bi, i: (bi, i, 0))
    col = lambda width: pl.BlockSpec((1, width, tm), lambda bi, i: (bi, 0, i))
    tab = pl.BlockSpec((tm, LANES), lambda bi, i: (i, 0))
    out_shape = (
        jax.ShapeDtypeStruct((b, KV_WIDTH, t), F32),
        jax.ShapeDtypeStruct((b, KV_WIDTH, t), F32),
        jax.ShapeDtypeStruct((b, IDX_DIM, t), F32),
        jax.ShapeDtypeStruct((b, nq, N_KV_HEADS, HEAD_AUG, Q_PER_KV * Q_TILE), BF16),
        jax.ShapeDtypeStruct((b, nq, IDX_DIM, IDX_HEADS * Q_TILE), BF16),
        jax.ShapeDtypeStruct((b, nq, IDX_HEADS, Q_TILE), F32),
        jax.ShapeDtypeStruct((b, N_KV_HEADS, t, HEAD_AUG), BF16),
        jax.ShapeDtypeStruct((b, t // tm, N_KV_HEADS * HEAD_AUG, tm), BF16),
        jax.ShapeDtypeStruct((b, t, IDX_DIM), BF16),
    )
    nqb = tm // Q_TILE
    out_specs = (
        col(KV_WIDTH), col(KV_WIDTH), col(IDX_DIM),
        pl.BlockSpec((1, nqb, N_KV_HEADS, HEAD_AUG, Q_PER_KV * Q_TILE), lambda bi, i: (bi, i, 0, 0, 0)),
        pl.BlockSpec((1, nqb, IDX_DIM, IDX_HEADS * Q_TILE), lambda bi, i: (bi, i, 0, 0)),
        pl.BlockSpec((1, nqb, IDX_HEADS, Q_TILE), lambda bi, i: (bi, i, 0, 0)),
        pl.BlockSpec((1, N_KV_HEADS, tm, HEAD_AUG), lambda bi, i: (bi, 0, i, 0)),
        pl.BlockSpec((1, 1, N_KV_HEADS * HEAD_AUG, tm), lambda bi, i: (bi, i, 0, 0)),
        row(IDX_DIM),
    )
    return pl.pallas_call(
        _attn_proj_prompt_kernel,
        grid=grid,
        in_specs=[row(d), pl.BlockSpec((1, d), lambda bi, i: (0, 0)),
                  _mod_spec(shift, tm), _mod_spec(scale, tm),
                  pl.BlockSpec(w_pad.shape, lambda bi, i: (0, 0)), tab, tab, tab],
        out_specs=out_specs,
        out_shape=out_shape,
        scratch_shapes=[pltpu.VMEM((N_KV_HEADS, 1, 1), F32)],
        compiler_params=_cparams(2),
        name="attn_proj_prompt",
    )(x, g, shift, scale, w_pad, *tables)


def _attn_proj_sample_kernel(x_ref, g_ref, sh_ref, sc_ref, w_ref, cos_ref, sa_ref, sb_ref,
                             q_ref, k_ref, v_ref, qi_ref, ki_ref, wi_ref):
    h = _modulated_norm(x_ref[0], g_ref[...], sh_ref[0], sc_ref[0]).astype(BF16)
    acc = jnp.dot(h, w_ref[...], preferred_element_type=F32)
    cos, sa, sb = cos_ref[...], sa_ref[...], sb_ref[...]
    grp = lambda off, j: acc[:, off + j * LANES: off + (j + 1) * LANES]
    for j in range(Q_WIDTH // LANES):
        q_ref[:, j * LANES:(j + 1) * LANES] = _rope_group(grp(0, j), cos, sa, sb) * (HEAD_DIM ** -0.5)
    for j in range(KV_WIDTH // LANES):
        k_ref[:, j * LANES:(j + 1) * LANES] = _rope_group(grp(Q_WIDTH, j), cos, sa, sb)
        v_ref[:, j * LANES:(j + 1) * LANES] = grp(Q_WIDTH + KV_WIDTH, j)
    off_qi = Q_WIDTH + 2 * KV_WIDTH
    for j in range(QI_WIDTH // LANES):
        qi_ref[:, j * LANES:(j + 1) * LANES] = _rope_group(grp(off_qi, j), cos, sa, sb)
    last = grp(off_qi + QI_WIDTH, 0)
    ki_ref[...] = _rope_group(last, cos, sa, sb)[:, :IDX_DIM]
    wi_ref[...] = last[:, IDX_DIM:IDX_DIM + IDX_HEADS]


def _attn_proj_sample(x, g, shift, scale, w_pad, tables):
    _, rows, d = x.shape
    full = lambda shape: pl.BlockSpec(shape, lambda i: (0,) * len(shape))
    widths = (Q_WIDTH, KV_WIDTH, KV_WIDTH, QI_WIDTH, IDX_DIM, IDX_HEADS)
    return pl.pallas_call(
        _attn_proj_sample_kernel,
        grid=(1,),
        in_specs=[full((1, rows, d)), full((1, d)), full((1, rows, d)), full((1, rows, d)),
                  full(w_pad.shape), full((rows, LANES)), full((rows, LANES)), full((rows, LANES))],
        out_specs=tuple(full((rows, w)) for w in widths),
        out_shape=tuple(jax.ShapeDtypeStruct((rows, w), F32) for w in widths),
        compiler_params=_cparams(1),
        name="attn_proj_sample",
    )(x, g, shift, scale, w_pad, *tables)


def _dsa_prompt_kernel(qT_ref, qiT_ref, wiT_ref, kb_ref, vT_ref, kib_ref, o_ref,
                       keys_ref, acc_ref, acc2_ref, m_ref, l_ref, j_ref, *, topk, idx_bits):
    kc = KEY_CHUNK
    qb = pl.program_id(1)
    n_chunks = (qb * Q_TILE + Q_TILE + kc - 1) // kc
    rows = lax.broadcasted_iota(I32, (kc, Q_TILE), 0)
    q_pos = qb * Q_TILE + lax.broadcasted_iota(I32, (kc, Q_TILE), 1)

    w_heads = wiT_ref[0, 0] * IDX_SCALE
    qiT = qiT_ref[0, 0]

    def score_chunk(c, carry):
        off = pl.multiple_of(c * kc, kc)
        dots = jnp.dot(kib_ref[0, pl.ds(off, kc), :], qiT, preferred_element_type=F32)
        s = jnp.zeros((kc, Q_TILE), F32)
        for h in range(IDX_HEADS):
            s = s + jnp.maximum(dots[:, h * Q_TILE:(h + 1) * Q_TILE], 0.0) * w_heads[h:h + 1, :]
        keys_ref[pl.ds(off, kc), :] = jnp.where(rows + off <= q_pos, _float_order_key(s), INT_MIN)
        return carry

    lax.fori_loop(0, n_chunks, score_chunk, 0)

    def count(pred):
        def body(c, cnt):
            off = pl.multiple_of(c * kc, kc)
            hit = jnp.where(pred(keys_ref[pl.ds(off, kc), :], rows + off), 1, 0)
            return cnt + hit.reshape(kc // SUBLANES, SUBLANES, Q_TILE).sum(axis=0)
        cnt = lax.fori_loop(0, n_chunks, body, jnp.zeros((SUBLANES, Q_TILE), I32))
        return cnt.sum(axis=0, keepdims=True)

    def thr_bit(i, state):
        thr, n_ge = state
        cand = thr + lax.shift_left(jnp.int32(1), 31 - i)
        n_cand = count(lambda k, s: k >= cand)
        ok = n_cand >= topk
        return jnp.where(ok, cand, thr), jnp.where(ok, n_cand, n_ge)

    n_all = jnp.full((1, Q_TILE), 1, I32) * (n_chunks * kc)
    thr, n_ge = lax.fori_loop(0, 32, thr_bit, (jnp.full((1, Q_TILE), INT_MIN, I32), n_all))

    j_ref[...] = jnp.full((1, Q_TILE), INT_MAX, I32)
    tie_split = jnp.max(jnp.where((n_ge > topk) & (thr > INT_MIN), 1, 0))

    @pl.when(tie_split > 0)
    def _():
        need = topk - count(lambda k, s: k > thr)

        def idx_bit(i, j0):
            cand = j0 + lax.shift_left(jnp.int32(1), idx_bits - 1 - i)
            taken_before = count(lambda k, s: (k == thr) & (s < cand))
            return jnp.where(taken_before < need, cand, j0)
        j_ref[...] = lax.fori_loop(0, idx_bits, idx_bit, jnp.zeros((1, Q_TILE), I32))

    j_sel = j_ref[...]

    def selection_bias(c):
        off = pl.multiple_of(c * kc, kc)
        k = keys_ref[pl.ds(off, kc), :]
        s_idx = rows + off
        sel = ((k > thr) | ((k == thr) & (s_idx <= j_sel))) & (s_idx <= q_pos)
        bias = jnp.where(sel, 0.0, NEG)
        return off, jnp.concatenate([bias] * Q_PER_KV, axis=1)

    acc_ref[...] = jnp.zeros(acc_ref.shape, F32)

    def attend_chunk(c, carry):
        off, bias = selection_bias(c)
        for n in range(N_KV_HEADS):
            s = jnp.dot(kb_ref[0, n, pl.ds(off, kc), :], qT_ref[0, 0, n], preferred_element_type=F32) + bias
            acc_ref[n] += jnp.dot(vT_ref[0, c, n * HEAD_AUG:(n + 1) * HEAD_AUG, :], jnp.exp(s).astype(BF16),
                                  preferred_element_type=F32)
        return carry

    lax.fori_loop(0, n_chunks, attend_chunk, 0)
    denom = jnp.concatenate([acc_ref[n][HEAD_DIM:HEAD_DIM + 1, :] for n in range(N_KV_HEADS)], axis=0)
    healthy = (jnp.min(denom) >= DENOM_FLOOR) & (jnp.max(denom) <= DENOM_CEIL)

    @pl.when(jnp.logical_not(healthy))
    def _():
        m_ref[...] = jnp.full(m_ref.shape, NEG, F32)
        l_ref[...] = jnp.zeros(l_ref.shape, F32)
        acc2_ref[...] = jnp.zeros(acc2_ref.shape, F32)

        def attend_chunk_online(c, carry):
            off, bias = selection_bias(c)
            for n in range(N_KV_HEADS):
                s = jnp.dot(kb_ref[0, n, pl.ds(off, kc), 0:HEAD_DIM], qT_ref[0, 0, n, 0:HEAD_DIM, :],
                            preferred_element_type=F32) + bias
                m_prev = m_ref[n]
                m_new = jnp.maximum(m_prev, s.max(axis=0, keepdims=True))
                alpha = jnp.exp(m_prev - m_new)
                p = jnp.exp(s - m_new)
                l_ref[n] = alpha * l_ref[n] + p.sum(axis=0, keepdims=True)
                pv = jnp.dot(vT_ref[0, c, n * HEAD_AUG:n * HEAD_AUG + HEAD_DIM, :], p.astype(BF16),
                             preferred_element_type=F32)
                acc2_ref[n] = alpha * acc2_ref[n] + pv
                m_ref[n] = m_new
            return carry

        lax.fori_loop(0, n_chunks, attend_chunk_online, 0)
        for n in range(N_KV_HEADS):
            acc_ref[n, 0:HEAD_DIM, :] = acc2_ref[n]
            acc_ref[n, HEAD_DIM:HEAD_DIM + 1, :] = l_ref[n]

    for n in range(N_KV_HEADS):
        on = acc_ref[n, 0:HEAD_DIM, :] / acc_ref[n, HEAD_DIM:HEAD_DIM + 1, :]
        for gp in range(Q_PER_KV // 2):
            pair = jnp.concatenate([on[:, (2 * gp) * Q_TILE:(2 * gp + 1) * Q_TILE],
                                    on[:, (2 * gp + 1) * Q_TILE:(2 * gp + 2) * Q_TILE]], axis=0)
            col = (n * Q_PER_KV // 2 + gp) * LANES
            o_ref[0, :, col:col + LANES] = pair.T.astype(BF16)


def _dsa_prompt(qT, qiT, wiT, kb, vT, kib):
    b, nq = qT.shape[:2]
    t = kb.shape[2]
    topk = min(TOPK_MAX, t // 4)
    idx_bits = max(1, (t - 1).bit_length())
    per_q = lambda shape: pl.BlockSpec((1, 1) + shape, lambda bi, qi: (bi, qi) + (0,) * len(shape))
    per_b = lambda shape: pl.BlockSpec((1,) + shape, lambda bi, qi: (bi,) + (0,) * len(shape))
    return pl.pallas_call(
        functools.partial(_dsa_prompt_kernel, topk=topk, idx_bits=idx_bits),
        grid=(b, nq),
        in_specs=[per_q(qT.shape[2:]), per_q(qiT.shape[2:]), per_q(wiT.shape[2:]),
                  per_b(kb.shape[1:]), per_b(vT.shape[1:]), per_b(kib.shape[1:])],
        out_specs=pl.BlockSpec((1, Q_TILE, Q_WIDTH), lambda bi, qi: (bi, qi, 0)),
        out_shape=jax.ShapeDtypeStruct((b, t, Q_WIDTH), BF16),
        scratch_shapes=[pltpu.VMEM((t, Q_TILE), I32),
                        pltpu.VMEM((N_KV_HEADS, HEAD_AUG, Q_PER_KV * Q_TILE), F32),
                        pltpu.VMEM((N_KV_HEADS, HEAD_DIM, Q_PER_KV * Q_TILE), F32),
                        pltpu.VMEM((N_KV_HEADS, 1, Q_PER_KV * Q_TILE), F32),
                        pltpu.VMEM((N_KV_HEADS, 1, Q_PER_KV * Q_TILE), F32),
                        pltpu.VMEM((1, Q_TILE), I32)],
        compiler_params=_cparams(2),
        name="dsa_prompt",
    )(qT, qiT, wiT, kb, vT, kib)


def _dsa_sample_select_kernel(pt_ref, qi_ref, w_ref, *refs, n_pages, pg, topk, idx_bits):
    kiT_refs, (kin_ref, bias_ref, keys_ref) = refs[:pg], refs[pg:]
    g = pl.program_id(1)
    rows_pad = keys_ref.shape[0]
    qi = qi_ref[0]
    w = w_ref[0] * IDX_SCALE
    kiT = jnp.concatenate([r[0].astype(BF16) for r in kiT_refs], axis=1)
    dots = jnp.dot(qi, kiT, preferred_element_type=F32)
    page_keys = _float_order_key(jnp.sum(jnp.maximum(dots, 0.0) * w, axis=0, keepdims=True))
    for j in range(pg):
        keys_ref[pl.ds(g * pg + j, 1), :] = page_keys[:, j * PAGE_SIZE:(j + 1) * PAGE_SIZE]

    @pl.when(g == n_pages // pg - 1)
    def _():
        lane1 = lax.broadcasted_iota(I32, (1, LANES), 1)
        kn = kin_ref[0].astype(BF16).astype(F32)
        dn = jnp.sum(qi.astype(F32) * kn, axis=1, keepdims=True)
        sn = jnp.sum(jnp.maximum(dn, 0.0) * w, axis=0, keepdims=True)
        keys_ref[n_pages:n_pages + 1, :] = jnp.where(lane1 == 0, _float_order_key(sn), INT_MIN)
        keys_ref[n_pages + 1:, :] = jnp.full((rows_pad - n_pages - 1, LANES), INT_MIN, I32)

        keys = keys_ref[...]
        row = lax.broadcasted_iota(I32, keys.shape, 0)
        lane = lax.broadcasted_iota(I32, keys.shape, 1)
        idx = row * LANES + lane
        real = (row < n_pages) | ((row == n_pages) & (lane == 0))
        count = lambda hit: jnp.sum(jnp.where(hit, 1, 0), keepdims=True)

        def thr_bit(i, thr):
            cand = thr + lax.shift_left(jnp.int32(1), 31 - i)
            return jnp.where(count(keys >= cand) >= topk, cand, thr)

        thr = lax.fori_loop(0, 32, thr_bit, jnp.full((1, 1), INT_MIN, I32))
        need = topk - count(keys > thr)

        def idx_bit(i, j0):
            cand = j0 + lax.shift_left(jnp.int32(1), idx_bits - 1 - i)
            return jnp.where(count((keys == thr) & (idx < cand)) < need, cand, j0)

        j_sel = lax.fori_loop(0, idx_bits, idx_bit, jnp.zeros((1, 1), I32))
        sel = ((keys > thr) | ((keys == thr) & (idx <= j_sel))) & real
        bias_ref[0] = jnp.where(sel, 0.0, NEG)


def _page_specs(block, layer, n_pool, pg):
    def spec(j):
        return pl.BlockSpec(block, lambda b, g, pt: (layer * n_pool + pt[b, g * pg + j], 0, 0))
    return [spec(j) for j in range(pg)]


def _dsa_sample_select(page_table, qi3, w3, cache_kiT, ki_new, layer, n_pool):
    db, n_pages = page_table.shape
    pg = min(SELECT_PAGES_PER_STEP, n_pages)
    rows_pad = -(-(n_pages + 1) // SUBLANES) * SUBLANES
    n_keys = n_pages * PAGE_SIZE + 1
    topk = min(TOPK_MAX, n_keys // 4)
    idx_bits = max(1, (n_keys - 1).bit_length())
    grid_spec = pltpu.PrefetchScalarGridSpec(
        num_scalar_prefetch=1,
        grid=(db, n_pages // pg),
        in_specs=[pl.BlockSpec((1, IDX_HEADS, IDX_DIM), lambda b, g, pt: (b, 0, 0)),
                  pl.BlockSpec((1, IDX_HEADS, 1), lambda b, g, pt: (b, 0, 0))]
                 + _page_specs((1, IDX_DIM, PAGE_SIZE), layer, n_pool, pg)
                 + [pl.BlockSpec((1, 1, IDX_DIM), lambda b, g, pt: (b, 0, 0))],
        out_specs=pl.BlockSpec((1, rows_pad, LANES), lambda b, g, pt: (b, 0, 0)),
        scratch_shapes=[pltpu.VMEM((rows_pad, LANES), I32)],
    )
    return pl.pallas_call(
        functools.partial(_dsa_sample_select_kernel, n_pages=n_pages, pg=pg, topk=topk, idx_bits=idx_bits),
        grid_spec=grid_spec,
        out_shape=jax.ShapeDtypeStruct((db, rows_pad, LANES), F32),
        compiler_params=_cparams(2),
        name="dsa_sample_select",
    )(page_table, qi3, w3, *([cache_kiT] * pg), ki_new)


def _dsa_sample_attend_kernel(pt_ref, q_ref, bias_ref, *refs, n_pages, pg):
    kT_refs, vT_refs = refs[:pg], refs[pg:2 * pg]
    kn_ref, vn_ref, o_ref, acc_ref, m_ref, l_ref = refs[2 * pg:]
    g = pl.program_id(1)

    @pl.when(g == 0)
    def _():
        m_ref[...] = jnp.full(m_ref.shape, NEG, F32)
        l_ref[...] = jnp.zeros(l_ref.shape, F32)
        acc_ref[...] = jnp.zeros(acc_ref.shape, F32)

    q = q_ref[0]
    kT = jnp.concatenate([r[0].astype(BF16) for r in kT_refs], axis=1)
    bias = jnp.concatenate([bias_ref[0, pl.ds(g * pg + j, 1), :] for j in range(pg)], axis=1)
    s = jnp.dot(q, kT, preferred_element_type=F32) + bias
    m_prev = m_ref[...]
    m_new = jnp.maximum(m_prev, s.max(axis=1, keepdims=True))
    alpha = jnp.exp(m_prev - m_new)
    pr = jnp.exp(s - m_new)
    l_ref[...] = alpha * l_ref[...] + pr.sum(axis=1, keepdims=True)
    vT = jnp.concatenate([r[0].astype(BF16) for r in vT_refs], axis=1)
    acc_ref[...] = alpha * acc_ref[...] + lax.dot_general(
        pr.astype(BF16), vT, (((1,), (1,)), ((), ())), preferred_element_type=F32)
    m_ref[...] = m_new

    @pl.when(g == n_pages // pg - 1)
    def _():
        kn = kn_ref[0].astype(BF16).astype(F32)
        vn = vn_ref[0].astype(BF16).astype(F32)
        sn = jnp.sum(q.astype(F32) * kn, axis=1, keepdims=True) + bias_ref[0, n_pages:n_pages + 1, 0:1]
        m_fin = jnp.maximum(m_new, sn)
        a_fin = jnp.exp(m_new - m_fin)
        pn = jnp.exp(sn - m_fin)
        l_fin = a_fin * l_ref[...] + pn
        acc = a_fin * acc_ref[...] + pn.astype(BF16).astype(F32) * vn
        out = acc / l_fin
        head_grp = lax.broadcasted_iota(I32, out.shape, 0) // Q_PER_KV
        lane_grp = lax.broadcasted_iota(I32, out.shape, 1) // HEAD_DIM
        out = jnp.where(head_grp == lane_grp, out, 0.0)
        o = out[:, 0:HEAD_DIM]
        for n in range(1, N_KV_HEADS):
            o = o + out[:, n * HEAD_DIM:(n + 1) * HEAD_DIM]
        o_ref[0] = o


def _dsa_sample_attend(page_table, q_bd, bias, cache_kT, cache_vT, k_new, v_new, layer, n_pool):
    db, n_pages = page_table.shape
    pg = min(ATTEND_PAGES_PER_STEP, n_pages)
    per_b = lambda shape: pl.BlockSpec((1,) + shape, lambda b, g, pt: (b,) + (0,) * len(shape))
    pages = _page_specs((1, KV_WIDTH, PAGE_SIZE), layer, n_pool, pg)
    grid_spec = pltpu.PrefetchScalarGridSpec(
        num_scalar_prefetch=1,
        grid=(db, n_pages // pg),
        in_specs=[per_b(q_bd.shape[1:]), per_b(bias.shape[1:])] + pages + pages
                 + [per_b((1, KV_WIDTH)), per_b((1, KV_WIDTH))],
        out_specs=per_b((N_HEADS, HEAD_DIM)),
        scratch_shapes=[pltpu.VMEM((N_HEADS, KV_WIDTH), F32),
                        pltpu.VMEM((N_HEADS, 1), F32),
                        pltpu.VMEM((N_HEADS, 1), F32)],
    )
    return pl.pallas_call(
        functools.partial(_dsa_sample_attend_kernel, n_pages=n_pages, pg=pg),
        grid_spec=grid_spec,
        out_shape=jax.ShapeDtypeStruct((db, N_HEADS, HEAD_DIM), F32),
        compiler_params=_cparams(2),
        name="dsa_sample_attend",
    )(page_table, q_bd, bias, *([cache_kT] * pg), *([cache_vT] * pg), k_new, v_new)


def _out_proj_kernel(a_ref, w_ref, x_ref, gt_ref, o_ref):
    y = jnp.dot(a_ref[0], w_ref[...], preferred_element_type=F32)
    o_ref[0] = x_ref[0] + gt_ref[0] * y


def _out_proj(a, w, x, gate):
    b, t, d = x.shape
    tm = min(ROW_TILE, t)
    row = lambda width: pl.BlockSpec((1, tm, width), lambda bi, i: (bi, i, 0))
    return pl.pallas_call(
        _out_proj_kernel,
        grid=(b, t // tm),
        in_specs=[row(a.shape[2]), pl.BlockSpec(w.shape, lambda bi, i: (0, 0)), row(d), _mod_spec(gate, tm)],
        out_specs=row(d),
        out_shape=jax.ShapeDtypeStruct((b, t, d), F32),
        compiler_params=_cparams(2),
        name="out_proj",
    )(a, w, x, gate)


def _gelu_tanh(x):
    return x * (0.5 * (1.0 + jnp.tanh(math.sqrt(2.0 / math.pi) * (x + 0.044715 * (x * x * x)))))


def _rec_proj_kernel(x_ref, g_ref, sh_ref, sc_ref, w_ref, gate_ref, xb_ref):
    h = _modulated_norm(x_ref[0], g_ref[...], sh_ref[0], sc_ref[0]).astype(BF16)
    acc = jnp.dot(h, w_ref[...], preferred_element_type=F32)
    gate_ref[0] = _gelu_tanh(acc[:, :D_RNN])
    xb_ref[0] = acc[:, D_RNN:]


def _rec_proj(x, g, shift, scale, w):
    b, t, d = x.shape
    tm = min(ROW_TILE, t)
    row = lambda width: pl.BlockSpec((1, tm, width), lambda bi, i: (bi, i, 0))
    return pl.pallas_call(
        _rec_proj_kernel,
        grid=(b, t // tm),
        in_specs=[row(d), pl.BlockSpec((1, d), lambda bi, i: (0, 0)), _mod_spec(shift, tm),
                  _mod_spec(scale, tm), pl.BlockSpec(w.shape, lambda bi, i: (0, 0))],
        out_specs=(row(D_RNN), row(D_RNN)),
        out_shape=(jax.ShapeDtypeStruct((b, t, D_RNN), F32),) * 2,
        compiler_params=_cparams(2),
        name="rec_proj",
    )(x, g, shift, scale, w)


def _rglru_gates(xc, wa_ref, ba, wx_ref, bx, lam):
    xcb = xc.astype(BF16)
    blk = lambda w_ref: jnp.concatenate(
        [jnp.dot(xcb[:, n * RG_BLOCK_W:(n + 1) * RG_BLOCK_W], w_ref[n], preferred_element_type=F32)
         for n in range(RG_BLOCKS)], axis=1)
    r = jax.nn.sigmoid(blk(wa_ref) + ba)
    i = jax.nn.sigmoid(blk(wx_ref) + bx)
    neg_lam = -lam
    softplus = jnp.maximum(neg_lam, 0.0) + jnp.log1p(jnp.exp(-jnp.abs(neg_lam)))
    log_a = -RG_C * r * softplus
    a = jnp.exp(log_a)
    u = jnp.sqrt(1.0 - a * a) * (i * xc)
    return a, u


def _rglru_prompt_kernel(gate_ref, xb_ref, cw_ref, cb_ref, wa_ref, ba_ref, wx_ref, bx_ref, lam_ref,
                         y_ref, h_ref, conv_ref, xcat_ref, a_ref, u_ref, hs_ref, hc_ref):
    tc = xb_ref.shape[1]
    c = pl.program_id(1)
    pad = SUBLANES

    @pl.when(c == 0)
    def _():
        xcat_ref[0:pad, :] = jnp.zeros((pad, D_RNN), F32)
        hc_ref[...] = jnp.zeros(hc_ref.shape, F32)

    @pl.when(c > 0)
    def _():
        xcat_ref[0:pad, :] = xcat_ref[tc:tc + pad, :]

    xcat_ref[pad:pad + tc, :] = xb_ref[0]
    cw = cw_ref[...]
    xc = xcat_ref[pad - 3:pad - 3 + tc, :] * cw[0:1, :]
    for j in range(1, CONV_W):
        xc = xc + xcat_ref[pad - 3 + j:pad - 3 + j + tc, :] * cw[j:j + 1, :]
    xc = cb_ref[...] + xc
    a, u = _rglru_gates(xc, wa_ref, ba_ref[...], wx_ref, bx_ref[...], lam_ref[...])
    a_ref[...] = a
    u_ref[...] = u

    def step(t, h):
        h = a_ref[pl.ds(t, 1), :] * h + u_ref[pl.ds(t, 1), :]
        hs_ref[pl.ds(t, 1), :] = h
        return h

    h = lax.fori_loop(0, tc, step, hc_ref[...], unroll=8)
    hc_ref[...] = h
    y_ref[0] = (hs_ref[...] * gate_ref[0]).astype(BF16)
    h_ref[0] = h
    conv_ref[0] = xcat_ref[pad + tc - (CONV_W - 1):pad + tc, :]


def _rglru_prompt(gate, xb, cw, cb, wa, ba, wx, bx, lam):
    b, t, d = xb.shape
    tc = min(SCAN_CHUNK, t)
    row = pl.BlockSpec((1, tc, d), lambda bi, i: (bi, i, 0))
    vec = pl.BlockSpec((1, d), lambda bi, i: (0, 0))
    full = lambda shape: pl.BlockSpec(shape, lambda bi, i: (0,) * len(shape))
    return pl.pallas_call(
        _rglru_prompt_kernel,
        grid=(b, t // tc),
        in_specs=[row, row, full(cw.shape), vec, full(wa.shape), vec, full(wx.shape), vec, vec],
        out_specs=(row, pl.BlockSpec((1, 1, d), lambda bi, i: (bi, 0, 0)),
                   pl.BlockSpec((1, CONV_W - 1, d), lambda bi, i: (bi, 0, 0))),
        out_shape=(jax.ShapeDtypeStruct((b, t, d), BF16), jax.ShapeDtypeStruct((b, 1, d), F32),
                   jax.ShapeDtypeStruct((b, CONV_W - 1, d), F32)),
        scratch_shapes=[pltpu.VMEM((tc + 2 * SUBLANES, d), F32), pltpu.VMEM((tc, d), F32),
                        pltpu.VMEM((tc, d), F32), pltpu.VMEM((tc, d), F32), pltpu.VMEM((1, d), F32)],
        compiler_params=_cparams(2),
        name="rglru_prompt",
    )(gate, xb, cw, cb, wa, ba, wx, bx, lam)


def _rglru_sample_kernel(gate_ref, xb_ref, buf_ref, h0_ref, cw_ref, cb_ref, wa_ref, ba_ref, wx_ref, bx_ref,
                         lam_ref, y_ref, h_ref, nbuf_ref):
    cw = cw_ref[...]
    xb = xb_ref[...]
    xc = buf_ref[0] * cw[0:1, :]
    for j in range(1, CONV_W - 1):
        xc = xc + buf_ref[j] * cw[j:j + 1, :]
    xc = cb_ref[...] + (xc + xb * cw[CONV_W - 1:CONV_W, :])
    a, u = _rglru_gates(xc, wa_ref, ba_ref[...], wx_ref, bx_ref[...], lam_ref[...])
    h = a * h0_ref[...] + u
    h_ref[...] = h
    y_ref[...] = (h * gate_ref[...]).astype(BF16)
    for j in range(CONV_W - 2):
        nbuf_ref[j] = buf_ref[j + 1]
    nbuf_ref[CONV_W - 2] = xb


def _rglru_sample(gate, xb, buf, h0, cw, cb, wa, ba, wx, bx, lam):
    rows, d = xb.shape
    full = lambda a: pl.BlockSpec(a.shape, lambda i: (0,) * a.ndim)
    args = (gate, xb, buf, h0, cw, cb, wa, ba, wx, bx, lam)
    return pl.pallas_call(
        _rglru_sample_kernel,
        grid=(1,),
        in_specs=[full(a) for a in args],
        out_specs=(full(xb), full(xb), full(buf)),
        out_shape=(jax.ShapeDtypeStruct((rows, d), BF16), jax.ShapeDtypeStruct((rows, d), F32),
                   jax.ShapeDtypeStruct(buf.shape, F32)),
        compiler_params=_cparams(1),
        name="rglru_sample",
    )(*args)


def _route(logits):
    lane = lax.broadcasted_iota(I32, logits.shape, 1)

    def first_max(v):
        m = jnp.max(v, axis=1, keepdims=True)
        return m, jnp.min(jnp.where(v == m, lane, LANES), axis=1, keepdims=True)

    is_grp = lane < N_GROUPS
    gm, grp = first_max(jnp.where(is_grp, logits, -jnp.inf))
    g_w = 1.0 / jnp.sum(jnp.where(is_grp, jnp.exp(logits - gm), 0.0), axis=1, keepdims=True)
    lo = N_GROUPS + EXPERTS_PER_GROUP * grp
    el = jnp.where((lane >= lo) & (lane < lo + EXPERTS_PER_GROUP), logits, -jnp.inf)
    m1, i1 = first_max(el)
    m2, i2 = first_max(jnp.where(lane == i1, -jnp.inf, el))
    e2 = jnp.exp(m2 - m1)
    den = 1.0 + e2
    return i1 - N_GROUPS, i2 - N_GROUPS, (1.0 / den) * g_w, (e2 / den) * g_w


def _moe_kernel(x_ref, g_ref, sh_ref, sc_ref, gt_ref, wr_ref, br_ref, w1_ref, w2_ref, gf_ref, o_ref,
                h_ref, acc_ref, e0_ref, e1_ref, c0_ref, c1_ref, *, final_norm):
    e = pl.program_id(2)

    @pl.when(e == 0)
    def _():
        h = _modulated_norm(x_ref[0], g_ref[...], sh_ref[0], sc_ref[0])
        h_ref[...] = h.astype(BF16)
        logits = jnp.dot(h_ref[...], wr_ref[...], preferred_element_type=F32) + br_ref[...]
        e0_ref[...], e1_ref[...], c0_ref[...], c1_ref[...] = _route(logits)
        acc_ref[...] = jnp.zeros(acc_ref.shape, F32)

    gu = jnp.dot(h_ref[...], w1_ref[0].astype(BF16), preferred_element_type=F32)
    gp, up = gu[:, :D_EXPERT], gu[:, D_EXPERT:]
    act = ((gp * jax.nn.sigmoid(gp)) * up).astype(BF16)
    y = jnp.dot(act, w2_ref[0].astype(BF16), preferred_element_type=F32)
    wt = jnp.where(e0_ref[...] == e, c0_ref[...], 0.0) + jnp.where(e1_ref[...] == e, c1_ref[...], 0.0)
    acc_ref[...] += y * wt

    @pl.when(e == N_EXPERTS - 1)
    def _():
        out = x_ref[0] + gt_ref[0] * acc_ref[...]
        if final_norm:
            out = (out * lax.rsqrt(jnp.mean(out * out, axis=-1, keepdims=True) + EPS)) * gf_ref[...]
        o_ref[0] = out


def _moe(x, g, shift, scale, gate, wr, br, w1, w2, layer, gf, final_norm):
    b, t, d = x.shape
    tm = min(MOE_ROW_TILE, t)
    row = pl.BlockSpec((1, tm, d), lambda bi, i, e: (bi, i, 0))
    vec = lambda width: pl.BlockSpec((1, width), lambda bi, i, e: (0, 0))
    col = lambda dt: pltpu.VMEM((tm, 1), dt)
    return pl.pallas_call(
        functools.partial(_moe_kernel, final_norm=final_norm),
        grid=(b, t // tm, N_EXPERTS),
        in_specs=[row, vec(d), _mod_spec(shift, tm), _mod_spec(scale, tm), _mod_spec(gate, tm),
                  pl.BlockSpec(wr.shape, lambda bi, i, e: (0, 0)), vec(ROUTER_PAD),
                  pl.BlockSpec((1, d, 2 * D_EXPERT), lambda bi, i, e: (layer * N_EXPERTS + e, 0, 0)),
                  pl.BlockSpec((1, D_EXPERT, d), lambda bi, i, e: (layer * N_EXPERTS + e, 0, 0)),
                  vec(d)],
        out_specs=row,
        out_shape=jax.ShapeDtypeStruct((b, t, d), F32),
        scratch_shapes=[pltpu.VMEM((tm, d), BF16), pltpu.VMEM((tm, d), F32),
                        col(I32), col(I32), col(F32), col(F32)],
        compiler_params=_cparams(3),
        name="moe",
    )(x, g, shift, scale, gate, wr, br, w1, w2, gf)


REC_E0, REC_E1, REC_RANK0, REC_RANK1, REC_C0, REC_C1 = range(6)


def _moe_route_kernel(x_ref, g_ref, sh_ref, sc_ref, wr_ref, br_ref, h_ref, rec_ref, cnt_ref,
                      tri_ref, carry_ref):
    tm = x_ref.shape[1]

    @pl.when((pl.program_id(0) == 0) & (pl.program_id(1) == 0))
    def _():
        carry_ref[...] = jnp.zeros(carry_ref.shape, F32)
        earlier = lax.broadcasted_iota(I32, (tm, tm), 0) > lax.broadcasted_iota(I32, (tm, tm), 1)
        tri_ref[...] = jnp.where(earlier, 1.0, 0.0).astype(BF16)

    h = _modulated_norm(x_ref[0], g_ref[...], sh_ref[0], sc_ref[0])
    h_ref[0] = h
    logits = jnp.dot(h.astype(BF16), wr_ref[...], preferred_element_type=F32) + br_ref[...]
    e0, e1, c0, c1 = _route(logits)
    lane = lax.broadcasted_iota(I32, logits.shape, 1)
    pick0, pick1 = lane == e0, lane == e1
    picks = jnp.where(pick0 | pick1, 1.0, 0.0)
    before = carry_ref[...] + jnp.dot(tri_ref[...], picks.astype(BF16), preferred_element_type=F32)
    rank0 = jnp.sum(jnp.where(pick0, before, 0.0), axis=1, keepdims=True)
    rank1 = jnp.sum(jnp.where(pick1, before, 0.0), axis=1, keepdims=True)
    carry_ref[...] = carry_ref[...] + jnp.sum(picks, axis=0, keepdims=True)
    cnt_ref[...] = carry_ref[...]
    rec = jnp.zeros(logits.shape, F32)
    for pos, val in ((REC_E0, e0.astype(F32)), (REC_E1, e1.astype(F32)), (REC_RANK0, rank0),
                     (REC_RANK1, rank1), (REC_C0, c0), (REC_C1, c1)):
        rec = jnp.where(lane == pos, val, rec)
    rec_ref[0] = rec


def _moe_route(x, g, shift, scale, wr, br):
    b, t, d = x.shape
    tm = ROW_TILE
    row = lambda width: pl.BlockSpec((1, tm, width), lambda bi, i: (bi, i, 0))
    vec = lambda width: pl.BlockSpec((1, width), lambda bi, i: (0, 0))
    return pl.pallas_call(
        _moe_route_kernel,
        grid=(b, t // tm),
        in_specs=[row(d), vec(d), _mod_spec(shift, tm), _mod_spec(scale, tm),
                  pl.BlockSpec(wr.shape, lambda bi, i: (0, 0)), vec(ROUTER_PAD)],
        out_specs=(row(d), row(ROUTER_PAD), vec(ROUTER_PAD)),
        out_shape=(jax.ShapeDtypeStruct((b, t, d), F32), jax.ShapeDtypeStruct((b, t, ROUTER_PAD), F32),
                   jax.ShapeDtypeStruct((1, ROUTER_PAD), F32)),
        scratch_shapes=[pltpu.VMEM((tm, tm), BF16), pltpu.VMEM((1, ROUTER_PAD), F32)],
        compiler_params=_cparams(2),
        name="moe_route",
    )(x, g, shift, scale, wr, br)


def _gather_rows_start(idx_ref, idx_base, src_hbm, dst, sem, n_rows):
    def body(r, carry):
        pltpu.make_async_copy(src_hbm.at[pl.ds(idx_ref[idx_base + r], 1)], dst.at[pl.ds(r, 1)], sem).start()
        return carry
    lax.fori_loop(0, n_rows, body, 0, unroll=8)


def _gather_rows_wait(src_hbm, dst, sem, n_rows):
    pltpu.make_async_copy(src_hbm.at[pl.ds(0, n_rows)], dst, sem).wait()


def _moe_experts_kernel(blk_e_ref, n_used_ref, row_tok_ref, h_hbm, w1_ref, w2_ref, y_ref,
                        buf_ref, w1b_ref, w2b_ref, sem):
    i = pl.program_id(0)
    blk = buf_ref.shape[1]
    n_used = n_used_ref[0]
    slot = i % 2

    @pl.when(i == 0)
    def _():
        _gather_rows_start(row_tok_ref, 0, h_hbm, buf_ref.at[0], sem.at[0], blk)

    @pl.when(i + 1 < n_used)
    def _():
        _gather_rows_start(row_tok_ref, (i + 1) * blk, h_hbm, buf_ref.at[1 - slot], sem.at[1 - slot], blk)

    @pl.when(i < n_used)
    def _():
        @pl.when((i == 0) | (blk_e_ref[i] != blk_e_ref[jnp.maximum(i - 1, 0)]))
        def _():
            w1b_ref[...] = w1_ref[0].astype(BF16)
            w2b_ref[...] = w2_ref[0].astype(BF16)

        _gather_rows_wait(h_hbm, buf_ref.at[slot], sem.at[slot], blk)
        gu = jnp.dot(buf_ref[slot].astype(BF16), w1b_ref[...], preferred_element_type=F32)
        gp, up = gu[:, :D_EXPERT], gu[:, D_EXPERT:]
        act = ((gp * jax.nn.sigmoid(gp)) * up).astype(BF16)
        y_ref[...] = jnp.dot(act, w2b_ref[...], preferred_element_type=F32)

    @pl.when(i >= n_used)
    def _():
        y_ref[...] = jnp.zeros(y_ref.shape, F32)


def _moe_experts(blk_expert, n_used, row_tok, h2d, w1, w2, layer):
    n_blk = blk_expert.shape[0]
    blk = MOE_BLOCK_ROWS
    d = h2d.shape[1]
    grid_spec = pltpu.PrefetchScalarGridSpec(
        num_scalar_prefetch=3,
        grid=(n_blk,),
        in_specs=[pl.BlockSpec(memory_space=pl.ANY),
                  pl.BlockSpec((1, d, 2 * D_EXPERT), lambda i, be, nu, rt: (layer * N_EXPERTS + be[i], 0, 0)),
                  pl.BlockSpec((1, D_EXPERT, d), lambda i, be, nu, rt: (layer * N_EXPERTS + be[i], 0, 0))],
        out_specs=pl.BlockSpec((blk, d), lambda i, be, nu, rt: (i, 0)),
        scratch_shapes=[pltpu.VMEM((2, blk, d), F32), pltpu.VMEM((d, 2 * D_EXPERT), BF16),
                        pltpu.VMEM((D_EXPERT, d), BF16), pltpu.SemaphoreType.DMA((2,))],
    )
    return pl.pallas_call(
        _moe_experts_kernel,
        grid_spec=grid_spec,
        out_shape=jax.ShapeDtypeStruct((n_blk * blk, d), F32),
        compiler_params=_cparams(1),
        name="moe_experts",
    )(blk_expert, n_used, row_tok, h2d, w1, w2)


def _moe_combine_kernel(dest_ref, y_hbm, x_ref, gt_ref, rec_ref, gf_ref, o_ref, buf_ref, sem,
                        *, n_tokens, final_norm):
    tm = x_ref.shape[1]
    n_steps = pl.num_programs(0) * pl.num_programs(1)
    s = pl.program_id(0) * pl.num_programs(1) + pl.program_id(1)
    slot = s % 2

    def start(step, sl):
        for pick in range(2):
            _gather_rows_start(dest_ref, pick * n_tokens + step * tm, y_hbm, buf_ref.at[sl, pick],
                               sem.at[sl, pick], tm)

    @pl.when(s == 0)
    def _():
        start(0, 0)

    @pl.when(s + 1 < n_steps)
    def _():
        start(s + 1, 1 - slot)

    for pick in range(2):
        _gather_rows_wait(y_hbm, buf_ref.at[slot, pick], sem.at[slot, pick], tm)
    rec = rec_ref[0]
    moe = buf_ref[slot, 0] * rec[:, REC_C0:REC_C0 + 1] + buf_ref[slot, 1] * rec[:, REC_C1:REC_C1 + 1]
    out = x_ref[0] + gt_ref[0] * moe
    if final_norm:
        out = (out * lax.rsqrt(jnp.mean(out * out, axis=-1, keepdims=True) + EPS)) * gf_ref[...]
    o_ref[0] = out


def _moe_combine(dest, y, x, gate, rec, gf, final_norm):
    b, t, d = x.shape
    tm = MOE_BLOCK_ROWS
    row = lambda width: pl.BlockSpec((1, tm, width), lambda bi, i, dst: (bi, i, 0))
    grid_spec = pltpu.PrefetchScalarGridSpec(
        num_scalar_prefetch=1,
        grid=(b, t // tm),
        in_specs=[pl.BlockSpec(memory_space=pl.ANY), row(d), _mod_spec(gate, tm), row(ROUTER_PAD),
                  pl.BlockSpec((1, d), lambda bi, i, dst: (0, 0))],
        out_specs=row(d),
        scratch_shapes=[pltpu.VMEM((2, 2, tm, d), F32), pltpu.SemaphoreType.DMA((2, 2))],
    )
    return pl.pallas_call(
        functools.partial(_moe_combine_kernel, n_tokens=b * t, final_norm=final_norm),
        grid_spec=grid_spec,
        out_shape=jax.ShapeDtypeStruct((b, t, d), F32),
        compiler_params=_cparams(2),
        name="moe_combine",
    )(dest, y, x, gate, rec, gf)


def _moe_sorted(x, g, shift, scale, gate, wr, br, w1, w2, layer, gf, final_norm):
    b, t, d = x.shape
    n = b * t
    blk = MOE_BLOCK_ROWS
    n_blk = (2 * n) // blk + N_EXPERTS
    h, rec, cnt = _moe_route(x, g, shift, scale, wr, br)
    counts = cnt[0, :N_EXPERTS].astype(I32)
    padded = (counts + blk - 1) // blk * blk
    seg_end = jnp.cumsum(padded)
    seg_start = seg_end - padded
    rec2 = rec.reshape(n, ROUTER_PAD)
    col = lambda c: rec2[:, c].astype(I32)
    dest = jnp.concatenate([seg_start[col(REC_E0)] + col(REC_RANK0), seg_start[col(REC_E1)] + col(REC_RANK1)])
    tok = jnp.arange(n, dtype=I32)
    row_tok = jnp.zeros((n_blk * blk,), I32).at[dest].set(jnp.concatenate([tok, tok]))
    blk_expert = jnp.minimum(jnp.searchsorted(seg_end, jnp.arange(n_blk, dtype=I32) * blk, side="right"),
                             N_EXPERTS - 1).astype(I32)
    n_used = (seg_end[-1:] // blk).astype(I32)
    y = _moe_experts(blk_expert, n_used, row_tok, h.reshape(n, d), w1, w2, layer)
    return _moe_combine(dest, y, x, gate, rec, gf, final_norm)


def kernel(x_prompt, x_sample, cache_k, cache_v, cache_kidx, state_h, state_conv, page_table,
           c_prompt, c_sample, ada_w, ada_b, norm_mix_g, norm_ffn_g, norm_final_g,
           attn_w_in, attn_w_out, rec_w_in, rec_conv_w, rec_conv_b, rec_w_a, rec_b_a,
           rec_w_x, rec_b_x, rec_lambda, rec_w_out, moe_w_group, moe_b_group,
           moe_w_expert, moe_b_expert, moe_w1, moe_w2):
    b, t, d = x_prompt.shape
    db = x_sample.shape[0]
    n_pages = page_table.shape[1]
    past = n_pages * PAGE_SIZE
    n_pool = cache_k.shape[1]

    n_cond = b + db
    cond_rows = -(-n_cond // SUBLANES) * SUBLANES
    c_all = jnp.concatenate([c_prompt, c_sample, jnp.zeros((cond_rows - n_cond, d), F32)], axis=0)
    mod = _adaln(c_all, ada_w, ada_b)

    tables_p = _rope_tables(jnp.arange(t, dtype=I32))
    tables_s = _rope_tables(jnp.full((db,), past, I32))
    cache_kT = cache_k.transpose(0, 1, 3, 4, 2).reshape(-1, KV_WIDTH, PAGE_SIZE)
    cache_vT = cache_v.transpose(0, 1, 3, 4, 2).reshape(-1, KV_WIDTH, PAGE_SIZE)
    cache_kiT = cache_kidx.transpose(0, 1, 3, 2).reshape(-1, IDX_DIM, PAGE_SIZE)
    moe_w1f = moe_w1.reshape(-1, d, 2 * D_EXPERT)
    moe_w2f = moe_w2.reshape(-1, D_EXPERT, d)
    row_vec = lambda v: v.reshape(1, -1)
    head_grp = jnp.arange(N_HEADS)[:, None] // Q_PER_KV == jnp.arange(KV_WIDTH)[None, :] // HEAD_DIM

    xp = x_prompt
    xs = x_sample.reshape(1, db, d)
    outs = {name: [] for name in ("k_p", "v_p", "ki_p", "h_p", "cv_p", "k_s", "v_s", "ki_s", "h_s", "cv_s")}
    for l in range(DEPTH):
        parts = [mod[l, :, i * d:(i + 1) * d] for i in range(6)]
        sh1p, sc1p, g1p, sh2p, sc2p, g2p = [m[:b].reshape(b, 1, d) for m in parts]
        sh1s, sc1s, g1s, sh2s, sc2s, g2s = [m[b:n_cond].reshape(1, db, d) for m in parts]
        g_mix = row_vec(norm_mix_g[l])
        if l % 2 == 0:
            a = l // 2
            w_in = jnp.pad(attn_w_in[a], ((0, 0), (0, ATTN_PROJ_PAD - ATTN_PROJ))).astype(BF16)
            w_out = attn_w_out[a].astype(BF16)
            kT, vT, kiT, qT, qiT, wiT, kb, vTb, kib = _attn_proj_prompt(xp, g_mix, sh1p, sc1p, w_in, tables_p)
            op = _dsa_prompt(qT, qiT, wiT, kb, vTb, kib)
            outs["k_p"].append(kT.reshape(b, N_KV_HEADS, HEAD_DIM, t).transpose(0, 3, 1, 2))
            outs["v_p"].append(vT.reshape(b, N_KV_HEADS, HEAD_DIM, t).transpose(0, 3, 1, 2))
            outs["ki_p"].append(kiT.transpose(0, 2, 1))
            xp = _out_proj(op, w_out, xp, g1p)

            q, k, v, qi, ki, wi = _attn_proj_sample(xs, g_mix, sh1s, sc1s, w_in, tables_s)
            bias = _dsa_sample_select(page_table, qi.reshape(db, IDX_HEADS, IDX_DIM).astype(BF16),
                                      wi.reshape(db, IDX_HEADS, 1), cache_kiT,
                                      ki.reshape(db, 1, IDX_DIM), a, n_pool)
            q_bd = jnp.where(head_grp[None], jnp.tile(q.reshape(db, N_HEADS, HEAD_DIM), (1, 1, N_KV_HEADS)),
                             0.0).astype(BF16)
            os_ = _dsa_sample_attend(page_table, q_bd, bias, cache_kT, cache_vT,
                                     k.reshape(db, 1, KV_WIDTH), v.reshape(db, 1, KV_WIDTH), a, n_pool)
            outs["k_s"].append(k.reshape(db, 1, N_KV_HEADS, HEAD_DIM))
            outs["v_s"].append(v.reshape(db, 1, N_KV_HEADS, HEAD_DIM))
            outs["ki_s"].append(ki.reshape(db, 1, IDX_DIM))
            xs = _out_proj(os_.reshape(1, db, Q_WIDTH).astype(BF16), w_out, xs, g1s)
        else:
            r = l // 2
            w_in = rec_w_in[r].astype(BF16)
            w_out = rec_w_out[r].astype(BF16)
            rec = (rec_conv_w[r], row_vec(rec_conv_b[r]), rec_w_a[r].astype(BF16), row_vec(rec_b_a[r]),
                   rec_w_x[r].astype(BF16), row_vec(rec_b_x[r]), row_vec(rec_lambda[r]))
            gate, xb = _rec_proj(xp, g_mix, sh1p, sc1p, w_in)
            y, h_t, buf = _rglru_prompt(gate, xb, *rec)
            outs["h_p"].append(h_t.reshape(b, D_RNN))
            outs["cv_p"].append(buf)
            xp = _out_proj(y, w_out, xp, g1p)

            gate, xb = _rec_proj(xs, g_mix, sh1s, sc1s, w_in)
            y, h_t, buf = _rglru_sample(gate[0], xb[0], state_conv[r].swapaxes(0, 1), state_h[r], *rec)
            outs["h_s"].append(h_t)
            outs["cv_s"].append(buf.swapaxes(0, 1))
            xs = _out_proj(y.reshape(1, db, D_RNN), w_out, xs, g1s)

        wr = jnp.concatenate([moe_w_group[l], moe_w_expert[l],
                              jnp.zeros((d, ROUTER_PAD - N_GROUPS - N_EXPERTS), F32)], axis=1).astype(BF16)
        br = jnp.concatenate([moe_b_group[l], moe_b_expert[l],
                              jnp.zeros((ROUTER_PAD - N_GROUPS - N_EXPERTS,), F32)]).reshape(1, ROUTER_PAD)
        last = l == DEPTH - 1
        moe_args = (wr, br, moe_w1f, moe_w2f, l, row_vec(norm_final_g), last)
        xp = _moe_sorted(xp, row_vec(norm_ffn_g[l]), sh2p, sc2p, g2p, *moe_args)
        xs = _moe(xs, row_vec(norm_ffn_g[l]), sh2s, sc2s, g2s, *moe_args)

    st = lambda name: jnp.stack(outs[name])
    return (xp, xs.reshape(db, 1, d), st("k_p"), st("v_p"), st("ki_p"), st("h_p"), st("cv_p"),
            st("k_s"), st("v_s"), st("ki_s"), st("h_s"), st("cv_s"))
```

```python
import functools
import math

import jax
import jax.numpy as jnp
from jax import lax
from jax.experimental import pallas as pl
from jax.experimental.pallas import tpu as pltpu

F32 = jnp.float32
BF16 = jnp.bfloat16
I32 = jnp.int32

D_MODEL = 1024
DEPTH = 4
PAGE_SIZE = 128
N_HEADS = 16
HEAD_DIM = 64
N_KV_HEADS = 4
Q_PER_KV = N_HEADS // N_KV_HEADS
IDX_HEADS = 8
IDX_DIM = 64
IDX_SCALE = (IDX_HEADS * IDX_DIM) ** -0.5
TOPK_MAX = 256
ROPE_THETA = 500000.0
ROPE_FRACTION = 4
Q_WIDTH = N_HEADS * HEAD_DIM
KV_WIDTH = N_KV_HEADS * HEAD_DIM
QI_WIDTH = IDX_HEADS * IDX_DIM
ATTN_PROJ = Q_WIDTH + 2 * KV_WIDTH + QI_WIDTH + IDX_DIM + IDX_HEADS
D_RNN = D_MODEL
RG_BLOCKS = 8
RG_BLOCK_W = D_RNN // RG_BLOCKS
CONV_W = 4
RG_C = 8.0
N_GROUPS = 4
EXPERTS_PER_GROUP = 8
N_EXPERTS = N_GROUPS * EXPERTS_PER_GROUP
D_EXPERT = 256
EPS = 1e-6

LANES = 128
SUBLANES = 8
VMEM_LIMIT_BYTES = 56 * 1024 * 1024

ATTN_PROJ_PAD = -(-ATTN_PROJ // LANES) * LANES
ROUTER_PAD = LANES
Q_TILE = LANES
KEY_CHUNK = 512
ROW_TILE = 512
MOE_ROW_TILE = 1024
MOE_BLOCK_ROWS = 256
GATHER_SLOTS = 3
SCAN_CHUNK = 512
PAGE_GROUP = 8
BF16_SUBLANES = 16
HEAD_AUG = HEAD_DIM + BF16_SUBLANES
BOUND_SLACK = 1.02
DENOM_FLOOR = 2.0 ** -60
DENOM_CEIL = 2.0 ** 60
INT_MIN = -(2 ** 31)
INT_MAX = 2 ** 31 - 1
NEG = -1e30


def _cparams(n_axes):
    return pltpu.CompilerParams(dimension_semantics=("arbitrary",) * n_axes,
                                vmem_limit_bytes=VMEM_LIMIT_BYTES)


def _modulated_norm(x, g, shift, scale):
    xn = x * lax.rsqrt(jnp.mean(x * x, axis=-1, keepdims=True) + EPS)
    return (xn * g) * (1.0 + scale) + shift


def _rope_group(xs, cos, sa, sb):
    return xs * cos + pltpu.roll(xs, LANES - 8, 1) * sa + pltpu.roll(xs, 8, 1) * sb


def _float_order_key(s):
    bits = lax.bitcast_convert_type(s, I32)
    return bits ^ ((bits >> 31) & INT_MAX)


def _mod_spec(arr, tm):
    if arr.shape[1] == 1:
        return pl.BlockSpec((1, 1, arr.shape[2]), lambda b, i, *_: (b, 0, 0))
    return pl.BlockSpec((1, tm, arr.shape[2]), lambda b, i, *_: (b, i, 0))


def _adaln_kernel(c_ref, w_ref, b_ref, o_ref):
    c = c_ref[...]
    s = (c * jax.nn.sigmoid(c)).astype(BF16)
    o_ref[0] = jnp.dot(s, w_ref[0].astype(BF16), preferred_element_type=F32) + b_ref[0]


def _adaln(c_all, ada_w, ada_b):
    depth, d, n = ada_w.shape
    rows = c_all.shape[0]
    tn = 1536
    return pl.pallas_call(
        _adaln_kernel,
        grid=(depth, n // tn),
        in_specs=[pl.BlockSpec((rows, d), lambda l, j: (0, 0)),
                  pl.BlockSpec((1, d, tn), lambda l, j: (l, 0, j)),
                  pl.BlockSpec((1, 1, tn), lambda l, j: (l, 0, j))],
        out_specs=pl.BlockSpec((1, rows, tn), lambda l, j: (l, 0, j)),
        out_shape=jax.ShapeDtypeStruct((depth, rows, n), F32),
        compiler_params=_cparams(2),
        name="adaln",
    )(c_all, ada_w, ada_b.reshape(depth, 1, n))


def _rope_tables(pos):
    rot = HEAD_DIM // ROPE_FRACTION
    half = rot // 2
    inv = jnp.exp(-math.log(ROPE_THETA) * jnp.arange(half, dtype=F32) * (2.0 / rot))
    ang = pos.astype(F32)[:, None] * inv[None, :]
    cos, sin = jnp.cos(ang), jnp.sin(ang)
    r = pos.shape[0]
    z = lambda w: jnp.zeros((r, w), F32)
    cos64 = jnp.concatenate([cos, cos, jnp.ones((r, HEAD_DIM - rot), F32)], axis=1)
    sa64 = jnp.concatenate([-sin, z(HEAD_DIM - half)], axis=1)
    sb64 = jnp.concatenate([z(half), sin, z(HEAD_DIM - rot)], axis=1)
    two = lambda t: jnp.concatenate([t, t], axis=1)
    return two(cos64), two(sa64), two(sb64)


def _attn_proj_prompt_kernel(x_ref, g_ref, sh_ref, sc_ref, w_ref, cos_ref, sa_ref, sb_ref,
                             kT_ref, vT_ref, kiT_ref, qT_ref, qiT_ref, wiT_ref, kb_ref, vTb_ref, kib_ref,
                             kmax_ref):
    h = _modulated_norm(x_ref[0], g_ref[...], sh_ref[0], sc_ref[0]).astype(BF16)
    acc = jnp.dot(h, w_ref[...], preferred_element_type=F32)
    cos, sa, sb = cos_ref[...], sa_ref[...], sb_ref[...]
    n_qb = acc.shape[0] // Q_TILE
    grp = lambda off, j: acc[:, off + j * LANES: off + (j + 1) * LANES]

    tm = acc.shape[0]
    one_hot_row = lambda shape, axis: jnp.where(lax.broadcasted_iota(I32, shape, axis) == 0, 1.0, 0.0).astype(BF16)

    @pl.when(pl.program_id(1) == 0)
    def _():
        kmax_ref[...] = jnp.zeros(kmax_ref.shape, F32)

    for j in range(KV_WIDTH // LANES):
        kj = _rope_group(grp(Q_WIDTH, j), cos, sa, sb)
        kjT = kj.T
        kT_ref[0, j * LANES:(j + 1) * LANES, :] = kjT
        sq = kjT * kjT
        for hh in range(2):
            n = 2 * j + hh
            norm2 = jnp.sum(sq[hh * HEAD_DIM:(hh + 1) * HEAD_DIM, :], axis=0, keepdims=True)
            kmax_ref[n] = jnp.maximum(kmax_ref[n], jnp.max(norm2, axis=1, keepdims=True))
            kb_ref[0, n, :, 0:HEAD_DIM] = kj[:, hh * HEAD_DIM:(hh + 1) * HEAD_DIM].astype(BF16)
            kb_ref[0, n, :, HEAD_DIM:HEAD_AUG] = one_hot_row((tm, BF16_SUBLANES), 1)
    for j in range(Q_WIDTH // LANES):
        qjT = (_rope_group(grp(0, j), cos, sa, sb) * (HEAD_DIM ** -0.5)).T
        sq = qjT * qjT
        qjTb = qjT.astype(BF16)
        for hh in range(2):
            n, g = divmod(2 * j + hh, Q_PER_KV)
            norm2 = jnp.sum(sq[hh * HEAD_DIM:(hh + 1) * HEAD_DIM, :], axis=0, keepdims=True)
            shift = -BOUND_SLACK * jnp.sqrt(norm2 * kmax_ref[n])
            shift = jnp.concatenate([shift, jnp.zeros((BF16_SUBLANES - 1, tm), F32)], axis=0).astype(BF16)
            for jb in range(n_qb):
                qT_ref[0, jb, n, 0:HEAD_DIM, g * Q_TILE:(g + 1) * Q_TILE] = (
                    qjTb[hh * HEAD_DIM:(hh + 1) * HEAD_DIM, jb * Q_TILE:(jb + 1) * Q_TILE])
                qT_ref[0, jb, n, HEAD_DIM:HEAD_AUG, g * Q_TILE:(g + 1) * Q_TILE] = (
                    shift[:, jb * Q_TILE:(jb + 1) * Q_TILE])
    for j in range(KV_WIDTH // LANES):
        vjT = grp(Q_WIDTH + KV_WIDTH, j).T
        vT_ref[0, j * LANES:(j + 1) * LANES, :] = vjT
        for hh in range(2):
            n = 2 * j + hh
            vTb_ref[0, 0, n * HEAD_AUG:n * HEAD_AUG + HEAD_DIM, :] = (
                vjT[hh * HEAD_DIM:(hh + 1) * HEAD_DIM, :].astype(BF16))
            vTb_ref[0, 0, n * HEAD_AUG + HEAD_DIM:(n + 1) * HEAD_AUG, :] = one_hot_row((BF16_SUBLANES, tm), 0)
    off_qi = Q_WIDTH + 2 * KV_WIDTH
    for j in range(QI_WIDTH // LANES):
        qijT = _rope_group(grp(off_qi, j), cos, sa, sb).T.astype(BF16)
        for hh in range(2):
            head = 2 * j + hh
            for jb in range(n_qb):
                qiT_ref[0, jb, :, head * Q_TILE:(head + 1) * Q_TILE] = (
                    qijT[hh * IDX_DIM:(hh + 1) * IDX_DIM, jb * Q_TILE:(jb + 1) * Q_TILE])
    last = grp(off_qi + QI_WIDTH, 0)
    kir = _rope_group(last, cos, sa, sb)
    kiT_ref[0] = kir.T[:IDX_DIM, :]
    kib_ref[0] = kir[:, :IDX_DIM].astype(BF16)
    lastT = last.T
    for jb in range(n_qb):
        wiT_ref[0, jb] = lastT[IDX_DIM:IDX_DIM + IDX_HEADS, jb * Q_TILE:(jb + 1) * Q_TILE]


def _attn_proj_prompt(x, g, shift, scale, w_pad, tables):
    b, t, d = x.shape
    tm = ROW_TILE
    nq = t // Q_TILE
    grid = (b, t // tm)
    row = lambda width: pl.BlockSpec((1, tm, width), lambda bi, i: (bi, i, 0))
    col = lambda width: pl.BlockSpec((1, width, tm), lambda bi, i: (bi, 0, i))
    tab = pl.BlockSpec((tm, LANES), lambda bi, i: (i, 0))
    out_shape = (
        jax.ShapeDtypeStruct((b, KV_WIDTH, t), F32),
        jax.ShapeDtypeStruct((b, KV_WIDTH, t), F32),
        jax.ShapeDtypeStruct((b, IDX_DIM, t), F32),
        jax.ShapeDtypeStruct((b, nq, N_KV_HEADS, HEAD_AUG, Q_PER_KV * Q_TILE), BF16),
        jax.ShapeDtypeStruct((b, nq, IDX_DIM, IDX_HEADS * Q_TILE), BF16),
        jax.ShapeDtypeStruct((b, nq, IDX_HEADS, Q_TILE), F32),
        jax.ShapeDtypeStruct((b, N_KV_HEADS, t, HEAD_AUG), BF16),
        jax.ShapeDtypeStruct((b, t // tm, N_KV_HEADS * HEAD_AUG, tm), BF16),
        jax.ShapeDtypeStruct((b, t, IDX_DIM), BF16),
    )
    nqb = tm // Q_TILE
    out_specs = (
        col(KV_WIDTH), col(KV_WIDTH), col(IDX_DIM),
        pl.BlockSpec((1, nqb, N_KV_HEADS, HEAD_AUG, Q_PER_KV * Q_TILE), lambda bi, i: (bi, i, 0, 0, 0)),
        pl.BlockSpec((1, nqb, IDX_DIM, IDX_HEADS * Q_TILE), lambda bi, i: (bi, i, 0, 0)),
        pl.BlockSpec((1, nqb, IDX_HEADS, Q_TILE), lambda bi, i: (bi, i, 0, 0)),
        pl.BlockSpec((1, N_KV_HEADS, tm, HEAD_AUG), lambda bi, i: (bi, 0, i, 0)),
        pl.BlockSpec((1, 1, N_KV_HEADS * HEAD_AUG, tm), lambda bi, i: (bi, i, 0, 0)),
        row(IDX_DIM),
    )
    return pl.pallas_call(
        _attn_proj_prompt_kernel,
        grid=grid,
        in_specs=[row(d), pl.BlockSpec((1, d), lambda bi, i: (0, 0)),
                  _mod_spec(shift, tm), _mod_spec(scale, tm),
                  pl.BlockSpec(w_pad.shape, lambda bi, i: (0, 0)), tab, tab, tab],
        out_specs=out_specs,
        out_shape=out_shape,
        scratch_shapes=[pltpu.VMEM((N_KV_HEADS, 1, 1), F32)],
        compiler_params=_cparams(2),
        name="attn_proj_prompt",
    )(x, g, shift, scale, w_pad, *tables)


def _attn_proj_sample_kernel(x_ref, g_ref, sh_ref, sc_ref, w_ref, cos_ref, sa_ref, sb_ref,
                             q_ref, k_ref, v_ref, qi_ref, ki_ref, wi_ref):
    h = _modulated_norm(x_ref[0], g_ref[...], sh_ref[0], sc_ref[0]).astype(BF16)
    acc = jnp.dot(h, w_ref[...], preferred_element_type=F32)
    cos, sa, sb = cos_ref[...], sa_ref[...], sb_ref[...]
    grp = lambda off, j: acc[:, off + j * LANES: off + (j + 1) * LANES]
    for j in range(Q_WIDTH // LANES):
        q_ref[:, j * LANES:(j + 1) * LANES] = _rope_group(grp(0, j), cos, sa, sb) * (HEAD_DIM ** -0.5)
    for j in range(KV_WIDTH // LANES):
        k_ref[:, j * LANES:(j + 1) * LANES] = _rope_group(grp(Q_WIDTH, j), cos, sa, sb)
        v_ref[:, j * LANES:(j + 1) * LANES] = grp(Q_WIDTH + KV_WIDTH, j)
    off_qi = Q_WIDTH + 2 * KV_WIDTH
    for j in range(QI_WIDTH // LANES):
        qi_ref[:, j * LANES:(j + 1) * LANES] = _rope_group(grp(off_qi, j), cos, sa, sb)
    last = grp(off_qi + QI_WIDTH, 0)
    ki_ref[...] = _rope_group(last, cos, sa, sb)[:, :IDX_DIM]
    wi_ref[...] = last[:, IDX_DIM:IDX_DIM + IDX_HEADS]


def _attn_proj_sample(x, g, shift, scale, w_pad, tables):
    _, rows, d = x.shape
    full = lambda shape: pl.BlockSpec(shape, lambda i: (0,) * len(shape))
    widths = (Q_WIDTH, KV_WIDTH, KV_WIDTH, QI_WIDTH, IDX_DIM, IDX_HEADS)
    return pl.pallas_call(
        _attn_proj_sample_kernel,
        grid=(1,),
        in_specs=[full((1, rows, d)), full((1, d)), full((1, rows, d)), full((1, rows, d)),
                  full(w_pad.shape), full((rows, LANES)), full((rows, LANES)), full((rows, LANES))],
        out_specs=tuple(full((rows, w)) for w in widths),
        out_shape=tuple(jax.ShapeDtypeStruct((rows, w), F32) for w in widths),
        compiler_params=_cparams(1),
        name="attn_proj_sample",
    )(x, g, shift, scale, w_pad, *tables)


def _dsa_prompt_kernel(qT_ref, qiT_ref, wiT_ref, kb_ref, vT_ref, kib_ref, o_ref,
                       keys_ref, acc_ref, acc2_ref, m_ref, l_ref, j_ref, *, topk, idx_bits):
    kc = KEY_CHUNK
    qb = pl.program_id(1)
    n_chunks = (qb * Q_TILE + Q_TILE + kc - 1) // kc
    rows = lax.broadcasted_iota(I32, (kc, Q_TILE), 0)
    q_pos = qb * Q_TILE + lax.broadcasted_iota(I32, (kc, Q_TILE), 1)

    w_heads = wiT_ref[0, 0] * IDX_SCALE
    qiT = qiT_ref[0, 0]

    def score_chunk(c, carry):
        off = pl.multiple_of(c * kc, kc)
        dots = jnp.dot(kib_ref[0, pl.ds(off, kc), :], qiT, preferred_element_type=F32)
        s = jnp.zeros((kc, Q_TILE), F32)
        for h in range(IDX_HEADS):
            s = s + jnp.maximum(dots[:, h * Q_TILE:(h + 1) * Q_TILE], 0.0) * w_heads[h:h + 1, :]
        keys_ref[pl.ds(off, kc), :] = jnp.where(rows + off <= q_pos, _float_order_key(s), INT_MIN)
        return carry

    lax.fori_loop(0, n_chunks, score_chunk, 0)

    def count(pred):
        def body(c, cnt):
            off = pl.multiple_of(c * kc, kc)
            hit = jnp.where(pred(keys_ref[pl.ds(off, kc), :], rows + off), 1, 0)
            return cnt + hit.reshape(kc // SUBLANES, SUBLANES, Q_TILE).sum(axis=0)
        cnt = lax.fori_loop(0, n_chunks, body, jnp.zeros((SUBLANES, Q_TILE), I32))
        return cnt.sum(axis=0, keepdims=True)

    def thr_bit(i, state):
        thr, n_ge = state
        cand = thr + lax.shift_left(jnp.int32(1), 31 - i)
        n_cand = count(lambda k, s: k >= cand)
        ok = n_cand >= topk
        return jnp.where(ok, cand, thr), jnp.where(ok, n_cand, n_ge)

    n_all = jnp.full((1, Q_TILE), 1, I32) * (n_chunks * kc)
    thr, n_ge = lax.fori_loop(0, 32, thr_bit, (jnp.full((1, Q_TILE), INT_MIN, I32), n_all))

    j_ref[...] = jnp.full((1, Q_TILE), INT_MAX, I32)
    tie_split = jnp.max(jnp.where((n_ge > topk) & (thr > INT_MIN), 1, 0))

    @pl.when(tie_split > 0)
    def _():
        need = topk - count(lambda k, s: k > thr)

        def idx_bit(i, j0):
            cand = j0 + lax.shift_left(jnp.int32(1), idx_bits - 1 - i)
            taken_before = count(lambda k, s: (k == thr) & (s < cand))
            return jnp.where(taken_before < need, cand, j0)
        j_ref[...] = lax.fori_loop(0, idx_bits, idx_bit, jnp.zeros((1, Q_TILE), I32))

    j_sel = j_ref[...]

    def selection_bias(c):
        off = pl.multiple_of(c * kc, kc)
        k = keys_ref[pl.ds(off, kc), :]
        s_idx = rows + off
        sel = ((k > thr) | ((k == thr) & (s_idx <= j_sel))) & (s_idx <= q_pos)
        bias = jnp.where(sel, 0.0, NEG)
        return off, jnp.concatenate([bias] * Q_PER_KV, axis=1)

    acc_ref[...] = jnp.zeros(acc_ref.shape, F32)

    def attend_chunk(c, carry):
        off, bias = selection_bias(c)
        for n in range(N_KV_HEADS):
            s = jnp.dot(kb_ref[0, n, pl.ds(off, kc), :], qT_ref[0, 0, n], preferred_element_type=F32) + bias
            acc_ref[n] += jnp.dot(vT_ref[0, c, n * HEAD_AUG:(n + 1) * HEAD_AUG, :], jnp.exp(s).astype(BF16),
                                  preferred_element_type=F32)
        return carry

    lax.fori_loop(0, n_chunks, attend_chunk, 0)
    denom = jnp.concatenate([acc_ref[n][HEAD_DIM:HEAD_DIM + 1, :] for n in range(N_KV_HEADS)], axis=0)
    healthy = (jnp.min(denom) >= DENOM_FLOOR) & (jnp.max(denom) <= DENOM_CEIL)

    @pl.when(jnp.logical_not(healthy))
    def _():
        m_ref[...] = jnp.full(m_ref.shape, NEG, F32)
        l_ref[...] = jnp.zeros(l_ref.shape, F32)
        acc2_ref[...] = jnp.zeros(acc2_ref.shape, F32)

        def attend_chunk_online(c, carry):
            off, bias = selection_bias(c)
            for n in range(N_KV_HEADS):
                s = jnp.dot(kb_ref[0, n, pl.ds(off, kc), 0:HEAD_DIM], qT_ref[0, 0, n, 0:HEAD_DIM, :],
                            preferred_element_type=F32) + bias
                m_prev = m_ref[n]
                m_new = jnp.maximum(m_prev, s.max(axis=0, keepdims=True))
                alpha = jnp.exp(m_prev - m_new)
                p = jnp.exp(s - m_new)
                l_ref[n] = alpha * l_ref[n] + p.sum(axis=0, keepdims=True)
                pv = jnp.dot(vT_ref[0, c, n * HEAD_AUG:n * HEAD_AUG + HEAD_DIM, :], p.astype(BF16),
                             preferred_element_type=F32)
                acc2_ref[n] = alpha * acc2_ref[n] + pv
                m_ref[n] = m_new
            return carry

        lax.fori_loop(0, n_chunks, attend_chunk_online, 0)
        for n in range(N_KV_HEADS):
            acc_ref[n, 0:HEAD_DIM, :] = acc2_ref[n]
            acc_ref[n, HEAD_DIM:HEAD_DIM + 1, :] = l_ref[n]

    for n in range(N_KV_HEADS):
        on = acc_ref[n, 0:HEAD_DIM, :] / acc_ref[n, HEAD_DIM:HEAD_DIM + 1, :]
        for gp in range(Q_PER_KV // 2):
            pair = jnp.concatenate([on[:, (2 * gp) * Q_TILE:(2 * gp + 1) * Q_TILE],
                                    on[:, (2 * gp + 1) * Q_TILE:(2 * gp + 2) * Q_TILE]], axis=0)
            col = (n * Q_PER_KV // 2 + gp) * LANES
            o_ref[0, :, col:col + LANES] = pair.T.astype(BF16)


def _dsa_prompt(qT, qiT, wiT, kb, vT, kib):
    b, nq = qT.shape[:2]
    t = kb.shape[2]
    topk = min(TOPK_MAX, t // 4)
    idx_bits = max(1, (t - 1).bit_length())
    per_q = lambda shape: pl.BlockSpec((1, 1) + shape, lambda bi, qi: (bi, qi) + (0,) * len(shape))
    per_b = lambda shape: pl.BlockSpec((1,) + shape, lambda bi, qi: (bi,) + (0,) * len(shape))
    return pl.pallas_call(
        functools.partial(_dsa_prompt_kernel, topk=topk, idx_bits=idx_bits),
        grid=(b, nq),
        in_specs=[per_q(qT.shape[2:]), per_q(qiT.shape[2:]), per_q(wiT.shape[2:]),
                  per_b(kb.shape[1:]), per_b(vT.shape[1:]), per_b(kib.shape[1:])],
        out_specs=pl.BlockSpec((1, Q_TILE, Q_WIDTH), lambda bi, qi: (bi, qi, 0)),
        out_shape=jax.ShapeDtypeStruct((b, t, Q_WIDTH), BF16),
        scratch_shapes=[pltpu.VMEM((t, Q_TILE), I32),
                        pltpu.VMEM((N_KV_HEADS, HEAD_AUG, Q_PER_KV * Q_TILE), F32),
                        pltpu.VMEM((N_KV_HEADS, HEAD_DIM, Q_PER_KV * Q_TILE), F32),
                        pltpu.VMEM((N_KV_HEADS, 1, Q_PER_KV * Q_TILE), F32),
                        pltpu.VMEM((N_KV_HEADS, 1, Q_PER_KV * Q_TILE), F32),
                        pltpu.VMEM((1, Q_TILE), I32)],
        compiler_params=_cparams(2),
        name="dsa_prompt",
    )(qT, qiT, wiT, kb, vT, kib)


def _fetch_pages_start(pt_ref, seq, layer_base, caches, bufs, sems, n_pages):
    def body(p, carry):
        page = layer_base + pt_ref[seq * n_pages + p]
        for cache, buf, sem in zip(caches, bufs, sems):
            pltpu.make_async_copy(cache.at[page], buf.at[p], sem).start()
        return carry
    lax.fori_loop(0, n_pages, body, 0)


def _fetch_pages_wait(caches, bufs, sems, n_pages):
    for cache, buf, sem in zip(caches, bufs, sems):
        pltpu.make_async_copy(cache.at[pl.ds(0, n_pages)], buf, sem).wait()


def _fetch_pages_pipelined(pt_ref, layer_base, caches, bufs, sems, n_pages):
    step, n_steps = pl.program_id(0), pl.num_programs(0)
    slot = step % 2
    at = lambda sl: ([b.at[sl] for b in bufs], [s.at[sl] for s in sems])

    @pl.when(step == 0)
    def _():
        _fetch_pages_start(pt_ref, 0, layer_base, caches, *at(0), n_pages)

    @pl.when(step + 1 < n_steps)
    def _():
        _fetch_pages_start(pt_ref, step + 1, layer_base, caches, *at(1 - slot), n_pages)

    _fetch_pages_wait(caches, *at(slot), n_pages)
    return slot


def _pages_bf16(buf, slot, first, count):
    return jnp.concatenate([buf[slot, first + j].astype(BF16) for j in range(count)], axis=1)


def _dsa_sample_scores_kernel(pt_ref, qi_ref, w_ref, kin_ref, kiT_hbm, keys_ref, buf_ref, sem,
                              *, n_pages, layer_base):
    slot = _fetch_pages_pipelined(pt_ref, layer_base, [kiT_hbm], [buf_ref], [sem], n_pages)
    qi = qi_ref[0]
    w = w_ref[0] * IDX_SCALE
    score = lambda dots: jnp.sum(jnp.maximum(dots, 0.0) * w, axis=0, keepdims=True)
    for g in range(n_pages // PAGE_GROUP):
        kiT = _pages_bf16(buf_ref, slot, g * PAGE_GROUP, PAGE_GROUP)
        keys = _float_order_key(score(jnp.dot(qi, kiT, preferred_element_type=F32)))
        keys_ref[0, :, g * PAGE_GROUP * PAGE_SIZE:(g + 1) * PAGE_GROUP * PAGE_SIZE] = keys
    kn = kin_ref[0].astype(BF16).astype(F32)
    sn = score(jnp.sum(qi.astype(F32) * kn, axis=1, keepdims=True))
    lane = lax.broadcasted_iota(I32, (1, LANES), 1)
    keys_ref[0, :, n_pages * PAGE_SIZE:] = jnp.where(lane == 0, _float_order_key(sn), INT_MIN)


def _dsa_sample_scores(pt_flat, qi3, w3, ki_new, cache_kiT, layer, n_pool, n_pages):
    db = qi3.shape[0]
    n_lanes = n_pages * PAGE_SIZE + LANES
    per_b = lambda shape: pl.BlockSpec((1,) + shape, lambda b, pt: (b,) + (0,) * len(shape))
    grid_spec = pltpu.PrefetchScalarGridSpec(
        num_scalar_prefetch=1,
        grid=(db,),
        in_specs=[per_b((IDX_HEADS, IDX_DIM)), per_b((IDX_HEADS, 1)), per_b((1, IDX_DIM)),
                  pl.BlockSpec(memory_space=pl.ANY)],
        out_specs=per_b((1, n_lanes)),
        scratch_shapes=[pltpu.VMEM((2, n_pages, IDX_DIM, PAGE_SIZE), F32), pltpu.SemaphoreType.DMA((2,))],
    )
    return pl.pallas_call(
        functools.partial(_dsa_sample_scores_kernel, n_pages=n_pages, layer_base=layer * n_pool),
        grid_spec=grid_spec,
        out_shape=jax.ShapeDtypeStruct((db, 1, n_lanes), I32),
        compiler_params=_cparams(1),
        name="dsa_sample_scores",
    )(pt_flat, qi3, w3, ki_new, cache_kiT)


def _dsa_sample_select_kernel(keys_ref, bias_ref, *, n_keys, topk, idx_bits):
    keys = keys_ref[...]
    idx = lax.broadcasted_iota(I32, keys.shape, 1)
    count = lambda hit: jnp.sum(jnp.where(hit, 1, 0), axis=1, keepdims=True)
    keys = jnp.where(idx < n_keys, keys, INT_MIN)

    def thr_bit(i, thr):
        cand = thr + lax.shift_left(jnp.int32(1), 31 - i)
        return jnp.where(count(keys >= cand) >= topk, cand, thr)

    thr = lax.fori_loop(0, 32, thr_bit, jnp.full((keys.shape[0], 1), INT_MIN, I32))
    need = topk - count(keys > thr)

    def idx_bit(i, j0):
        cand = j0 + lax.shift_left(jnp.int32(1), idx_bits - 1 - i)
        return jnp.where(count((keys == thr) & (idx < cand)) < need, cand, j0)

    j_sel = lax.fori_loop(0, idx_bits, idx_bit, jnp.zeros((keys.shape[0], 1), I32))
    sel = ((keys > thr) | ((keys == thr) & (idx <= j_sel))) & (idx < n_keys)
    bias_ref[...] = jnp.where(sel, 0.0, NEG)


def _dsa_sample_select(keys, n_keys):
    topk = min(TOPK_MAX, n_keys // 4)
    idx_bits = max(1, (n_keys - 1).bit_length())
    full = pl.BlockSpec(keys.shape, lambda i: (0, 0))
    return pl.pallas_call(
        functools.partial(_dsa_sample_select_kernel, n_keys=n_keys, topk=topk, idx_bits=idx_bits),
        grid=(1,),
        in_specs=[full],
        out_specs=full,
        out_shape=jax.ShapeDtypeStruct(keys.shape, F32),
        compiler_params=_cparams(1),
        name="dsa_sample_select",
    )(keys)


def _dsa_sample_attend_kernel(pt_ref, q_ref, bias_ref, kn_ref, vn_ref, kT_hbm, vT_hbm, o_ref,
                              kbuf_ref, vbuf_ref, ksem, vsem, *, n_pages, layer_base):
    slot = _fetch_pages_pipelined(pt_ref, layer_base, [kT_hbm, vT_hbm], [kbuf_ref, vbuf_ref],
                                  [ksem, vsem], n_pages)
    q = q_ref[0]
    n_past = n_pages * PAGE_SIZE
    group = PAGE_GROUP * PAGE_SIZE
    scores = [jnp.dot(q, _pages_bf16(kbuf_ref, slot, g * PAGE_GROUP, PAGE_GROUP), preferred_element_type=F32)
              + bias_ref[0, :, g * group:(g + 1) * group] for g in range(n_pages // PAGE_GROUP)]
    kn = kn_ref[0].astype(BF16).astype(F32)
    vn = vn_ref[0].astype(BF16).astype(F32)
    sn = jnp.sum(q.astype(F32) * kn, axis=1, keepdims=True) + bias_ref[0, :, n_past:n_past + 1]
    m = sn
    for s in scores:
        m = jnp.maximum(m, s.max(axis=1, keepdims=True))
    pn = jnp.exp(sn - m)
    denom = pn
    acc = pn.astype(BF16).astype(F32) * vn
    for g, s in enumerate(scores):
        p = jnp.exp(s - m)
        denom = denom + p.sum(axis=1, keepdims=True)
        acc = acc + lax.dot_general(p.astype(BF16), _pages_bf16(vbuf_ref, slot, g * PAGE_GROUP, PAGE_GROUP),
                                    (((1,), (1,)), ((), ())), preferred_element_type=F32)
    out = acc / denom
    head_grp = lax.broadcasted_iota(I32, out.shape, 0) // Q_PER_KV
    lane_grp = lax.broadcasted_iota(I32, out.shape, 1) // HEAD_DIM
    out = jnp.where(head_grp == lane_grp, out, 0.0)
    o = out[:, 0:HEAD_DIM]
    for n in range(1, N_KV_HEADS):
        o = o + out[:, n * HEAD_DIM:(n + 1) * HEAD_DIM]
    o_ref[0] = o


def _dsa_sample_attend(pt_flat, q_bd, bias, k_new, v_new, cache_kT, cache_vT, layer, n_pool, n_pages):
    db = q_bd.shape[0]
    per_b = lambda shape: pl.BlockSpec((1,) + shape, lambda b, pt: (b,) + (0,) * len(shape))
    hbm = pl.BlockSpec(memory_space=pl.ANY)
    pages = pltpu.VMEM((2, n_pages, KV_WIDTH, PAGE_SIZE), F32)
    grid_spec = pltpu.PrefetchScalarGridSpec(
        num_scalar_prefetch=1,
        grid=(db,),
        in_specs=[per_b(q_bd.shape[1:]), per_b(bias.shape[1:]), per_b((1, KV_WIDTH)), per_b((1, KV_WIDTH)),
                  hbm, hbm],
        out_specs=per_b((N_HEADS, HEAD_DIM)),
        scratch_shapes=[pages, pages, pltpu.SemaphoreType.DMA((2,)), pltpu.SemaphoreType.DMA((2,))],
    )
    return pl.pallas_call(
        functools.partial(_dsa_sample_attend_kernel, n_pages=n_pages, layer_base=layer * n_pool),
        grid_spec=grid_spec,
        out_shape=jax.ShapeDtypeStruct((db, N_HEADS, HEAD_DIM), F32),
        compiler_params=_cparams(1),
        name="dsa_sample_attend",
    )(pt_flat, q_bd, bias, k_new, v_new, cache_kT, cache_vT)


def _out_proj_kernel(a_ref, w_ref, x_ref, gt_ref, o_ref):
    y = jnp.dot(a_ref[0], w_ref[...], preferred_element_type=F32)
    o_ref[0] = x_ref[0] + gt_ref[0] * y


def _out_proj(a, w, x, gate):
    b, t, d = x.shape
    tm = min(ROW_TILE, t)
    row = lambda width: pl.BlockSpec((1, tm, width), lambda bi, i: (bi, i, 0))
    return pl.pallas_call(
        _out_proj_kernel,
        grid=(b, t // tm),
        in_specs=[row(a.shape[2]), pl.BlockSpec(w.shape, lambda bi, i: (0, 0)), row(d), _mod_spec(gate, tm)],
        out_specs=row(d),
        out_shape=jax.ShapeDtypeStruct((b, t, d), F32),
        compiler_params=_cparams(2),
        name="out_proj",
    )(a, w, x, gate)


def _gelu_tanh(x):
    return x * (0.5 * (1.0 + jnp.tanh(math.sqrt(2.0 / math.pi) * (x + 0.044715 * (x * x * x)))))


def _rec_proj_kernel(x_ref, g_ref, sh_ref, sc_ref, w_ref, gate_ref, xb_ref):
    h = _modulated_norm(x_ref[0], g_ref[...], sh_ref[0], sc_ref[0]).astype(BF16)
    acc = jnp.dot(h, w_ref[...], preferred_element_type=F32)
    gate_ref[0] = _gelu_tanh(acc[:, :D_RNN])
    xb_ref[0] = acc[:, D_RNN:]


def _rec_proj(x, g, shift, scale, w):
    b, t, d = x.shape
    tm = min(ROW_TILE, t)
    row = lambda width: pl.BlockSpec((1, tm, width), lambda bi, i: (bi, i, 0))
    return pl.pallas_call(
        _rec_proj_kernel,
        grid=(b, t // tm),
        in_specs=[row(d), pl.BlockSpec((1, d), lambda bi, i: (0, 0)), _mod_spec(shift, tm),
                  _mod_spec(scale, tm), pl.BlockSpec(w.shape, lambda bi, i: (0, 0))],
        out_specs=(row(D_RNN), row(D_RNN)),
        out_shape=(jax.ShapeDtypeStruct((b, t, D_RNN), F32),) * 2,
        compiler_params=_cparams(2),
        name="rec_proj",
    )(x, g, shift, scale, w)


def _rglru_gates(xc, wa_ref, ba, wx_ref, bx, lam):
    xcb = xc.astype(BF16)
    blk = lambda w_ref: jnp.concatenate(
        [jnp.dot(xcb[:, n * RG_BLOCK_W:(n + 1) * RG_BLOCK_W], w_ref[n], preferred_element_type=F32)
         for n in range(RG_BLOCKS)], axis=1)
    r = jax.nn.sigmoid(blk(wa_ref) + ba)
    i = jax.nn.sigmoid(blk(wx_ref) + bx)
    neg_lam = -lam
    softplus = jnp.maximum(neg_lam, 0.0) + jnp.log1p(jnp.exp(-jnp.abs(neg_lam)))
    log_a = -RG_C * r * softplus
    a = jnp.exp(log_a)
    u = jnp.sqrt(1.0 - a * a) * (i * xc)
    return a, u


def _rglru_prompt_kernel(gate_ref, xb_ref, cw_ref, cb_ref, wa_ref, ba_ref, wx_ref, bx_ref, lam_ref,
                         y_ref, h_ref, conv_ref, xcat_ref, a_ref, u_ref, hs_ref, hc_ref):
    tc = xb_ref.shape[1]
    c = pl.program_id(1)
    pad = SUBLANES

    @pl.when(c == 0)
    def _():
        xcat_ref[0:pad, :] = jnp.zeros((pad, D_RNN), F32)
        hc_ref[...] = jnp.zeros(hc_ref.shape, F32)

    @pl.when(c > 0)
    def _():
        xcat_ref[0:pad, :] = xcat_ref[tc:tc + pad, :]

    xcat_ref[pad:pad + tc, :] = xb_ref[0]
    cw = cw_ref[...]
    xc = xcat_ref[pad - 3:pad - 3 + tc, :] * cw[0:1, :]
    for j in range(1, CONV_W):
        xc = xc + xcat_ref[pad - 3 + j:pad - 3 + j + tc, :] * cw[j:j + 1, :]
    xc = cb_ref[...] + xc
    a, u = _rglru_gates(xc, wa_ref, ba_ref[...], wx_ref, bx_ref[...], lam_ref[...])
    a_ref[...] = a
    u_ref[...] = u

    def step(t, h):
        h = a_ref[pl.ds(t, 1), :] * h + u_ref[pl.ds(t, 1), :]
        hs_ref[pl.ds(t, 1), :] = h
        return h

    h = lax.fori_loop(0, tc, step, hc_ref[...], unroll=8)
    hc_ref[...] = h
    y_ref[0] = (hs_ref[...] * gate_ref[0]).astype(BF16)
    h_ref[0] = h
    conv_ref[0] = xcat_ref[pad + tc - (CONV_W - 1):pad + tc, :]


def _rglru_prompt(gate, xb, cw, cb, wa, ba, wx, bx, lam):
    b, t, d = xb.shape
    tc = min(SCAN_CHUNK, t)
    row = pl.BlockSpec((1, tc, d), lambda bi, i: (bi, i, 0))
    vec = pl.BlockSpec((1, d), lambda bi, i: (0, 0))
    full = lambda shape: pl.BlockSpec(shape, lambda bi, i: (0,) * len(shape))
    return pl.pallas_call(
        _rglru_prompt_kernel,
        grid=(b, t // tc),
        in_specs=[row, row, full(cw.shape), vec, full(wa.shape), vec, full(wx.shape), vec, vec],
        out_specs=(row, pl.BlockSpec((1, 1, d), lambda bi, i: (bi, 0, 0)),
                   pl.BlockSpec((1, CONV_W - 1, d), lambda bi, i: (bi, 0, 0))),
        out_shape=(jax.ShapeDtypeStruct((b, t, d), BF16), jax.ShapeDtypeStruct((b, 1, d), F32),
                   jax.ShapeDtypeStruct((b, CONV_W - 1, d), F32)),
        scratch_shapes=[pltpu.VMEM((tc + 2 * SUBLANES, d), F32), pltpu.VMEM((tc, d), F32),
                        pltpu.VMEM((tc, d), F32), pltpu.VMEM((tc, d), F32), pltpu.VMEM((1, d), F32)],
        compiler_params=_cparams(2),
        name="rglru_prompt",
    )(gate, xb, cw, cb, wa, ba, wx, bx, lam)


def _rglru_sample_kernel(gate_ref, xb_ref, buf_ref, h0_ref, cw_ref, cb_ref, wa_ref, ba_ref, wx_ref, bx_ref,
                         lam_ref, y_ref, h_ref, nbuf_ref):
    cw = cw_ref[...]
    xb = xb_ref[...]
    xc = buf_ref[0] * cw[0:1, :]
    for j in range(1, CONV_W - 1):
        xc = xc + buf_ref[j] * cw[j:j + 1, :]
    xc = cb_ref[...] + (xc + xb * cw[CONV_W - 1:CONV_W, :])
    a, u = _rglru_gates(xc, wa_ref, ba_ref[...], wx_ref, bx_ref[...], lam_ref[...])
    h = a * h0_ref[...] + u
    h_ref[...] = h
    y_ref[...] = (h * gate_ref[...]).astype(BF16)
    for j in range(CONV_W - 2):
        nbuf_ref[j] = buf_ref[j + 1]
    nbuf_ref[CONV_W - 2] = xb


def _rglru_sample(gate, xb, buf, h0, cw, cb, wa, ba, wx, bx, lam):
    rows, d = xb.shape
    full = lambda a: pl.BlockSpec(a.shape, lambda i: (0,) * a.ndim)
    args = (gate, xb, buf, h0, cw, cb, wa, ba, wx, bx, lam)
    return pl.pallas_call(
        _rglru_sample_kernel,
        grid=(1,),
        in_specs=[full(a) for a in args],
        out_specs=(full(xb), full(xb), full(buf)),
        out_shape=(jax.ShapeDtypeStruct((rows, d), BF16), jax.ShapeDtypeStruct((rows, d), F32),
                   jax.ShapeDtypeStruct(buf.shape, F32)),
        compiler_params=_cparams(1),
        name="rglru_sample",
    )(*args)


def _route(logits):
    lane = lax.broadcasted_iota(I32, logits.shape, 1)

    def first_max(v):
        m = jnp.max(v, axis=1, keepdims=True)
        return m, jnp.min(jnp.where(v == m, lane, LANES), axis=1, keepdims=True)

    is_grp = lane < N_GROUPS
    gm, grp = first_max(jnp.where(is_grp, logits, -jnp.inf))
    g_w = 1.0 / jnp.sum(jnp.where(is_grp, jnp.exp(logits - gm), 0.0), axis=1, keepdims=True)
    lo = N_GROUPS + EXPERTS_PER_GROUP * grp
    el = jnp.where((lane >= lo) & (lane < lo + EXPERTS_PER_GROUP), logits, -jnp.inf)
    m1, i1 = first_max(el)
    m2, i2 = first_max(jnp.where(lane == i1, -jnp.inf, el))
    e2 = jnp.exp(m2 - m1)
    den = 1.0 + e2
    return i1 - N_GROUPS, i2 - N_GROUPS, (1.0 / den) * g_w, (e2 / den) * g_w


def _moe_kernel(x_ref, g_ref, sh_ref, sc_ref, gt_ref, wr_ref, br_ref, w1_ref, w2_ref, gf_ref, o_ref,
                h_ref, acc_ref, e0_ref, e1_ref, c0_ref, c1_ref, *, final_norm):
    e = pl.program_id(2)

    @pl.when(e == 0)
    def _():
        h = _modulated_norm(x_ref[0], g_ref[...], sh_ref[0], sc_ref[0])
        h_ref[...] = h.astype(BF16)
        logits = jnp.dot(h_ref[...], wr_ref[...], preferred_element_type=F32) + br_ref[...]
        e0_ref[...], e1_ref[...], c0_ref[...], c1_ref[...] = _route(logits)
        acc_ref[...] = jnp.zeros(acc_ref.shape, F32)

    gu = jnp.dot(h_ref[...], w1_ref[0].astype(BF16), preferred_element_type=F32)
    gp, up = gu[:, :D_EXPERT], gu[:, D_EXPERT:]
    act = ((gp * jax.nn.sigmoid(gp)) * up).astype(BF16)
    y = jnp.dot(act, w2_ref[0].astype(BF16), preferred_element_type=F32)
    wt = jnp.where(e0_ref[...] == e, c0_ref[...], 0.0) + jnp.where(e1_ref[...] == e, c1_ref[...], 0.0)
    acc_ref[...] += y * wt

    @pl.when(e == N_EXPERTS - 1)
    def _():
        out = x_ref[0] + gt_ref[0] * acc_ref[...]
        if final_norm:
            out = (out * lax.rsqrt(jnp.mean(out * out, axis=-1, keepdims=True) + EPS)) * gf_ref[...]
        o_ref[0] = out


def _moe(x, g, shift, scale, gate, wr, br, w1, w2, layer, gf, final_norm):
    b, t, d = x.shape
    tm = min(MOE_ROW_TILE, t)
    row = pl.BlockSpec((1, tm, d), lambda bi, i, e: (bi, i, 0))
    vec = lambda width: pl.BlockSpec((1, width), lambda bi, i, e: (0, 0))
    col = lambda dt: pltpu.VMEM((tm, 1), dt)
    return pl.pallas_call(
        functools.partial(_moe_kernel, final_norm=final_norm),
        grid=(b, t // tm, N_EXPERTS),
        in_specs=[row, vec(d), _mod_spec(shift, tm), _mod_spec(scale, tm), _mod_spec(gate, tm),
                  pl.BlockSpec(wr.shape, lambda bi, i, e: (0, 0)), vec(ROUTER_PAD),
                  pl.BlockSpec((1, d, 2 * D_EXPERT), lambda bi, i, e: (layer * N_EXPERTS + e, 0, 0)),
                  pl.BlockSpec((1, D_EXPERT, d), lambda bi, i, e: (layer * N_EXPERTS + e, 0, 0)),
                  vec(d)],
        out_specs=row,
        out_shape=jax.ShapeDtypeStruct((b, t, d), F32),
        scratch_shapes=[pltpu.VMEM((tm, d), BF16), pltpu.VMEM((tm, d), F32),
                        col(I32), col(I32), col(F32), col(F32)],
        compiler_params=_cparams(3),
        name="moe",
    )(x, g, shift, scale, gate, wr, br, w1, w2, gf)


REC_E0, REC_E1, REC_RANK0, REC_RANK1, REC_C0, REC_C1 = range(6)


def _moe_route_kernel(x_ref, g_ref, sh_ref, sc_ref, wr_ref, br_ref, h_ref, rec_ref, cnt_ref,
                      tri_ref, carry_ref):
    tm = x_ref.shape[1]

    @pl.when((pl.program_id(0) == 0) & (pl.program_id(1) == 0))
    def _():
        carry_ref[...] = jnp.zeros(carry_ref.shape, F32)
        earlier = lax.broadcasted_iota(I32, (tm, tm), 0) > lax.broadcasted_iota(I32, (tm, tm), 1)
        tri_ref[...] = jnp.where(earlier, 1.0, 0.0).astype(BF16)

    h = _modulated_norm(x_ref[0], g_ref[...], sh_ref[0], sc_ref[0])
    h_ref[0] = h
    logits = jnp.dot(h.astype(BF16), wr_ref[...], preferred_element_type=F32) + br_ref[...]
    e0, e1, c0, c1 = _route(logits)
    lane = lax.broadcasted_iota(I32, logits.shape, 1)
    pick0, pick1 = lane == e0, lane == e1
    picks = jnp.where(pick0 | pick1, 1.0, 0.0)
    before = carry_ref[...] + jnp.dot(tri_ref[...], picks.astype(BF16), preferred_element_type=F32)
    rank0 = jnp.sum(jnp.where(pick0, before, 0.0), axis=1, keepdims=True)
    rank1 = jnp.sum(jnp.where(pick1, before, 0.0), axis=1, keepdims=True)
    carry_ref[...] = carry_ref[...] + jnp.sum(picks, axis=0, keepdims=True)
    cnt_ref[...] = carry_ref[...]
    rec = jnp.zeros(logits.shape, F32)
    for pos, val in ((REC_E0, e0.astype(F32)), (REC_E1, e1.astype(F32)), (REC_RANK0, rank0),
                     (REC_RANK1, rank1), (REC_C0, c0), (REC_C1, c1)):
        rec = jnp.where(lane == pos, val, rec)
    rec_ref[0] = rec


def _moe_route(x, g, shift, scale, wr, br):
    b, t, d = x.shape
    tm = ROW_TILE
    row = lambda width: pl.BlockSpec((1, tm, width), lambda bi, i: (bi, i, 0))
    vec = lambda width: pl.BlockSpec((1, width), lambda bi, i: (0, 0))
    return pl.pallas_call(
        _moe_route_kernel,
        grid=(b, t // tm),
        in_specs=[row(d), vec(d), _mod_spec(shift, tm), _mod_spec(scale, tm),
                  pl.BlockSpec(wr.shape, lambda bi, i: (0, 0)), vec(ROUTER_PAD)],
        out_specs=(row(d), row(ROUTER_PAD), vec(ROUTER_PAD)),
        out_shape=(jax.ShapeDtypeStruct((b, t, d), F32), jax.ShapeDtypeStruct((b, t, ROUTER_PAD), F32),
                   jax.ShapeDtypeStruct((1, ROUTER_PAD), F32)),
        scratch_shapes=[pltpu.VMEM((tm, tm), BF16), pltpu.VMEM((1, ROUTER_PAD), F32)],
        compiler_params=_cparams(2),
        name="moe_route",
    )(x, g, shift, scale, wr, br)


def _gather_rows_start(idx_ref, idx_base, src_hbm, dst, sem, n_rows):
    def body(c, carry):
        r0 = pl.multiple_of(c * SUBLANES, SUBLANES)
        for j in range(SUBLANES):
            pltpu.make_async_copy(src_hbm.at[pl.ds(idx_ref[idx_base + r0 + j], 1)],
                                  dst.at[pl.ds(r0 + j, 1)], sem).start()
        return carry
    lax.fori_loop(0, n_rows // SUBLANES, body, 0)


def _gather_rows_wait(src_hbm, dst, sem, n_rows):
    pltpu.make_async_copy(src_hbm.at[pl.ds(0, n_rows)], dst, sem).wait()


def _moe_experts_kernel(blk_e_ref, n_used_ref, row_tok_ref, h_hbm, w1_ref, w2_ref, y_ref,
                        buf_ref, w1b_ref, w2b_ref, sem):
    i = pl.program_id(0)
    blk = buf_ref.shape[1]
    n_used = n_used_ref[0]
    n_slots = buf_ref.shape[0]
    slot = i % n_slots

    def start(block):
        s = block % n_slots
        _gather_rows_start(row_tok_ref, block * blk, h_hbm, buf_ref.at[s], sem.at[s], blk)

    @pl.when(i == 0)
    def _():
        for ahead in range(n_slots - 1):
            @pl.when(ahead < n_used)
            def _():
                start(ahead)

    @pl.when(i + n_slots - 1 < n_used)
    def _():
        start(i + n_slots - 1)

    @pl.when(i < n_used)
    def _():
        @pl.when((i == 0) | (blk_e_ref[i] != blk_e_ref[jnp.maximum(i - 1, 0)]))
        def _():
            w1b_ref[...] = w1_ref[0].astype(BF16)
            w2b_ref[...] = w2_ref[0].astype(BF16)

        _gather_rows_wait(h_hbm, buf_ref.at[slot], sem.at[slot], blk)
        gu = jnp.dot(buf_ref[slot].astype(BF16), w1b_ref[...], preferred_element_type=F32)
        gp, up = gu[:, :D_EXPERT], gu[:, D_EXPERT:]
        act = ((gp * jax.nn.sigmoid(gp)) * up).astype(BF16)
        y_ref[...] = jnp.dot(act, w2b_ref[...], preferred_element_type=F32)

    @pl.when(i >= n_used)
    def _():
        y_ref[...] = jnp.zeros(y_ref.shape, F32)


def _moe_experts(blk_expert, n_used, row_tok, h2d, w1, w2, layer):
    n_blk = blk_expert.shape[0]
    blk = MOE_BLOCK_ROWS
    d = h2d.shape[1]
    grid_spec = pltpu.PrefetchScalarGridSpec(
        num_scalar_prefetch=3,
        grid=(n_blk,),
        in_specs=[pl.BlockSpec(memory_space=pl.ANY),
                  pl.BlockSpec((1, d, 2 * D_EXPERT), lambda i, be, nu, rt: (layer * N_EXPERTS + be[i], 0, 0)),
                  pl.BlockSpec((1, D_EXPERT, d), lambda i, be, nu, rt: (layer * N_EXPERTS + be[i], 0, 0))],
        out_specs=pl.BlockSpec((blk, d), lambda i, be, nu, rt: (i, 0)),
        scratch_shapes=[pltpu.VMEM((GATHER_SLOTS, blk, d), F32), pltpu.VMEM((d, 2 * D_EXPERT), BF16),
                        pltpu.VMEM((D_EXPERT, d), BF16), pltpu.SemaphoreType.DMA((GATHER_SLOTS,))],
    )
    return pl.pallas_call(
        _moe_experts_kernel,
        grid_spec=grid_spec,
        out_shape=jax.ShapeDtypeStruct((n_blk * blk, d), F32),
        compiler_params=_cparams(1),
        name="moe_experts",
    )(blk_expert, n_used, row_tok, h2d, w1, w2)


def _moe_combine_kernel(dest_ref, y_hbm, x_ref, gt_ref, rec_ref, gf_ref, o_ref, buf_ref, sem,
                        *, n_tokens, final_norm):
    tm = x_ref.shape[1]
    n_steps = pl.num_programs(0) * pl.num_programs(1)
    s = pl.program_id(0) * pl.num_programs(1) + pl.program_id(1)
    slot = s % 2

    def start(step, sl):
        for pick in range(2):
            _gather_rows_start(dest_ref, pick * n_tokens + step * tm, y_hbm, buf_ref.at[sl, pick],
                               sem.at[sl, pick], tm)

    @pl.when(s == 0)
    def _():
        start(0, 0)

    @pl.when(s + 1 < n_steps)
    def _():
        start(s + 1, 1 - slot)

    for pick in range(2):
        _gather_rows_wait(y_hbm, buf_ref.at[slot, pick], sem.at[slot, pick], tm)
    rec = rec_ref[0]
    moe = buf_ref[slot, 0] * rec[:, REC_C0:REC_C0 + 1] + buf_ref[slot, 1] * rec[:, REC_C1:REC_C1 + 1]
    out = x_ref[0] + gt_ref[0] * moe
    if final_norm:
        out = (out * lax.rsqrt(jnp.mean(out * out, axis=-1, keepdims=True) + EPS)) * gf_ref[...]
    o_ref[0] = out


def _moe_combine(dest, y, x, gate, rec, gf, final_norm):
    b, t, d = x.shape
    tm = MOE_BLOCK_ROWS
    row = lambda width: pl.BlockSpec((1, tm, width), lambda bi, i, dst: (bi, i, 0))
    grid_spec = pltpu.PrefetchScalarGridSpec(
        num_scalar_prefetch=1,
        grid=(b, t // tm),
        in_specs=[pl.BlockSpec(memory_space=pl.ANY), row(d), _mod_spec(gate, tm), row(ROUTER_PAD),
                  pl.BlockSpec((1, d), lambda bi, i, dst: (0, 0))],
        out_specs=row(d),
        scratch_shapes=[pltpu.VMEM((2, 2, tm, d), F32), pltpu.SemaphoreType.DMA((2, 2))],
    )
    return pl.pallas_call(
        functools.partial(_moe_combine_kernel, n_tokens=b * t, final_norm=final_norm),
        grid_spec=grid_spec,
        out_shape=jax.ShapeDtypeStruct((b, t, d), F32),
        compiler_params=_cparams(2),
        name="moe_combine",
    )(dest, y, x, gate, rec, gf)


def _moe_sorted(x, g, shift, scale, gate, wr, br, w1, w2, layer, gf, final_norm):
    b, t, d = x.shape
    n = b * t
    blk = MOE_BLOCK_ROWS
    n_blk = (2 * n) // blk + N_EXPERTS
    h, rec, cnt = _moe_route(x, g, shift, scale, wr, br)
    counts = cnt[0, :N_EXPERTS].astype(I32)
    padded = (counts + blk - 1) // blk * blk
    seg_end = jnp.cumsum(padded)
    seg_start = seg_end - padded
    rec2 = rec.reshape(n, ROUTER_PAD)
    col = lambda c: rec2[:, c].astype(I32)
    dest = jnp.concatenate([seg_start[col(REC_E0)] + col(REC_RANK0), seg_start[col(REC_E1)] + col(REC_RANK1)])
    tok = jnp.arange(n, dtype=I32)
    row_tok = jnp.zeros((n_blk * blk,), I32).at[dest].set(jnp.concatenate([tok, tok]), unique_indices=True)
    blk_first_row = jnp.arange(n_blk, dtype=I32) * blk
    blk_expert = jnp.minimum(jnp.sum((seg_end[None, :] <= blk_first_row[:, None]).astype(I32), axis=1),
                             N_EXPERTS - 1)
    n_used = (seg_end[-1:] // blk).astype(I32)
    y = _moe_experts(blk_expert, n_used, row_tok, h.reshape(n, d), w1, w2, layer)
    return _moe_combine(dest, y, x, gate, rec, gf, final_norm)


def kernel(x_prompt, x_sample, cache_k, cache_v, cache_kidx, state_h, state_conv, page_table,
           c_prompt, c_sample, ada_w, ada_b, norm_mix_g, norm_ffn_g, norm_final_g,
           attn_w_in, attn_w_out, rec_w_in, rec_conv_w, rec_conv_b, rec_w_a, rec_b_a,
           rec_w_x, rec_b_x, rec_lambda, rec_w_out, moe_w_group, moe_b_group,
           moe_w_expert, moe_b_expert, moe_w1, moe_w2):
    b, t, d = x_prompt.shape
    db = x_sample.shape[0]
    n_pages = page_table.shape[1]
    past = n_pages * PAGE_SIZE
    n_pool = cache_k.shape[1]

    n_cond = b + db
    cond_rows = -(-n_cond // SUBLANES) * SUBLANES
    c_all = jnp.concatenate([c_prompt, c_sample, jnp.zeros((cond_rows - n_cond, d), F32)], axis=0)
    mod = _adaln(c_all, ada_w, ada_b)

    tables_p = _rope_tables(jnp.arange(t, dtype=I32))
    tables_s = _rope_tables(jnp.full((db,), past, I32))
    cache_kT = cache_k.transpose(0, 1, 3, 4, 2).reshape(-1, KV_WIDTH, PAGE_SIZE)
    cache_vT = cache_v.transpose(0, 1, 3, 4, 2).reshape(-1, KV_WIDTH, PAGE_SIZE)
    cache_kiT = cache_kidx.transpose(0, 1, 3, 2).reshape(-1, IDX_DIM, PAGE_SIZE)
    pt_flat = page_table.reshape(-1)
    moe_w1f = moe_w1.reshape(-1, d, 2 * D_EXPERT)
    moe_w2f = moe_w2.reshape(-1, D_EXPERT, d)
    row_vec = lambda v: v.reshape(1, -1)
    head_grp = jnp.arange(N_HEADS)[:, None] // Q_PER_KV == jnp.arange(KV_WIDTH)[None, :] // HEAD_DIM

    xp = x_prompt
    xs = x_sample.reshape(1, db, d)
    outs = {name: [] for name in ("k_p", "v_p", "ki_p", "h_p", "cv_p", "k_s", "v_s", "ki_s", "h_s", "cv_s")}
    for l in range(DEPTH):
        parts = [mod[l, :, i * d:(i + 1) * d] for i in range(6)]
        sh1p, sc1p, g1p, sh2p, sc2p, g2p = [m[:b].reshape(b, 1, d) for m in parts]
        sh1s, sc1s, g1s, sh2s, sc2s, g2s = [m[b:n_cond].reshape(1, db, d) for m in parts]
        g_mix = row_vec(norm_mix_g[l])
        if l % 2 == 0:
            a = l // 2
            w_in = jnp.pad(attn_w_in[a], ((0, 0), (0, ATTN_PROJ_PAD - ATTN_PROJ))).astype(BF16)
            w_out = attn_w_out[a].astype(BF16)
            kT, vT, kiT, qT, qiT, wiT, kb, vTb, kib = _attn_proj_prompt(xp, g_mix, sh1p, sc1p, w_in, tables_p)
            op = _dsa_prompt(qT, qiT, wiT, kb, vTb, kib)
            outs["k_p"].append(kT.reshape(b, N_KV_HEADS, HEAD_DIM, t).transpose(0, 3, 1, 2))
            outs["v_p"].append(vT.reshape(b, N_KV_HEADS, HEAD_DIM, t).transpose(0, 3, 1, 2))
            outs["ki_p"].append(kiT.transpose(0, 2, 1))
            xp = _out_proj(op, w_out, xp, g1p)

            q, k, v, qi, ki, wi = _attn_proj_sample(xs, g_mix, sh1s, sc1s, w_in, tables_s)
            keys = _dsa_sample_scores(pt_flat, qi.reshape(db, IDX_HEADS, IDX_DIM).astype(BF16),
                                      wi.reshape(db, IDX_HEADS, 1), ki.reshape(db, 1, IDX_DIM),
                                      cache_kiT, a, n_pool, n_pages)
            bias = _dsa_sample_select(keys.reshape(db, -1), past + 1).reshape(keys.shape)
            q_bd = jnp.where(head_grp[None], jnp.tile(q.reshape(db, N_HEADS, HEAD_DIM), (1, 1, N_KV_HEADS)),
                             0.0).astype(BF16)
            os_ = _dsa_sample_attend(pt_flat, q_bd, bias, k.reshape(db, 1, KV_WIDTH), v.reshape(db, 1, KV_WIDTH),
                                     cache_kT, cache_vT, a, n_pool, n_pages)
            outs["k_s"].append(k.reshape(db, 1, N_KV_HEADS, HEAD_DIM))
            outs["v_s"].append(v.reshape(db, 1, N_KV_HEADS, HEAD_DIM))
            outs["ki_s"].append(ki.reshape(db, 1, IDX_DIM))
            xs = _out_proj(os_.reshape(1, db, Q_WIDTH).astype(BF16), w_out, xs, g1s)
        else:
            r = l // 2
            w_in = rec_w_in[r].astype(BF16)
            w_out = rec_w_out[r].astype(BF16)
            rec = (rec_conv_w[r], row_vec(rec_conv_b[r]), rec_w_a[r].astype(BF16), row_vec(rec_b_a[r]),
                   rec_w_x[r].astype(BF16), row_vec(rec_b_x[r]), row_vec(rec_lambda[r]))
            gate, xb = _rec_proj(xp, g_mix, sh1p, sc1p, w_in)
            y, h_t, buf = _rglru_prompt(gate, xb, *rec)
            outs["h_p"].append(h_t.reshape(b, D_RNN))
            outs["cv_p"].append(buf)
            xp = _out_proj(y, w_out, xp, g1p)

            gate, xb = _rec_proj(xs, g_mix, sh1s, sc1s, w_in)
            y, h_t, buf = _rglru_sample(gate[0], xb[0], state_conv[r].swapaxes(0, 1), state_h[r], *rec)
            outs["h_s"].append(h_t)
            outs["cv_s"].append(buf.swapaxes(0, 1))
            xs = _out_proj(y.reshape(1, db, D_RNN), w_out, xs, g1s)

        wr = jnp.concatenate([moe_w_group[l], moe_w_expert[l],
                              jnp.zeros((d, ROUTER_PAD - N_GROUPS - N_EXPERTS), F32)], axis=1).astype(BF16)
        br = jnp.concatenate([moe_b_group[l], moe_b_expert[l],
                              jnp.zeros((ROUTER_PAD - N_GROUPS - N_EXPERTS,), F32)]).reshape(1, ROUTER_PAD)
        last = l == DEPTH - 1
        moe_args = (wr, br, moe_w1f, moe_w2f, l, row_vec(norm_final_g), last)
        xp = _moe_sorted(xp, row_vec(norm_ffn_g[l]), sh2p, sc2p, g2p, *moe_args)
        xs = _moe(xs, row_vec(norm_ffn_g[l]), sh2s, sc2s, g2s, *moe_args)

    st = lambda name: jnp.stack(outs[name])
    return (xp, xs.reshape(db, 1, d), st("k_p"), st("v_p"), st("ki_p"), st("h_p"), st("cv_p"),
            st("k_s"), st("v_s"), st("ki_s"), st("h_s"), st("cv_s"))
```

```python
import functools
import math

import jax
import jax.numpy as jnp
from jax import lax
from jax.experimental import pallas as pl
from jax.experimental.pallas import tpu as pltpu

F32 = jnp.float32
BF16 = jnp.bfloat16
I32 = jnp.int32

D_MODEL = 1024
DEPTH = 4
PAGE_SIZE = 128
N_HEADS = 16
HEAD_DIM = 64
N_KV_HEADS = 4
Q_PER_KV = N_HEADS // N_KV_HEADS
IDX_HEADS = 8
IDX_DIM = 64
IDX_SCALE = (IDX_HEADS * IDX_DIM) ** -0.5
TOPK_MAX = 256
ROPE_THETA = 500000.0
ROPE_FRACTION = 4
Q_WIDTH = N_HEADS * HEAD_DIM
KV_WIDTH = N_KV_HEADS * HEAD_DIM
QI_WIDTH = IDX_HEADS * IDX_DIM
ATTN_PROJ = Q_WIDTH + 2 * KV_WIDTH + QI_WIDTH + IDX_DIM + IDX_HEADS
D_RNN = D_MODEL
RG_BLOCKS = 8
RG_BLOCK_W = D_RNN // RG_BLOCKS
CONV_W = 4
RG_C = 8.0
N_GROUPS = 4
EXPERTS_PER_GROUP = 8
N_EXPERTS = N_GROUPS * EXPERTS_PER_GROUP
D_EXPERT = 256
EPS = 1e-6

LANES = 128
SUBLANES = 8
VMEM_LIMIT_BYTES = 56 * 1024 * 1024

ATTN_PROJ_PAD = -(-ATTN_PROJ // LANES) * LANES
ROUTER_PAD = LANES
Q_TILE = LANES
KEY_CHUNK = 512
ROW_TILE = 512
MOE_ROW_TILE = 1024
MOE_BLOCK_ROWS = 256
GATHER_SLOTS = 3
SCAN_CHUNK = 512
PAGE_GROUP = 8
BF16_SUBLANES = 16
HEAD_AUG = HEAD_DIM + BF16_SUBLANES
BOUND_SLACK = 1.02
DENOM_FLOOR = 2.0 ** -60
DENOM_CEIL = 2.0 ** 60
INT_MIN = -(2 ** 31)
INT_MAX = 2 ** 31 - 1
NEG = -1e30


def _cparams(n_axes):
    return pltpu.CompilerParams(dimension_semantics=("arbitrary",) * n_axes,
                                vmem_limit_bytes=VMEM_LIMIT_BYTES)


def _modulated_norm(x, g, shift, scale):
    xn = x * lax.rsqrt(jnp.mean(x * x, axis=-1, keepdims=True) + EPS)
    return (xn * g) * (1.0 + scale) + shift


def _rope_group(xs, cos, sa, sb):
    return xs * cos + pltpu.roll(xs, LANES - 8, 1) * sa + pltpu.roll(xs, 8, 1) * sb


def _float_order_key(s):
    bits = lax.bitcast_convert_type(s, I32)
    return bits ^ ((bits >> 31) & INT_MAX)


def _mod_spec(arr, tm):
    if arr.shape[1] == 1:
        return pl.BlockSpec((1, 1, arr.shape[2]), lambda b, i, *_: (b, 0, 0))
    return pl.BlockSpec((1, tm, arr.shape[2]), lambda b, i, *_: (b, i, 0))


def _adaln_kernel(c_ref, w_ref, b_ref, o_ref):
    c = c_ref[...]
    s = (c * jax.nn.sigmoid(c)).astype(BF16)
    o_ref[0] = jnp.dot(s, w_ref[0].astype(BF16), preferred_element_type=F32) + b_ref[0]


def _adaln(c_all, ada_w, ada_b):
    depth, d, n = ada_w.shape
    rows = c_all.shape[0]
    tn = 1536
    return pl.pallas_call(
        _adaln_kernel,
        grid=(depth, n // tn),
        in_specs=[pl.BlockSpec((rows, d), lambda l, j: (0, 0)),
                  pl.BlockSpec((1, d, tn), lambda l, j: (l, 0, j)),
                  pl.BlockSpec((1, 1, tn), lambda l, j: (l, 0, j))],
        out_specs=pl.BlockSpec((1, rows, tn), lambda l, j: (l, 0, j)),
        out_shape=jax.ShapeDtypeStruct((depth, rows, n), F32),
        compiler_params=_cparams(2),
        name="adaln",
    )(c_all, ada_w, ada_b.reshape(depth, 1, n))


def _rope_tables(pos):
    rot = HEAD_DIM // ROPE_FRACTION
    half = rot // 2
    inv = jnp.exp(-math.log(ROPE_THETA) * jnp.arange(half, dtype=F32) * (2.0 / rot))
    ang = pos.astype(F32)[:, None] * inv[None, :]
    cos, sin = jnp.cos(ang), jnp.sin(ang)
    r = pos.shape[0]
    z = lambda w: jnp.zeros((r, w), F32)
    cos64 = jnp.concatenate([cos, cos, jnp.ones((r, HEAD_DIM - rot), F32)], axis=1)
    sa64 = jnp.concatenate([-sin, z(HEAD_DIM - half)], axis=1)
    sb64 = jnp.concatenate([z(half), sin, z(HEAD_DIM - rot)], axis=1)
    two = lambda t: jnp.concatenate([t, t], axis=1)
    return two(cos64), two(sa64), two(sb64)


def _attn_proj_prompt_kernel(x_ref, g_ref, sh_ref, sc_ref, w_ref, cos_ref, sa_ref, sb_ref,
                             kT_ref, vT_ref, kiT_ref, qT_ref, qiT_ref, wiT_ref, kb_ref, vTb_ref, kib_ref,
                             kmax_ref):
    h = _modulated_norm(x_ref[0], g_ref[...], sh_ref[0], sc_ref[0]).astype(BF16)
    acc = jnp.dot(h, w_ref[...], preferred_element_type=F32)
    cos, sa, sb = cos_ref[...], sa_ref[...], sb_ref[...]
    n_qb = acc.shape[0] // Q_TILE
    grp = lambda off, j: acc[:, off + j * LANES: off + (j + 1) * LANES]

    tm = acc.shape[0]
    one_hot_row = lambda shape, axis: jnp.where(lax.broadcasted_iota(I32, shape, axis) == 0, 1.0, 0.0).astype(BF16)

    @pl.when(pl.program_id(1) == 0)
    def _():
        kmax_ref[...] = jnp.zeros(kmax_ref.shape, F32)

    for j in range(KV_WIDTH // LANES):
        kj = _rope_group(grp(Q_WIDTH, j), cos, sa, sb)
        kjT = kj.T
        kT_ref[0, j * LANES:(j + 1) * LANES, :] = kjT
        sq = kjT * kjT
        for hh in range(2):
            n = 2 * j + hh
            norm2 = jnp.sum(sq[hh * HEAD_DIM:(hh + 1) * HEAD_DIM, :], axis=0, keepdims=True)
            kmax_ref[n] = jnp.maximum(kmax_ref[n], jnp.max(norm2, axis=1, keepdims=True))
            kb_ref[0, n, :, 0:HEAD_DIM] = kj[:, hh * HEAD_DIM:(hh + 1) * HEAD_DIM].astype(BF16)
            kb_ref[0, n, :, HEAD_DIM:HEAD_AUG] = one_hot_row((tm, BF16_SUBLANES), 1)
    for j in range(Q_WIDTH // LANES):
        qjT = (_rope_group(grp(0, j), cos, sa, sb) * (HEAD_DIM ** -0.5)).T
        sq = qjT * qjT
        qjTb = qjT.astype(BF16)
        for hh in range(2):
            n, g = divmod(2 * j + hh, Q_PER_KV)
            norm2 = jnp.sum(sq[hh * HEAD_DIM:(hh + 1) * HEAD_DIM, :], axis=0, keepdims=True)
            shift = -BOUND_SLACK * jnp.sqrt(norm2 * kmax_ref[n])
            shift = jnp.concatenate([shift, jnp.zeros((BF16_SUBLANES - 1, tm), F32)], axis=0).astype(BF16)
            for jb in range(n_qb):
                qT_ref[0, jb, n, 0:HEAD_DIM, g * Q_TILE:(g + 1) * Q_TILE] = (
                    qjTb[hh * HEAD_DIM:(hh + 1) * HEAD_DIM, jb * Q_TILE:(jb + 1) * Q_TILE])
                qT_ref[0, jb, n, HEAD_DIM:HEAD_AUG, g * Q_TILE:(g + 1) * Q_TILE] = (
                    shift[:, jb * Q_TILE:(jb + 1) * Q_TILE])
    for j in range(KV_WIDTH // LANES):
        vjT = grp(Q_WIDTH + KV_WIDTH, j).T
        vT_ref[0, j * LANES:(j + 1) * LANES, :] = vjT
        for hh in range(2):
            n = 2 * j + hh
            vTb_ref[0, 0, n * HEAD_AUG:n * HEAD_AUG + HEAD_DIM, :] = (
                vjT[hh * HEAD_DIM:(hh + 1) * HEAD_DIM, :].astype(BF16))
            vTb_ref[0, 0, n * HEAD_AUG + HEAD_DIM:(n + 1) * HEAD_AUG, :] = one_hot_row((BF16_SUBLANES, tm), 0)
    off_qi = Q_WIDTH + 2 * KV_WIDTH
    for j in range(QI_WIDTH // LANES):
        qijT = _rope_group(grp(off_qi, j), cos, sa, sb).T.astype(BF16)
        for hh in range(2):
            head = 2 * j + hh
            for jb in range(n_qb):
                qiT_ref[0, jb, :, head * Q_TILE:(head + 1) * Q_TILE] = (
                    qijT[hh * IDX_DIM:(hh + 1) * IDX_DIM, jb * Q_TILE:(jb + 1) * Q_TILE])
    last = grp(off_qi + QI_WIDTH, 0)
    kir = _rope_group(last, cos, sa, sb)
    kiT_ref[0] = kir.T[:IDX_DIM, :]
    kib_ref[0] = kir[:, :IDX_DIM].astype(BF16)
    lastT = last.T
    for jb in range(n_qb):
        wiT_ref[0, jb] = lastT[IDX_DIM:IDX_DIM + IDX_HEADS, jb * Q_TILE:(jb + 1) * Q_TILE]


def _attn_proj_prompt(x, g, shift, scale, w_pad, tables):
    b, t, d = x.shape
    tm = ROW_TILE
    nq = t // Q_TILE
    grid = (b, t // tm)
    row = lambda width: pl.BlockSpec((1, tm, width), lambda bi, i: (bi, i, 0))
    col = lambda width: pl.BlockSpec((1, width, tm), lambda bi, i: (bi, 0, i))
    tab = pl.BlockSpec((tm, LANES), lambda bi, i: (i, 0))
    out_shape = (
        jax.ShapeDtypeStruct((b, KV_WIDTH, t), F32),
        jax.ShapeDtypeStruct((b, KV_WIDTH, t), F32),
        jax.ShapeDtypeStruct((b, IDX_DIM, t), F32),
        jax.ShapeDtypeStruct((b, nq, N_KV_HEADS, HEAD_AUG, Q_PER_KV * Q_TILE), BF16),
        jax.ShapeDtypeStruct((b, nq, IDX_DIM, IDX_HEADS * Q_TILE), BF16),
        jax.ShapeDtypeStruct((b, nq, IDX_HEADS, Q_TILE), F32),
        jax.ShapeDtypeStruct((b, N_KV_HEADS, t, HEAD_AUG), BF16),
        jax.ShapeDtypeStruct((b, t // tm, N_KV_HEADS * HEAD_AUG, tm), BF16),
        jax.ShapeDtypeStruct((b, t, IDX_DIM), BF16),
    )
    nqb = tm // Q_TILE
    out_specs = (
        col(KV_WIDTH), col(KV_WIDTH), col(IDX_DIM),
        pl.BlockSpec((1, nqb, N_KV_HEADS, HEAD_AUG, Q_PER_KV * Q_TILE), lambda bi, i: (bi, i, 0, 0, 0)),
        pl.BlockSpec((1, nqb, IDX_DIM, IDX_HEADS * Q_TILE), lambda bi, i: (bi, i, 0, 0)),
        pl.BlockSpec((1, nqb, IDX_HEADS, Q_TILE), lambda bi, i: (bi, i, 0, 0)),
        pl.BlockSpec((1, N_KV_HEADS, tm, HEAD_AUG), lambda bi, i: (bi, 0, i, 0)),
        pl.BlockSpec((1, 1, N_KV_HEADS * HEAD_AUG, tm), lambda bi, i: (bi, i, 0, 0)),
        row(IDX_DIM),
    )
    return pl.pallas_call(
        _attn_proj_prompt_kernel,
        grid=grid,
        in_specs=[row(d), pl.BlockSpec((1, d), lambda bi, i: (0, 0)),
                  _mod_spec(shift, tm), _mod_spec(scale, tm),
                  pl.BlockSpec(w_pad.shape, lambda bi, i: (0, 0)), tab, tab, tab],
        out_specs=out_specs,
        out_shape=out_shape,
        scratch_shapes=[pltpu.VMEM((N_KV_HEADS, 1, 1), F32)],
        compiler_params=_cparams(2),
        name="attn_proj_prompt",
    )(x, g, shift, scale, w_pad, *tables)


def _attn_proj_sample_kernel(x_ref, g_ref, sh_ref, sc_ref, w_ref, cos_ref, sa_ref, sb_ref,
                             q_ref, k_ref, v_ref, qi_ref, ki_ref, wi_ref):
    h = _modulated_norm(x_ref[0], g_ref[...], sh_ref[0], sc_ref[0]).astype(BF16)
    acc = jnp.dot(h, w_ref[...], preferred_element_type=F32)
    cos, sa, sb = cos_ref[...], sa_ref[...], sb_ref[...]
    grp = lambda off, j: acc[:, off + j * LANES: off + (j + 1) * LANES]
    for j in range(Q_WIDTH // LANES):
        q_ref[:, j * LANES:(j + 1) * LANES] = _rope_group(grp(0, j), cos, sa, sb) * (HEAD_DIM ** -0.5)
    for j in range(KV_WIDTH // LANES):
        k_ref[:, j * LANES:(j + 1) * LANES] = _rope_group(grp(Q_WIDTH, j), cos, sa, sb)
        v_ref[:, j * LANES:(j + 1) * LANES] = grp(Q_WIDTH + KV_WIDTH, j)
    off_qi = Q_WIDTH + 2 * KV_WIDTH
    for j in range(QI_WIDTH // LANES):
        qi_ref[:, j * LANES:(j + 1) * LANES] = _rope_group(grp(off_qi, j), cos, sa, sb)
    last = grp(off_qi + QI_WIDTH, 0)
    ki_ref[...] = _rope_group(last, cos, sa, sb)[:, :IDX_DIM]
    wi_ref[...] = last[:, IDX_DIM:IDX_DIM + IDX_HEADS]


def _attn_proj_sample(x, g, shift, scale, w_pad, tables):
    _, rows, d = x.shape
    full = lambda shape: pl.BlockSpec(shape, lambda i: (0,) * len(shape))
    widths = (Q_WIDTH, KV_WIDTH, KV_WIDTH, QI_WIDTH, IDX_DIM, IDX_HEADS)
    return pl.pallas_call(
        _attn_proj_sample_kernel,
        grid=(1,),
        in_specs=[full((1, rows, d)), full((1, d)), full((1, rows, d)), full((1, rows, d)),
                  full(w_pad.shape), full((rows, LANES)), full((rows, LANES)), full((rows, LANES))],
        out_specs=tuple(full((rows, w)) for w in widths),
        out_shape=tuple(jax.ShapeDtypeStruct((rows, w), F32) for w in widths),
        compiler_params=_cparams(1),
        name="attn_proj_sample",
    )(x, g, shift, scale, w_pad, *tables)


def _dsa_prompt_kernel(qT_ref, qiT_ref, wiT_ref, kb_ref, vT_ref, kib_ref, o_ref,
                       keys_ref, acc_ref, acc2_ref, m_ref, l_ref, j_ref, *, topk, idx_bits):
    kc = KEY_CHUNK
    qb = pl.program_id(1)
    n_chunks = (qb * Q_TILE + Q_TILE + kc - 1) // kc
    rows = lax.broadcasted_iota(I32, (kc, Q_TILE), 0)
    q_pos = qb * Q_TILE + lax.broadcasted_iota(I32, (kc, Q_TILE), 1)

    w_heads = wiT_ref[0, 0] * IDX_SCALE
    qiT = qiT_ref[0, 0]

    def score_chunk(c, carry):
        off = pl.multiple_of(c * kc, kc)
        dots = jnp.dot(kib_ref[0, pl.ds(off, kc), :], qiT, preferred_element_type=F32)
        s = jnp.zeros((kc, Q_TILE), F32)
        for h in range(IDX_HEADS):
            s = s + jnp.maximum(dots[:, h * Q_TILE:(h + 1) * Q_TILE], 0.0) * w_heads[h:h + 1, :]
        keys_ref[pl.ds(off, kc), :] = jnp.where(rows + off <= q_pos, _float_order_key(s), INT_MIN)
        return carry

    lax.fori_loop(0, n_chunks, score_chunk, 0)

    def count(pred):
        def body(c, cnt):
            off = pl.multiple_of(c * kc, kc)
            hit = jnp.where(pred(keys_ref[pl.ds(off, kc), :], rows + off), 1, 0)
            return cnt + hit.reshape(kc // SUBLANES, SUBLANES, Q_TILE).sum(axis=0)
        cnt = lax.fori_loop(0, n_chunks, body, jnp.zeros((SUBLANES, Q_TILE), I32))
        return cnt.sum(axis=0, keepdims=True)

    def thr_bit(i, state):
        thr, n_ge = state
        cand = thr + lax.shift_left(jnp.int32(1), 31 - i)
        n_cand = count(lambda k, s: k >= cand)
        ok = n_cand >= topk
        return jnp.where(ok, cand, thr), jnp.where(ok, n_cand, n_ge)

    n_all = jnp.full((1, Q_TILE), 1, I32) * (n_chunks * kc)
    thr, n_ge = lax.fori_loop(0, 32, thr_bit, (jnp.full((1, Q_TILE), INT_MIN, I32), n_all))

    j_ref[...] = jnp.full((1, Q_TILE), INT_MAX, I32)
    tie_split = jnp.max(jnp.where((n_ge > topk) & (thr > INT_MIN), 1, 0))

    @pl.when(tie_split > 0)
    def _():
        need = topk - count(lambda k, s: k > thr)

        def idx_bit(i, j0):
            cand = j0 + lax.shift_left(jnp.int32(1), idx_bits - 1 - i)
            taken_before = count(lambda k, s: (k == thr) & (s < cand))
            return jnp.where(taken_before < need, cand, j0)
        j_ref[...] = lax.fori_loop(0, idx_bits, idx_bit, jnp.zeros((1, Q_TILE), I32))

    j_sel = j_ref[...]

    def selection_bias(c):
        off = pl.multiple_of(c * kc, kc)
        k = keys_ref[pl.ds(off, kc), :]
        s_idx = rows + off
        sel = ((k > thr) | ((k == thr) & (s_idx <= j_sel))) & (s_idx <= q_pos)
        bias = jnp.where(sel, 0.0, NEG)
        return off, jnp.concatenate([bias] * Q_PER_KV, axis=1)

    acc_ref[...] = jnp.zeros(acc_ref.shape, F32)

    def attend_chunk(c, carry):
        off, bias = selection_bias(c)
        for n in range(N_KV_HEADS):
            s = jnp.dot(kb_ref[0, n, pl.ds(off, kc), :], qT_ref[0, 0, n], preferred_element_type=F32) + bias
            acc_ref[n] += jnp.dot(vT_ref[0, c, n * HEAD_AUG:(n + 1) * HEAD_AUG, :], jnp.exp(s).astype(BF16),
                                  preferred_element_type=F32)
        return carry

    lax.fori_loop(0, n_chunks, attend_chunk, 0)
    denom = jnp.concatenate([acc_ref[n][HEAD_DIM:HEAD_DIM + 1, :] for n in range(N_KV_HEADS)], axis=0)
    healthy = (jnp.min(denom) >= DENOM_FLOOR) & (jnp.max(denom) <= DENOM_CEIL)

    @pl.when(jnp.logical_not(healthy))
    def _():
        m_ref[...] = jnp.full(m_ref.shape, NEG, F32)
        l_ref[...] = jnp.zeros(l_ref.shape, F32)
        acc2_ref[...] = jnp.zeros(acc2_ref.shape, F32)

        def attend_chunk_online(c, carry):
            off, bias = selection_bias(c)
            for n in range(N_KV_HEADS):
                s = jnp.dot(kb_ref[0, n, pl.ds(off, kc), 0:HEAD_DIM], qT_ref[0, 0, n, 0:HEAD_DIM, :],
                            preferred_element_type=F32) + bias
                m_prev = m_ref[n]
                m_new = jnp.maximum(m_prev, s.max(axis=0, keepdims=True))
                alpha = jnp.exp(m_prev - m_new)
                p = jnp.exp(s - m_new)
                l_ref[n] = alpha * l_ref[n] + p.sum(axis=0, keepdims=True)
                pv = jnp.dot(vT_ref[0, c, n * HEAD_AUG:n * HEAD_AUG + HEAD_DIM, :], p.astype(BF16),
                             preferred_element_type=F32)
                acc2_ref[n] = alpha * acc2_ref[n] + pv
                m_ref[n] = m_new
            return carry

        lax.fori_loop(0, n_chunks, attend_chunk_online, 0)
        for n in range(N_KV_HEADS):
            acc_ref[n, 0:HEAD_DIM, :] = acc2_ref[n]
            acc_ref[n, HEAD_DIM:HEAD_DIM + 1, :] = l_ref[n]

    for n in range(N_KV_HEADS):
        on = acc_ref[n, 0:HEAD_DIM, :] / acc_ref[n, HEAD_DIM:HEAD_DIM + 1, :]
        for gp in range(Q_PER_KV // 2):
            pair = jnp.concatenate([on[:, (2 * gp) * Q_TILE:(2 * gp + 1) * Q_TILE],
                                    on[:, (2 * gp + 1) * Q_TILE:(2 * gp + 2) * Q_TILE]], axis=0)
            col = (n * Q_PER_KV // 2 + gp) * LANES
            o_ref[0, :, col:col + LANES] = pair.T.astype(BF16)


def _dsa_prompt(qT, qiT, wiT, kb, vT, kib):
    b, nq = qT.shape[:2]
    t = kb.shape[2]
    topk = min(TOPK_MAX, t // 4)
    idx_bits = max(1, (t - 1).bit_length())
    per_q = lambda shape: pl.BlockSpec((1, 1) + shape, lambda bi, qi: (bi, qi) + (0,) * len(shape))
    per_b = lambda shape: pl.BlockSpec((1,) + shape, lambda bi, qi: (bi,) + (0,) * len(shape))
    return pl.pallas_call(
        functools.partial(_dsa_prompt_kernel, topk=topk, idx_bits=idx_bits),
        grid=(b, nq),
        in_specs=[per_q(qT.shape[2:]), per_q(qiT.shape[2:]), per_q(wiT.shape[2:]),
                  per_b(kb.shape[1:]), per_b(vT.shape[1:]), per_b(kib.shape[1:])],
        out_specs=pl.BlockSpec((1, Q_TILE, Q_WIDTH), lambda bi, qi: (bi, qi, 0)),
        out_shape=jax.ShapeDtypeStruct((b, t, Q_WIDTH), BF16),
        scratch_shapes=[pltpu.VMEM((t, Q_TILE), I32),
                        pltpu.VMEM((N_KV_HEADS, HEAD_AUG, Q_PER_KV * Q_TILE), F32),
                        pltpu.VMEM((N_KV_HEADS, HEAD_DIM, Q_PER_KV * Q_TILE), F32),
                        pltpu.VMEM((N_KV_HEADS, 1, Q_PER_KV * Q_TILE), F32),
                        pltpu.VMEM((N_KV_HEADS, 1, Q_PER_KV * Q_TILE), F32),
                        pltpu.VMEM((1, Q_TILE), I32)],
        compiler_params=_cparams(2),
        name="dsa_prompt",
    )(qT, qiT, wiT, kb, vT, kib)


def _fetch_pages_start(pt_ref, seq, layer_base, caches, bufs, sems, n_pages):
    def body(p, carry):
        page = layer_base + pt_ref[seq * n_pages + p]
        for cache, buf, sem in zip(caches, bufs, sems):
            pltpu.make_async_copy(cache.at[page], buf.at[p], sem).start()
        return carry
    lax.fori_loop(0, n_pages, body, 0)


def _fetch_pages_wait(caches, bufs, sems, n_pages):
    for cache, buf, sem in zip(caches, bufs, sems):
        pltpu.make_async_copy(cache.at[pl.ds(0, n_pages)], buf, sem).wait()


def _fetch_pages_pipelined(pt_ref, layer_base, caches, bufs, sems, n_pages):
    step, n_steps = pl.program_id(0), pl.num_programs(0)
    slot = step % 2
    at = lambda sl: ([b.at[sl] for b in bufs], [s.at[sl] for s in sems])

    @pl.when(step == 0)
    def _():
        _fetch_pages_start(pt_ref, 0, layer_base, caches, *at(0), n_pages)

    @pl.when(step + 1 < n_steps)
    def _():
        _fetch_pages_start(pt_ref, step + 1, layer_base, caches, *at(1 - slot), n_pages)

    _fetch_pages_wait(caches, *at(slot), n_pages)
    return slot


def _pages_bf16(buf, slot, first, count):
    return jnp.concatenate([buf[slot, first + j].astype(BF16) for j in range(count)], axis=1)


def _dsa_sample_scores_kernel(pt_ref, qi_ref, w_ref, kin_ref, kiT_hbm, keys_ref, buf_ref, sem,
                              *, n_pages, layer_base):
    slot = _fetch_pages_pipelined(pt_ref, layer_base, [kiT_hbm], [buf_ref], [sem], n_pages)
    qi = qi_ref[0]
    w = w_ref[0] * IDX_SCALE
    score = lambda dots: jnp.sum(jnp.maximum(dots, 0.0) * w, axis=0, keepdims=True)
    for g in range(n_pages // PAGE_GROUP):
        kiT = _pages_bf16(buf_ref, slot, g * PAGE_GROUP, PAGE_GROUP)
        keys = _float_order_key(score(jnp.dot(qi, kiT, preferred_element_type=F32)))
        keys_ref[0, :, g * PAGE_GROUP * PAGE_SIZE:(g + 1) * PAGE_GROUP * PAGE_SIZE] = keys
    kn = kin_ref[0].astype(BF16).astype(F32)
    sn = score(jnp.sum(qi.astype(F32) * kn, axis=1, keepdims=True))
    lane = lax.broadcasted_iota(I32, (1, LANES), 1)
    keys_ref[0, :, n_pages * PAGE_SIZE:] = jnp.where(lane == 0, _float_order_key(sn), INT_MIN)


def _dsa_sample_scores(pt_flat, qi3, w3, ki_new, cache_kiT, layer, n_pool, n_pages):
    db = qi3.shape[0]
    n_lanes = n_pages * PAGE_SIZE + LANES
    per_b = lambda shape: pl.BlockSpec((1,) + shape, lambda b, pt: (b,) + (0,) * len(shape))
    grid_spec = pltpu.PrefetchScalarGridSpec(
        num_scalar_prefetch=1,
        grid=(db,),
        in_specs=[per_b((IDX_HEADS, IDX_DIM)), per_b((IDX_HEADS, 1)), per_b((1, IDX_DIM)),
                  pl.BlockSpec(memory_space=pl.ANY)],
        out_specs=per_b((1, n_lanes)),
        scratch_shapes=[pltpu.VMEM((2, n_pages, IDX_DIM, PAGE_SIZE), F32), pltpu.SemaphoreType.DMA((2,))],
    )
    return pl.pallas_call(
        functools.partial(_dsa_sample_scores_kernel, n_pages=n_pages, layer_base=layer * n_pool),
        grid_spec=grid_spec,
        out_shape=jax.ShapeDtypeStruct((db, 1, n_lanes), I32),
        compiler_params=_cparams(1),
        name="dsa_sample_scores",
    )(pt_flat, qi3, w3, ki_new, cache_kiT)


def _dsa_sample_select_kernel(keys_ref, bias_ref, *, n_keys, topk, idx_bits):
    keys = keys_ref[...]
    idx = lax.broadcasted_iota(I32, keys.shape, 1)
    count = lambda hit: jnp.sum(jnp.where(hit, 1, 0), axis=1, keepdims=True)
    keys = jnp.where(idx < n_keys, keys, INT_MIN)

    def thr_bit(i, thr):
        cand = thr + lax.shift_left(jnp.int32(1), 31 - i)
        return jnp.where(count(keys >= cand) >= topk, cand, thr)

    thr = lax.fori_loop(0, 32, thr_bit, jnp.full((keys.shape[0], 1), INT_MIN, I32))
    need = topk - count(keys > thr)

    def idx_bit(i, j0):
        cand = j0 + lax.shift_left(jnp.int32(1), idx_bits - 1 - i)
        return jnp.where(count((keys == thr) & (idx < cand)) < need, cand, j0)

    j_sel = lax.fori_loop(0, idx_bits, idx_bit, jnp.zeros((keys.shape[0], 1), I32))
    sel = ((keys > thr) | ((keys == thr) & (idx <= j_sel))) & (idx < n_keys)
    bias_ref[...] = jnp.where(sel, 0.0, NEG)


def _dsa_sample_select(keys, n_keys):
    topk = min(TOPK_MAX, n_keys // 4)
    idx_bits = max(1, (n_keys - 1).bit_length())
    full = pl.BlockSpec(keys.shape, lambda i: (0, 0))
    return pl.pallas_call(
        functools.partial(_dsa_sample_select_kernel, n_keys=n_keys, topk=topk, idx_bits=idx_bits),
        grid=(1,),
        in_specs=[full],
        out_specs=full,
        out_shape=jax.ShapeDtypeStruct(keys.shape, F32),
        compiler_params=_cparams(1),
        name="dsa_sample_select",
    )(keys)


def _dsa_sample_attend_kernel(pt_ref, q_ref, bias_ref, kn_ref, vn_ref, kT_hbm, vT_hbm, o_ref,
                              kbuf_ref, vbuf_ref, ksem, vsem, *, n_pages, layer_base):
    slot = _fetch_pages_pipelined(pt_ref, layer_base, [kT_hbm, vT_hbm], [kbuf_ref, vbuf_ref],
                                  [ksem, vsem], n_pages)
    q = q_ref[0]
    n_past = n_pages * PAGE_SIZE
    group = PAGE_GROUP * PAGE_SIZE
    scores = [jnp.dot(q, _pages_bf16(kbuf_ref, slot, g * PAGE_GROUP, PAGE_GROUP), preferred_element_type=F32)
              + bias_ref[0, :, g * group:(g + 1) * group] for g in range(n_pages // PAGE_GROUP)]
    kn = kn_ref[0].astype(BF16).astype(F32)
    vn = vn_ref[0].astype(BF16).astype(F32)
    sn = jnp.sum(q.astype(F32) * kn, axis=1, keepdims=True) + bias_ref[0, :, n_past:n_past + 1]
    m = sn
    for s in scores:
        m = jnp.maximum(m, s.max(axis=1, keepdims=True))
    pn = jnp.exp(sn - m)
    denom = pn
    acc = pn.astype(BF16).astype(F32) * vn
    for g, s in enumerate(scores):
        p = jnp.exp(s - m)
        denom = denom + p.sum(axis=1, keepdims=True)
        acc = acc + lax.dot_general(p.astype(BF16), _pages_bf16(vbuf_ref, slot, g * PAGE_GROUP, PAGE_GROUP),
                                    (((1,), (1,)), ((), ())), preferred_element_type=F32)
    out = acc / denom
    head_grp = lax.broadcasted_iota(I32, out.shape, 0) // Q_PER_KV
    lane_grp = lax.broadcasted_iota(I32, out.shape, 1) // HEAD_DIM
    out = jnp.where(head_grp == lane_grp, out, 0.0)
    o = out[:, 0:HEAD_DIM]
    for n in range(1, N_KV_HEADS):
        o = o + out[:, n * HEAD_DIM:(n + 1) * HEAD_DIM]
    o_ref[0] = o


def _dsa_sample_attend(pt_flat, q_bd, bias, k_new, v_new, cache_kT, cache_vT, layer, n_pool, n_pages):
    db = q_bd.shape[0]
    per_b = lambda shape: pl.BlockSpec((1,) + shape, lambda b, pt: (b,) + (0,) * len(shape))
    hbm = pl.BlockSpec(memory_space=pl.ANY)
    pages = pltpu.VMEM((2, n_pages, KV_WIDTH, PAGE_SIZE), F32)
    grid_spec = pltpu.PrefetchScalarGridSpec(
        num_scalar_prefetch=1,
        grid=(db,),
        in_specs=[per_b(q_bd.shape[1:]), per_b(bias.shape[1:]), per_b((1, KV_WIDTH)), per_b((1, KV_WIDTH)),
                  hbm, hbm],
        out_specs=per_b((N_HEADS, HEAD_DIM)),
        scratch_shapes=[pages, pages, pltpu.SemaphoreType.DMA((2,)), pltpu.SemaphoreType.DMA((2,))],
    )
    return pl.pallas_call(
        functools.partial(_dsa_sample_attend_kernel, n_pages=n_pages, layer_base=layer * n_pool),
        grid_spec=grid_spec,
        out_shape=jax.ShapeDtypeStruct((db, N_HEADS, HEAD_DIM), F32),
        compiler_params=_cparams(1),
        name="dsa_sample_attend",
    )(pt_flat, q_bd, bias, k_new, v_new, cache_kT, cache_vT)


def _out_proj_kernel(a_ref, w_ref, x_ref, gt_ref, o_ref):
    y = jnp.dot(a_ref[0], w_ref[...], preferred_element_type=F32)
    o_ref[0] = x_ref[0] + gt_ref[0] * y


def _out_proj(a, w, x, gate):
    b, t, d = x.shape
    tm = min(ROW_TILE, t)
    row = lambda width: pl.BlockSpec((1, tm, width), lambda bi, i: (bi, i, 0))
    return pl.pallas_call(
        _out_proj_kernel,
        grid=(b, t // tm),
        in_specs=[row(a.shape[2]), pl.BlockSpec(w.shape, lambda bi, i: (0, 0)), row(d), _mod_spec(gate, tm)],
        out_specs=row(d),
        out_shape=jax.ShapeDtypeStruct((b, t, d), F32),
        compiler_params=_cparams(2),
        name="out_proj",
    )(a, w, x, gate)


def _gelu_tanh(x):
    return x * (0.5 * (1.0 + jnp.tanh(math.sqrt(2.0 / math.pi) * (x + 0.044715 * (x * x * x)))))


def _rec_proj_kernel(x_ref, g_ref, sh_ref, sc_ref, w_ref, gate_ref, xb_ref):
    h = _modulated_norm(x_ref[0], g_ref[...], sh_ref[0], sc_ref[0]).astype(BF16)
    acc = jnp.dot(h, w_ref[...], preferred_element_type=F32)
    gate_ref[0] = _gelu_tanh(acc[:, :D_RNN])
    xb_ref[0] = acc[:, D_RNN:]


def _rec_proj(x, g, shift, scale, w):
    b, t, d = x.shape
    tm = min(ROW_TILE, t)
    row = lambda width: pl.BlockSpec((1, tm, width), lambda bi, i: (bi, i, 0))
    return pl.pallas_call(
        _rec_proj_kernel,
        grid=(b, t // tm),
        in_specs=[row(d), pl.BlockSpec((1, d), lambda bi, i: (0, 0)), _mod_spec(shift, tm),
                  _mod_spec(scale, tm), pl.BlockSpec(w.shape, lambda bi, i: (0, 0))],
        out_specs=(row(D_RNN), row(D_RNN)),
        out_shape=(jax.ShapeDtypeStruct((b, t, D_RNN), F32),) * 2,
        compiler_params=_cparams(2),
        name="rec_proj",
    )(x, g, shift, scale, w)


def _rglru_gates(xc, wa_ref, ba, wx_ref, bx, lam):
    xcb = xc.astype(BF16)
    blk = lambda w_ref: jnp.concatenate(
        [jnp.dot(xcb[:, n * RG_BLOCK_W:(n + 1) * RG_BLOCK_W], w_ref[n], preferred_element_type=F32)
         for n in range(RG_BLOCKS)], axis=1)
    r = jax.nn.sigmoid(blk(wa_ref) + ba)
    i = jax.nn.sigmoid(blk(wx_ref) + bx)
    neg_lam = -lam
    softplus = jnp.maximum(neg_lam, 0.0) + jnp.log1p(jnp.exp(-jnp.abs(neg_lam)))
    log_a = -RG_C * r * softplus
    a = jnp.exp(log_a)
    u = jnp.sqrt(1.0 - a * a) * (i * xc)
    return a, u


def _rglru_prompt_kernel(gate_ref, xb_ref, cw_ref, cb_ref, wa_ref, ba_ref, wx_ref, bx_ref, lam_ref,
                         y_ref, h_ref, conv_ref, xcat_ref, a_ref, u_ref, hs_ref, hc_ref):
    tc = xb_ref.shape[1]
    c = pl.program_id(1)
    pad = SUBLANES

    @pl.when(c == 0)
    def _():
        xcat_ref[0:pad, :] = jnp.zeros((pad, D_RNN), F32)
        hc_ref[...] = jnp.zeros(hc_ref.shape, F32)

    @pl.when(c > 0)
    def _():
        xcat_ref[0:pad, :] = xcat_ref[tc:tc + pad, :]

    xcat_ref[pad:pad + tc, :] = xb_ref[0]
    cw = cw_ref[...]
    xc = xcat_ref[pad - 3:pad - 3 + tc, :] * cw[0:1, :]
    for j in range(1, CONV_W):
        xc = xc + xcat_ref[pad - 3 + j:pad - 3 + j + tc, :] * cw[j:j + 1, :]
    xc = cb_ref[...] + xc
    a, u = _rglru_gates(xc, wa_ref, ba_ref[...], wx_ref, bx_ref[...], lam_ref[...])
    a_ref[...] = a
    u_ref[...] = u

    def step(t, h):
        h = a_ref[pl.ds(t, 1), :] * h + u_ref[pl.ds(t, 1), :]
        hs_ref[pl.ds(t, 1), :] = h
        return h

    h = lax.fori_loop(0, tc, step, hc_ref[...], unroll=8)
    hc_ref[...] = h
    y_ref[0] = (hs_ref[...] * gate_ref[0]).astype(BF16)
    h_ref[0] = h
    conv_ref[0] = xcat_ref[pad + tc - (CONV_W - 1):pad + tc, :]


def _rglru_prompt(gate, xb, cw, cb, wa, ba, wx, bx, lam):
    b, t, d = xb.shape
    tc = min(SCAN_CHUNK, t)
    row = pl.BlockSpec((1, tc, d), lambda bi, i: (bi, i, 0))
    vec = pl.BlockSpec((1, d), lambda bi, i: (0, 0))
    full = lambda shape: pl.BlockSpec(shape, lambda bi, i: (0,) * len(shape))
    return pl.pallas_call(
        _rglru_prompt_kernel,
        grid=(b, t // tc),
        in_specs=[row, row, full(cw.shape), vec, full(wa.shape), vec, full(wx.shape), vec, vec],
        out_specs=(row, pl.BlockSpec((1, 1, d), lambda bi, i: (bi, 0, 0)),
                   pl.BlockSpec((1, CONV_W - 1, d), lambda bi, i: (bi, 0, 0))),
        out_shape=(jax.ShapeDtypeStruct((b, t, d), BF16), jax.ShapeDtypeStruct((b, 1, d), F32),
                   jax.ShapeDtypeStruct((b, CONV_W - 1, d), F32)),
        scratch_shapes=[pltpu.VMEM((tc + 2 * SUBLANES, d), F32), pltpu.VMEM((tc, d), F32),
                        pltpu.VMEM((tc, d), F32), pltpu.VMEM((tc, d), F32), pltpu.VMEM((1, d), F32)],
        compiler_params=_cparams(2),
        name="rglru_prompt",
    )(gate, xb, cw, cb, wa, ba, wx, bx, lam)


def _rglru_sample_kernel(gate_ref, xb_ref, buf_ref, h0_ref, cw_ref, cb_ref, wa_ref, ba_ref, wx_ref, bx_ref,
                         lam_ref, y_ref, h_ref, nbuf_ref):
    cw = cw_ref[...]
    xb = xb_ref[...]
    xc = buf_ref[0] * cw[0:1, :]
    for j in range(1, CONV_W - 1):
        xc = xc + buf_ref[j] * cw[j:j + 1, :]
    xc = cb_ref[...] + (xc + xb * cw[CONV_W - 1:CONV_W, :])
    a, u = _rglru_gates(xc, wa_ref, ba_ref[...], wx_ref, bx_ref[...], lam_ref[...])
    h = a * h0_ref[...] + u
    h_ref[...] = h
    y_ref[...] = (h * gate_ref[...]).astype(BF16)
    for j in range(CONV_W - 2):
        nbuf_ref[j] = buf_ref[j + 1]
    nbuf_ref[CONV_W - 2] = xb


def _rglru_sample(gate, xb, buf, h0, cw, cb, wa, ba, wx, bx, lam):
    rows, d = xb.shape
    full = lambda a: pl.BlockSpec(a.shape, lambda i: (0,) * a.ndim)
    args = (gate, xb, buf, h0, cw, cb, wa, ba, wx, bx, lam)
    return pl.pallas_call(
        _rglru_sample_kernel,
        grid=(1,),
        in_specs=[full(a) for a in args],
        out_specs=(full(xb), full(xb), full(buf)),
        out_shape=(jax.ShapeDtypeStruct((rows, d), BF16), jax.ShapeDtypeStruct((rows, d), F32),
                   jax.ShapeDtypeStruct(buf.shape, F32)),
        compiler_params=_cparams(1),
        name="rglru_sample",
    )(*args)


def _route(logits):
    lane = lax.broadcasted_iota(I32, logits.shape, 1)

    def first_max(v):
        m = jnp.max(v, axis=1, keepdims=True)
        return m, jnp.min(jnp.where(v == m, lane, LANES), axis=1, keepdims=True)

    is_grp = lane < N_GROUPS
    gm, grp = first_max(jnp.where(is_grp, logits, -jnp.inf))
    g_w = 1.0 / jnp.sum(jnp.where(is_grp, jnp.exp(logits - gm), 0.0), axis=1, keepdims=True)
    lo = N_GROUPS + EXPERTS_PER_GROUP * grp
    el = jnp.where((lane >= lo) & (lane < lo + EXPERTS_PER_GROUP), logits, -jnp.inf)
    m1, i1 = first_max(el)
    m2, i2 = first_max(jnp.where(lane == i1, -jnp.inf, el))
    e2 = jnp.exp(m2 - m1)
    den = 1.0 + e2
    return i1 - N_GROUPS, i2 - N_GROUPS, (1.0 / den) * g_w, (e2 / den) * g_w


def _moe_kernel(x_ref, g_ref, sh_ref, sc_ref, gt_ref, wr_ref, br_ref, w1_ref, w2_ref, gf_ref, o_ref,
                h_ref, acc_ref, e0_ref, e1_ref, c0_ref, c1_ref, *, final_norm):
    e = pl.program_id(2)

    @pl.when(e == 0)
    def _():
        h = _modulated_norm(x_ref[0], g_ref[...], sh_ref[0], sc_ref[0])
        h_ref[...] = h.astype(BF16)
        logits = jnp.dot(h_ref[...], wr_ref[...], preferred_element_type=F32) + br_ref[...]
        e0_ref[...], e1_ref[...], c0_ref[...], c1_ref[...] = _route(logits)
        acc_ref[...] = jnp.zeros(acc_ref.shape, F32)

    gu = jnp.dot(h_ref[...], w1_ref[0].astype(BF16), preferred_element_type=F32)
    gp, up = gu[:, :D_EXPERT], gu[:, D_EXPERT:]
    act = ((gp * jax.nn.sigmoid(gp)) * up).astype(BF16)
    y = jnp.dot(act, w2_ref[0].astype(BF16), preferred_element_type=F32)
    wt = jnp.where(e0_ref[...] == e, c0_ref[...], 0.0) + jnp.where(e1_ref[...] == e, c1_ref[...], 0.0)
    acc_ref[...] += y * wt

    @pl.when(e == N_EXPERTS - 1)
    def _():
        out = x_ref[0] + gt_ref[0] * acc_ref[...]
        if final_norm:
            out = (out * lax.rsqrt(jnp.mean(out * out, axis=-1, keepdims=True) + EPS)) * gf_ref[...]
        o_ref[0] = out


def _moe(x, g, shift, scale, gate, wr, br, w1, w2, layer, gf, final_norm):
    b, t, d = x.shape
    tm = min(MOE_ROW_TILE, t)
    row = pl.BlockSpec((1, tm, d), lambda bi, i, e: (bi, i, 0))
    vec = lambda width: pl.BlockSpec((1, width), lambda bi, i, e: (0, 0))
    col = lambda dt: pltpu.VMEM((tm, 1), dt)
    return pl.pallas_call(
        functools.partial(_moe_kernel, final_norm=final_norm),
        grid=(b, t // tm, N_EXPERTS),
        in_specs=[row, vec(d), _mod_spec(shift, tm), _mod_spec(scale, tm), _mod_spec(gate, tm),
                  pl.BlockSpec(wr.shape, lambda bi, i, e: (0, 0)), vec(ROUTER_PAD),
                  pl.BlockSpec((1, d, 2 * D_EXPERT), lambda bi, i, e: (layer * N_EXPERTS + e, 0, 0)),
                  pl.BlockSpec((1, D_EXPERT, d), lambda bi, i, e: (layer * N_EXPERTS + e, 0, 0)),
                  vec(d)],
        out_specs=row,
        out_shape=jax.ShapeDtypeStruct((b, t, d), F32),
        scratch_shapes=[pltpu.VMEM((tm, d), BF16), pltpu.VMEM((tm, d), F32),
                        col(I32), col(I32), col(F32), col(F32)],
        compiler_params=_cparams(3),
        name="moe",
    )(x, g, shift, scale, gate, wr, br, w1, w2, gf)


REC_E0, REC_E1, REC_RANK0, REC_RANK1, REC_C0, REC_C1 = range(6)


def _moe_route_kernel(x_ref, g_ref, sh_ref, sc_ref, wr_ref, br_ref, h_ref, rec_ref, recT_ref, cnt_ref,
                      tri_ref, carry_ref):
    tm = x_ref.shape[1]

    @pl.when((pl.program_id(0) == 0) & (pl.program_id(1) == 0))
    def _():
        carry_ref[...] = jnp.zeros(carry_ref.shape, F32)
        earlier = lax.broadcasted_iota(I32, (tm, tm), 0) > lax.broadcasted_iota(I32, (tm, tm), 1)
        tri_ref[...] = jnp.where(earlier, 1.0, 0.0).astype(BF16)

    h = _modulated_norm(x_ref[0], g_ref[...], sh_ref[0], sc_ref[0])
    h_ref[0] = h
    logits = jnp.dot(h.astype(BF16), wr_ref[...], preferred_element_type=F32) + br_ref[...]
    e0, e1, c0, c1 = _route(logits)
    lane = lax.broadcasted_iota(I32, logits.shape, 1)
    pick0, pick1 = lane == e0, lane == e1
    picks = jnp.where(pick0 | pick1, 1.0, 0.0)
    before = carry_ref[...] + jnp.dot(tri_ref[...], picks.astype(BF16), preferred_element_type=F32)
    rank0 = jnp.sum(jnp.where(pick0, before, 0.0), axis=1, keepdims=True)
    rank1 = jnp.sum(jnp.where(pick1, before, 0.0), axis=1, keepdims=True)
    carry_ref[...] = carry_ref[...] + jnp.sum(picks, axis=0, keepdims=True)
    cnt_ref[...] = carry_ref[...]
    rec = jnp.zeros(logits.shape, F32)
    for pos, val in ((REC_E0, e0.astype(F32)), (REC_E1, e1.astype(F32)), (REC_RANK0, rank0),
                     (REC_RANK1, rank1), (REC_C0, c0), (REC_C1, c1)):
        rec = jnp.where(lane == pos, val, rec)
    rec_ref[0] = rec
    recT_ref[0] = rec.T[0:SUBLANES, :]


def _moe_route(x, g, shift, scale, wr, br):
    b, t, d = x.shape
    tm = ROW_TILE
    row = lambda width: pl.BlockSpec((1, tm, width), lambda bi, i: (bi, i, 0))
    vec = lambda width: pl.BlockSpec((1, width), lambda bi, i: (0, 0))
    return pl.pallas_call(
        _moe_route_kernel,
        grid=(b, t // tm),
        in_specs=[row(d), vec(d), _mod_spec(shift, tm), _mod_spec(scale, tm),
                  pl.BlockSpec(wr.shape, lambda bi, i: (0, 0)), vec(ROUTER_PAD)],
        out_specs=(row(d), row(ROUTER_PAD), pl.BlockSpec((1, SUBLANES, tm), lambda bi, i: (bi, 0, i)),
                   vec(ROUTER_PAD)),
        out_shape=(jax.ShapeDtypeStruct((b, t, d), F32), jax.ShapeDtypeStruct((b, t, ROUTER_PAD), F32),
                   jax.ShapeDtypeStruct((b, SUBLANES, t), F32), jax.ShapeDtypeStruct((1, ROUTER_PAD), F32)),
        scratch_shapes=[pltpu.VMEM((tm, tm), BF16), pltpu.VMEM((1, ROUTER_PAD), F32)],
        compiler_params=_cparams(2),
        name="moe_route",
    )(x, g, shift, scale, wr, br)


def _gather_rows_start(idx_ref, idx_base, src_hbm, dst, sem, n_rows):
    def body(c, carry):
        r0 = pl.multiple_of(c * SUBLANES, SUBLANES)
        for j in range(SUBLANES):
            pltpu.make_async_copy(src_hbm.at[pl.ds(idx_ref[idx_base + r0 + j], 1)],
                                  dst.at[pl.ds(r0 + j, 1)], sem).start()
        return carry
    lax.fori_loop(0, n_rows // SUBLANES, body, 0)


def _gather_rows_wait(src_hbm, dst, sem, n_rows):
    pltpu.make_async_copy(src_hbm.at[pl.ds(0, n_rows)], dst, sem).wait()


def _moe_experts_kernel(blk_e_ref, n_used_ref, row_tok_ref, h_hbm, w1_ref, w2_ref, y_ref,
                        buf_ref, w1b_ref, w2b_ref, sem):
    i = pl.program_id(0)
    blk = buf_ref.shape[1]
    n_used = n_used_ref[0]
    n_slots = buf_ref.shape[0]
    slot = i % n_slots

    def start(block):
        s = block % n_slots
        _gather_rows_start(row_tok_ref, block * blk, h_hbm, buf_ref.at[s], sem.at[s], blk)

    @pl.when(i == 0)
    def _():
        for ahead in range(n_slots - 1):
            @pl.when(ahead < n_used)
            def _():
                start(ahead)

    @pl.when(i + n_slots - 1 < n_used)
    def _():
        start(i + n_slots - 1)

    @pl.when(i < n_used)
    def _():
        @pl.when((i == 0) | (blk_e_ref[i] != blk_e_ref[jnp.maximum(i - 1, 0)]))
        def _():
            w1b_ref[...] = w1_ref[0].astype(BF16)
            w2b_ref[...] = w2_ref[0].astype(BF16)

        _gather_rows_wait(h_hbm, buf_ref.at[slot], sem.at[slot], blk)
        gu = jnp.dot(buf_ref[slot].astype(BF16), w1b_ref[...], preferred_element_type=F32)
        gp, up = gu[:, :D_EXPERT], gu[:, D_EXPERT:]
        act = ((gp * jax.nn.sigmoid(gp)) * up).astype(BF16)
        y_ref[...] = jnp.dot(act, w2b_ref[...], preferred_element_type=F32)

    @pl.when(i >= n_used)
    def _():
        y_ref[...] = jnp.zeros(y_ref.shape, F32)


def _moe_experts(blk_expert, n_used, row_tok, h2d, w1, w2, layer):
    n_blk = blk_expert.shape[0]
    blk = MOE_BLOCK_ROWS
    d = h2d.shape[1]
    grid_spec = pltpu.PrefetchScalarGridSpec(
        num_scalar_prefetch=3,
        grid=(n_blk,),
        in_specs=[pl.BlockSpec(memory_space=pl.ANY),
                  pl.BlockSpec((1, d, 2 * D_EXPERT), lambda i, be, nu, rt: (layer * N_EXPERTS + be[i], 0, 0)),
                  pl.BlockSpec((1, D_EXPERT, d), lambda i, be, nu, rt: (layer * N_EXPERTS + be[i], 0, 0))],
        out_specs=pl.BlockSpec((blk, d), lambda i, be, nu, rt: (i, 0)),
        scratch_shapes=[pltpu.VMEM((GATHER_SLOTS, blk, d), F32), pltpu.VMEM((d, 2 * D_EXPERT), BF16),
                        pltpu.VMEM((D_EXPERT, d), BF16), pltpu.SemaphoreType.DMA((GATHER_SLOTS,))],
    )
    return pl.pallas_call(
        _moe_experts_kernel,
        grid_spec=grid_spec,
        out_shape=jax.ShapeDtypeStruct((n_blk * blk, d), F32),
        compiler_params=_cparams(1),
        name="moe_experts",
    )(blk_expert, n_used, row_tok, h2d, w1, w2)


def _moe_combine_kernel(dest_ref, y_hbm, x_ref, gt_ref, rec_ref, gf_ref, o_ref, buf_ref, sem,
                        *, n_tokens, final_norm):
    tm = x_ref.shape[1]
    n_steps = pl.num_programs(0) * pl.num_programs(1)
    s = pl.program_id(0) * pl.num_programs(1) + pl.program_id(1)
    slot = s % 2

    def start(step, sl):
        for pick in range(2):
            _gather_rows_start(dest_ref, pick * n_tokens + step * tm, y_hbm, buf_ref.at[sl, pick],
                               sem.at[sl, pick], tm)

    @pl.when(s == 0)
    def _():
        start(0, 0)

    @pl.when(s + 1 < n_steps)
    def _():
        start(s + 1, 1 - slot)

    for pick in range(2):
        _gather_rows_wait(y_hbm, buf_ref.at[slot, pick], sem.at[slot, pick], tm)
    rec = rec_ref[0]
    moe = buf_ref[slot, 0] * rec[:, REC_C0:REC_C0 + 1] + buf_ref[slot, 1] * rec[:, REC_C1:REC_C1 + 1]
    out = x_ref[0] + gt_ref[0] * moe
    if final_norm:
        out = (out * lax.rsqrt(jnp.mean(out * out, axis=-1, keepdims=True) + EPS)) * gf_ref[...]
    o_ref[0] = out


def _moe_combine(dest, y, x, gate, rec, gf, final_norm):
    b, t, d = x.shape
    tm = MOE_BLOCK_ROWS
    row = lambda width: pl.BlockSpec((1, tm, width), lambda bi, i, dst: (bi, i, 0))
    grid_spec = pltpu.PrefetchScalarGridSpec(
        num_scalar_prefetch=1,
        grid=(b, t // tm),
        in_specs=[pl.BlockSpec(memory_space=pl.ANY), row(d), _mod_spec(gate, tm), row(ROUTER_PAD),
                  pl.BlockSpec((1, d), lambda bi, i, dst: (0, 0))],
        out_specs=row(d),
        scratch_shapes=[pltpu.VMEM((2, 2, tm, d), F32), pltpu.SemaphoreType.DMA((2, 2))],
    )
    return pl.pallas_call(
        functools.partial(_moe_combine_kernel, n_tokens=b * t, final_norm=final_norm),
        grid_spec=grid_spec,
        out_shape=jax.ShapeDtypeStruct((b, t, d), F32),
        compiler_params=_cparams(2),
        name="moe_combine",
    )(dest, y, x, gate, rec, gf)


def _moe_sorted(x, g, shift, scale, gate, wr, br, w1, w2, layer, gf, final_norm):
    b, t, d = x.shape
    n = b * t
    blk = MOE_BLOCK_ROWS
    n_blk = (2 * n) // blk + N_EXPERTS
    h, rec, rec_t, cnt = _moe_route(x, g, shift, scale, wr, br)
    counts = cnt[0, :N_EXPERTS].astype(I32)
    padded = (counts + blk - 1) // blk * blk
    seg_end = jnp.cumsum(padded)
    seg_start = seg_end - padded
    col = lambda c: rec_t[:, c, :].reshape(n).astype(I32)
    dest = jnp.concatenate([seg_start[col(REC_E0)] + col(REC_RANK0), seg_start[col(REC_E1)] + col(REC_RANK1)])
    tok = jnp.arange(n, dtype=I32)
    row_tok = jnp.zeros((n_blk * blk,), I32).at[dest].set(jnp.concatenate([tok, tok]), unique_indices=True)
    blk_first_row = jnp.arange(n_blk, dtype=I32) * blk
    blk_expert = jnp.minimum(jnp.sum((seg_end[None, :] <= blk_first_row[:, None]).astype(I32), axis=1),
                             N_EXPERTS - 1)
    n_used = (seg_end[-1:] // blk).astype(I32)
    y = _moe_experts(blk_expert, n_used, row_tok, h.reshape(n, d), w1, w2, layer)
    return _moe_combine(dest, y, x, gate, rec, gf, final_norm)


def kernel(x_prompt, x_sample, cache_k, cache_v, cache_kidx, state_h, state_conv, page_table,
           c_prompt, c_sample, ada_w, ada_b, norm_mix_g, norm_ffn_g, norm_final_g,
           attn_w_in, attn_w_out, rec_w_in, rec_conv_w, rec_conv_b, rec_w_a, rec_b_a,
           rec_w_x, rec_b_x, rec_lambda, rec_w_out, moe_w_group, moe_b_group,
           moe_w_expert, moe_b_expert, moe_w1, moe_w2):
    b, t, d = x_prompt.shape
    db = x_sample.shape[0]
    n_pages = page_table.shape[1]
    past = n_pages * PAGE_SIZE
    n_pool = cache_k.shape[1]

    n_cond = b + db
    cond_rows = -(-n_cond // SUBLANES) * SUBLANES
    c_all = jnp.concatenate([c_prompt, c_sample, jnp.zeros((cond_rows - n_cond, d), F32)], axis=0)
    mod = _adaln(c_all, ada_w, ada_b)

    tables_p = _rope_tables(jnp.arange(t, dtype=I32))
    tables_s = _rope_tables(jnp.full((db,), past, I32))
    cache_kT = cache_k.transpose(0, 1, 3, 4, 2).reshape(-1, KV_WIDTH, PAGE_SIZE)
    cache_vT = cache_v.transpose(0, 1, 3, 4, 2).reshape(-1, KV_WIDTH, PAGE_SIZE)
    cache_kiT = cache_kidx.transpose(0, 1, 3, 2).reshape(-1, IDX_DIM, PAGE_SIZE)
    pt_flat = page_table.reshape(-1)
    moe_w1f = moe_w1.reshape(-1, d, 2 * D_EXPERT)
    moe_w2f = moe_w2.reshape(-1, D_EXPERT, d)
    row_vec = lambda v: v.reshape(1, -1)
    head_grp = jnp.arange(N_HEADS)[:, None] // Q_PER_KV == jnp.arange(KV_WIDTH)[None, :] // HEAD_DIM

    xp = x_prompt
    xs = x_sample.reshape(1, db, d)
    outs = {name: [] for name in ("k_p", "v_p", "ki_p", "h_p", "cv_p", "k_s", "v_s", "ki_s", "h_s", "cv_s")}
    for l in range(DEPTH):
        parts = [mod[l, :, i * d:(i + 1) * d] for i in range(6)]
        sh1p, sc1p, g1p, sh2p, sc2p, g2p = [m[:b].reshape(b, 1, d) for m in parts]
        sh1s, sc1s, g1s, sh2s, sc2s, g2s = [m[b:n_cond].reshape(1, db, d) for m in parts]
        g_mix = row_vec(norm_mix_g[l])
        if l % 2 == 0:
            a = l // 2
            w_in = jnp.pad(attn_w_in[a], ((0, 0), (0, ATTN_PROJ_PAD - ATTN_PROJ))).astype(BF16)
            w_out = attn_w_out[a].astype(BF16)
            kT, vT, kiT, qT, qiT, wiT, kb, vTb, kib = _attn_proj_prompt(xp, g_mix, sh1p, sc1p, w_in, tables_p)
            op = _dsa_prompt(qT, qiT, wiT, kb, vTb, kib)
            outs["k_p"].append(kT.reshape(b, N_KV_HEADS, HEAD_DIM, t).transpose(0, 3, 1, 2))
            outs["v_p"].append(vT.reshape(b, N_KV_HEADS, HEAD_DIM, t).transpose(0, 3, 1, 2))
            outs["ki_p"].append(kiT.transpose(0, 2, 1))
            xp = _out_proj(op, w_out, xp, g1p)

            q, k, v, qi, ki, wi = _attn_proj_sample(xs, g_mix, sh1s, sc1s, w_in, tables_s)
            keys = _dsa_sample_scores(pt_flat, qi.reshape(db, IDX_HEADS, IDX_DIM).astype(BF16),
                                      wi.reshape(db, IDX_HEADS, 1), ki.reshape(db, 1, IDX_DIM),
                                      cache_kiT, a, n_pool, n_pages)
            bias = _dsa_sample_select(keys.reshape(db, -1), past + 1).reshape(keys.shape)
            q_bd = jnp.where(head_grp[None], jnp.tile(q.reshape(db, N_HEADS, HEAD_DIM), (1, 1, N_KV_HEADS)),
                             0.0).astype(BF16)
            os_ = _dsa_sample_attend(pt_flat, q_bd, bias, k.reshape(db, 1, KV_WIDTH), v.reshape(db, 1, KV_WIDTH),
                                     cache_kT, cache_vT, a, n_pool, n_pages)
            outs["k_s"].append(k.reshape(db, 1, N_KV_HEADS, HEAD_DIM))
            outs["v_s"].append(v.reshape(db, 1, N_KV_HEADS, HEAD_DIM))
            outs["ki_s"].append(ki.reshape(db, 1, IDX_DIM))
            xs = _out_proj(os_.reshape(1, db, Q_WIDTH).astype(BF16), w_out, xs, g1s)
        else:
            r = l // 2
            w_in = rec_w_in[r].astype(BF16)
            w_out = rec_w_out[r].astype(BF16)
            rec = (rec_conv_w[r], row_vec(rec_conv_b[r]), rec_w_a[r].astype(BF16), row_vec(rec_b_a[r]),
                   rec_w_x[r].astype(BF16), row_vec(rec_b_x[r]), row_vec(rec_lambda[r]))
            gate, xb = _rec_proj(xp, g_mix, sh1p, sc1p, w_in)
            y, h_t, buf = _rglru_prompt(gate, xb, *rec)
            outs["h_p"].append(h_t.reshape(b, D_RNN))
            outs["cv_p"].append(buf)
            xp = _out_proj(y, w_out, xp, g1p)

            gate, xb = _rec_proj(xs, g_mix, sh1s, sc1s, w_in)
            y, h_t, buf = _rglru_sample(gate[0], xb[0], state_conv[r].swapaxes(0, 1), state_h[r], *rec)
            outs["h_s"].append(h_t)
            outs["cv_s"].append(buf.swapaxes(0, 1))
            xs = _out_proj(y.reshape(1, db, D_RNN), w_out, xs, g1s)

        wr = jnp.concatenate([moe_w_group[l], moe_w_expert[l],
                              jnp.zeros((d, ROUTER_PAD - N_GROUPS - N_EXPERTS), F32)], axis=1).astype(BF16)
        br = jnp.concatenate([moe_b_group[l], moe_b_expert[l],
                              jnp.zeros((ROUTER_PAD - N_GROUPS - N_EXPERTS,), F32)]).reshape(1, ROUTER_PAD)
        last = l == DEPTH - 1
        moe_args = (wr, br, moe_w1f, moe_w2f, l, row_vec(norm_final_g), last)
        xp = _moe_sorted(xp, row_vec(norm_ffn_g[l]), sh2p, sc2p, g2p, *moe_args)
        xs = _moe(xs, row_vec(norm_ffn_g[l]), sh2s, sc2s, g2s, *moe_args)

    st = lambda name: jnp.stack(outs[name])
    return (xp, xs.reshape(db, 1, d), st("k_p"), st("v_p"), st("ki_p"), st("h_p"), st("cv_p"),
            st("k_s"), st("v_s"), st("ki_s"), st("h_s"), st("cv_s"))
```

```python
import functools
import math

import jax
import jax.numpy as jnp
from jax import lax
from jax.experimental import pallas as pl
from jax.experimental.pallas import tpu as pltpu

F32 = jnp.float32
BF16 = jnp.bfloat16
I32 = jnp.int32

D_MODEL = 1024
DEPTH = 4
PAGE_SIZE = 128
N_HEADS = 16
HEAD_DIM = 64
N_KV_HEADS = 4
Q_PER_KV = N_HEADS // N_KV_HEADS
IDX_HEADS = 8
IDX_DIM = 64
IDX_SCALE = (IDX_HEADS * IDX_DIM) ** -0.5
TOPK_MAX = 256
ROPE_THETA = 500000.0
ROPE_FRACTION = 4
Q_WIDTH = N_HEADS * HEAD_DIM
KV_WIDTH = N_KV_HEADS * HEAD_DIM
QI_WIDTH = IDX_HEADS * IDX_DIM
ATTN_PROJ = Q_WIDTH + 2 * KV_WIDTH + QI_WIDTH + IDX_DIM + IDX_HEADS
D_RNN = D_MODEL
RG_BLOCKS = 8
RG_BLOCK_W = D_RNN // RG_BLOCKS
CONV_W = 4
RG_C = 8.0
N_GROUPS = 4
EXPERTS_PER_GROUP = 8
N_EXPERTS = N_GROUPS * EXPERTS_PER_GROUP
D_EXPERT = 256
EPS = 1e-6

LANES = 128
SUBLANES = 8
VMEM_LIMIT_BYTES = 56 * 1024 * 1024

ATTN_PROJ_PAD = -(-ATTN_PROJ // LANES) * LANES
ROUTER_PAD = LANES
Q_TILE = LANES
KEY_CHUNK = 512
ROW_TILE = 512
MOE_ROW_TILE = 1024
MOE_BLOCK_ROWS = 256
GATHER_SLOTS = 3
SCAN_CHUNK = 512
PAGE_GROUP = 8
BF16_SUBLANES = 16
HEAD_AUG = HEAD_DIM + BF16_SUBLANES
BOUND_SLACK = 1.02
DENOM_FLOOR = 2.0 ** -60
DENOM_CEIL = 2.0 ** 60
INT_MIN = -(2 ** 31)
INT_MAX = 2 ** 31 - 1
NEG = -1e30


def _cparams(n_axes):
    return pltpu.CompilerParams(dimension_semantics=("arbitrary",) * n_axes,
                                vmem_limit_bytes=VMEM_LIMIT_BYTES)


def _modulated_norm(x, g, shift, scale):
    xn = x * lax.rsqrt(jnp.mean(x * x, axis=-1, keepdims=True) + EPS)
    return (xn * g) * (1.0 + scale) + shift


def _rope_group(xs, cos, sa, sb):
    return xs * cos + pltpu.roll(xs, LANES - 8, 1) * sa + pltpu.roll(xs, 8, 1) * sb


def _float_order_key(s):
    bits = lax.bitcast_convert_type(s, I32)
    return bits ^ ((bits >> 31) & INT_MAX)


def _mod_spec(arr, tm):
    if arr.shape[1] == 1:
        return pl.BlockSpec((1, 1, arr.shape[2]), lambda b, i, *_: (b, 0, 0))
    return pl.BlockSpec((1, tm, arr.shape[2]), lambda b, i, *_: (b, i, 0))


def _adaln_kernel(c_ref, w_ref, b_ref, o_ref):
    c = c_ref[...]
    s = (c * jax.nn.sigmoid(c)).astype(BF16)
    o_ref[0] = jnp.dot(s, w_ref[0].astype(BF16), preferred_element_type=F32) + b_ref[0]


def _adaln(c_all, ada_w, ada_b):
    depth, d, n = ada_w.shape
    rows = c_all.shape[0]
    tn = 1536
    return pl.pallas_call(
        _adaln_kernel,
        grid=(depth, n // tn),
        in_specs=[pl.BlockSpec((rows, d), lambda l, j: (0, 0)),
                  pl.BlockSpec((1, d, tn), lambda l, j: (l, 0, j)),
                  pl.BlockSpec((1, 1, tn), lambda l, j: (l, 0, j))],
        out_specs=pl.BlockSpec((1, rows, tn), lambda l, j: (l, 0, j)),
        out_shape=jax.ShapeDtypeStruct((depth, rows, n), F32),
        compiler_params=_cparams(2),
        name="adaln",
    )(c_all, ada_w, ada_b.reshape(depth, 1, n))


def _rope_tables(pos):
    rot = HEAD_DIM // ROPE_FRACTION
    half = rot // 2
    inv = jnp.exp(-math.log(ROPE_THETA) * jnp.arange(half, dtype=F32) * (2.0 / rot))
    ang = pos.astype(F32)[:, None] * inv[None, :]
    cos, sin = jnp.cos(ang), jnp.sin(ang)
    r = pos.shape[0]
    z = lambda w: jnp.zeros((r, w), F32)
    cos64 = jnp.concatenate([cos, cos, jnp.ones((r, HEAD_DIM - rot), F32)], axis=1)
    sa64 = jnp.concatenate([-sin, z(HEAD_DIM - half)], axis=1)
    sb64 = jnp.concatenate([z(half), sin, z(HEAD_DIM - rot)], axis=1)
    two = lambda t: jnp.concatenate([t, t], axis=1)
    return two(cos64), two(sa64), two(sb64)


def _attn_proj_prompt_kernel(x_ref, g_ref, sh_ref, sc_ref, w_ref, cos_ref, sa_ref, sb_ref,
                             kT_ref, vT_ref, kiT_ref, qT_ref, qiT_ref, wiT_ref, kb_ref, vTb_ref, kib_ref,
                             kmax_ref):
    h = _modulated_norm(x_ref[0], g_ref[...], sh_ref[0], sc_ref[0]).astype(BF16)
    acc = jnp.dot(h, w_ref[...], preferred_element_type=F32)
    cos, sa, sb = cos_ref[...], sa_ref[...], sb_ref[...]
    n_qb = acc.shape[0] // Q_TILE
    grp = lambda off, j: acc[:, off + j * LANES: off + (j + 1) * LANES]

    tm = acc.shape[0]
    one_hot_row = lambda shape, axis: jnp.where(lax.broadcasted_iota(I32, shape, axis) == 0, 1.0, 0.0).astype(BF16)

    @pl.when(pl.program_id(1) == 0)
    def _():
        kmax_ref[...] = jnp.zeros(kmax_ref.shape, F32)

    for j in range(KV_WIDTH // LANES):
        kj = _rope_group(grp(Q_WIDTH, j), cos, sa, sb)
        kjT = kj.T
        kT_ref[0, j * LANES:(j + 1) * LANES, :] = kjT
        sq = kjT * kjT
        for hh in range(2):
            n = 2 * j + hh
            norm2 = jnp.sum(sq[hh * HEAD_DIM:(hh + 1) * HEAD_DIM, :], axis=0, keepdims=True)
            kmax_ref[n] = jnp.maximum(kmax_ref[n], jnp.max(norm2, axis=1, keepdims=True))
            kb_ref[0, n, :, 0:HEAD_DIM] = kj[:, hh * HEAD_DIM:(hh + 1) * HEAD_DIM].astype(BF16)
            kb_ref[0, n, :, HEAD_DIM:HEAD_AUG] = one_hot_row((tm, BF16_SUBLANES), 1)
    for j in range(Q_WIDTH // LANES):
        qjT = (_rope_group(grp(0, j), cos, sa, sb) * (HEAD_DIM ** -0.5)).T
        sq = qjT * qjT
        qjTb = qjT.astype(BF16)
        for hh in range(2):
            n, g = divmod(2 * j + hh, Q_PER_KV)
            norm2 = jnp.sum(sq[hh * HEAD_DIM:(hh + 1) * HEAD_DIM, :], axis=0, keepdims=True)
            shift = -BOUND_SLACK * jnp.sqrt(norm2 * kmax_ref[n])
            shift = jnp.concatenate([shift, jnp.zeros((BF16_SUBLANES - 1, tm), F32)], axis=0).astype(BF16)
            for jb in range(n_qb):
                qT_ref[0, jb, n, 0:HEAD_DIM, g * Q_TILE:(g + 1) * Q_TILE] = (
                    qjTb[hh * HEAD_DIM:(hh + 1) * HEAD_DIM, jb * Q_TILE:(jb + 1) * Q_TILE])
                qT_ref[0, jb, n, HEAD_DIM:HEAD_AUG, g * Q_TILE:(g + 1) * Q_TILE] = (
                    shift[:, jb * Q_TILE:(jb + 1) * Q_TILE])
    for j in range(KV_WIDTH // LANES):
        vjT = grp(Q_WIDTH + KV_WIDTH, j).T
        vT_ref[0, j * LANES:(j + 1) * LANES, :] = vjT
        for hh in range(2):
            n = 2 * j + hh
            vTb_ref[0, 0, n * HEAD_AUG:n * HEAD_AUG + HEAD_DIM, :] = (
                vjT[hh * HEAD_DIM:(hh + 1) * HEAD_DIM, :].astype(BF16))
            vTb_ref[0, 0, n * HEAD_AUG + HEAD_DIM:(n + 1) * HEAD_AUG, :] = one_hot_row((BF16_SUBLANES, tm), 0)
    off_qi = Q_WIDTH + 2 * KV_WIDTH
    for j in range(QI_WIDTH // LANES):
        qijT = _rope_group(grp(off_qi, j), cos, sa, sb).T.astype(BF16)
        for hh in range(2):
            head = 2 * j + hh
            for jb in range(n_qb):
                qiT_ref[0, jb, :, head * Q_TILE:(head + 1) * Q_TILE] = (
                    qijT[hh * IDX_DIM:(hh + 1) * IDX_DIM, jb * Q_TILE:(jb + 1) * Q_TILE])
    last = grp(off_qi + QI_WIDTH, 0)
    kir = _rope_group(last, cos, sa, sb)
    kiT_ref[0] = kir.T[:IDX_DIM, :]
    kib_ref[0] = kir[:, :IDX_DIM].astype(BF16)
    lastT = last.T
    for jb in range(n_qb):
        wiT_ref[0, jb] = lastT[IDX_DIM:IDX_DIM + IDX_HEADS, jb * Q_TILE:(jb + 1) * Q_TILE]


def _attn_proj_prompt(x, g, shift, scale, w_pad, tables):
    b, t, d = x.shape
    tm = ROW_TILE
    nq = t // Q_TILE
    grid = (b, t // tm)
    row = lambda width: pl.BlockSpec((1, tm, width), lambda bi, i: (bi, i, 0))
    col = lambda width: pl.BlockSpec((1, width, tm), lambda bi, i: (bi, 0, i))
    tab = pl.BlockSpec((tm, LANES), lambda bi, i: (i, 0))
    out_shape = (
        jax.ShapeDtypeStruct((b, KV_WIDTH, t), F32),
        jax.ShapeDtypeStruct((b, KV_WIDTH, t), F32),
        jax.ShapeDtypeStruct((b, IDX_DIM, t), F32),
        jax.ShapeDtypeStruct((b, nq, N_KV_HEADS, HEAD_AUG, Q_PER_KV * Q_TILE), BF16),
        jax.ShapeDtypeStruct((b, nq, IDX_DIM, IDX_HEADS * Q_TILE), BF16),
        jax.ShapeDtypeStruct((b, nq, IDX_HEADS, Q_TILE), F32),
        jax.ShapeDtypeStruct((b, N_KV_HEADS, t, HEAD_AUG), BF16),
        jax.ShapeDtypeStruct((b, t // tm, N_KV_HEADS * HEAD_AUG, tm), BF16),
        jax.ShapeDtypeStruct((b, t, IDX_DIM), BF16),
    )
    nqb = tm // Q_TILE
    out_specs = (
        col(KV_WIDTH), col(KV_WIDTH), col(IDX_DIM),
        pl.BlockSpec((1, nqb, N_KV_HEADS, HEAD_AUG, Q_PER_KV * Q_TILE), lambda bi, i: (bi, i, 0, 0, 0)),
        pl.BlockSpec((1, nqb, IDX_DIM, IDX_HEADS * Q_TILE), lambda bi, i: (bi, i, 0, 0)),
        pl.BlockSpec((1, nqb, IDX_HEADS, Q_TILE), lambda bi, i: (bi, i, 0, 0)),
        pl.BlockSpec((1, N_KV_HEADS, tm, HEAD_AUG), lambda bi, i: (bi, 0, i, 0)),
        pl.BlockSpec((1, 1, N_KV_HEADS * HEAD_AUG, tm), lambda bi, i: (bi, i, 0, 0)),
        row(IDX_DIM),
    )
    return pl.pallas_call(
        _attn_proj_prompt_kernel,
        grid=grid,
        in_specs=[row(d), pl.BlockSpec((1, d), lambda bi, i: (0, 0)),
                  _mod_spec(shift, tm), _mod_spec(scale, tm),
                  pl.BlockSpec(w_pad.shape, lambda bi, i: (0, 0)), tab, tab, tab],
        out_specs=out_specs,
        out_shape=out_shape,
        scratch_shapes=[pltpu.VMEM((N_KV_HEADS, 1, 1), F32)],
        compiler_params=_cparams(2),
        name="attn_proj_prompt",
    )(x, g, shift, scale, w_pad, *tables)


def _attn_proj_sample_kernel(x_ref, g_ref, sh_ref, sc_ref, w_ref, cos_ref, sa_ref, sb_ref,
                             q_ref, k_ref, v_ref, qi_ref, ki_ref, wi_ref):
    h = _modulated_norm(x_ref[0], g_ref[...], sh_ref[0], sc_ref[0]).astype(BF16)
    acc = jnp.dot(h, w_ref[...], preferred_element_type=F32)
    cos, sa, sb = cos_ref[...], sa_ref[...], sb_ref[...]
    grp = lambda off, j: acc[:, off + j * LANES: off + (j + 1) * LANES]
    for j in range(Q_WIDTH // LANES):
        q_ref[:, j * LANES:(j + 1) * LANES] = _rope_group(grp(0, j), cos, sa, sb) * (HEAD_DIM ** -0.5)
    for j in range(KV_WIDTH // LANES):
        k_ref[:, j * LANES:(j + 1) * LANES] = _rope_group(grp(Q_WIDTH, j), cos, sa, sb)
        v_ref[:, j * LANES:(j + 1) * LANES] = grp(Q_WIDTH + KV_WIDTH, j)
    off_qi = Q_WIDTH + 2 * KV_WIDTH
    for j in range(QI_WIDTH // LANES):
        qi_ref[:, j * LANES:(j + 1) * LANES] = _rope_group(grp(off_qi, j), cos, sa, sb)
    last = grp(off_qi + QI_WIDTH, 0)
    ki_ref[...] = _rope_group(last, cos, sa, sb)[:, :IDX_DIM]
    wi_ref[...] = last[:, IDX_DIM:IDX_DIM + IDX_HEADS]


def _attn_proj_sample(x, g, shift, scale, w_pad, tables):
    _, rows, d = x.shape
    full = lambda shape: pl.BlockSpec(shape, lambda i: (0,) * len(shape))
    widths = (Q_WIDTH, KV_WIDTH, KV_WIDTH, QI_WIDTH, IDX_DIM, IDX_HEADS)
    return pl.pallas_call(
        _attn_proj_sample_kernel,
        grid=(1,),
        in_specs=[full((1, rows, d)), full((1, d)), full((1, rows, d)), full((1, rows, d)),
                  full(w_pad.shape), full((rows, LANES)), full((rows, LANES)), full((rows, LANES))],
        out_specs=tuple(full((rows, w)) for w in widths),
        out_shape=tuple(jax.ShapeDtypeStruct((rows, w), F32) for w in widths),
        compiler_params=_cparams(1),
        name="attn_proj_sample",
    )(x, g, shift, scale, w_pad, *tables)


def _dsa_prompt_kernel(qT_ref, qiT_ref, wiT_ref, kb_ref, vT_ref, kib_ref, o_ref,
                       keys_ref, acc_ref, acc2_ref, m_ref, l_ref, j_ref, *, topk, idx_bits):
    kc = KEY_CHUNK
    qb = pl.program_id(1)
    n_chunks = (qb * Q_TILE + Q_TILE + kc - 1) // kc
    rows = lax.broadcasted_iota(I32, (kc, Q_TILE), 0)
    q_pos = qb * Q_TILE + lax.broadcasted_iota(I32, (kc, Q_TILE), 1)

    w_heads = wiT_ref[0, 0] * IDX_SCALE
    qiT = qiT_ref[0, 0]

    def score_chunk(c, carry):
        off = pl.multiple_of(c * kc, kc)
        dots = jnp.dot(kib_ref[0, pl.ds(off, kc), :], qiT, preferred_element_type=F32)
        s = jnp.zeros((kc, Q_TILE), F32)
        for h in range(IDX_HEADS):
            s = s + jnp.maximum(dots[:, h * Q_TILE:(h + 1) * Q_TILE], 0.0) * w_heads[h:h + 1, :]
        keys_ref[pl.ds(off, kc), :] = jnp.where(rows + off <= q_pos, _float_order_key(s), INT_MIN)
        return carry

    lax.fori_loop(0, n_chunks, score_chunk, 0)

    def count(pred):
        def body(c, cnt):
            off = pl.multiple_of(c * kc, kc)
            hit = jnp.where(pred(keys_ref[pl.ds(off, kc), :], rows + off), 1, 0)
            return cnt + hit.reshape(kc // SUBLANES, SUBLANES, Q_TILE).sum(axis=0)
        cnt = lax.fori_loop(0, n_chunks, body, jnp.zeros((SUBLANES, Q_TILE), I32))
        return cnt.sum(axis=0, keepdims=True)

    def thr_bit(i, state):
        thr, n_ge = state
        cand = thr + lax.shift_left(jnp.int32(1), 31 - i)
        n_cand = count(lambda k, s: k >= cand)
        ok = n_cand >= topk
        return jnp.where(ok, cand, thr), jnp.where(ok, n_cand, n_ge)

    n_all = jnp.full((1, Q_TILE), 1, I32) * (n_chunks * kc)
    thr, n_ge = lax.fori_loop(0, 32, thr_bit, (jnp.full((1, Q_TILE), INT_MIN, I32), n_all))

    j_ref[...] = jnp.full((1, Q_TILE), INT_MAX, I32)
    tie_split = jnp.max(jnp.where((n_ge > topk) & (thr > INT_MIN), 1, 0))

    @pl.when(tie_split > 0)
    def _():
        need = topk - count(lambda k, s: k > thr)

        def idx_bit(i, j0):
            cand = j0 + lax.shift_left(jnp.int32(1), idx_bits - 1 - i)
            taken_before = count(lambda k, s: (k == thr) & (s < cand))
            return jnp.where(taken_before < need, cand, j0)
        j_ref[...] = lax.fori_loop(0, idx_bits, idx_bit, jnp.zeros((1, Q_TILE), I32))

    j_sel = j_ref[...]

    def selection_bias(c):
        off = pl.multiple_of(c * kc, kc)
        k = keys_ref[pl.ds(off, kc), :]
        s_idx = rows + off
        sel = ((k > thr) | ((k == thr) & (s_idx <= j_sel))) & (s_idx <= q_pos)
        bias = jnp.where(sel, 0.0, NEG)
        return off, jnp.concatenate([bias] * Q_PER_KV, axis=1)

    acc_ref[...] = jnp.zeros(acc_ref.shape, F32)

    def attend_chunk(c, carry):
        off, bias = selection_bias(c)
        for n in range(N_KV_HEADS):
            s = jnp.dot(kb_ref[0, n, pl.ds(off, kc), :], qT_ref[0, 0, n], preferred_element_type=F32) + bias
            acc_ref[n] += jnp.dot(vT_ref[0, c, n * HEAD_AUG:(n + 1) * HEAD_AUG, :], jnp.exp(s).astype(BF16),
                                  preferred_element_type=F32)
        return carry

    lax.fori_loop(0, n_chunks, attend_chunk, 0)
    denom = jnp.concatenate([acc_ref[n][HEAD_DIM:HEAD_DIM + 1, :] for n in range(N_KV_HEADS)], axis=0)
    healthy = (jnp.min(denom) >= DENOM_FLOOR) & (jnp.max(denom) <= DENOM_CEIL)

    @pl.when(jnp.logical_not(healthy))
    def _():
        m_ref[...] = jnp.full(m_ref.shape, NEG, F32)
        l_ref[...] = jnp.zeros(l_ref.shape, F32)
        acc2_ref[...] = jnp.zeros(acc2_ref.shape, F32)

        def attend_chunk_online(c, carry):
            off, bias = selection_bias(c)
            for n in range(N_KV_HEADS):
                s = jnp.dot(kb_ref[0, n, pl.ds(off, kc), 0:HEAD_DIM], qT_ref[0, 0, n, 0:HEAD_DIM, :],
                            preferred_element_type=F32) + bias
                m_prev = m_ref[n]
                m_new = jnp.maximum(m_prev, s.max(axis=0, keepdims=True))
                alpha = jnp.exp(m_prev - m_new)
                p = jnp.exp(s - m_new)
                l_ref[n] = alpha * l_ref[n] + p.sum(axis=0, keepdims=True)
                pv = jnp.dot(vT_ref[0, c, n * HEAD_AUG:n * HEAD_AUG + HEAD_DIM, :], p.astype(BF16),
                             preferred_element_type=F32)
                acc2_ref[n] = alpha * acc2_ref[n] + pv
                m_ref[n] = m_new
            return carry

        lax.fori_loop(0, n_chunks, attend_chunk_online, 0)
        for n in range(N_KV_HEADS):
            acc_ref[n, 0:HEAD_DIM, :] = acc2_ref[n]
            acc_ref[n, HEAD_DIM:HEAD_DIM + 1, :] = l_ref[n]

    for n in range(N_KV_HEADS):
        on = acc_ref[n, 0:HEAD_DIM, :] / acc_ref[n, HEAD_DIM:HEAD_DIM + 1, :]
        for gp in range(Q_PER_KV // 2):
            pair = jnp.concatenate([on[:, (2 * gp) * Q_TILE:(2 * gp + 1) * Q_TILE],
                                    on[:, (2 * gp + 1) * Q_TILE:(2 * gp + 2) * Q_TILE]], axis=0)
            col = (n * Q_PER_KV // 2 + gp) * LANES
            o_ref[0, :, col:col + LANES] = pair.T.astype(BF16)


def _dsa_prompt(qT, qiT, wiT, kb, vT, kib):
    b, nq = qT.shape[:2]
    t = kb.shape[2]
    topk = min(TOPK_MAX, t // 4)
    idx_bits = max(1, (t - 1).bit_length())
    per_q = lambda shape: pl.BlockSpec((1, 1) + shape, lambda bi, qi: (bi, qi) + (0,) * len(shape))
    per_b = lambda shape: pl.BlockSpec((1,) + shape, lambda bi, qi: (bi,) + (0,) * len(shape))
    return pl.pallas_call(
        functools.partial(_dsa_prompt_kernel, topk=topk, idx_bits=idx_bits),
        grid=(b, nq),
        in_specs=[per_q(qT.shape[2:]), per_q(qiT.shape[2:]), per_q(wiT.shape[2:]),
                  per_b(kb.shape[1:]), per_b(vT.shape[1:]), per_b(kib.shape[1:])],
        out_specs=pl.BlockSpec((1, Q_TILE, Q_WIDTH), lambda bi, qi: (bi, qi, 0)),
        out_shape=jax.ShapeDtypeStruct((b, t, Q_WIDTH), BF16),
        scratch_shapes=[pltpu.VMEM((t, Q_TILE), I32),
                        pltpu.VMEM((N_KV_HEADS, HEAD_AUG, Q_PER_KV * Q_TILE), F32),
                        pltpu.VMEM((N_KV_HEADS, HEAD_DIM, Q_PER_KV * Q_TILE), F32),
                        pltpu.VMEM((N_KV_HEADS, 1, Q_PER_KV * Q_TILE), F32),
                        pltpu.VMEM((N_KV_HEADS, 1, Q_PER_KV * Q_TILE), F32),
                        pltpu.VMEM((1, Q_TILE), I32)],
        compiler_params=_cparams(2),
        name="dsa_prompt",
    )(qT, qiT, wiT, kb, vT, kib)


def _fetch_pages_start(pt_ref, seq, layer_base, caches, bufs, sems, n_pages):
    def body(p, carry):
        page = layer_base + pt_ref[seq * n_pages + p]
        for cache, buf, sem in zip(caches, bufs, sems):
            pltpu.make_async_copy(cache.at[page], buf.at[p], sem).start()
        return carry
    lax.fori_loop(0, n_pages, body, 0)


def _fetch_pages_wait(caches, bufs, sems, n_pages):
    for cache, buf, sem in zip(caches, bufs, sems):
        pltpu.make_async_copy(cache.at[pl.ds(0, n_pages)], buf, sem).wait()


def _fetch_pages_pipelined(pt_ref, layer_base, caches, bufs, sems, n_pages):
    step, n_steps = pl.program_id(0), pl.num_programs(0)
    slot = step % 2
    at = lambda sl: ([b.at[sl] for b in bufs], [s.at[sl] for s in sems])

    @pl.when(step == 0)
    def _():
        _fetch_pages_start(pt_ref, 0, layer_base, caches, *at(0), n_pages)

    @pl.when(step + 1 < n_steps)
    def _():
        _fetch_pages_start(pt_ref, step + 1, layer_base, caches, *at(1 - slot), n_pages)

    _fetch_pages_wait(caches, *at(slot), n_pages)
    return slot


def _pages_bf16(buf, slot, first, count):
    return jnp.concatenate([buf[slot, first + j].astype(BF16) for j in range(count)], axis=1)


def _dsa_sample_scores_kernel(pt_ref, qi_ref, w_ref, kin_ref, kiT_hbm, keys_ref, buf_ref, sem,
                              *, n_pages, layer_base):
    slot = _fetch_pages_pipelined(pt_ref, layer_base, [kiT_hbm], [buf_ref], [sem], n_pages)
    qi = qi_ref[0]
    w = w_ref[0] * IDX_SCALE
    score = lambda dots: jnp.sum(jnp.maximum(dots, 0.0) * w, axis=0, keepdims=True)
    for g in range(n_pages // PAGE_GROUP):
        kiT = _pages_bf16(buf_ref, slot, g * PAGE_GROUP, PAGE_GROUP)
        keys = _float_order_key(score(jnp.dot(qi, kiT, preferred_element_type=F32)))
        keys_ref[0, :, g * PAGE_GROUP * PAGE_SIZE:(g + 1) * PAGE_GROUP * PAGE_SIZE] = keys
    kn = kin_ref[0].astype(BF16).astype(F32)
    sn = score(jnp.sum(qi.astype(F32) * kn, axis=1, keepdims=True))
    lane = lax.broadcasted_iota(I32, (1, LANES), 1)
    keys_ref[0, :, n_pages * PAGE_SIZE:] = jnp.where(lane == 0, _float_order_key(sn), INT_MIN)


def _dsa_sample_scores(pt_flat, qi3, w3, ki_new, cache_kiT, layer, n_pool, n_pages):
    db = qi3.shape[0]
    n_lanes = n_pages * PAGE_SIZE + LANES
    per_b = lambda shape: pl.BlockSpec((1,) + shape, lambda b, pt: (b,) + (0,) * len(shape))
    grid_spec = pltpu.PrefetchScalarGridSpec(
        num_scalar_prefetch=1,
        grid=(db,),
        in_specs=[per_b((IDX_HEADS, IDX_DIM)), per_b((IDX_HEADS, 1)), per_b((1, IDX_DIM)),
                  pl.BlockSpec(memory_space=pl.ANY)],
        out_specs=per_b((1, n_lanes)),
        scratch_shapes=[pltpu.VMEM((2, n_pages, IDX_DIM, PAGE_SIZE), F32), pltpu.SemaphoreType.DMA((2,))],
    )
    return pl.pallas_call(
        functools.partial(_dsa_sample_scores_kernel, n_pages=n_pages, layer_base=layer * n_pool),
        grid_spec=grid_spec,
        out_shape=jax.ShapeDtypeStruct((db, 1, n_lanes), I32),
        compiler_params=_cparams(1),
        name="dsa_sample_scores",
    )(pt_flat, qi3, w3, ki_new, cache_kiT)


def _dsa_sample_select_kernel(keys_ref, bias_ref, *, n_keys, topk, idx_bits):
    keys = keys_ref[...]
    idx = lax.broadcasted_iota(I32, keys.shape, 1)
    count = lambda hit: jnp.sum(jnp.where(hit, 1, 0), axis=1, keepdims=True)
    keys = jnp.where(idx < n_keys, keys, INT_MIN)

    def thr_bit(i, thr):
        cand = thr + lax.shift_left(jnp.int32(1), 31 - i)
        return jnp.where(count(keys >= cand) >= topk, cand, thr)

    thr = lax.fori_loop(0, 32, thr_bit, jnp.full((keys.shape[0], 1), INT_MIN, I32))
    need = topk - count(keys > thr)

    def idx_bit(i, j0):
        cand = j0 + lax.shift_left(jnp.int32(1), idx_bits - 1 - i)
        return jnp.where(count((keys == thr) & (idx < cand)) < need, cand, j0)

    j_sel = lax.fori_loop(0, idx_bits, idx_bit, jnp.zeros((keys.shape[0], 1), I32))
    sel = ((keys > thr) | ((keys == thr) & (idx <= j_sel))) & (idx < n_keys)
    bias_ref[...] = jnp.where(sel, 0.0, NEG)


def _dsa_sample_select(keys, n_keys):
    topk = min(TOPK_MAX, n_keys // 4)
    idx_bits = max(1, (n_keys - 1).bit_length())
    full = pl.BlockSpec(keys.shape, lambda i: (0, 0))
    return pl.pallas_call(
        functools.partial(_dsa_sample_select_kernel, n_keys=n_keys, topk=topk, idx_bits=idx_bits),
        grid=(1,),
        in_specs=[full],
        out_specs=full,
        out_shape=jax.ShapeDtypeStruct(keys.shape, F32),
        compiler_params=_cparams(1),
        name="dsa_sample_select",
    )(keys)


def _dsa_sample_attend_kernel(pt_ref, q_ref, bias_ref, kn_ref, vn_ref, kT_hbm, vT_hbm, o_ref,
                              kbuf_ref, vbuf_ref, ksem, vsem, *, n_pages, layer_base):
    slot = _fetch_pages_pipelined(pt_ref, layer_base, [kT_hbm, vT_hbm], [kbuf_ref, vbuf_ref],
                                  [ksem, vsem], n_pages)
    q = q_ref[0]
    n_past = n_pages * PAGE_SIZE
    group = PAGE_GROUP * PAGE_SIZE
    scores = [jnp.dot(q, _pages_bf16(kbuf_ref, slot, g * PAGE_GROUP, PAGE_GROUP), preferred_element_type=F32)
              + bias_ref[0, :, g * group:(g + 1) * group] for g in range(n_pages // PAGE_GROUP)]
    kn = kn_ref[0].astype(BF16).astype(F32)
    vn = vn_ref[0].astype(BF16).astype(F32)
    sn = jnp.sum(q.astype(F32) * kn, axis=1, keepdims=True) + bias_ref[0, :, n_past:n_past + 1]
    m = sn
    for s in scores:
        m = jnp.maximum(m, s.max(axis=1, keepdims=True))
    pn = jnp.exp(sn - m)
    denom = pn
    acc = pn.astype(BF16).astype(F32) * vn
    for g, s in enumerate(scores):
        p = jnp.exp(s - m)
        denom = denom + p.sum(axis=1, keepdims=True)
        acc = acc + lax.dot_general(p.astype(BF16), _pages_bf16(vbuf_ref, slot, g * PAGE_GROUP, PAGE_GROUP),
                                    (((1,), (1,)), ((), ())), preferred_element_type=F32)
    out = acc / denom
    head_grp = lax.broadcasted_iota(I32, out.shape, 0) // Q_PER_KV
    lane_grp = lax.broadcasted_iota(I32, out.shape, 1) // HEAD_DIM
    out = jnp.where(head_grp == lane_grp, out, 0.0)
    o = out[:, 0:HEAD_DIM]
    for n in range(1, N_KV_HEADS):
        o = o + out[:, n * HEAD_DIM:(n + 1) * HEAD_DIM]
    o_ref[0] = o


def _dsa_sample_attend(pt_flat, q_bd, bias, k_new, v_new, cache_kT, cache_vT, layer, n_pool, n_pages):
    db = q_bd.shape[0]
    per_b = lambda shape: pl.BlockSpec((1,) + shape, lambda b, pt: (b,) + (0,) * len(shape))
    hbm = pl.BlockSpec(memory_space=pl.ANY)
    pages = pltpu.VMEM((2, n_pages, KV_WIDTH, PAGE_SIZE), F32)
    grid_spec = pltpu.PrefetchScalarGridSpec(
        num_scalar_prefetch=1,
        grid=(db,),
        in_specs=[per_b(q_bd.shape[1:]), per_b(bias.shape[1:]), per_b((1, KV_WIDTH)), per_b((1, KV_WIDTH)),
                  hbm, hbm],
        out_specs=per_b((N_HEADS, HEAD_DIM)),
        scratch_shapes=[pages, pages, pltpu.SemaphoreType.DMA((2,)), pltpu.SemaphoreType.DMA((2,))],
    )
    return pl.pallas_call(
        functools.partial(_dsa_sample_attend_kernel, n_pages=n_pages, layer_base=layer * n_pool),
        grid_spec=grid_spec,
        out_shape=jax.ShapeDtypeStruct((db, N_HEADS, HEAD_DIM), F32),
        compiler_params=_cparams(1),
        name="dsa_sample_attend",
    )(pt_flat, q_bd, bias, k_new, v_new, cache_kT, cache_vT)


def _out_proj_kernel(a_ref, w_ref, x_ref, gt_ref, o_ref):
    y = jnp.dot(a_ref[0], w_ref[...], preferred_element_type=F32)
    o_ref[0] = x_ref[0] + gt_ref[0] * y


def _out_proj(a, w, x, gate):
    b, t, d = x.shape
    tm = min(ROW_TILE, t)
    row = lambda width: pl.BlockSpec((1, tm, width), lambda bi, i: (bi, i, 0))
    return pl.pallas_call(
        _out_proj_kernel,
        grid=(b, t // tm),
        in_specs=[row(a.shape[2]), pl.BlockSpec(w.shape, lambda bi, i: (0, 0)), row(d), _mod_spec(gate, tm)],
        out_specs=row(d),
        out_shape=jax.ShapeDtypeStruct((b, t, d), F32),
        compiler_params=_cparams(2),
        name="out_proj",
    )(a, w, x, gate)


def _gelu_tanh(x):
    return x * (0.5 * (1.0 + jnp.tanh(math.sqrt(2.0 / math.pi) * (x + 0.044715 * (x * x * x)))))


def _rec_proj_kernel(x_ref, g_ref, sh_ref, sc_ref, w_ref, gate_ref, xb_ref):
    h = _modulated_norm(x_ref[0], g_ref[...], sh_ref[0], sc_ref[0]).astype(BF16)
    acc = jnp.dot(h, w_ref[...], preferred_element_type=F32)
    gate_ref[0] = _gelu_tanh(acc[:, :D_RNN])
    xb_ref[0] = acc[:, D_RNN:]


def _rec_proj(x, g, shift, scale, w):
    b, t, d = x.shape
    tm = min(ROW_TILE, t)
    row = lambda width: pl.BlockSpec((1, tm, width), lambda bi, i: (bi, i, 0))
    return pl.pallas_call(
        _rec_proj_kernel,
        grid=(b, t // tm),
        in_specs=[row(d), pl.BlockSpec((1, d), lambda bi, i: (0, 0)), _mod_spec(shift, tm),
                  _mod_spec(scale, tm), pl.BlockSpec(w.shape, lambda bi, i: (0, 0))],
        out_specs=(row(D_RNN), row(D_RNN)),
        out_shape=(jax.ShapeDtypeStruct((b, t, D_RNN), F32),) * 2,
        compiler_params=_cparams(2),
        name="rec_proj",
    )(x, g, shift, scale, w)


def _rglru_gates(xc, wa_ref, ba, wx_ref, bx, lam):
    xcb = xc.astype(BF16)
    blk = lambda w_ref: jnp.concatenate(
        [jnp.dot(xcb[:, n * RG_BLOCK_W:(n + 1) * RG_BLOCK_W], w_ref[n], preferred_element_type=F32)
         for n in range(RG_BLOCKS)], axis=1)
    r = jax.nn.sigmoid(blk(wa_ref) + ba)
    i = jax.nn.sigmoid(blk(wx_ref) + bx)
    neg_lam = -lam
    softplus = jnp.maximum(neg_lam, 0.0) + jnp.log1p(jnp.exp(-jnp.abs(neg_lam)))
    log_a = -RG_C * r * softplus
    a = jnp.exp(log_a)
    u = jnp.sqrt(1.0 - a * a) * (i * xc)
    return a, u


def _rglru_prompt_kernel(gate_ref, xb_ref, cw_ref, cb_ref, wa_ref, ba_ref, wx_ref, bx_ref, lam_ref,
                         y_ref, h_ref, conv_ref, xcat_ref, a_ref, u_ref, hs_ref, hc_ref):
    tc = xb_ref.shape[1]
    c = pl.program_id(1)
    pad = SUBLANES

    @pl.when(c == 0)
    def _():
        xcat_ref[0:pad, :] = jnp.zeros((pad, D_RNN), F32)
        hc_ref[...] = jnp.zeros(hc_ref.shape, F32)

    @pl.when(c > 0)
    def _():
        xcat_ref[0:pad, :] = xcat_ref[tc:tc + pad, :]

    xcat_ref[pad:pad + tc, :] = xb_ref[0]
    cw = cw_ref[...]
    xc = xcat_ref[pad - 3:pad - 3 + tc, :] * cw[0:1, :]
    for j in range(1, CONV_W):
        xc = xc + xcat_ref[pad - 3 + j:pad - 3 + j + tc, :] * cw[j:j + 1, :]
    xc = cb_ref[...] + xc
    a, u = _rglru_gates(xc, wa_ref, ba_ref[...], wx_ref, bx_ref[...], lam_ref[...])
    a_ref[...] = a
    u_ref[...] = u

    def step(t, h):
        h = a_ref[pl.ds(t, 1), :] * h + u_ref[pl.ds(t, 1), :]
        hs_ref[pl.ds(t, 1), :] = h
        return h

    h = lax.fori_loop(0, tc, step, hc_ref[...], unroll=8)
    hc_ref[...] = h
    y_ref[0] = (hs_ref[...] * gate_ref[0]).astype(BF16)
    h_ref[0] = h
    conv_ref[0] = xcat_ref[pad + tc - (CONV_W - 1):pad + tc, :]


def _rglru_prompt(gate, xb, cw, cb, wa, ba, wx, bx, lam):
    b, t, d = xb.shape
    tc = min(SCAN_CHUNK, t)
    row = pl.BlockSpec((1, tc, d), lambda bi, i: (bi, i, 0))
    vec = pl.BlockSpec((1, d), lambda bi, i: (0, 0))
    full = lambda shape: pl.BlockSpec(shape, lambda bi, i: (0,) * len(shape))
    return pl.pallas_call(
        _rglru_prompt_kernel,
        grid=(b, t // tc),
        in_specs=[row, row, full(cw.shape), vec, full(wa.shape), vec, full(wx.shape), vec, vec],
        out_specs=(row, pl.BlockSpec((1, 1, d), lambda bi, i: (bi, 0, 0)),
                   pl.BlockSpec((1, CONV_W - 1, d), lambda bi, i: (bi, 0, 0))),
        out_shape=(jax.ShapeDtypeStruct((b, t, d), BF16), jax.ShapeDtypeStruct((b, 1, d), F32),
                   jax.ShapeDtypeStruct((b, CONV_W - 1, d), F32)),
        scratch_shapes=[pltpu.VMEM((tc + 2 * SUBLANES, d), F32), pltpu.VMEM((tc, d), F32),
                        pltpu.VMEM((tc, d), F32), pltpu.VMEM((tc, d), F32), pltpu.VMEM((1, d), F32)],
        compiler_params=_cparams(2),
        name="rglru_prompt",
    )(gate, xb, cw, cb, wa, ba, wx, bx, lam)


def _rglru_sample_kernel(gate_ref, xb_ref, buf_ref, h0_ref, cw_ref, cb_ref, wa_ref, ba_ref, wx_ref, bx_ref,
                         lam_ref, y_ref, h_ref, nbuf_ref):
    cw = cw_ref[...]
    xb = xb_ref[...]
    xc = buf_ref[0] * cw[0:1, :]
    for j in range(1, CONV_W - 1):
        xc = xc + buf_ref[j] * cw[j:j + 1, :]
    xc = cb_ref[...] + (xc + xb * cw[CONV_W - 1:CONV_W, :])
    a, u = _rglru_gates(xc, wa_ref, ba_ref[...], wx_ref, bx_ref[...], lam_ref[...])
    h = a * h0_ref[...] + u
    h_ref[...] = h
    y_ref[...] = (h * gate_ref[...]).astype(BF16)
    for j in range(CONV_W - 2):
        nbuf_ref[j] = buf_ref[j + 1]
    nbuf_ref[CONV_W - 2] = xb


def _rglru_sample(gate, xb, buf, h0, cw, cb, wa, ba, wx, bx, lam):
    rows, d = xb.shape
    full = lambda a: pl.BlockSpec(a.shape, lambda i: (0,) * a.ndim)
    args = (gate, xb, buf, h0, cw, cb, wa, ba, wx, bx, lam)
    return pl.pallas_call(
        _rglru_sample_kernel,
        grid=(1,),
        in_specs=[full(a) for a in args],
        out_specs=(full(xb), full(xb), full(buf)),
        out_shape=(jax.ShapeDtypeStruct((rows, d), BF16), jax.ShapeDtypeStruct((rows, d), F32),
                   jax.ShapeDtypeStruct(buf.shape, F32)),
        compiler_params=_cparams(1),
        name="rglru_sample",
    )(*args)


def _route(logits):
    lane = lax.broadcasted_iota(I32, logits.shape, 1)

    def first_max(v):
        m = jnp.max(v, axis=1, keepdims=True)
        return m, jnp.min(jnp.where(v == m, lane, LANES), axis=1, keepdims=True)

    is_grp = lane < N_GROUPS
    gm, grp = first_max(jnp.where(is_grp, logits, -jnp.inf))
    g_w = 1.0 / jnp.sum(jnp.where(is_grp, jnp.exp(logits - gm), 0.0), axis=1, keepdims=True)
    lo = N_GROUPS + EXPERTS_PER_GROUP * grp
    el = jnp.where((lane >= lo) & (lane < lo + EXPERTS_PER_GROUP), logits, -jnp.inf)
    m1, i1 = first_max(el)
    m2, i2 = first_max(jnp.where(lane == i1, -jnp.inf, el))
    e2 = jnp.exp(m2 - m1)
    den = 1.0 + e2
    return i1 - N_GROUPS, i2 - N_GROUPS, (1.0 / den) * g_w, (e2 / den) * g_w


def _moe_kernel(x_ref, g_ref, sh_ref, sc_ref, gt_ref, wr_ref, br_ref, w1_ref, w2_ref, gf_ref, o_ref,
                h_ref, acc_ref, e0_ref, e1_ref, c0_ref, c1_ref, *, final_norm):
    e = pl.program_id(2)

    @pl.when(e == 0)
    def _():
        h = _modulated_norm(x_ref[0], g_ref[...], sh_ref[0], sc_ref[0])
        h_ref[...] = h.astype(BF16)
        logits = jnp.dot(h_ref[...], wr_ref[...], preferred_element_type=F32) + br_ref[...]
        e0_ref[...], e1_ref[...], c0_ref[...], c1_ref[...] = _route(logits)
        acc_ref[...] = jnp.zeros(acc_ref.shape, F32)

    gu = jnp.dot(h_ref[...], w1_ref[0].astype(BF16), preferred_element_type=F32)
    gp, up = gu[:, :D_EXPERT], gu[:, D_EXPERT:]
    act = ((gp * jax.nn.sigmoid(gp)) * up).astype(BF16)
    y = jnp.dot(act, w2_ref[0].astype(BF16), preferred_element_type=F32)
    wt = jnp.where(e0_ref[...] == e, c0_ref[...], 0.0) + jnp.where(e1_ref[...] == e, c1_ref[...], 0.0)
    acc_ref[...] += y * wt

    @pl.when(e == N_EXPERTS - 1)
    def _():
        out = x_ref[0] + gt_ref[0] * acc_ref[...]
        if final_norm:
            out = (out * lax.rsqrt(jnp.mean(out * out, axis=-1, keepdims=True) + EPS)) * gf_ref[...]
        o_ref[0] = out


def _moe(x, g, shift, scale, gate, wr, br, w1, w2, layer, gf, final_norm):
    b, t, d = x.shape
    tm = min(MOE_ROW_TILE, t)
    row = pl.BlockSpec((1, tm, d), lambda bi, i, e: (bi, i, 0))
    vec = lambda width: pl.BlockSpec((1, width), lambda bi, i, e: (0, 0))
    col = lambda dt: pltpu.VMEM((tm, 1), dt)
    return pl.pallas_call(
        functools.partial(_moe_kernel, final_norm=final_norm),
        grid=(b, t // tm, N_EXPERTS),
        in_specs=[row, vec(d), _mod_spec(shift, tm), _mod_spec(scale, tm), _mod_spec(gate, tm),
                  pl.BlockSpec(wr.shape, lambda bi, i, e: (0, 0)), vec(ROUTER_PAD),
                  pl.BlockSpec((1, d, 2 * D_EXPERT), lambda bi, i, e: (layer * N_EXPERTS + e, 0, 0)),
                  pl.BlockSpec((1, D_EXPERT, d), lambda bi, i, e: (layer * N_EXPERTS + e, 0, 0)),
                  vec(d)],
        out_specs=row,
        out_shape=jax.ShapeDtypeStruct((b, t, d), F32),
        scratch_shapes=[pltpu.VMEM((tm, d), BF16), pltpu.VMEM((tm, d), F32),
                        col(I32), col(I32), col(F32), col(F32)],
        compiler_params=_cparams(3),
        name="moe",
    )(x, g, shift, scale, gate, wr, br, w1, w2, gf)


REC_E0, REC_E1, REC_RANK0, REC_RANK1, REC_C0, REC_C1 = range(6)


def _moe_route_kernel(x_ref, g_ref, sh_ref, sc_ref, wr_ref, br_ref, h_ref, rec_ref, recT_ref, cnt_ref,
                      tri_ref, carry_ref):
    tm = x_ref.shape[1]

    @pl.when((pl.program_id(0) == 0) & (pl.program_id(1) == 0))
    def _():
        carry_ref[...] = jnp.zeros(carry_ref.shape, F32)
        earlier = lax.broadcasted_iota(I32, (tm, tm), 0) > lax.broadcasted_iota(I32, (tm, tm), 1)
        tri_ref[...] = jnp.where(earlier, 1.0, 0.0).astype(BF16)

    h = _modulated_norm(x_ref[0], g_ref[...], sh_ref[0], sc_ref[0])
    h_ref[0] = h
    logits = jnp.dot(h.astype(BF16), wr_ref[...], preferred_element_type=F32) + br_ref[...]
    e0, e1, c0, c1 = _route(logits)
    lane = lax.broadcasted_iota(I32, logits.shape, 1)
    pick0, pick1 = lane == e0, lane == e1
    picks = jnp.where(pick0 | pick1, 1.0, 0.0)
    before = carry_ref[...] + jnp.dot(tri_ref[...], picks.astype(BF16), preferred_element_type=F32)
    rank0 = jnp.sum(jnp.where(pick0, before, 0.0), axis=1, keepdims=True)
    rank1 = jnp.sum(jnp.where(pick1, before, 0.0), axis=1, keepdims=True)
    carry_ref[...] = carry_ref[...] + jnp.sum(picks, axis=0, keepdims=True)
    cnt_ref[...] = carry_ref[...]
    rec = jnp.zeros(logits.shape, F32)
    for pos, val in ((REC_E0, e0.astype(F32)), (REC_E1, e1.astype(F32)), (REC_RANK0, rank0),
                     (REC_RANK1, rank1), (REC_C0, c0), (REC_C1, c1)):
        rec = jnp.where(lane == pos, val, rec)
    rec_ref[0] = rec
    recT_ref[0] = rec.T[0:SUBLANES, :]


def _moe_route(x, g, shift, scale, wr, br):
    b, t, d = x.shape
    tm = ROW_TILE
    row = lambda width: pl.BlockSpec((1, tm, width), lambda bi, i: (bi, i, 0))
    vec = lambda width: pl.BlockSpec((1, width), lambda bi, i: (0, 0))
    return pl.pallas_call(
        _moe_route_kernel,
        grid=(b, t // tm),
        in_specs=[row(d), vec(d), _mod_spec(shift, tm), _mod_spec(scale, tm),
                  pl.BlockSpec(wr.shape, lambda bi, i: (0, 0)), vec(ROUTER_PAD)],
        out_specs=(row(d), row(ROUTER_PAD), pl.BlockSpec((1, SUBLANES, tm), lambda bi, i: (bi, 0, i)),
                   vec(ROUTER_PAD)),
        out_shape=(jax.ShapeDtypeStruct((b, t, d), F32), jax.ShapeDtypeStruct((b, t, ROUTER_PAD), F32),
                   jax.ShapeDtypeStruct((b, SUBLANES, t), F32), jax.ShapeDtypeStruct((1, ROUTER_PAD), F32)),
        scratch_shapes=[pltpu.VMEM((tm, tm), BF16), pltpu.VMEM((1, ROUTER_PAD), F32)],
        compiler_params=_cparams(2),
        name="moe_route",
    )(x, g, shift, scale, wr, br)


def _gather_rows_start(idx_ref, idx_base, src_hbm, dst, sem, n_rows):
    def body(c, carry):
        r0 = pl.multiple_of(c * SUBLANES, SUBLANES)
        for j in range(SUBLANES):
            pltpu.make_async_copy(src_hbm.at[pl.ds(idx_ref[idx_base + r0 + j], 1)],
                                  dst.at[pl.ds(r0 + j, 1)], sem).start(priority=j % 2)
        return carry
    lax.fori_loop(0, n_rows // SUBLANES, body, 0)


def _gather_rows_wait(src_hbm, dst, sem, n_rows):
    pltpu.make_async_copy(src_hbm.at[pl.ds(0, n_rows)], dst, sem).wait()


def _moe_experts_kernel(blk_e_ref, n_used_ref, row_tok_ref, h_hbm, w1_ref, w2_ref, y_ref,
                        buf_ref, w1b_ref, w2b_ref, sem):
    i = pl.program_id(0)
    blk = buf_ref.shape[1]
    n_used = n_used_ref[0]
    n_slots = buf_ref.shape[0]
    slot = i % n_slots

    def start(block):
        s = block % n_slots
        _gather_rows_start(row_tok_ref, block * blk, h_hbm, buf_ref.at[s], sem.at[s], blk)

    @pl.when(i == 0)
    def _():
        for ahead in range(n_slots - 1):
            @pl.when(ahead < n_used)
            def _():
                start(ahead)

    @pl.when(i + n_slots - 1 < n_used)
    def _():
        start(i + n_slots - 1)

    @pl.when(i < n_used)
    def _():
        @pl.when((i == 0) | (blk_e_ref[i] != blk_e_ref[jnp.maximum(i - 1, 0)]))
        def _():
            w1b_ref[...] = w1_ref[0].astype(BF16)
            w2b_ref[...] = w2_ref[0].astype(BF16)

        _gather_rows_wait(h_hbm, buf_ref.at[slot], sem.at[slot], blk)
        gu = jnp.dot(buf_ref[slot].astype(BF16), w1b_ref[...], preferred_element_type=F32)
        gp, up = gu[:, :D_EXPERT], gu[:, D_EXPERT:]
        act = ((gp * jax.nn.sigmoid(gp)) * up).astype(BF16)
        y_ref[...] = jnp.dot(act, w2b_ref[...], preferred_element_type=F32)

    @pl.when(i >= n_used)
    def _():
        y_ref[...] = jnp.zeros(y_ref.shape, F32)


def _moe_experts(blk_expert, n_used, row_tok, h2d, w1, w2, layer):
    n_blk = blk_expert.shape[0]
    blk = MOE_BLOCK_ROWS
    d = h2d.shape[1]
    grid_spec = pltpu.PrefetchScalarGridSpec(
        num_scalar_prefetch=3,
        grid=(n_blk,),
        in_specs=[pl.BlockSpec(memory_space=pl.ANY),
                  pl.BlockSpec((1, d, 2 * D_EXPERT), lambda i, be, nu, rt: (layer * N_EXPERTS + be[i], 0, 0)),
                  pl.BlockSpec((1, D_EXPERT, d), lambda i, be, nu, rt: (layer * N_EXPERTS + be[i], 0, 0))],
        out_specs=pl.BlockSpec((blk, d), lambda i, be, nu, rt: (i, 0)),
        scratch_shapes=[pltpu.VMEM((GATHER_SLOTS, blk, d), F32), pltpu.VMEM((d, 2 * D_EXPERT), BF16),
                        pltpu.VMEM((D_EXPERT, d), BF16), pltpu.SemaphoreType.DMA((GATHER_SLOTS,))],
    )
    return pl.pallas_call(
        _moe_experts_kernel,
        grid_spec=grid_spec,
        out_shape=jax.ShapeDtypeStruct((n_blk * blk, d), F32),
        compiler_params=_cparams(1),
        name="moe_experts",
    )(blk_expert, n_used, row_tok, h2d, w1, w2)


def _moe_combine_kernel(dest_ref, y_hbm, x_ref, gt_ref, rec_ref, gf_ref, o_ref, buf_ref, sem,
                        *, n_tokens, final_norm):
    tm = x_ref.shape[1]
    n_steps = pl.num_programs(0) * pl.num_programs(1)
    s = pl.program_id(0) * pl.num_programs(1) + pl.program_id(1)
    slot = s % 2

    def start(step, sl):
        for pick in range(2):
            _gather_rows_start(dest_ref, pick * n_tokens + step * tm, y_hbm, buf_ref.at[sl, pick],
                               sem.at[sl, pick], tm)

    @pl.when(s == 0)
    def _():
        start(0, 0)

    @pl.when(s + 1 < n_steps)
    def _():
        start(s + 1, 1 - slot)

    for pick in range(2):
        _gather_rows_wait(y_hbm, buf_ref.at[slot, pick], sem.at[slot, pick], tm)
    rec = rec_ref[0]
    moe = buf_ref[slot, 0] * rec[:, REC_C0:REC_C0 + 1] + buf_ref[slot, 1] * rec[:, REC_C1:REC_C1 + 1]
    out = x_ref[0] + gt_ref[0] * moe
    if final_norm:
        out = (out * lax.rsqrt(jnp.mean(out * out, axis=-1, keepdims=True) + EPS)) * gf_ref[...]
    o_ref[0] = out


def _moe_combine(dest, y, x, gate, rec, gf, final_norm):
    b, t, d = x.shape
    tm = MOE_BLOCK_ROWS
    row = lambda width: pl.BlockSpec((1, tm, width), lambda bi, i, dst: (bi, i, 0))
    grid_spec = pltpu.PrefetchScalarGridSpec(
        num_scalar_prefetch=1,
        grid=(b, t // tm),
        in_specs=[pl.BlockSpec(memory_space=pl.ANY), row(d), _mod_spec(gate, tm), row(ROUTER_PAD),
                  pl.BlockSpec((1, d), lambda bi, i, dst: (0, 0))],
        out_specs=row(d),
        scratch_shapes=[pltpu.VMEM((2, 2, tm, d), F32), pltpu.SemaphoreType.DMA((2, 2))],
    )
    return pl.pallas_call(
        functools.partial(_moe_combine_kernel, n_tokens=b * t, final_norm=final_norm),
        grid_spec=grid_spec,
        out_shape=jax.ShapeDtypeStruct((b, t, d), F32),
        compiler_params=_cparams(2),
        name="moe_combine",
    )(dest, y, x, gate, rec, gf)


def _moe_sorted(x, g, shift, scale, gate, wr, br, w1, w2, layer, gf, final_norm):
    b, t, d = x.shape
    n = b * t
    blk = MOE_BLOCK_ROWS
    n_blk = (2 * n) // blk + N_EXPERTS
    h, rec, rec_t, cnt = _moe_route(x, g, shift, scale, wr, br)
    counts = cnt[0, :N_EXPERTS].astype(I32)
    padded = (counts + blk - 1) // blk * blk
    seg_end = jnp.cumsum(padded)
    seg_start = seg_end - padded
    col = lambda c: rec_t[:, c, :].reshape(n).astype(I32)
    dest = jnp.concatenate([seg_start[col(REC_E0)] + col(REC_RANK0), seg_start[col(REC_E1)] + col(REC_RANK1)])
    tok = jnp.arange(n, dtype=I32)
    row_tok = jnp.zeros((n_blk * blk,), I32).at[dest].set(jnp.concatenate([tok, tok]), unique_indices=True)
    blk_first_row = jnp.arange(n_blk, dtype=I32) * blk
    blk_expert = jnp.minimum(jnp.sum((seg_end[None, :] <= blk_first_row[:, None]).astype(I32), axis=1),
                             N_EXPERTS - 1)
    n_used = (seg_end[-1:] // blk).astype(I32)
    y = _moe_experts(blk_expert, n_used, row_tok, h.reshape(n, d), w1, w2, layer)
    return _moe_combine(dest, y, x, gate, rec, gf, final_norm)


def kernel(x_prompt, x_sample, cache_k, cache_v, cache_kidx, state_h, state_conv, page_table,
           c_prompt, c_sample, ada_w, ada_b, norm_mix_g, norm_ffn_g, norm_final_g,
           attn_w_in, attn_w_out, rec_w_in, rec_conv_w, rec_conv_b, rec_w_a, rec_b_a,
           rec_w_x, rec_b_x, rec_lambda, rec_w_out, moe_w_group, moe_b_group,
           moe_w_expert, moe_b_expert, moe_w1, moe_w2):
    b, t, d = x_prompt.shape
    db = x_sample.shape[0]
    n_pages = page_table.shape[1]
    past = n_pages * PAGE_SIZE
    n_pool = cache_k.shape[1]

    n_cond = b + db
    cond_rows = -(-n_cond // SUBLANES) * SUBLANES
    c_all = jnp.concatenate([c_prompt, c_sample, jnp.zeros((cond_rows - n_cond, d), F32)], axis=0)
    mod = _adaln(c_all, ada_w, ada_b)

    tables_p = _rope_tables(jnp.arange(t, dtype=I32))
    tables_s = _rope_tables(jnp.full((db,), past, I32))
    cache_kT = cache_k.transpose(0, 1, 3, 4, 2).reshape(-1, KV_WIDTH, PAGE_SIZE)
    cache_vT = cache_v.transpose(0, 1, 3, 4, 2).reshape(-1, KV_WIDTH, PAGE_SIZE)
    cache_kiT = cache_kidx.transpose(0, 1, 3, 2).reshape(-1, IDX_DIM, PAGE_SIZE)
    pt_flat = page_table.reshape(-1)
    moe_w1f = moe_w1.reshape(-1, d, 2 * D_EXPERT)
    moe_w2f = moe_w2.reshape(-1, D_EXPERT, d)
    row_vec = lambda v: v.reshape(1, -1)
    head_grp = jnp.arange(N_HEADS)[:, None] // Q_PER_KV == jnp.arange(KV_WIDTH)[None, :] // HEAD_DIM

    xp = x_prompt
    xs = x_sample.reshape(1, db, d)
    outs = {name: [] for name in ("k_p", "v_p", "ki_p", "h_p", "cv_p", "k_s", "v_s", "ki_s", "h_s", "cv_s")}
    for l in range(DEPTH):
        parts = [mod[l, :, i * d:(i + 1) * d] for i in range(6)]
        sh1p, sc1p, g1p, sh2p, sc2p, g2p = [m[:b].reshape(b, 1, d) for m in parts]
        sh1s, sc1s, g1s, sh2s, sc2s, g2s = [m[b:n_cond].reshape(1, db, d) for m in parts]
        g_mix = row_vec(norm_mix_g[l])
        if l % 2 == 0:
            a = l // 2
            w_in = jnp.pad(attn_w_in[a], ((0, 0), (0, ATTN_PROJ_PAD - ATTN_PROJ))).astype(BF16)
            w_out = attn_w_out[a].astype(BF16)
            kT, vT, kiT, qT, qiT, wiT, kb, vTb, kib = _attn_proj_prompt(xp, g_mix, sh1p, sc1p, w_in, tables_p)
            op = _dsa_prompt(qT, qiT, wiT, kb, vTb, kib)
            outs["k_p"].append(kT.reshape(b, N_KV_HEADS, HEAD_DIM, t).transpose(0, 3, 1, 2))
            outs["v_p"].append(vT.reshape(b, N_KV_HEADS, HEAD_DIM, t).transpose(0, 3, 1, 2))
            outs["ki_p"].append(kiT.transpose(0, 2, 1))
            xp = _out_proj(op, w_out, xp, g1p)

            q, k, v, qi, ki, wi = _attn_proj_sample(xs, g_mix, sh1s, sc1s, w_in, tables_s)
            keys = _dsa_sample_scores(pt_flat, qi.reshape(db, IDX_HEADS, IDX_DIM).astype(BF16),
                                      wi.reshape(db, IDX_HEADS, 1), ki.reshape(db, 1, IDX_DIM),
                                      cache_kiT, a, n_pool, n_pages)
            bias = _dsa_sample_select(keys.reshape(db, -1), past + 1).reshape(keys.shape)
            q_bd = jnp.where(head_grp[None], jnp.tile(q.reshape(db, N_HEADS, HEAD_DIM), (1, 1, N_KV_HEADS)),
                             0.0).astype(BF16)
            os_ = _dsa_sample_attend(pt_flat, q_bd, bias, k.reshape(db, 1, KV_WIDTH), v.reshape(db, 1, KV_WIDTH),
                                     cache_kT, cache_vT, a, n_pool, n_pages)
            outs["k_s"].append(k.reshape(db, 1, N_KV_HEADS, HEAD_DIM))
            outs["v_s"].append(v.reshape(db, 1, N_KV_HEADS, HEAD_DIM))
            outs["ki_s"].append(ki.reshape(db, 1, IDX_DIM))
            xs = _out_proj(os_.reshape(1, db, Q_WIDTH).astype(BF16), w_out, xs, g1s)
        else:
            r = l // 2
            w_in = rec_w_in[r].astype(BF16)
            w_out = rec_w_out[r].astype(BF16)
            rec = (rec_conv_w[r], row_vec(rec_conv_b[r]), rec_w_a[r].astype(BF16), row_vec(rec_b_a[r]),
                   rec_w_x[r].astype(BF16), row_vec(rec_b_x[r]), row_vec(rec_lambda[r]))
            gate, xb = _rec_proj(xp, g_mix, sh1p, sc1p, w_in)
            y, h_t, buf = _rglru_prompt(gate, xb, *rec)
            outs["h_p"].append(h_t.reshape(b, D_RNN))
            outs["cv_p"].append(buf)
            xp = _out_proj(y, w_out, xp, g1p)

            gate, xb = _rec_proj(xs, g_mix, sh1s, sc1s, w_in)
            y, h_t, buf = _rglru_sample(gate[0], xb[0], state_conv[r].swapaxes(0, 1), state_h[r], *rec)
            outs["h_s"].append(h_t)
            outs["cv_s"].append(buf.swapaxes(0, 1))
            xs = _out_proj(y.reshape(1, db, D_RNN), w_out, xs, g1s)

        wr = jnp.concatenate([moe_w_group[l], moe_w_expert[l],
                              jnp.zeros((d, ROUTER_PAD - N_GROUPS - N_EXPERTS), F32)], axis=1).astype(BF16)
        br = jnp.concatenate([moe_b_group[l], moe_b_expert[l],
                              jnp.zeros((ROUTER_PAD - N_GROUPS - N_EXPERTS,), F32)]).reshape(1, ROUTER_PAD)
        last = l == DEPTH - 1
        moe_args = (wr, br, moe_w1f, moe_w2f, l, row_vec(norm_final_g), last)
        xp = _moe_sorted(xp, row_vec(norm_ffn_g[l]), sh2p, sc2p, g2p, *moe_args)
        xs = _moe(xs, row_vec(norm_ffn_g[l]), sh2s, sc2s, g2s, *moe_args)

    st = lambda name: jnp.stack(outs[name])
    return (xp, xs.reshape(db, 1, d), st("k_p"), st("v_p"), st("ki_p"), st("h_p"), st("cv_p"),
            st("k_s"), st("v_s"), st("ki_s"), st("h_s"), st("cv_s"))
```

```python
import functools
import math

import jax
import jax.numpy as jnp
from jax import lax
from jax.experimental import pallas as pl
from jax.experimental.pallas import tpu as pltpu

F32 = jnp.float32
BF16 = jnp.bfloat16
I32 = jnp.int32

D_MODEL = 1024
DEPTH = 4
PAGE_SIZE = 128
N_HEADS = 16
HEAD_DIM = 64
N_KV_HEADS = 4
Q_PER_KV = N_HEADS // N_KV_HEADS
IDX_HEADS = 8
IDX_DIM = 64
IDX_SCALE = (IDX_HEADS * IDX_DIM) ** -0.5
TOPK_MAX = 256
ROPE_THETA = 500000.0
ROPE_FRACTION = 4
Q_WIDTH = N_HEADS * HEAD_DIM
KV_WIDTH = N_KV_HEADS * HEAD_DIM
QI_WIDTH = IDX_HEADS * IDX_DIM
ATTN_PROJ = Q_WIDTH + 2 * KV_WIDTH + QI_WIDTH + IDX_DIM + IDX_HEADS
D_RNN = D_MODEL
RG_BLOCKS = 8
RG_BLOCK_W = D_RNN // RG_BLOCKS
CONV_W = 4
RG_C = 8.0
N_GROUPS = 4
EXPERTS_PER_GROUP = 8
N_EXPERTS = N_GROUPS * EXPERTS_PER_GROUP
D_EXPERT = 256
EPS = 1e-6

LANES = 128
SUBLANES = 8
VMEM_LIMIT_BYTES = 56 * 1024 * 1024

ATTN_PROJ_PAD = -(-ATTN_PROJ // LANES) * LANES
ROUTER_PAD = LANES
Q_TILE = LANES
KEY_CHUNK = 512
ROW_TILE = 512
MOE_ROW_TILE = 1024
MOE_BLOCK_ROWS = 256
GATHER_SLOTS = 3
SCAN_CHUNK = 512
PAGE_GROUP = 8
BF16_SUBLANES = 16
HEAD_AUG = HEAD_DIM + BF16_SUBLANES
BOUND_SLACK = 1.02
DENOM_FLOOR = 2.0 ** -60
DENOM_CEIL = 2.0 ** 60
INT_MIN = -(2 ** 31)
INT_MAX = 2 ** 31 - 1
NEG = -1e30


def _cparams(n_axes):
    return pltpu.CompilerParams(dimension_semantics=("arbitrary",) * n_axes,
                                vmem_limit_bytes=VMEM_LIMIT_BYTES)


def _modulated_norm(x, g, shift, scale):
    xn = x * lax.rsqrt(jnp.mean(x * x, axis=-1, keepdims=True) + EPS)
    return (xn * g) * (1.0 + scale) + shift


def _rope_group(xs, cos, sa, sb):
    return xs * cos + pltpu.roll(xs, LANES - 8, 1) * sa + pltpu.roll(xs, 8, 1) * sb


def _float_order_key(s):
    bits = lax.bitcast_convert_type(s, I32)
    return bits ^ ((bits >> 31) & INT_MAX)


def _mod_spec(arr, tm):
    if arr.shape[1] == 1:
        return pl.BlockSpec((1, 1, arr.shape[2]), lambda b, i, *_: (b, 0, 0))
    return pl.BlockSpec((1, tm, arr.shape[2]), lambda b, i, *_: (b, i, 0))


def _adaln_kernel(c_ref, w_ref, b_ref, o_ref):
    c = c_ref[...]
    s = (c * jax.nn.sigmoid(c)).astype(BF16)
    o_ref[0] = jnp.dot(s, w_ref[0].astype(BF16), preferred_element_type=F32) + b_ref[0]


def _adaln(c_all, ada_w, ada_b):
    depth, d, n = ada_w.shape
    rows = c_all.shape[0]
    tn = 1536
    return pl.pallas_call(
        _adaln_kernel,
        grid=(depth, n // tn),
        in_specs=[pl.BlockSpec((rows, d), lambda l, j: (0, 0)),
                  pl.BlockSpec((1, d, tn), lambda l, j: (l, 0, j)),
                  pl.BlockSpec((1, 1, tn), lambda l, j: (l, 0, j))],
        out_specs=pl.BlockSpec((1, rows, tn), lambda l, j: (l, 0, j)),
        out_shape=jax.ShapeDtypeStruct((depth, rows, n), F32),
        compiler_params=_cparams(2),
        name="adaln",
    )(c_all, ada_w, ada_b.reshape(depth, 1, n))


def _rope_tables(pos):
    rot = HEAD_DIM // ROPE_FRACTION
    half = rot // 2
    inv = jnp.exp(-math.log(ROPE_THETA) * jnp.arange(half, dtype=F32) * (2.0 / rot))
    ang = pos.astype(F32)[:, None] * inv[None, :]
    cos, sin = jnp.cos(ang), jnp.sin(ang)
    r = pos.shape[0]
    z = lambda w: jnp.zeros((r, w), F32)
    cos64 = jnp.concatenate([cos, cos, jnp.ones((r, HEAD_DIM - rot), F32)], axis=1)
    sa64 = jnp.concatenate([-sin, z(HEAD_DIM - half)], axis=1)
    sb64 = jnp.concatenate([z(half), sin, z(HEAD_DIM - rot)], axis=1)
    two = lambda t: jnp.concatenate([t, t], axis=1)
    return two(cos64), two(sa64), two(sb64)


def _attn_proj_prompt_kernel(x_ref, g_ref, sh_ref, sc_ref, w_ref, cos_ref, sa_ref, sb_ref,
                             kT_ref, vT_ref, kiT_ref, qT_ref, qiT_ref, wiT_ref, kb_ref, vTb_ref, kib_ref,
                             kmax_ref):
    h = _modulated_norm(x_ref[0], g_ref[...], sh_ref[0], sc_ref[0]).astype(BF16)
    acc = jnp.dot(h, w_ref[...], preferred_element_type=F32)
    cos, sa, sb = cos_ref[...], sa_ref[...], sb_ref[...]
    n_qb = acc.shape[0] // Q_TILE
    grp = lambda off, j: acc[:, off + j * LANES: off + (j + 1) * LANES]

    tm = acc.shape[0]
    one_hot_row = lambda shape, axis: jnp.where(lax.broadcasted_iota(I32, shape, axis) == 0, 1.0, 0.0).astype(BF16)

    @pl.when(pl.program_id(1) == 0)
    def _():
        kmax_ref[...] = jnp.zeros(kmax_ref.shape, F32)

    for j in range(KV_WIDTH // LANES):
        kj = _rope_group(grp(Q_WIDTH, j), cos, sa, sb)
        kjT = kj.T
        kT_ref[0, j * LANES:(j + 1) * LANES, :] = kjT
        sq = kjT * kjT
        for hh in range(2):
            n = 2 * j + hh
            norm2 = jnp.sum(sq[hh * HEAD_DIM:(hh + 1) * HEAD_DIM, :], axis=0, keepdims=True)
            kmax_ref[n] = jnp.maximum(kmax_ref[n], jnp.max(norm2, axis=1, keepdims=True))
            kb_ref[0, n, :, 0:HEAD_DIM] = kj[:, hh * HEAD_DIM:(hh + 1) * HEAD_DIM].astype(BF16)
            kb_ref[0, n, :, HEAD_DIM:HEAD_AUG] = one_hot_row((tm, BF16_SUBLANES), 1)
    for j in range(Q_WIDTH // LANES):
        qjT = (_rope_group(grp(0, j), cos, sa, sb) * (HEAD_DIM ** -0.5)).T
        sq = qjT * qjT
        qjTb = qjT.astype(BF16)
        for hh in range(2):
            n, g = divmod(2 * j + hh, Q_PER_KV)
            norm2 = jnp.sum(sq[hh * HEAD_DIM:(hh + 1) * HEAD_DIM, :], axis=0, keepdims=True)
            shift = -BOUND_SLACK * jnp.sqrt(norm2 * kmax_ref[n])
            shift = jnp.concatenate([shift, jnp.zeros((BF16_SUBLANES - 1, tm), F32)], axis=0).astype(BF16)
            for jb in range(n_qb):
                qT_ref[0, jb, n, 0:HEAD_DIM, g * Q_TILE:(g + 1) * Q_TILE] = (
                    qjTb[hh * HEAD_DIM:(hh + 1) * HEAD_DIM, jb * Q_TILE:(jb + 1) * Q_TILE])
                qT_ref[0, jb, n, HEAD_DIM:HEAD_AUG, g * Q_TILE:(g + 1) * Q_TILE] = (
                    shift[:, jb * Q_TILE:(jb + 1) * Q_TILE])
    for j in range(KV_WIDTH // LANES):
        vjT = grp(Q_WIDTH + KV_WIDTH, j).T
        vT_ref[0, j * LANES:(j + 1) * LANES, :] = vjT
        for hh in range(2):
            n = 2 * j + hh
            vTb_ref[0, 0, n * HEAD_AUG:n * HEAD_AUG + HEAD_DIM, :] = (
                vjT[hh * HEAD_DIM:(hh + 1) * HEAD_DIM, :].astype(BF16))
            vTb_ref[0, 0, n * HEAD_AUG + HEAD_DIM:(n + 1) * HEAD_AUG, :] = one_hot_row((BF16_SUBLANES, tm), 0)
    off_qi = Q_WIDTH + 2 * KV_WIDTH
    for j in range(QI_WIDTH // LANES):
        qijT = _rope_group(grp(off_qi, j), cos, sa, sb).T.astype(BF16)
        for hh in range(2):
            head = 2 * j + hh
            for jb in range(n_qb):
                qiT_ref[0, jb, :, head * Q_TILE:(head + 1) * Q_TILE] = (
                    qijT[hh * IDX_DIM:(hh + 1) * IDX_DIM, jb * Q_TILE:(jb + 1) * Q_TILE])
    last = grp(off_qi + QI_WIDTH, 0)
    kir = _rope_group(last, cos, sa, sb)
    kiT_ref[0] = kir.T[:IDX_DIM, :]
    kib_ref[0] = kir[:, :IDX_DIM].astype(BF16)
    lastT = last.T
    for jb in range(n_qb):
        wiT_ref[0, jb] = lastT[IDX_DIM:IDX_DIM + IDX_HEADS, jb * Q_TILE:(jb + 1) * Q_TILE]


def _attn_proj_prompt(x, g, shift, scale, w_pad, tables):
    b, t, d = x.shape
    tm = ROW_TILE
    nq = t // Q_TILE
    grid = (b, t // tm)
    row = lambda width: pl.BlockSpec((1, tm, width), lambda bi, i: (bi, i, 0))
    col = lambda width: pl.BlockSpec((1, width, tm), lambda bi, i: (bi, 0, i))
    tab = pl.BlockSpec((tm, LANES), lambda bi, i: (i, 0))
    out_shape = (
        jax.ShapeDtypeStruct((b, KV_WIDTH, t), F32),
        jax.ShapeDtypeStruct((b, KV_WIDTH, t), F32),
        jax.ShapeDtypeStruct((b, IDX_DIM, t), F32),
        jax.ShapeDtypeStruct((b, nq, N_KV_HEADS, HEAD_AUG, Q_PER_KV * Q_TILE), BF16),
        jax.ShapeDtypeStruct((b, nq, IDX_DIM, IDX_HEADS * Q_TILE), BF16),
        jax.ShapeDtypeStruct((b, nq, IDX_HEADS, Q_TILE), F32),
        jax.ShapeDtypeStruct((b, N_KV_HEADS, t, HEAD_AUG), BF16),
        jax.ShapeDtypeStruct((b, t // tm, N_KV_HEADS * HEAD_AUG, tm), BF16),
        jax.ShapeDtypeStruct((b, t, IDX_DIM), BF16),
    )
    nqb = tm // Q_TILE
    out_specs = (
        col(KV_WIDTH), col(KV_WIDTH), col(IDX_DIM),
        pl.BlockSpec((1, nqb, N_KV_HEADS, HEAD_AUG, Q_PER_KV * Q_TILE), lambda bi, i: (bi, i, 0, 0, 0)),
        pl.BlockSpec((1, nqb, IDX_DIM, IDX_HEADS * Q_TILE), lambda bi, i: (bi, i, 0, 0)),
        pl.BlockSpec((1, nqb, IDX_HEADS, Q_TILE), lambda bi, i: (bi, i, 0, 0)),
        pl.BlockSpec((1, N_KV_HEADS, tm, HEAD_AUG), lambda bi, i: (bi, 0, i, 0)),
        pl.BlockSpec((1, 1, N_KV_HEADS * HEAD_AUG, tm), lambda bi, i: (bi, i, 0, 0)),
        row(IDX_DIM),
    )
    return pl.pallas_call(
        _attn_proj_prompt_kernel,
        grid=grid,
        in_specs=[row(d), pl.BlockSpec((1, d), lambda bi, i: (0, 0)),
                  _mod_spec(shift, tm), _mod_spec(scale, tm),
                  pl.BlockSpec(w_pad.shape, lambda bi, i: (0, 0)), tab, tab, tab],
        out_specs=out_specs,
        out_shape=out_shape,
        scratch_shapes=[pltpu.VMEM((N_KV_HEADS, 1, 1), F32)],
        compiler_params=_cparams(2),
        name="attn_proj_prompt",
    )(x, g, shift, scale, w_pad, *tables)


def _attn_proj_sample_kernel(x_ref, g_ref, sh_ref, sc_ref, w_ref, cos_ref, sa_ref, sb_ref,
                             q_ref, k_ref, v_ref, qi_ref, ki_ref, wi_ref):
    h = _modulated_norm(x_ref[0], g_ref[...], sh_ref[0], sc_ref[0]).astype(BF16)
    acc = jnp.dot(h, w_ref[...], preferred_element_type=F32)
    cos, sa, sb = cos_ref[...], sa_ref[...], sb_ref[...]
    grp = lambda off, j: acc[:, off + j * LANES: off + (j + 1) * LANES]
    for j in range(Q_WIDTH // LANES):
        q_ref[:, j * LANES:(j + 1) * LANES] = _rope_group(grp(0, j), cos, sa, sb) * (HEAD_DIM ** -0.5)
    for j in range(KV_WIDTH // LANES):
        k_ref[:, j * LANES:(j + 1) * LANES] = _rope_group(grp(Q_WIDTH, j), cos, sa, sb)
        v_ref[:, j * LANES:(j + 1) * LANES] = grp(Q_WIDTH + KV_WIDTH, j)
    off_qi = Q_WIDTH + 2 * KV_WIDTH
    for j in range(QI_WIDTH // LANES):
        qi_ref[:, j * LANES:(j + 1) * LANES] = _rope_group(grp(off_qi, j), cos, sa, sb)
    last = grp(off_qi + QI_WIDTH, 0)
    ki_ref[...] = _rope_group(last, cos, sa, sb)[:, :IDX_DIM]
    wi_ref[...] = last[:, IDX_DIM:IDX_DIM + IDX_HEADS]


def _attn_proj_sample(x, g, shift, scale, w_pad, tables):
    _, rows, d = x.shape
    full = lambda shape: pl.BlockSpec(shape, lambda i: (0,) * len(shape))
    widths = (Q_WIDTH, KV_WIDTH, KV_WIDTH, QI_WIDTH, IDX_DIM, IDX_HEADS)
    return pl.pallas_call(
        _attn_proj_sample_kernel,
        grid=(1,),
        in_specs=[full((1, rows, d)), full((1, d)), full((1, rows, d)), full((1, rows, d)),
                  full(w_pad.shape), full((rows, LANES)), full((rows, LANES)), full((rows, LANES))],
        out_specs=tuple(full((rows, w)) for w in widths),
        out_shape=tuple(jax.ShapeDtypeStruct((rows, w), F32) for w in widths),
        compiler_params=_cparams(1),
        name="attn_proj_sample",
    )(x, g, shift, scale, w_pad, *tables)


def _dsa_prompt_kernel(qT_ref, qiT_ref, wiT_ref, kb_ref, vT_ref, kib_ref, o_ref,
                       keys_ref, acc_ref, acc2_ref, m_ref, l_ref, j_ref, *, topk, idx_bits):
    kc = KEY_CHUNK
    qb = pl.program_id(1)
    n_chunks = (qb * Q_TILE + Q_TILE + kc - 1) // kc
    rows = lax.broadcasted_iota(I32, (kc, Q_TILE), 0)
    q_pos = qb * Q_TILE + lax.broadcasted_iota(I32, (kc, Q_TILE), 1)

    w_heads = wiT_ref[0, 0] * IDX_SCALE
    qiT = qiT_ref[0, 0]

    def score_chunk(c, carry):
        off = pl.multiple_of(c * kc, kc)
        dots = jnp.dot(kib_ref[0, pl.ds(off, kc), :], qiT, preferred_element_type=F32)
        s = jnp.zeros((kc, Q_TILE), F32)
        for h in range(IDX_HEADS):
            s = s + jnp.maximum(dots[:, h * Q_TILE:(h + 1) * Q_TILE], 0.0) * w_heads[h:h + 1, :]
        keys_ref[pl.ds(off, kc), :] = jnp.where(rows + off <= q_pos, _float_order_key(s), INT_MIN)
        return carry

    lax.fori_loop(0, n_chunks, score_chunk, 0)

    def count(pred):
        def body(c, cnt):
            off = pl.multiple_of(c * kc, kc)
            hit = jnp.where(pred(keys_ref[pl.ds(off, kc), :], rows + off), 1, 0)
            return cnt + hit.reshape(kc // SUBLANES, SUBLANES, Q_TILE).sum(axis=0)
        cnt = lax.fori_loop(0, n_chunks, body, jnp.zeros((SUBLANES, Q_TILE), I32))
        return cnt.sum(axis=0, keepdims=True)

    def thr_bit(i, state):
        thr, n_ge = state
        cand = thr + lax.shift_left(jnp.int32(1), 31 - i)
        n_cand = count(lambda k, s: k >= cand)
        ok = n_cand >= topk
        return jnp.where(ok, cand, thr), jnp.where(ok, n_cand, n_ge)

    n_all = jnp.full((1, Q_TILE), 1, I32) * (n_chunks * kc)
    thr, n_ge = lax.fori_loop(0, 32, thr_bit, (jnp.full((1, Q_TILE), INT_MIN, I32), n_all))

    j_ref[...] = jnp.full((1, Q_TILE), INT_MAX, I32)
    tie_split = jnp.max(jnp.where((n_ge > topk) & (thr > INT_MIN), 1, 0))

    @pl.when(tie_split > 0)
    def _():
        need = topk - count(lambda k, s: k > thr)

        def idx_bit(i, j0):
            cand = j0 + lax.shift_left(jnp.int32(1), idx_bits - 1 - i)
            taken_before = count(lambda k, s: (k == thr) & (s < cand))
            return jnp.where(taken_before < need, cand, j0)
        j_ref[...] = lax.fori_loop(0, idx_bits, idx_bit, jnp.zeros((1, Q_TILE), I32))

    j_sel = j_ref[...]

    def selection_bias(c):
        off = pl.multiple_of(c * kc, kc)
        k = keys_ref[pl.ds(off, kc), :]
        s_idx = rows + off
        sel = ((k > thr) | ((k == thr) & (s_idx <= j_sel))) & (s_idx <= q_pos)
        bias = jnp.where(sel, 0.0, NEG)
        return off, jnp.concatenate([bias] * Q_PER_KV, axis=1)

    acc_ref[...] = jnp.zeros(acc_ref.shape, F32)

    def attend_chunk(c, carry):
        off, bias = selection_bias(c)
        for n in range(N_KV_HEADS):
            s = jnp.dot(kb_ref[0, n, pl.ds(off, kc), :], qT_ref[0, 0, n], preferred_element_type=F32) + bias
            acc_ref[n] += jnp.dot(vT_ref[0, c, n * HEAD_AUG:(n + 1) * HEAD_AUG, :], jnp.exp(s).astype(BF16),
                                  preferred_element_type=F32)
        return carry

    lax.fori_loop(0, n_chunks, attend_chunk, 0)
    denom = jnp.concatenate([acc_ref[n][HEAD_DIM:HEAD_DIM + 1, :] for n in range(N_KV_HEADS)], axis=0)
    healthy = (jnp.min(denom) >= DENOM_FLOOR) & (jnp.max(denom) <= DENOM_CEIL)

    @pl.when(jnp.logical_not(healthy))
    def _():
        m_ref[...] = jnp.full(m_ref.shape, NEG, F32)
        l_ref[...] = jnp.zeros(l_ref.shape, F32)
        acc2_ref[...] = jnp.zeros(acc2_ref.shape, F32)

        def attend_chunk_online(c, carry):
            off, bias = selection_bias(c)
            for n in range(N_KV_HEADS):
                s = jnp.dot(kb_ref[0, n, pl.ds(off, kc), 0:HEAD_DIM], qT_ref[0, 0, n, 0:HEAD_DIM, :],
                            preferred_element_type=F32) + bias
                m_prev = m_ref[n]
                m_new = jnp.maximum(m_prev, s.max(axis=0, keepdims=True))
                alpha = jnp.exp(m_prev - m_new)
                p = jnp.exp(s - m_new)
                l_ref[n] = alpha * l_ref[n] + p.sum(axis=0, keepdims=True)
                pv = jnp.dot(vT_ref[0, c, n * HEAD_AUG:n * HEAD_AUG + HEAD_DIM, :], p.astype(BF16),
                             preferred_element_type=F32)
                acc2_ref[n] = alpha * acc2_ref[n] + pv
                m_ref[n] = m_new
            return carry

        lax.fori_loop(0, n_chunks, attend_chunk_online, 0)
        for n in range(N_KV_HEADS):
            acc_ref[n, 0:HEAD_DIM, :] = acc2_ref[n]
            acc_ref[n, HEAD_DIM:HEAD_DIM + 1, :] = l_ref[n]

    for n in range(N_KV_HEADS):
        on = acc_ref[n, 0:HEAD_DIM, :] / acc_ref[n, HEAD_DIM:HEAD_DIM + 1, :]
        for gp in range(Q_PER_KV // 2):
            pair = jnp.concatenate([on[:, (2 * gp) * Q_TILE:(2 * gp + 1) * Q_TILE],
                                    on[:, (2 * gp + 1) * Q_TILE:(2 * gp + 2) * Q_TILE]], axis=0)
            col = (n * Q_PER_KV // 2 + gp) * LANES
            o_ref[0, :, col:col + LANES] = pair.T.astype(BF16)


def _dsa_prompt(qT, qiT, wiT, kb, vT, kib):
    b, nq = qT.shape[:2]
    t = kb.shape[2]
    topk = min(TOPK_MAX, t // 4)
    idx_bits = max(1, (t - 1).bit_length())
    per_q = lambda shape: pl.BlockSpec((1, 1) + shape, lambda bi, qi: (bi, qi) + (0,) * len(shape))
    per_b = lambda shape: pl.BlockSpec((1,) + shape, lambda bi, qi: (bi,) + (0,) * len(shape))
    return pl.pallas_call(
        functools.partial(_dsa_prompt_kernel, topk=topk, idx_bits=idx_bits),
        grid=(b, nq),
        in_specs=[per_q(qT.shape[2:]), per_q(qiT.shape[2:]), per_q(wiT.shape[2:]),
                  per_b(kb.shape[1:]), per_b(vT.shape[1:]), per_b(kib.shape[1:])],
        out_specs=pl.BlockSpec((1, Q_TILE, Q_WIDTH), lambda bi, qi: (bi, qi, 0)),
        out_shape=jax.ShapeDtypeStruct((b, t, Q_WIDTH), BF16),
        scratch_shapes=[pltpu.VMEM((t, Q_TILE), I32),
                        pltpu.VMEM((N_KV_HEADS, HEAD_AUG, Q_PER_KV * Q_TILE), F32),
                        pltpu.VMEM((N_KV_HEADS, HEAD_DIM, Q_PER_KV * Q_TILE), F32),
                        pltpu.VMEM((N_KV_HEADS, 1, Q_PER_KV * Q_TILE), F32),
                        pltpu.VMEM((N_KV_HEADS, 1, Q_PER_KV * Q_TILE), F32),
                        pltpu.VMEM((1, Q_TILE), I32)],
        compiler_params=_cparams(2),
        name="dsa_prompt",
    )(qT, qiT, wiT, kb, vT, kib)


def _fetch_pages_start(pt_ref, seq, layer_base, caches, bufs, sems, n_pages):
    def body(p, carry):
        page = layer_base + pt_ref[seq * n_pages + p]
        for cache, buf, sem in zip(caches, bufs, sems):
            pltpu.make_async_copy(cache.at[page], buf.at[p], sem).start()
        return carry
    lax.fori_loop(0, n_pages, body, 0)


def _fetch_pages_wait(caches, bufs, sems, n_pages):
    for cache, buf, sem in zip(caches, bufs, sems):
        pltpu.make_async_copy(cache.at[pl.ds(0, n_pages)], buf, sem).wait()


def _fetch_pages_pipelined(pt_ref, layer_base, caches, bufs, sems, n_pages):
    step, n_steps = pl.program_id(0), pl.num_programs(0)
    slot = step % 2
    at = lambda sl: ([b.at[sl] for b in bufs], [s.at[sl] for s in sems])

    @pl.when(step == 0)
    def _():
        _fetch_pages_start(pt_ref, 0, layer_base, caches, *at(0), n_pages)

    @pl.when(step + 1 < n_steps)
    def _():
        _fetch_pages_start(pt_ref, step + 1, layer_base, caches, *at(1 - slot), n_pages)

    _fetch_pages_wait(caches, *at(slot), n_pages)
    return slot


def _pages_bf16(buf, slot, first, count):
    return jnp.concatenate([buf[slot, first + j].astype(BF16) for j in range(count)], axis=1)


def _dsa_sample_scores_kernel(pt_ref, qi_ref, w_ref, kin_ref, kiT_hbm, keys_ref, buf_ref, sem,
                              *, n_pages, layer_base):
    slot = _fetch_pages_pipelined(pt_ref, layer_base, [kiT_hbm], [buf_ref], [sem], n_pages)
    qi = qi_ref[0]
    w = w_ref[0] * IDX_SCALE
    score = lambda dots: jnp.sum(jnp.maximum(dots, 0.0) * w, axis=0, keepdims=True)
    for g in range(n_pages // PAGE_GROUP):
        kiT = _pages_bf16(buf_ref, slot, g * PAGE_GROUP, PAGE_GROUP)
        keys = _float_order_key(score(jnp.dot(qi, kiT, preferred_element_type=F32)))
        keys_ref[0, :, g * PAGE_GROUP * PAGE_SIZE:(g + 1) * PAGE_GROUP * PAGE_SIZE] = keys
    kn = kin_ref[0].astype(BF16).astype(F32)
    sn = score(jnp.sum(qi.astype(F32) * kn, axis=1, keepdims=True))
    lane = lax.broadcasted_iota(I32, (1, LANES), 1)
    keys_ref[0, :, n_pages * PAGE_SIZE:] = jnp.where(lane == 0, _float_order_key(sn), INT_MIN)


def _dsa_sample_scores(pt_flat, qi3, w3, ki_new, cache_kiT, layer, n_pool, n_pages):
    db = qi3.shape[0]
    n_lanes = n_pages * PAGE_SIZE + LANES
    per_b = lambda shape: pl.BlockSpec((1,) + shape, lambda b, pt: (b,) + (0,) * len(shape))
    grid_spec = pltpu.PrefetchScalarGridSpec(
        num_scalar_prefetch=1,
        grid=(db,),
        in_specs=[per_b((IDX_HEADS, IDX_DIM)), per_b((IDX_HEADS, 1)), per_b((1, IDX_DIM)),
                  pl.BlockSpec(memory_space=pl.ANY)],
        out_specs=per_b((1, n_lanes)),
        scratch_shapes=[pltpu.VMEM((2, n_pages, IDX_DIM, PAGE_SIZE), F32), pltpu.SemaphoreType.DMA((2,))],
    )
    return pl.pallas_call(
        functools.partial(_dsa_sample_scores_kernel, n_pages=n_pages, layer_base=layer * n_pool),
        grid_spec=grid_spec,
        out_shape=jax.ShapeDtypeStruct((db, 1, n_lanes), I32),
        compiler_params=_cparams(1),
        name="dsa_sample_scores",
    )(pt_flat, qi3, w3, ki_new, cache_kiT)


def _dsa_sample_select_kernel(keys_ref, bias_ref, *, n_keys, topk, idx_bits):
    keys = keys_ref[...]
    idx = lax.broadcasted_iota(I32, keys.shape, 1)
    count = lambda hit: jnp.sum(jnp.where(hit, 1, 0), axis=1, keepdims=True)
    keys = jnp.where(idx < n_keys, keys, INT_MIN)

    def thr_bit(i, thr):
        cand = thr + lax.shift_left(jnp.int32(1), 31 - i)
        return jnp.where(count(keys >= cand) >= topk, cand, thr)

    thr = lax.fori_loop(0, 32, thr_bit, jnp.full((keys.shape[0], 1), INT_MIN, I32))
    need = topk - count(keys > thr)

    def idx_bit(i, j0):
        cand = j0 + lax.shift_left(jnp.int32(1), idx_bits - 1 - i)
        return jnp.where(count((keys == thr) & (idx < cand)) < need, cand, j0)

    j_sel = lax.fori_loop(0, idx_bits, idx_bit, jnp.zeros((keys.shape[0], 1), I32))
    sel = ((keys > thr) | ((keys == thr) & (idx <= j_sel))) & (idx < n_keys)
    bias_ref[...] = jnp.where(sel, 0.0, NEG)


def _dsa_sample_select(keys, n_keys):
    topk = min(TOPK_MAX, n_keys // 4)
    idx_bits = max(1, (n_keys - 1).bit_length())
    full = pl.BlockSpec(keys.shape, lambda i: (0, 0))
    return pl.pallas_call(
        functools.partial(_dsa_sample_select_kernel, n_keys=n_keys, topk=topk, idx_bits=idx_bits),
        grid=(1,),
        in_specs=[full],
        out_specs=full,
        out_shape=jax.ShapeDtypeStruct(keys.shape, F32),
        compiler_params=_cparams(1),
        name="dsa_sample_select",
    )(keys)


def _dsa_sample_attend_kernel(pt_ref, q_ref, bias_ref, kn_ref, vn_ref, kT_hbm, vT_hbm, o_ref,
                              kbuf_ref, vbuf_ref, ksem, vsem, *, n_pages, layer_base):
    slot = _fetch_pages_pipelined(pt_ref, layer_base, [kT_hbm, vT_hbm], [kbuf_ref, vbuf_ref],
                                  [ksem, vsem], n_pages)
    q = q_ref[0]
    n_past = n_pages * PAGE_SIZE
    group = PAGE_GROUP * PAGE_SIZE
    scores = [jnp.dot(q, _pages_bf16(kbuf_ref, slot, g * PAGE_GROUP, PAGE_GROUP), preferred_element_type=F32)
              + bias_ref[0, :, g * group:(g + 1) * group] for g in range(n_pages // PAGE_GROUP)]
    kn = kn_ref[0].astype(BF16).astype(F32)
    vn = vn_ref[0].astype(BF16).astype(F32)
    sn = jnp.sum(q.astype(F32) * kn, axis=1, keepdims=True) + bias_ref[0, :, n_past:n_past + 1]
    m = sn
    for s in scores:
        m = jnp.maximum(m, s.max(axis=1, keepdims=True))
    pn = jnp.exp(sn - m)
    denom = pn
    acc = pn.astype(BF16).astype(F32) * vn
    for g, s in enumerate(scores):
        p = jnp.exp(s - m)
        denom = denom + p.sum(axis=1, keepdims=True)
        acc = acc + lax.dot_general(p.astype(BF16), _pages_bf16(vbuf_ref, slot, g * PAGE_GROUP, PAGE_GROUP),
                                    (((1,), (1,)), ((), ())), preferred_element_type=F32)
    out = acc / denom
    head_grp = lax.broadcasted_iota(I32, out.shape, 0) // Q_PER_KV
    lane_grp = lax.broadcasted_iota(I32, out.shape, 1) // HEAD_DIM
    out = jnp.where(head_grp == lane_grp, out, 0.0)
    o = out[:, 0:HEAD_DIM]
    for n in range(1, N_KV_HEADS):
        o = o + out[:, n * HEAD_DIM:(n + 1) * HEAD_DIM]
    o_ref[0] = o


def _dsa_sample_attend(pt_flat, q_bd, bias, k_new, v_new, cache_kT, cache_vT, layer, n_pool, n_pages):
    db = q_bd.shape[0]
    per_b = lambda shape: pl.BlockSpec((1,) + shape, lambda b, pt: (b,) + (0,) * len(shape))
    hbm = pl.BlockSpec(memory_space=pl.ANY)
    pages = pltpu.VMEM((2, n_pages, KV_WIDTH, PAGE_SIZE), F32)
    grid_spec = pltpu.PrefetchScalarGridSpec(
        num_scalar_prefetch=1,
        grid=(db,),
        in_specs=[per_b(q_bd.shape[1:]), per_b(bias.shape[1:]), per_b((1, KV_WIDTH)), per_b((1, KV_WIDTH)),
                  hbm, hbm],
        out_specs=per_b((N_HEADS, HEAD_DIM)),
        scratch_shapes=[pages, pages, pltpu.SemaphoreType.DMA((2,)), pltpu.SemaphoreType.DMA((2,))],
    )
    return pl.pallas_call(
        functools.partial(_dsa_sample_attend_kernel, n_pages=n_pages, layer_base=layer * n_pool),
        grid_spec=grid_spec,
        out_shape=jax.ShapeDtypeStruct((db, N_HEADS, HEAD_DIM), F32),
        compiler_params=_cparams(1),
        name="dsa_sample_attend",
    )(pt_flat, q_bd, bias, k_new, v_new, cache_kT, cache_vT)


def _out_proj_kernel(a_ref, w_ref, x_ref, gt_ref, o_ref):
    y = jnp.dot(a_ref[0], w_ref[...], preferred_element_type=F32)
    o_ref[0] = x_ref[0] + gt_ref[0] * y


def _out_proj(a, w, x, gate):
    b, t, d = x.shape
    tm = min(ROW_TILE, t)
    row = lambda width: pl.BlockSpec((1, tm, width), lambda bi, i: (bi, i, 0))
    return pl.pallas_call(
        _out_proj_kernel,
        grid=(b, t // tm),
        in_specs=[row(a.shape[2]), pl.BlockSpec(w.shape, lambda bi, i: (0, 0)), row(d), _mod_spec(gate, tm)],
        out_specs=row(d),
        out_shape=jax.ShapeDtypeStruct((b, t, d), F32),
        compiler_params=_cparams(2),
        name="out_proj",
    )(a, w, x, gate)


def _gelu_tanh(x):
    return x * (0.5 * (1.0 + jnp.tanh(math.sqrt(2.0 / math.pi) * (x + 0.044715 * (x * x * x)))))


def _rec_proj_kernel(x_ref, g_ref, sh_ref, sc_ref, w_ref, gate_ref, xb_ref):
    h = _modulated_norm(x_ref[0], g_ref[...], sh_ref[0], sc_ref[0]).astype(BF16)
    acc = jnp.dot(h, w_ref[...], preferred_element_type=F32)
    gate_ref[0] = _gelu_tanh(acc[:, :D_RNN])
    xb_ref[0] = acc[:, D_RNN:]


def _rec_proj(x, g, shift, scale, w):
    b, t, d = x.shape
    tm = min(ROW_TILE, t)
    row = lambda width: pl.BlockSpec((1, tm, width), lambda bi, i: (bi, i, 0))
    return pl.pallas_call(
        _rec_proj_kernel,
        grid=(b, t // tm),
        in_specs=[row(d), pl.BlockSpec((1, d), lambda bi, i: (0, 0)), _mod_spec(shift, tm),
                  _mod_spec(scale, tm), pl.BlockSpec(w.shape, lambda bi, i: (0, 0))],
        out_specs=(row(D_RNN), row(D_RNN)),
        out_shape=(jax.ShapeDtypeStruct((b, t, D_RNN), F32),) * 2,
        compiler_params=_cparams(2),
        name="rec_proj",
    )(x, g, shift, scale, w)


def _rglru_gates(xc, wa_ref, ba, wx_ref, bx, lam):
    xcb = xc.astype(BF16)
    blk = lambda w_ref: jnp.concatenate(
        [jnp.dot(xcb[:, n * RG_BLOCK_W:(n + 1) * RG_BLOCK_W], w_ref[n], preferred_element_type=F32)
         for n in range(RG_BLOCKS)], axis=1)
    r = jax.nn.sigmoid(blk(wa_ref) + ba)
    i = jax.nn.sigmoid(blk(wx_ref) + bx)
    neg_lam = -lam
    softplus = jnp.maximum(neg_lam, 0.0) + jnp.log1p(jnp.exp(-jnp.abs(neg_lam)))
    log_a = -RG_C * r * softplus
    a = jnp.exp(log_a)
    u = jnp.sqrt(1.0 - a * a) * (i * xc)
    return a, u


def _rglru_prompt_kernel(gate_ref, xb_ref, cw_ref, cb_ref, wa_ref, ba_ref, wx_ref, bx_ref, lam_ref,
                         y_ref, h_ref, conv_ref, xcat_ref, a_ref, u_ref, hs_ref, hc_ref):
    tc = xb_ref.shape[1]
    c = pl.program_id(1)
    pad = SUBLANES

    @pl.when(c == 0)
    def _():
        xcat_ref[0:pad, :] = jnp.zeros((pad, D_RNN), F32)
        hc_ref[...] = jnp.zeros(hc_ref.shape, F32)

    @pl.when(c > 0)
    def _():
        xcat_ref[0:pad, :] = xcat_ref[tc:tc + pad, :]

    xcat_ref[pad:pad + tc, :] = xb_ref[0]
    cw = cw_ref[...]
    xc = xcat_ref[pad - 3:pad - 3 + tc, :] * cw[0:1, :]
    for j in range(1, CONV_W):
        xc = xc + xcat_ref[pad - 3 + j:pad - 3 + j + tc, :] * cw[j:j + 1, :]
    xc = cb_ref[...] + xc
    a, u = _rglru_gates(xc, wa_ref, ba_ref[...], wx_ref, bx_ref[...], lam_ref[...])
    a_ref[...] = a
    u_ref[...] = u

    def step(t, h):
        h = a_ref[pl.ds(t, 1), :] * h + u_ref[pl.ds(t, 1), :]
        hs_ref[pl.ds(t, 1), :] = h
        return h

    h = lax.fori_loop(0, tc, step, hc_ref[...], unroll=8)
    hc_ref[...] = h
    y_ref[0] = (hs_ref[...] * gate_ref[0]).astype(BF16)
    h_ref[0] = h
    conv_ref[0] = xcat_ref[pad + tc - (CONV_W - 1):pad + tc, :]


def _rglru_prompt(gate, xb, cw, cb, wa, ba, wx, bx, lam):
    b, t, d = xb.shape
    tc = min(SCAN_CHUNK, t)
    row = pl.BlockSpec((1, tc, d), lambda bi, i: (bi, i, 0))
    vec = pl.BlockSpec((1, d), lambda bi, i: (0, 0))
    full = lambda shape: pl.BlockSpec(shape, lambda bi, i: (0,) * len(shape))
    return pl.pallas_call(
        _rglru_prompt_kernel,
        grid=(b, t // tc),
        in_specs=[row, row, full(cw.shape), vec, full(wa.shape), vec, full(wx.shape), vec, vec],
        out_specs=(row, pl.BlockSpec((1, 1, d), lambda bi, i: (bi, 0, 0)),
                   pl.BlockSpec((1, CONV_W - 1, d), lambda bi, i: (bi, 0, 0))),
        out_shape=(jax.ShapeDtypeStruct((b, t, d), BF16), jax.ShapeDtypeStruct((b, 1, d), F32),
                   jax.ShapeDtypeStruct((b, CONV_W - 1, d), F32)),
        scratch_shapes=[pltpu.VMEM((tc + 2 * SUBLANES, d), F32), pltpu.VMEM((tc, d), F32),
                        pltpu.VMEM((tc, d), F32), pltpu.VMEM((tc, d), F32), pltpu.VMEM((1, d), F32)],
        compiler_params=_cparams(2),
        name="rglru_prompt",
    )(gate, xb, cw, cb, wa, ba, wx, bx, lam)


def _rglru_sample_kernel(gate_ref, xb_ref, buf_ref, h0_ref, cw_ref, cb_ref, wa_ref, ba_ref, wx_ref, bx_ref,
                         lam_ref, y_ref, h_ref, nbuf_ref):
    cw = cw_ref[...]
    xb = xb_ref[...]
    xc = buf_ref[0] * cw[0:1, :]
    for j in range(1, CONV_W - 1):
        xc = xc + buf_ref[j] * cw[j:j + 1, :]
    xc = cb_ref[...] + (xc + xb * cw[CONV_W - 1:CONV_W, :])
    a, u = _rglru_gates(xc, wa_ref, ba_ref[...], wx_ref, bx_ref[...], lam_ref[...])
    h = a * h0_ref[...] + u
    h_ref[...] = h
    y_ref[...] = (h * gate_ref[...]).astype(BF16)
    for j in range(CONV_W - 2):
        nbuf_ref[j] = buf_ref[j + 1]
    nbuf_ref[CONV_W - 2] = xb


def _rglru_sample(gate, xb, buf, h0, cw, cb, wa, ba, wx, bx, lam):
    rows, d = xb.shape
    full = lambda a: pl.BlockSpec(a.shape, lambda i: (0,) * a.ndim)
    args = (gate, xb, buf, h0, cw, cb, wa, ba, wx, bx, lam)
    return pl.pallas_call(
        _rglru_sample_kernel,
        grid=(1,),
        in_specs=[full(a) for a in args],
        out_specs=(full(xb), full(xb), full(buf)),
        out_shape=(jax.ShapeDtypeStruct((rows, d), BF16), jax.ShapeDtypeStruct((rows, d), F32),
                   jax.ShapeDtypeStruct(buf.shape, F32)),
        compiler_params=_cparams(1),
        name="rglru_sample",
    )(*args)


def _route(logits):
    lane = lax.broadcasted_iota(I32, logits.shape, 1)

    def first_max(v):
        m = jnp.max(v, axis=1, keepdims=True)
        return m, jnp.min(jnp.where(v == m, lane, LANES), axis=1, keepdims=True)

    is_grp = lane < N_GROUPS
    gm, grp = first_max(jnp.where(is_grp, logits, -jnp.inf))
    g_w = 1.0 / jnp.sum(jnp.where(is_grp, jnp.exp(logits - gm), 0.0), axis=1, keepdims=True)
    lo = N_GROUPS + EXPERTS_PER_GROUP * grp
    el = jnp.where((lane >= lo) & (lane < lo + EXPERTS_PER_GROUP), logits, -jnp.inf)
    m1, i1 = first_max(el)
    m2, i2 = first_max(jnp.where(lane == i1, -jnp.inf, el))
    e2 = jnp.exp(m2 - m1)
    den = 1.0 + e2
    return i1 - N_GROUPS, i2 - N_GROUPS, (1.0 / den) * g_w, (e2 / den) * g_w


def _moe_kernel(x_ref, g_ref, sh_ref, sc_ref, gt_ref, wr_ref, br_ref, w1_ref, w2_ref, gf_ref, o_ref,
                h_ref, acc_ref, e0_ref, e1_ref, c0_ref, c1_ref, *, final_norm):
    e = pl.program_id(2)

    @pl.when(e == 0)
    def _():
        h = _modulated_norm(x_ref[0], g_ref[...], sh_ref[0], sc_ref[0])
        h_ref[...] = h.astype(BF16)
        logits = jnp.dot(h_ref[...], wr_ref[...], preferred_element_type=F32) + br_ref[...]
        e0_ref[...], e1_ref[...], c0_ref[...], c1_ref[...] = _route(logits)
        acc_ref[...] = jnp.zeros(acc_ref.shape, F32)

    gu = jnp.dot(h_ref[...], w1_ref[0].astype(BF16), preferred_element_type=F32)
    gp, up = gu[:, :D_EXPERT], gu[:, D_EXPERT:]
    act = ((gp * jax.nn.sigmoid(gp)) * up).astype(BF16)
    y = jnp.dot(act, w2_ref[0].astype(BF16), preferred_element_type=F32)
    wt = jnp.where(e0_ref[...] == e, c0_ref[...], 0.0) + jnp.where(e1_ref[...] == e, c1_ref[...], 0.0)
    acc_ref[...] += y * wt

    @pl.when(e == N_EXPERTS - 1)
    def _():
        out = x_ref[0] + gt_ref[0] * acc_ref[...]
        if final_norm:
            out = (out * lax.rsqrt(jnp.mean(out * out, axis=-1, keepdims=True) + EPS)) * gf_ref[...]
        o_ref[0] = out


def _moe(x, g, shift, scale, gate, wr, br, w1, w2, layer, gf, final_norm):
    b, t, d = x.shape
    tm = min(MOE_ROW_TILE, t)
    row = pl.BlockSpec((1, tm, d), lambda bi, i, e: (bi, i, 0))
    vec = lambda width: pl.BlockSpec((1, width), lambda bi, i, e: (0, 0))
    col = lambda dt: pltpu.VMEM((tm, 1), dt)
    return pl.pallas_call(
        functools.partial(_moe_kernel, final_norm=final_norm),
        grid=(b, t // tm, N_EXPERTS),
        in_specs=[row, vec(d), _mod_spec(shift, tm), _mod_spec(scale, tm), _mod_spec(gate, tm),
                  pl.BlockSpec(wr.shape, lambda bi, i, e: (0, 0)), vec(ROUTER_PAD),
                  pl.BlockSpec((1, d, 2 * D_EXPERT), lambda bi, i, e: (layer * N_EXPERTS + e, 0, 0)),
                  pl.BlockSpec((1, D_EXPERT, d), lambda bi, i, e: (layer * N_EXPERTS + e, 0, 0)),
                  vec(d)],
        out_specs=row,
        out_shape=jax.ShapeDtypeStruct((b, t, d), F32),
        scratch_shapes=[pltpu.VMEM((tm, d), BF16), pltpu.VMEM((tm, d), F32),
                        col(I32), col(I32), col(F32), col(F32)],
        compiler_params=_cparams(3),
        name="moe",
    )(x, g, shift, scale, gate, wr, br, w1, w2, gf)


REC_E0, REC_E1, REC_RANK0, REC_RANK1, REC_C0, REC_C1 = range(6)


def _moe_route_kernel(x_ref, g_ref, sh_ref, sc_ref, wr_ref, br_ref, h_ref, rec_ref, recT_ref, cnt_ref,
                      tri_ref, carry_ref):
    tm = x_ref.shape[1]

    @pl.when((pl.program_id(0) == 0) & (pl.program_id(1) == 0))
    def _():
        carry_ref[...] = jnp.zeros(carry_ref.shape, F32)
        earlier = lax.broadcasted_iota(I32, (tm, tm), 0) > lax.broadcasted_iota(I32, (tm, tm), 1)
        tri_ref[...] = jnp.where(earlier, 1.0, 0.0).astype(BF16)

    h = _modulated_norm(x_ref[0], g_ref[...], sh_ref[0], sc_ref[0])
    h_ref[0] = h
    logits = jnp.dot(h.astype(BF16), wr_ref[...], preferred_element_type=F32) + br_ref[...]
    e0, e1, c0, c1 = _route(logits)
    lane = lax.broadcasted_iota(I32, logits.shape, 1)
    pick0, pick1 = lane == e0, lane == e1
    picks = jnp.where(pick0 | pick1, 1.0, 0.0)
    before = carry_ref[...] + jnp.dot(tri_ref[...], picks.astype(BF16), preferred_element_type=F32)
    rank0 = jnp.sum(jnp.where(pick0, before, 0.0), axis=1, keepdims=True)
    rank1 = jnp.sum(jnp.where(pick1, before, 0.0), axis=1, keepdims=True)
    carry_ref[...] = carry_ref[...] + jnp.sum(picks, axis=0, keepdims=True)
    cnt_ref[...] = carry_ref[...]
    rec = jnp.zeros(logits.shape, F32)
    for pos, val in ((REC_E0, e0.astype(F32)), (REC_E1, e1.astype(F32)), (REC_RANK0, rank0),
                     (REC_RANK1, rank1), (REC_C0, c0), (REC_C1, c1)):
        rec = jnp.where(lane == pos, val, rec)
    rec_ref[0] = rec
    recT_ref[0] = rec.T[0:SUBLANES, :]


def _moe_route(x, g, shift, scale, wr, br):
    b, t, d = x.shape
    tm = ROW_TILE
    row = lambda width: pl.BlockSpec((1, tm, width), lambda bi, i: (bi, i, 0))
    vec = lambda width: pl.BlockSpec((1, width), lambda bi, i: (0, 0))
    return pl.pallas_call(
        _moe_route_kernel,
        grid=(b, t // tm),
        in_specs=[row(d), vec(d), _mod_spec(shift, tm), _mod_spec(scale, tm),
                  pl.BlockSpec(wr.shape, lambda bi, i: (0, 0)), vec(ROUTER_PAD)],
        out_specs=(row(d), row(ROUTER_PAD), pl.BlockSpec((1, SUBLANES, tm), lambda bi, i: (bi, 0, i)),
                   vec(ROUTER_PAD)),
        out_shape=(jax.ShapeDtypeStruct((b, t, d), F32), jax.ShapeDtypeStruct((b, t, ROUTER_PAD), F32),
                   jax.ShapeDtypeStruct((b, SUBLANES, t), F32), jax.ShapeDtypeStruct((1, ROUTER_PAD), F32)),
        scratch_shapes=[pltpu.VMEM((tm, tm), BF16), pltpu.VMEM((1, ROUTER_PAD), F32)],
        compiler_params=_cparams(2),
        name="moe_route",
    )(x, g, shift, scale, wr, br)


def _gather_rows_start(idx_ref, idx_base, src_hbm, dst, sem, n_rows):
    for r in range(n_rows):
        pltpu.make_async_copy(src_hbm.at[pl.ds(idx_ref[idx_base + r], 1)], dst.at[pl.ds(r, 1)], sem).start()


def _gather_rows_wait(src_hbm, dst, sem, n_rows):
    pltpu.make_async_copy(src_hbm.at[pl.ds(0, n_rows)], dst, sem).wait()


def _moe_experts_kernel(blk_e_ref, n_used_ref, row_tok_ref, h_hbm, w1_ref, w2_ref, y_ref,
                        buf_ref, w1b_ref, w2b_ref, sem):
    i = pl.program_id(0)
    blk = buf_ref.shape[1]
    n_used = n_used_ref[0]
    n_slots = buf_ref.shape[0]
    slot = i % n_slots

    def start(block):
        s = block % n_slots
        _gather_rows_start(row_tok_ref, block * blk, h_hbm, buf_ref.at[s], sem.at[s], blk)

    @pl.when(i == 0)
    def _():
        for ahead in range(n_slots - 1):
            @pl.when(ahead < n_used)
            def _():
                start(ahead)

    @pl.when(i + n_slots - 1 < n_used)
    def _():
        start(i + n_slots - 1)

    @pl.when(i < n_used)
    def _():
        @pl.when((i == 0) | (blk_e_ref[i] != blk_e_ref[jnp.maximum(i - 1, 0)]))
        def _():
            w1b_ref[...] = w1_ref[0].astype(BF16)
            w2b_ref[...] = w2_ref[0].astype(BF16)

        _gather_rows_wait(h_hbm, buf_ref.at[slot], sem.at[slot], blk)
        gu = jnp.dot(buf_ref[slot].astype(BF16), w1b_ref[...], preferred_element_type=F32)
        gp, up = gu[:, :D_EXPERT], gu[:, D_EXPERT:]
        act = ((gp * jax.nn.sigmoid(gp)) * up).astype(BF16)
        y_ref[...] = jnp.dot(act, w2b_ref[...], preferred_element_type=F32)

    @pl.when(i >= n_used)
    def _():
        y_ref[...] = jnp.zeros(y_ref.shape, F32)


def _moe_experts(blk_expert, n_used, row_tok, h2d, w1, w2, layer):
    n_blk = blk_expert.shape[0]
    blk = MOE_BLOCK_ROWS
    d = h2d.shape[1]
    grid_spec = pltpu.PrefetchScalarGridSpec(
        num_scalar_prefetch=3,
        grid=(n_blk,),
        in_specs=[pl.BlockSpec(memory_space=pl.ANY),
                  pl.BlockSpec((1, d, 2 * D_EXPERT), lambda i, be, nu, rt: (layer * N_EXPERTS + be[i], 0, 0)),
                  pl.BlockSpec((1, D_EXPERT, d), lambda i, be, nu, rt: (layer * N_EXPERTS + be[i], 0, 0))],
        out_specs=pl.BlockSpec((blk, d), lambda i, be, nu, rt: (i, 0)),
        scratch_shapes=[pltpu.VMEM((GATHER_SLOTS, blk, d), F32), pltpu.VMEM((d, 2 * D_EXPERT), BF16),
                        pltpu.VMEM((D_EXPERT, d), BF16), pltpu.SemaphoreType.DMA((GATHER_SLOTS,))],
    )
    return pl.pallas_call(
        _moe_experts_kernel,
        grid_spec=grid_spec,
        out_shape=jax.ShapeDtypeStruct((n_blk * blk, d), F32),
        compiler_params=_cparams(1),
        name="moe_experts",
    )(blk_expert, n_used, row_tok, h2d, w1, w2)


def _moe_combine_kernel(dest_ref, y_hbm, x_ref, gt_ref, rec_ref, gf_ref, o_ref, buf_ref, sem,
                        *, n_tokens, final_norm):
    tm = x_ref.shape[1]
    n_steps = pl.num_programs(0) * pl.num_programs(1)
    s = pl.program_id(0) * pl.num_programs(1) + pl.program_id(1)
    slot = s % 2

    def start(step, sl):
        for pick in range(2):
            _gather_rows_start(dest_ref, pick * n_tokens + step * tm, y_hbm, buf_ref.at[sl, pick],
                               sem.at[sl, pick], tm)

    @pl.when(s == 0)
    def _():
        start(0, 0)

    @pl.when(s + 1 < n_steps)
    def _():
        start(s + 1, 1 - slot)

    for pick in range(2):
        _gather_rows_wait(y_hbm, buf_ref.at[slot, pick], sem.at[slot, pick], tm)
    rec = rec_ref[0]
    moe = buf_ref[slot, 0] * rec[:, REC_C0:REC_C0 + 1] + buf_ref[slot, 1] * rec[:, REC_C1:REC_C1 + 1]
    out = x_ref[0] + gt_ref[0] * moe
    if final_norm:
        out = (out * lax.rsqrt(jnp.mean(out * out, axis=-1, keepdims=True) + EPS)) * gf_ref[...]
    o_ref[0] = out


def _moe_combine(dest, y, x, gate, rec, gf, final_norm):
    b, t, d = x.shape
    tm = MOE_BLOCK_ROWS
    row = lambda width: pl.BlockSpec((1, tm, width), lambda bi, i, dst: (bi, i, 0))
    grid_spec = pltpu.PrefetchScalarGridSpec(
        num_scalar_prefetch=1,
        grid=(b, t // tm),
        in_specs=[pl.BlockSpec(memory_space=pl.ANY), row(d), _mod_spec(gate, tm), row(ROUTER_PAD),
                  pl.BlockSpec((1, d), lambda bi, i, dst: (0, 0))],
        out_specs=row(d),
        scratch_shapes=[pltpu.VMEM((2, 2, tm, d), F32), pltpu.SemaphoreType.DMA((2, 2))],
    )
    return pl.pallas_call(
        functools.partial(_moe_combine_kernel, n_tokens=b * t, final_norm=final_norm),
        grid_spec=grid_spec,
        out_shape=jax.ShapeDtypeStruct((b, t, d), F32),
        compiler_params=_cparams(2),
        name="moe_combine",
    )(dest, y, x, gate, rec, gf)


def _moe_sorted(x, g, shift, scale, gate, wr, br, w1, w2, layer, gf, final_norm):
    b, t, d = x.shape
    n = b * t
    blk = MOE_BLOCK_ROWS
    n_blk = (2 * n) // blk + N_EXPERTS
    h, rec, rec_t, cnt = _moe_route(x, g, shift, scale, wr, br)
    counts = cnt[0, :N_EXPERTS].astype(I32)
    padded = (counts + blk - 1) // blk * blk
    seg_end = jnp.cumsum(padded)
    seg_start = seg_end - padded
    col = lambda c: rec_t[:, c, :].reshape(n).astype(I32)
    dest = jnp.concatenate([seg_start[col(REC_E0)] + col(REC_RANK0), seg_start[col(REC_E1)] + col(REC_RANK1)])
    tok = jnp.arange(n, dtype=I32)
    row_tok = jnp.zeros((n_blk * blk,), I32).at[dest].set(jnp.concatenate([tok, tok]), unique_indices=True)
    blk_first_row = jnp.arange(n_blk, dtype=I32) * blk
    blk_expert = jnp.minimum(jnp.sum((seg_end[None, :] <= blk_first_row[:, None]).astype(I32), axis=1),
                             N_EXPERTS - 1)
    n_used = (seg_end[-1:] // blk).astype(I32)
    y = _moe_experts(blk_expert, n_used, row_tok, h.reshape(n, d), w1, w2, layer)
    return _moe_combine(dest, y, x, gate, rec, gf, final_norm)


def kernel(x_prompt, x_sample, cache_k, cache_v, cache_kidx, state_h, state_conv, page_table,
           c_prompt, c_sample, ada_w, ada_b, norm_mix_g, norm_ffn_g, norm_final_g,
           attn_w_in, attn_w_out, rec_w_in, rec_conv_w, rec_conv_b, rec_w_a, rec_b_a,
           rec_w_x, rec_b_x, rec_lambda, rec_w_out, moe_w_group, moe_b_group,
           moe_w_expert, moe_b_expert, moe_w1, moe_w2):
    b, t, d = x_prompt.shape
    db = x_sample.shape[0]
    n_pages = page_table.shape[1]
    past = n_pages * PAGE_SIZE
    n_pool = cache_k.shape[1]

    n_cond = b + db
    cond_rows = -(-n_cond // SUBLANES) * SUBLANES
    c_all = jnp.concatenate([c_prompt, c_sample, jnp.zeros((cond_rows - n_cond, d), F32)], axis=0)
    mod = _adaln(c_all, ada_w, ada_b)

    tables_p = _rope_tables(jnp.arange(t, dtype=I32))
    tables_s = _rope_tables(jnp.full((db,), past, I32))
    cache_kT = cache_k.transpose(0, 1, 3, 4, 2).reshape(-1, KV_WIDTH, PAGE_SIZE)
    cache_vT = cache_v.transpose(0, 1, 3, 4, 2).reshape(-1, KV_WIDTH, PAGE_SIZE)
    cache_kiT = cache_kidx.transpose(0, 1, 3, 2).reshape(-1, IDX_DIM, PAGE_SIZE)
    pt_flat = page_table.reshape(-1)
    moe_w1f = moe_w1.reshape(-1, d, 2 * D_EXPERT)
    moe_w2f = moe_w2.reshape(-1, D_EXPERT, d)
    row_vec = lambda v: v.reshape(1, -1)
    head_grp = jnp.arange(N_HEADS)[:, None] // Q_PER_KV == jnp.arange(KV_WIDTH)[None, :] // HEAD_DIM

    xp = x_prompt
    xs = x_sample.reshape(1, db, d)
    outs = {name: [] for name in ("k_p", "v_p", "ki_p", "h_p", "cv_p", "k_s", "v_s", "ki_s", "h_s", "cv_s")}
    for l in range(DEPTH):
        parts = [mod[l, :, i * d:(i + 1) * d] for i in range(6)]
        sh1p, sc1p, g1p, sh2p, sc2p, g2p = [m[:b].reshape(b, 1, d) for m in parts]
        sh1s, sc1s, g1s, sh2s, sc2s, g2s = [m[b:n_cond].reshape(1, db, d) for m in parts]
        g_mix = row_vec(norm_mix_g[l])
        if l % 2 == 0:
            a = l // 2
            w_in = jnp.pad(attn_w_in[a], ((0, 0), (0, ATTN_PROJ_PAD - ATTN_PROJ))).astype(BF16)
            w_out = attn_w_out[a].astype(BF16)
            kT, vT, kiT, qT, qiT, wiT, kb, vTb, kib = _attn_proj_prompt(xp, g_mix, sh1p, sc1p, w_in, tables_p)
            op = _dsa_prompt(qT, qiT, wiT, kb, vTb, kib)
            outs["k_p"].append(kT.reshape(b, N_KV_HEADS, HEAD_DIM, t).transpose(0, 3, 1, 2))
            outs["v_p"].append(vT.reshape(b, N_KV_HEADS, HEAD_DIM, t).transpose(0, 3, 1, 2))
            outs["ki_p"].append(kiT.transpose(0, 2, 1))
            xp = _out_proj(op, w_out, xp, g1p)

            q, k, v, qi, ki, wi = _attn_proj_sample(xs, g_mix, sh1s, sc1s, w_in, tables_s)
            keys = _dsa_sample_scores(pt_flat, qi.reshape(db, IDX_HEADS, IDX_DIM).astype(BF16),
                                      wi.reshape(db, IDX_HEADS, 1), ki.reshape(db, 1, IDX_DIM),
                                      cache_kiT, a, n_pool, n_pages)
            bias = _dsa_sample_select(keys.reshape(db, -1), past + 1).reshape(keys.shape)
            q_bd = jnp.where(head_grp[None], jnp.tile(q.reshape(db, N_HEADS, HEAD_DIM), (1, 1, N_KV_HEADS)),
                             0.0).astype(BF16)
            os_ = _dsa_sample_attend(pt_flat, q_bd, bias, k.reshape(db, 1, KV_WIDTH), v.reshape(db, 1, KV_WIDTH),
                                     cache_kT, cache_vT, a, n_pool, n_pages)
            outs["k_s"].append(k.reshape(db, 1, N_KV_HEADS, HEAD_DIM))
            outs["v_s"].append(v.reshape(db, 1, N_KV_HEADS, HEAD_DIM))
            outs["ki_s"].append(ki.reshape(db, 1, IDX_DIM))
            xs = _out_proj(os_.reshape(1, db, Q_WIDTH).astype(BF16), w_out, xs, g1s)
        else:
            r = l // 2
            w_in = rec_w_in[r].astype(BF16)
            w_out = rec_w_out[r].astype(BF16)
            rec = (rec_conv_w[r], row_vec(rec_conv_b[r]), rec_w_a[r].astype(BF16), row_vec(rec_b_a[r]),
                   rec_w_x[r].astype(BF16), row_vec(rec_b_x[r]), row_vec(rec_lambda[r]))
            gate, xb = _rec_proj(xp, g_mix, sh1p, sc1p, w_in)
            y, h_t, buf = _rglru_prompt(gate, xb, *rec)
            outs["h_p"].append(h_t.reshape(b, D_RNN))
            outs["cv_p"].append(buf)
            xp = _out_proj(y, w_out, xp, g1p)

            gate, xb = _rec_proj(xs, g_mix, sh1s, sc1s, w_in)
            y, h_t, buf = _rglru_sample(gate[0], xb[0], state_conv[r].swapaxes(0, 1), state_h[r], *rec)
            outs["h_s"].append(h_t)
            outs["cv_s"].append(buf.swapaxes(0, 1))
            xs = _out_proj(y.reshape(1, db, D_RNN), w_out, xs, g1s)

        wr = jnp.concatenate([moe_w_group[l], moe_w_expert[l],
                              jnp.zeros((d, ROUTER_PAD - N_GROUPS - N_EXPERTS), F32)], axis=1).astype(BF16)
        br = jnp.concatenate([moe_b_group[l], moe_b_expert[l],
                              jnp.zeros((ROUTER_PAD - N_GROUPS - N_EXPERTS,), F32)]).reshape(1, ROUTER_PAD)
        last = l == DEPTH - 1
        moe_args = (wr, br, moe_w1f, moe_w2f, l, row_vec(norm_final_g), last)
        xp = _moe_sorted(xp, row_vec(norm_ffn_g[l]), sh2p, sc2p, g2p, *moe_args)
        xs = _moe(xs, row_vec(norm_ffn_g[l]), sh2s, sc2s, g2s, *moe_args)

    st = lambda name: jnp.stack(outs[name])
    return (xp, xs.reshape(db, 1, d), st("k_p"), st("v_p"), st("ki_p"), st("h_p"), st("cv_p"),
            st("k_s"), st("v_s"), st("ki_s"), st("h_s"), st("cv_s"))
```

```python
import functools
import math

import jax
import jax.numpy as jnp
from jax import lax
from jax.experimental import pallas as pl
from jax.experimental.pallas import tpu as pltpu

F32 = jnp.float32
BF16 = jnp.bfloat16
I32 = jnp.int32

D_MODEL = 1024
DEPTH = 4
PAGE_SIZE = 128
N_HEADS = 16
HEAD_DIM = 64
N_KV_HEADS = 4
Q_PER_KV = N_HEADS // N_KV_HEADS
IDX_HEADS = 8
IDX_DIM = 64
IDX_SCALE = (IDX_HEADS * IDX_DIM) ** -0.5
TOPK_MAX = 256
ROPE_THETA = 500000.0
ROPE_FRACTION = 4
ROT_HALF = HEAD_DIM // ROPE_FRACTION // 2
Q_WIDTH = N_HEADS * HEAD_DIM
KV_WIDTH = N_KV_HEADS * HEAD_DIM
QI_WIDTH = IDX_HEADS * IDX_DIM
ATTN_PROJ = Q_WIDTH + 2 * KV_WIDTH + QI_WIDTH + IDX_DIM + IDX_HEADS
D_RNN = D_MODEL
RG_BLOCKS = 8
RG_BLOCK_W = D_RNN // RG_BLOCKS
CONV_W = 4
RG_C = 8.0
N_GROUPS = 4
EXPERTS_PER_GROUP = 8
N_EXPERTS = N_GROUPS * EXPERTS_PER_GROUP
D_EXPERT = 256
EPS = 1e-6

LANES = 128
SUBLANES = 8
VMEM_LIMIT_BYTES = 56 * 1024 * 1024

ATTN_PROJ_PAD = -(-ATTN_PROJ // LANES) * LANES
ROUTER_PAD = LANES
Q_TILE = LANES
KEY_CHUNK = 512
ROW_TILE = 512
MOE_ROW_TILE = 1024
MOE_BLOCK_ROWS = 256
GATHER_SLOTS = 3
SCAN_CHUNK = 512
PAGE_GROUP = 8
BF16_SUBLANES = 16
HEAD_AUG = HEAD_DIM + BF16_SUBLANES
BOUND_SLACK = 1.02
DENOM_FLOOR = 2.0 ** -60
DENOM_CEIL = 2.0 ** 60
INT_MIN = -(2 ** 31)
INT_MAX = 2 ** 31 - 1
NEG = -1e30


def _cparams(n_axes):
    return pltpu.CompilerParams(dimension_semantics=("arbitrary",) * n_axes,
                                vmem_limit_bytes=VMEM_LIMIT_BYTES)


def _modulated_norm(x, g, shift, scale):
    xn = x * lax.rsqrt(jnp.mean(x * x, axis=-1, keepdims=True) + EPS)
    return (xn * g) * (1.0 + scale) + shift


def _rope_group(xs, cos, sa, sb):
    return xs * cos + pltpu.roll(xs, LANES - 8, 1) * sa + pltpu.roll(xs, 8, 1) * sb


def _float_order_key(s):
    bits = lax.bitcast_convert_type(s, I32)
    return bits ^ ((bits >> 31) & INT_MAX)


def _mod_spec(arr, tm):
    if arr.shape[1] == 1:
        return pl.BlockSpec((1, 1, arr.shape[2]), lambda b, i, *_: (b, 0, 0))
    return pl.BlockSpec((1, tm, arr.shape[2]), lambda b, i, *_: (b, i, 0))


def _adaln_kernel(c_ref, w_ref, b_ref, o_ref):
    c = c_ref[...]
    s = (c * jax.nn.sigmoid(c)).astype(BF16)
    o_ref[0] = jnp.dot(s, w_ref[0].astype(BF16), preferred_element_type=F32) + b_ref[0]


def _adaln(c_all, ada_w, ada_b):
    depth, d, n = ada_w.shape
    rows = c_all.shape[0]
    tn = 1536
    return pl.pallas_call(
        _adaln_kernel,
        grid=(depth, n // tn),
        in_specs=[pl.BlockSpec((rows, d), lambda l, j: (0, 0)),
                  pl.BlockSpec((1, d, tn), lambda l, j: (l, 0, j)),
                  pl.BlockSpec((1, 1, tn), lambda l, j: (l, 0, j))],
        out_specs=pl.BlockSpec((1, rows, tn), lambda l, j: (l, 0, j)),
        out_shape=jax.ShapeDtypeStruct((depth, rows, n), F32),
        compiler_params=_cparams(2),
        name="adaln",
    )(c_all, ada_w, ada_b.reshape(depth, 1, n))


def _rope_tables(pos):
    rot = HEAD_DIM // ROPE_FRACTION
    half = rot // 2
    inv = jnp.exp(-math.log(ROPE_THETA) * jnp.arange(half, dtype=F32) * (2.0 / rot))
    ang = pos.astype(F32)[:, None] * inv[None, :]
    cos, sin = jnp.cos(ang), jnp.sin(ang)
    r = pos.shape[0]
    z = lambda w: jnp.zeros((r, w), F32)
    cos64 = jnp.concatenate([cos, cos, jnp.ones((r, HEAD_DIM - rot), F32)], axis=1)
    sa64 = jnp.concatenate([-sin, z(HEAD_DIM - half)], axis=1)
    sb64 = jnp.concatenate([z(half), sin, z(HEAD_DIM - rot)], axis=1)
    two = lambda t: jnp.concatenate([t, t], axis=1)
    return two(cos64), two(sa64), two(sb64), cos.T, sin.T


def _attn_proj_prompt_kernel(x_ref, g_ref, sh_ref, sc_ref, w_ref, cos_ref, sa_ref, sb_ref, cosT_ref, sinT_ref,
                             kT_ref, vT_ref, kiT_ref, qT_ref, qiT_ref, wiT_ref, kb_ref, vTb_ref, kib_ref,
                             kmax_ref):
    h = _modulated_norm(x_ref[0], g_ref[...], sh_ref[0], sc_ref[0]).astype(BF16)
    acc = jnp.dot(h, w_ref[...], preferred_element_type=F32)
    cos, sa, sb = cos_ref[...], sa_ref[...], sb_ref[...]
    n_qb = acc.shape[0] // Q_TILE
    grp = lambda off, j: acc[:, off + j * LANES: off + (j + 1) * LANES]

    tm = acc.shape[0]
    one_hot_row = lambda shape, axis: jnp.where(lax.broadcasted_iota(I32, shape, axis) == 0, 1.0, 0.0).astype(BF16)

    @pl.when(pl.program_id(1) == 0)
    def _():
        kmax_ref[...] = jnp.zeros(kmax_ref.shape, F32)

    for j in range(KV_WIDTH // LANES):
        kj = _rope_group(grp(Q_WIDTH, j), cos, sa, sb)
        kjT = kj.T
        kT_ref[0, j * LANES:(j + 1) * LANES, :] = kjT
        sq = kjT * kjT
        for hh in range(2):
            n = 2 * j + hh
            norm2 = jnp.sum(sq[hh * HEAD_DIM:(hh + 1) * HEAD_DIM, :], axis=0, keepdims=True)
            kmax_ref[n] = jnp.maximum(kmax_ref[n], jnp.max(norm2, axis=1, keepdims=True))
            kb_ref[0, n, :, 0:HEAD_DIM] = kj[:, hh * HEAD_DIM:(hh + 1) * HEAD_DIM].astype(BF16)
            kb_ref[0, n, :, HEAD_DIM:HEAD_AUG] = one_hot_row((tm, BF16_SUBLANES), 1)
    cosT, sinT = cosT_ref[...], sinT_ref[...]

    def rope_rows(xT):
        parts = []
        for hh in range(2):
            lo, mid, hi = hh * HEAD_DIM, hh * HEAD_DIM + ROT_HALF, hh * HEAD_DIM + 2 * ROT_HALF
            x1, x2 = xT[lo:mid], xT[mid:hi]
            parts += [x1 * cosT - x2 * sinT, x1 * sinT + x2 * cosT, xT[hi:(hh + 1) * HEAD_DIM]]
        return jnp.concatenate(parts, axis=0)

    for j in range(Q_WIDTH // LANES):
        qjT = rope_rows(grp(0, j).T) * (HEAD_DIM ** -0.5)
        sq = qjT * qjT
        qjTb = qjT.astype(BF16)
        for hh in range(2):
            n, g = divmod(2 * j + hh, Q_PER_KV)
            norm2 = jnp.sum(sq[hh * HEAD_DIM:(hh + 1) * HEAD_DIM, :], axis=0, keepdims=True)
            shift = -BOUND_SLACK * jnp.sqrt(norm2 * kmax_ref[n])
            shift = jnp.concatenate([shift, jnp.zeros((BF16_SUBLANES - 1, tm), F32)], axis=0).astype(BF16)
            for jb in range(n_qb):
                qT_ref[0, jb, n, 0:HEAD_DIM, g * Q_TILE:(g + 1) * Q_TILE] = (
                    qjTb[hh * HEAD_DIM:(hh + 1) * HEAD_DIM, jb * Q_TILE:(jb + 1) * Q_TILE])
                qT_ref[0, jb, n, HEAD_DIM:HEAD_AUG, g * Q_TILE:(g + 1) * Q_TILE] = (
                    shift[:, jb * Q_TILE:(jb + 1) * Q_TILE])
    for j in range(KV_WIDTH // LANES):
        vjT = grp(Q_WIDTH + KV_WIDTH, j).T
        vT_ref[0, j * LANES:(j + 1) * LANES, :] = vjT
        for hh in range(2):
            n = 2 * j + hh
            vTb_ref[0, 0, n * HEAD_AUG:n * HEAD_AUG + HEAD_DIM, :] = (
                vjT[hh * HEAD_DIM:(hh + 1) * HEAD_DIM, :].astype(BF16))
            vTb_ref[0, 0, n * HEAD_AUG + HEAD_DIM:(n + 1) * HEAD_AUG, :] = one_hot_row((BF16_SUBLANES, tm), 0)
    off_qi = Q_WIDTH + 2 * KV_WIDTH
    for j in range(QI_WIDTH // LANES):
        qijT = rope_rows(grp(off_qi, j).T).astype(BF16)
        for hh in range(2):
            head = 2 * j + hh
            for jb in range(n_qb):
                qiT_ref[0, jb, :, head * Q_TILE:(head + 1) * Q_TILE] = (
                    qijT[hh * IDX_DIM:(hh + 1) * IDX_DIM, jb * Q_TILE:(jb + 1) * Q_TILE])
    last = grp(off_qi + QI_WIDTH, 0)
    kir = _rope_group(last, cos, sa, sb)
    kiT_ref[0] = kir.T[:IDX_DIM, :]
    kib_ref[0] = kir[:, :IDX_DIM].astype(BF16)
    lastT = last.T
    for jb in range(n_qb):
        wiT_ref[0, jb] = lastT[IDX_DIM:IDX_DIM + IDX_HEADS, jb * Q_TILE:(jb + 1) * Q_TILE]


def _attn_proj_prompt(x, g, shift, scale, w_pad, tables):
    b, t, d = x.shape
    tm = ROW_TILE
    nq = t // Q_TILE
    grid = (b, t // tm)
    row = lambda width: pl.BlockSpec((1, tm, width), lambda bi, i: (bi, i, 0))
    col = lambda width: pl.BlockSpec((1, width, tm), lambda bi, i: (bi, 0, i))
    tab = pl.BlockSpec((tm, LANES), lambda bi, i: (i, 0))
    tab_t = pl.BlockSpec((ROT_HALF, tm), lambda bi, i: (0, i))
    out_shape = (
        jax.ShapeDtypeStruct((b, KV_WIDTH, t), F32),
        jax.ShapeDtypeStruct((b, KV_WIDTH, t), F32),
        jax.ShapeDtypeStruct((b, IDX_DIM, t), F32),
        jax.ShapeDtypeStruct((b, nq, N_KV_HEADS, HEAD_AUG, Q_PER_KV * Q_TILE), BF16),
        jax.ShapeDtypeStruct((b, nq, IDX_DIM, IDX_HEADS * Q_TILE), BF16),
        jax.ShapeDtypeStruct((b, nq, IDX_HEADS, Q_TILE), F32),
        jax.ShapeDtypeStruct((b, N_KV_HEADS, t, HEAD_AUG), BF16),
        jax.ShapeDtypeStruct((b, t // tm, N_KV_HEADS * HEAD_AUG, tm), BF16),
        jax.ShapeDtypeStruct((b, t, IDX_DIM), BF16),
    )
    nqb = tm // Q_TILE
    out_specs = (
        col(KV_WIDTH), col(KV_WIDTH), col(IDX_DIM),
        pl.BlockSpec((1, nqb, N_KV_HEADS, HEAD_AUG, Q_PER_KV * Q_TILE), lambda bi, i: (bi, i, 0, 0, 0)),
        pl.BlockSpec((1, nqb, IDX_DIM, IDX_HEADS * Q_TILE), lambda bi, i: (bi, i, 0, 0)),
        pl.BlockSpec((1, nqb, IDX_HEADS, Q_TILE), lambda bi, i: (bi, i, 0, 0)),
        pl.BlockSpec((1, N_KV_HEADS, tm, HEAD_AUG), lambda bi, i: (bi, 0, i, 0)),
        pl.BlockSpec((1, 1, N_KV_HEADS * HEAD_AUG, tm), lambda bi, i: (bi, i, 0, 0)),
        row(IDX_DIM),
    )
    return pl.pallas_call(
        _attn_proj_prompt_kernel,
        grid=grid,
        in_specs=[row(d), pl.BlockSpec((1, d), lambda bi, i: (0, 0)),
                  _mod_spec(shift, tm), _mod_spec(scale, tm),
                  pl.BlockSpec(w_pad.shape, lambda bi, i: (0, 0)), tab, tab, tab, tab_t, tab_t],
        out_specs=out_specs,
        out_shape=out_shape,
        scratch_shapes=[pltpu.VMEM((N_KV_HEADS, 1, 1), F32)],
        compiler_params=_cparams(2),
        name="attn_proj_prompt",
    )(x, g, shift, scale, w_pad, *tables)


def _attn_proj_sample_kernel(x_ref, g_ref, sh_ref, sc_ref, w_ref, cos_ref, sa_ref, sb_ref,
                             q_ref, k_ref, v_ref, qi_ref, ki_ref, wi_ref):
    h = _modulated_norm(x_ref[0], g_ref[...], sh_ref[0], sc_ref[0]).astype(BF16)
    acc = jnp.dot(h, w_ref[...], preferred_element_type=F32)
    cos, sa, sb = cos_ref[...], sa_ref[...], sb_ref[...]
    grp = lambda off, j: acc[:, off + j * LANES: off + (j + 1) * LANES]
    for j in range(Q_WIDTH // LANES):
        q_ref[:, j * LANES:(j + 1) * LANES] = _rope_group(grp(0, j), cos, sa, sb) * (HEAD_DIM ** -0.5)
    for j in range(KV_WIDTH // LANES):
        k_ref[:, j * LANES:(j + 1) * LANES] = _rope_group(grp(Q_WIDTH, j), cos, sa, sb)
        v_ref[:, j * LANES:(j + 1) * LANES] = grp(Q_WIDTH + KV_WIDTH, j)
    off_qi = Q_WIDTH + 2 * KV_WIDTH
    for j in range(QI_WIDTH // LANES):
        qi_ref[:, j * LANES:(j + 1) * LANES] = _rope_group(grp(off_qi, j), cos, sa, sb)
    last = grp(off_qi + QI_WIDTH, 0)
    ki_ref[...] = _rope_group(last, cos, sa, sb)[:, :IDX_DIM]
    wi_ref[...] = last[:, IDX_DIM:IDX_DIM + IDX_HEADS]


def _attn_proj_sample(x, g, shift, scale, w_pad, tables):
    _, rows, d = x.shape
    full = lambda shape: pl.BlockSpec(shape, lambda i: (0,) * len(shape))
    widths = (Q_WIDTH, KV_WIDTH, KV_WIDTH, QI_WIDTH, IDX_DIM, IDX_HEADS)
    return pl.pallas_call(
        _attn_proj_sample_kernel,
        grid=(1,),
        in_specs=[full((1, rows, d)), full((1, d)), full((1, rows, d)), full((1, rows, d)),
                  full(w_pad.shape), full((rows, LANES)), full((rows, LANES)), full((rows, LANES))],
        out_specs=tuple(full((rows, w)) for w in widths),
        out_shape=tuple(jax.ShapeDtypeStruct((rows, w), F32) for w in widths),
        compiler_params=_cparams(1),
        name="attn_proj_sample",
    )(x, g, shift, scale, w_pad, *tables[:3])


def _dsa_prompt_kernel(qT_ref, qiT_ref, wiT_ref, kb_ref, vT_ref, kib_ref, o_ref,
                       keys_ref, acc_ref, acc2_ref, m_ref, l_ref, j_ref, *, topk, idx_bits):
    kc = KEY_CHUNK
    qb = pl.program_id(1)
    n_chunks = (qb * Q_TILE + Q_TILE + kc - 1) // kc
    rows = lax.broadcasted_iota(I32, (kc, Q_TILE), 0)
    q_pos = qb * Q_TILE + lax.broadcasted_iota(I32, (kc, Q_TILE), 1)

    w_heads = wiT_ref[0, 0] * IDX_SCALE
    qiT = qiT_ref[0, 0]

    def score_chunk(c, carry):
        off = pl.multiple_of(c * kc, kc)
        dots = jnp.dot(kib_ref[0, pl.ds(off, kc), :], qiT, preferred_element_type=F32)
        s = jnp.zeros((kc, Q_TILE), F32)
        for h in range(IDX_HEADS):
            s = s + jnp.maximum(dots[:, h * Q_TILE:(h + 1) * Q_TILE], 0.0) * w_heads[h:h + 1, :]
        keys_ref[pl.ds(off, kc), :] = jnp.where(rows + off <= q_pos, _float_order_key(s), INT_MIN)
        return carry

    lax.fori_loop(0, n_chunks, score_chunk, 0)

    def count(pred):
        def body(c, cnt):
            off = pl.multiple_of(c * kc, kc)
            hit = jnp.where(pred(keys_ref[pl.ds(off, kc), :], rows + off), 1, 0)
            return cnt + hit.reshape(kc // SUBLANES, SUBLANES, Q_TILE).sum(axis=0)
        cnt = lax.fori_loop(0, n_chunks, body, jnp.zeros((SUBLANES, Q_TILE), I32))
        return cnt.sum(axis=0, keepdims=True)

    def thr_bit(i, state):
        thr, n_ge = state
        cand = thr + lax.shift_left(jnp.int32(1), 31 - i)
        n_cand = count(lambda k, s: k >= cand)
        ok = n_cand >= topk
        return jnp.where(ok, cand, thr), jnp.where(ok, n_cand, n_ge)

    n_all = jnp.full((1, Q_TILE), 1, I32) * (n_chunks * kc)
    thr, n_ge = lax.fori_loop(0, 32, thr_bit, (jnp.full((1, Q_TILE), INT_MIN, I32), n_all))

    j_ref[...] = jnp.full((1, Q_TILE), INT_MAX, I32)
    tie_split = jnp.max(jnp.where((n_ge > topk) & (thr > INT_MIN), 1, 0))

    @pl.when(tie_split > 0)
    def _():
        need = topk - count(lambda k, s: k > thr)

        def idx_bit(i, j0):
            cand = j0 + lax.shift_left(jnp.int32(1), idx_bits - 1 - i)
            taken_before = count(lambda k, s: (k == thr) & (s < cand))
            return jnp.where(taken_before < need, cand, j0)
        j_ref[...] = lax.fori_loop(0, idx_bits, idx_bit, jnp.zeros((1, Q_TILE), I32))

    j_sel = j_ref[...]

    def selection_bias(c):
        off = pl.multiple_of(c * kc, kc)
        k = keys_ref[pl.ds(off, kc), :]
        s_idx = rows + off
        sel = ((k > thr) | ((k == thr) & (s_idx <= j_sel))) & (s_idx <= q_pos)
        bias = jnp.where(sel, 0.0, NEG)
        return off, jnp.concatenate([bias] * Q_PER_KV, axis=1)

    acc_ref[...] = jnp.zeros(acc_ref.shape, F32)

    def attend_chunk(c, carry):
        off, bias = selection_bias(c)
        for n in range(N_KV_HEADS):
            s = jnp.dot(kb_ref[0, n, pl.ds(off, kc), :], qT_ref[0, 0, n], preferred_element_type=F32) + bias
            acc_ref[n] += jnp.dot(vT_ref[0, c, n * HEAD_AUG:(n + 1) * HEAD_AUG, :], jnp.exp(s).astype(BF16),
                                  preferred_element_type=F32)
        return carry

    lax.fori_loop(0, n_chunks, attend_chunk, 0)
    denom = jnp.concatenate([acc_ref[n][HEAD_DIM:HEAD_DIM + 1, :] for n in range(N_KV_HEADS)], axis=0)
    healthy = (jnp.min(denom) >= DENOM_FLOOR) & (jnp.max(denom) <= DENOM_CEIL)

    @pl.when(jnp.logical_not(healthy))
    def _():
        m_ref[...] = jnp.full(m_ref.shape, NEG, F32)
        l_ref[...] = jnp.zeros(l_ref.shape, F32)
        acc2_ref[...] = jnp.zeros(acc2_ref.shape, F32)

        def attend_chunk_online(c, carry):
            off, bias = selection_bias(c)
            for n in range(N_KV_HEADS):
                s = jnp.dot(kb_ref[0, n, pl.ds(off, kc), 0:HEAD_DIM], qT_ref[0, 0, n, 0:HEAD_DIM, :],
                            preferred_element_type=F32) + bias
                m_prev = m_ref[n]
                m_new = jnp.maximum(m_prev, s.max(axis=0, keepdims=True))
                alpha = jnp.exp(m_prev - m_new)
                p = jnp.exp(s - m_new)
                l_ref[n] = alpha * l_ref[n] + p.sum(axis=0, keepdims=True)
                pv = jnp.dot(vT_ref[0, c, n * HEAD_AUG:n * HEAD_AUG + HEAD_DIM, :], p.astype(BF16),
                             preferred_element_type=F32)
                acc2_ref[n] = alpha * acc2_ref[n] + pv
                m_ref[n] = m_new
            return carry

        lax.fori_loop(0, n_chunks, attend_chunk_online, 0)
        for n in range(N_KV_HEADS):
            acc_ref[n, 0:HEAD_DIM, :] = acc2_ref[n]
            acc_ref[n, HEAD_DIM:HEAD_DIM + 1, :] = l_ref[n]

    for n in range(N_KV_HEADS):
        on = acc_ref[n, 0:HEAD_DIM, :] / acc_ref[n, HEAD_DIM:HEAD_DIM + 1, :]
        for gp in range(Q_PER_KV // 2):
            pair = jnp.concatenate([on[:, (2 * gp) * Q_TILE:(2 * gp + 1) * Q_TILE],
                                    on[:, (2 * gp + 1) * Q_TILE:(2 * gp + 2) * Q_TILE]], axis=0)
            col = (n * Q_PER_KV // 2 + gp) * LANES
            o_ref[0, :, col:col + LANES] = pair.T.astype(BF16)


def _dsa_prompt(qT, qiT, wiT, kb, vT, kib):
    b, nq = qT.shape[:2]
    t = kb.shape[2]
    topk = min(TOPK_MAX, t // 4)
    idx_bits = max(1, (t - 1).bit_length())
    per_q = lambda shape: pl.BlockSpec((1, 1) + shape, lambda bi, qi: (bi, qi) + (0,) * len(shape))
    per_b = lambda shape: pl.BlockSpec((1,) + shape, lambda bi, qi: (bi,) + (0,) * len(shape))
    return pl.pallas_call(
        functools.partial(_dsa_prompt_kernel, topk=topk, idx_bits=idx_bits),
        grid=(b, nq),
        in_specs=[per_q(qT.shape[2:]), per_q(qiT.shape[2:]), per_q(wiT.shape[2:]),
                  per_b(kb.shape[1:]), per_b(vT.shape[1:]), per_b(kib.shape[1:])],
        out_specs=pl.BlockSpec((1, Q_TILE, Q_WIDTH), lambda bi, qi: (bi, qi, 0)),
        out_shape=jax.ShapeDtypeStruct((b, t, Q_WIDTH), BF16),
        scratch_shapes=[pltpu.VMEM((t, Q_TILE), I32),
                        pltpu.VMEM((N_KV_HEADS, HEAD_AUG, Q_PER_KV * Q_TILE), F32),
                        pltpu.VMEM((N_KV_HEADS, HEAD_DIM, Q_PER_KV * Q_TILE), F32),
                        pltpu.VMEM((N_KV_HEADS, 1, Q_PER_KV * Q_TILE), F32),
                        pltpu.VMEM((N_KV_HEADS, 1, Q_PER_KV * Q_TILE), F32),
                        pltpu.VMEM((1, Q_TILE), I32)],
        compiler_params=_cparams(2),
        name="dsa_prompt",
    )(qT, qiT, wiT, kb, vT, kib)


def _fetch_pages_start(pt_ref, seq, layer_base, caches, bufs, sems, n_pages):
    def body(p, carry):
        page = layer_base + pt_ref[seq * n_pages + p]
        for cache, buf, sem in zip(caches, bufs, sems):
            pltpu.make_async_copy(cache.at[page], buf.at[p], sem).start()
        return carry
    lax.fori_loop(0, n_pages, body, 0)


def _fetch_pages_wait(caches, bufs, sems, n_pages):
    for cache, buf, sem in zip(caches, bufs, sems):
        pltpu.make_async_copy(cache.at[pl.ds(0, n_pages)], buf, sem).wait()


def _fetch_pages_pipelined(pt_ref, layer_base, caches, bufs, sems, n_pages):
    step, n_steps = pl.program_id(0), pl.num_programs(0)
    slot = step % 2
    at = lambda sl: ([b.at[sl] for b in bufs], [s.at[sl] for s in sems])

    @pl.when(step == 0)
    def _():
        _fetch_pages_start(pt_ref, 0, layer_base, caches, *at(0), n_pages)

    @pl.when(step + 1 < n_steps)
    def _():
        _fetch_pages_start(pt_ref, step + 1, layer_base, caches, *at(1 - slot), n_pages)

    _fetch_pages_wait(caches, *at(slot), n_pages)
    return slot


def _pages_bf16(buf, slot, first, count):
    return jnp.concatenate([buf[slot, first + j].astype(BF16) for j in range(count)], axis=1)


def _dsa_sample_scores_kernel(pt_ref, qi_ref, w_ref, kin_ref, kiT_hbm, keys_ref, buf_ref, sem,
                              *, n_pages, layer_base):
    slot = _fetch_pages_pipelined(pt_ref, layer_base, [kiT_hbm], [buf_ref], [sem], n_pages)
    qi = qi_ref[0]
    w = w_ref[0] * IDX_SCALE
    score = lambda dots: jnp.sum(jnp.maximum(dots, 0.0) * w, axis=0, keepdims=True)
    for g in range(n_pages // PAGE_GROUP):
        kiT = _pages_bf16(buf_ref, slot, g * PAGE_GROUP, PAGE_GROUP)
        keys = _float_order_key(score(jnp.dot(qi, kiT, preferred_element_type=F32)))
        keys_ref[0, :, g * PAGE_GROUP * PAGE_SIZE:(g + 1) * PAGE_GROUP * PAGE_SIZE] = keys
    kn = kin_ref[0].astype(BF16).astype(F32)
    sn = score(jnp.sum(qi.astype(F32) * kn, axis=1, keepdims=True))
    lane = lax.broadcasted_iota(I32, (1, LANES), 1)
    keys_ref[0, :, n_pages * PAGE_SIZE:] = jnp.where(lane == 0, _float_order_key(sn), INT_MIN)


def _dsa_sample_scores(pt_flat, qi3, w3, ki_new, cache_kiT, layer, n_pool, n_pages):
    db = qi3.shape[0]
    n_lanes = n_pages * PAGE_SIZE + LANES
    per_b = lambda shape: pl.BlockSpec((1,) + shape, lambda b, pt: (b,) + (0,) * len(shape))
    grid_spec = pltpu.PrefetchScalarGridSpec(
        num_scalar_prefetch=1,
        grid=(db,),
        in_specs=[per_b((IDX_HEADS, IDX_DIM)), per_b((IDX_HEADS, 1)), per_b((1, IDX_DIM)),
                  pl.BlockSpec(memory_space=pl.ANY)],
        out_specs=per_b((1, n_lanes)),
        scratch_shapes=[pltpu.VMEM((2, n_pages, IDX_DIM, PAGE_SIZE), F32), pltpu.SemaphoreType.DMA((2,))],
    )
    return pl.pallas_call(
        functools.partial(_dsa_sample_scores_kernel, n_pages=n_pages, layer_base=layer * n_pool),
        grid_spec=grid_spec,
        out_shape=jax.ShapeDtypeStruct((db, 1, n_lanes), I32),
        compiler_params=_cparams(1),
        name="dsa_sample_scores",
    )(pt_flat, qi3, w3, ki_new, cache_kiT)


def _dsa_sample_select_kernel(keys_ref, bias_ref, *, n_keys, topk, idx_bits):
    keys = keys_ref[...]
    idx = lax.broadcasted_iota(I32, keys.shape, 1)
    count = lambda hit: jnp.sum(jnp.where(hit, 1, 0), axis=1, keepdims=True)
    keys = jnp.where(idx < n_keys, keys, INT_MIN)

    def thr_bit(i, thr):
        cand = thr + lax.shift_left(jnp.int32(1), 31 - i)
        return jnp.where(count(keys >= cand) >= topk, cand, thr)

    thr = lax.fori_loop(0, 32, thr_bit, jnp.full((keys.shape[0], 1), INT_MIN, I32))
    need = topk - count(keys > thr)

    def idx_bit(i, j0):
        cand = j0 + lax.shift_left(jnp.int32(1), idx_bits - 1 - i)
        return jnp.where(count((keys == thr) & (idx < cand)) < need, cand, j0)

    j_sel = lax.fori_loop(0, idx_bits, idx_bit, jnp.zeros((keys.shape[0], 1), I32))
    sel = ((keys > thr) | ((keys == thr) & (idx <= j_sel))) & (idx < n_keys)
    bias_ref[...] = jnp.where(sel, 0.0, NEG)


def _dsa_sample_select(keys, n_keys):
    topk = min(TOPK_MAX, n_keys // 4)
    idx_bits = max(1, (n_keys - 1).bit_length())
    full = pl.BlockSpec(keys.shape, lambda i: (0, 0))
    return pl.pallas_call(
        functools.partial(_dsa_sample_select_kernel, n_keys=n_keys, topk=topk, idx_bits=idx_bits),
        grid=(1,),
        in_specs=[full],
        out_specs=full,
        out_shape=jax.ShapeDtypeStruct(keys.shape, F32),
        compiler_params=_cparams(1),
        name="dsa_sample_select",
    )(keys)


def _dsa_sample_attend_kernel(pt_ref, q_ref, bias_ref, kn_ref, vn_ref, kT_hbm, vT_hbm, o_ref,
                              kbuf_ref, vbuf_ref, ksem, vsem, *, n_pages, layer_base):
    slot = _fetch_pages_pipelined(pt_ref, layer_base, [kT_hbm, vT_hbm], [kbuf_ref, vbuf_ref],
                                  [ksem, vsem], n_pages)
    q = q_ref[0]
    n_past = n_pages * PAGE_SIZE
    group = PAGE_GROUP * PAGE_SIZE
    scores = [jnp.dot(q, _pages_bf16(kbuf_ref, slot, g * PAGE_GROUP, PAGE_GROUP), preferred_element_type=F32)
              + bias_ref[0, :, g * group:(g + 1) * group] for g in range(n_pages // PAGE_GROUP)]
    kn = kn_ref[0].astype(BF16).astype(F32)
    vn = vn_ref[0].astype(BF16).astype(F32)
    sn = jnp.sum(q.astype(F32) * kn, axis=1, keepdims=True) + bias_ref[0, :, n_past:n_past + 1]
    m = sn
    for s in scores:
        m = jnp.maximum(m, s.max(axis=1, keepdims=True))
    pn = jnp.exp(sn - m)
    denom = pn
    acc = pn.astype(BF16).astype(F32) * vn
    for g, s in enumerate(scores):
        p = jnp.exp(s - m)
        denom = denom + p.sum(axis=1, keepdims=True)
        acc = acc + lax.dot_general(p.astype(BF16), _pages_bf16(vbuf_ref, slot, g * PAGE_GROUP, PAGE_GROUP),
                                    (((1,), (1,)), ((), ())), preferred_element_type=F32)
    out = acc / denom
    head_grp = lax.broadcasted_iota(I32, out.shape, 0) // Q_PER_KV
    lane_grp = lax.broadcasted_iota(I32, out.shape, 1) // HEAD_DIM
    out = jnp.where(head_grp == lane_grp, out, 0.0)
    o = out[:, 0:HEAD_DIM]
    for n in range(1, N_KV_HEADS):
        o = o + out[:, n * HEAD_DIM:(n + 1) * HEAD_DIM]
    o_ref[0] = o


def _dsa_sample_attend(pt_flat, q_bd, bias, k_new, v_new, cache_kT, cache_vT, layer, n_pool, n_pages):
    db = q_bd.shape[0]
    per_b = lambda shape: pl.BlockSpec((1,) + shape, lambda b, pt: (b,) + (0,) * len(shape))
    hbm = pl.BlockSpec(memory_space=pl.ANY)
    pages = pltpu.VMEM((2, n_pages, KV_WIDTH, PAGE_SIZE), F32)
    grid_spec = pltpu.PrefetchScalarGridSpec(
        num_scalar_prefetch=1,
        grid=(db,),
        in_specs=[per_b(q_bd.shape[1:]), per_b(bias.shape[1:]), per_b((1, KV_WIDTH)), per_b((1, KV_WIDTH)),
                  hbm, hbm],
        out_specs=per_b((N_HEADS, HEAD_DIM)),
        scratch_shapes=[pages, pages, pltpu.SemaphoreType.DMA((2,)), pltpu.SemaphoreType.DMA((2,))],
    )
    return pl.pallas_call(
        functools.partial(_dsa_sample_attend_kernel, n_pages=n_pages, layer_base=layer * n_pool),
        grid_spec=grid_spec,
        out_shape=jax.ShapeDtypeStruct((db, N_HEADS, HEAD_DIM), F32),
        compiler_params=_cparams(1),
        name="dsa_sample_attend",
    )(pt_flat, q_bd, bias, k_new, v_new, cache_kT, cache_vT)


def _out_proj_kernel(a_ref, w_ref, x_ref, gt_ref, o_ref):
    y = jnp.dot(a_ref[0], w_ref[...], preferred_element_type=F32)
    o_ref[0] = x_ref[0] + gt_ref[0] * y


def _out_proj(a, w, x, gate):
    b, t, d = x.shape
    tm = min(ROW_TILE, t)
    row = lambda width: pl.BlockSpec((1, tm, width), lambda bi, i: (bi, i, 0))
    return pl.pallas_call(
        _out_proj_kernel,
        grid=(b, t // tm),
        in_specs=[row(a.shape[2]), pl.BlockSpec(w.shape, lambda bi, i: (0, 0)), row(d), _mod_spec(gate, tm)],
        out_specs=row(d),
        out_shape=jax.ShapeDtypeStruct((b, t, d), F32),
        compiler_params=_cparams(2),
        name="out_proj",
    )(a, w, x, gate)


def _gelu_tanh(x):
    return x * (0.5 * (1.0 + jnp.tanh(math.sqrt(2.0 / math.pi) * (x + 0.044715 * (x * x * x)))))


def _rec_proj_kernel(x_ref, g_ref, sh_ref, sc_ref, w_ref, gate_ref, xb_ref):
    h = _modulated_norm(x_ref[0], g_ref[...], sh_ref[0], sc_ref[0]).astype(BF16)
    acc = jnp.dot(h, w_ref[...], preferred_element_type=F32)
    gate_ref[0] = _gelu_tanh(acc[:, :D_RNN])
    xb_ref[0] = acc[:, D_RNN:]


def _rec_proj(x, g, shift, scale, w):
    b, t, d = x.shape
    tm = min(ROW_TILE, t)
    row = lambda width: pl.BlockSpec((1, tm, width), lambda bi, i: (bi, i, 0))
    return pl.pallas_call(
        _rec_proj_kernel,
        grid=(b, t // tm),
        in_specs=[row(d), pl.BlockSpec((1, d), lambda bi, i: (0, 0)), _mod_spec(shift, tm),
                  _mod_spec(scale, tm), pl.BlockSpec(w.shape, lambda bi, i: (0, 0))],
        out_specs=(row(D_RNN), row(D_RNN)),
        out_shape=(jax.ShapeDtypeStruct((b, t, D_RNN), F32),) * 2,
        compiler_params=_cparams(2),
        name="rec_proj",
    )(x, g, shift, scale, w)


def _rglru_gates(xc, wa_ref, ba, wx_ref, bx, lam):
    xcb = xc.astype(BF16)
    blk = lambda w_ref: jnp.concatenate(
        [jnp.dot(xcb[:, n * RG_BLOCK_W:(n + 1) * RG_BLOCK_W], w_ref[n], preferred_element_type=F32)
         for n in range(RG_BLOCKS)], axis=1)
    r = jax.nn.sigmoid(blk(wa_ref) + ba)
    i = jax.nn.sigmoid(blk(wx_ref) + bx)
    neg_lam = -lam
    softplus = jnp.maximum(neg_lam, 0.0) + jnp.log1p(jnp.exp(-jnp.abs(neg_lam)))
    log_a = -RG_C * r * softplus
    a = jnp.exp(log_a)
    u = jnp.sqrt(1.0 - a * a) * (i * xc)
    return a, u


def _rglru_prompt_kernel(gate_ref, xb_ref, cw_ref, cb_ref, wa_ref, ba_ref, wx_ref, bx_ref, lam_ref,
                         y_ref, h_ref, conv_ref, xcat_ref, a_ref, u_ref, hs_ref, hc_ref):
    tc = xb_ref.shape[1]
    c = pl.program_id(1)
    pad = SUBLANES

    @pl.when(c == 0)
    def _():
        xcat_ref[0:pad, :] = jnp.zeros((pad, D_RNN), F32)
        hc_ref[...] = jnp.zeros(hc_ref.shape, F32)

    @pl.when(c > 0)
    def _():
        xcat_ref[0:pad, :] = xcat_ref[tc:tc + pad, :]

    xcat_ref[pad:pad + tc, :] = xb_ref[0]
    cw = cw_ref[...]
    xc = xcat_ref[pad - 3:pad - 3 + tc, :] * cw[0:1, :]
    for j in range(1, CONV_W):
        xc = xc + xcat_ref[pad - 3 + j:pad - 3 + j + tc, :] * cw[j:j + 1, :]
    xc = cb_ref[...] + xc
    a, u = _rglru_gates(xc, wa_ref, ba_ref[...], wx_ref, bx_ref[...], lam_ref[...])
    a_ref[...] = a
    u_ref[...] = u

    def step(t, h):
        h = a_ref[pl.ds(t, 1), :] * h + u_ref[pl.ds(t, 1), :]
        hs_ref[pl.ds(t, 1), :] = h
        return h

    h = lax.fori_loop(0, tc, step, hc_ref[...], unroll=8)
    hc_ref[...] = h
    y_ref[0] = (hs_ref[...] * gate_ref[0]).astype(BF16)
    h_ref[0] = h
    conv_ref[0] = xcat_ref[pad + tc - (CONV_W - 1):pad + tc, :]


def _rglru_prompt(gate, xb, cw, cb, wa, ba, wx, bx, lam):
    b, t, d = xb.shape
    tc = min(SCAN_CHUNK, t)
    row = pl.BlockSpec((1, tc, d), lambda bi, i: (bi, i, 0))
    vec = pl.BlockSpec((1, d), lambda bi, i: (0, 0))
    full = lambda shape: pl.BlockSpec(shape, lambda bi, i: (0,) * len(shape))
    return pl.pallas_call(
        _rglru_prompt_kernel,
        grid=(b, t // tc),
        in_specs=[row, row, full(cw.shape), vec, full(wa.shape), vec, full(wx.shape), vec, vec],
        out_specs=(row, pl.BlockSpec((1, 1, d), lambda bi, i: (bi, 0, 0)),
                   pl.BlockSpec((1, CONV_W - 1, d), lambda bi, i: (bi, 0, 0))),
        out_shape=(jax.ShapeDtypeStruct((b, t, d), BF16), jax.ShapeDtypeStruct((b, 1, d), F32),
                   jax.ShapeDtypeStruct((b, CONV_W - 1, d), F32)),
        scratch_shapes=[pltpu.VMEM((tc + 2 * SUBLANES, d), F32), pltpu.VMEM((tc, d), F32),
                        pltpu.VMEM((tc, d), F32), pltpu.VMEM((tc, d), F32), pltpu.VMEM((1, d), F32)],
        compiler_params=_cparams(2),
        name="rglru_prompt",
    )(gate, xb, cw, cb, wa, ba, wx, bx, lam)


def _rglru_sample_kernel(gate_ref, xb_ref, buf_ref, h0_ref, cw_ref, cb_ref, wa_ref, ba_ref, wx_ref, bx_ref,
                         lam_ref, y_ref, h_ref, nbuf_ref):
    cw = cw_ref[...]
    xb = xb_ref[...]
    xc = buf_ref[0] * cw[0:1, :]
    for j in range(1, CONV_W - 1):
        xc = xc + buf_ref[j] * cw[j:j + 1, :]
    xc = cb_ref[...] + (xc + xb * cw[CONV_W - 1:CONV_W, :])
    a, u = _rglru_gates(xc, wa_ref, ba_ref[...], wx_ref, bx_ref[...], lam_ref[...])
    h = a * h0_ref[...] + u
    h_ref[...] = h
    y_ref[...] = (h * gate_ref[...]).astype(BF16)
    for j in range(CONV_W - 2):
        nbuf_ref[j] = buf_ref[j + 1]
    nbuf_ref[CONV_W - 2] = xb


def _rglru_sample(gate, xb, buf, h0, cw, cb, wa, ba, wx, bx, lam):
    rows, d = xb.shape
    full = lambda a: pl.BlockSpec(a.shape, lambda i: (0,) * a.ndim)
    args = (gate, xb, buf, h0, cw, cb, wa, ba, wx, bx, lam)
    return pl.pallas_call(
        _rglru_sample_kernel,
        grid=(1,),
        in_specs=[full(a) for a in args],
        out_specs=(full(xb), full(xb), full(buf)),
        out_shape=(jax.ShapeDtypeStruct((rows, d), BF16), jax.ShapeDtypeStruct((rows, d), F32),
                   jax.ShapeDtypeStruct(buf.shape, F32)),
        compiler_params=_cparams(1),
        name="rglru_sample",
    )(*args)


def _route(logits):
    lane = lax.broadcasted_iota(I32, logits.shape, 1)

    def first_max(v):
        m = jnp.max(v, axis=1, keepdims=True)
        return m, jnp.min(jnp.where(v == m, lane, LANES), axis=1, keepdims=True)

    is_grp = lane < N_GROUPS
    gm, grp = first_max(jnp.where(is_grp, logits, -jnp.inf))
    g_w = 1.0 / jnp.sum(jnp.where(is_grp, jnp.exp(logits - gm), 0.0), axis=1, keepdims=True)
    lo = N_GROUPS + EXPERTS_PER_GROUP * grp
    el = jnp.where((lane >= lo) & (lane < lo + EXPERTS_PER_GROUP), logits, -jnp.inf)
    m1, i1 = first_max(el)
    m2, i2 = first_max(jnp.where(lane == i1, -jnp.inf, el))
    e2 = jnp.exp(m2 - m1)
    den = 1.0 + e2
    return i1 - N_GROUPS, i2 - N_GROUPS, (1.0 / den) * g_w, (e2 / den) * g_w


def _moe_kernel(x_ref, g_ref, sh_ref, sc_ref, gt_ref, wr_ref, br_ref, w1_ref, w2_ref, gf_ref, o_ref,
                h_ref, acc_ref, e0_ref, e1_ref, c0_ref, c1_ref, *, final_norm):
    e = pl.program_id(2)

    @pl.when(e == 0)
    def _():
        h = _modulated_norm(x_ref[0], g_ref[...], sh_ref[0], sc_ref[0])
        h_ref[...] = h.astype(BF16)
        logits = jnp.dot(h_ref[...], wr_ref[...], preferred_element_type=F32) + br_ref[...]
        e0_ref[...], e1_ref[...], c0_ref[...], c1_ref[...] = _route(logits)
        acc_ref[...] = jnp.zeros(acc_ref.shape, F32)

    gu = jnp.dot(h_ref[...], w1_ref[0].astype(BF16), preferred_element_type=F32)
    gp, up = gu[:, :D_EXPERT], gu[:, D_EXPERT:]
    act = ((gp * jax.nn.sigmoid(gp)) * up).astype(BF16)
    y = jnp.dot(act, w2_ref[0].astype(BF16), preferred_element_type=F32)
    wt = jnp.where(e0_ref[...] == e, c0_ref[...], 0.0) + jnp.where(e1_ref[...] == e, c1_ref[...], 0.0)
    acc_ref[...] += y * wt

    @pl.when(e == N_EXPERTS - 1)
    def _():
        out = x_ref[0] + gt_ref[0] * acc_ref[...]
        if final_norm:
            out = (out * lax.rsqrt(jnp.mean(out * out, axis=-1, keepdims=True) + EPS)) * gf_ref[...]
        o_ref[0] = out


def _moe(x, g, shift, scale, gate, wr, br, w1, w2, layer, gf, final_norm):
    b, t, d = x.shape
    tm = min(MOE_ROW_TILE, t)
    row = pl.BlockSpec((1, tm, d), lambda bi, i, e: (bi, i, 0))
    vec = lambda width: pl.BlockSpec((1, width), lambda bi, i, e: (0, 0))
    col = lambda dt: pltpu.VMEM((tm, 1), dt)
    return pl.pallas_call(
        functools.partial(_moe_kernel, final_norm=final_norm),
        grid=(b, t // tm, N_EXPERTS),
        in_specs=[row, vec(d), _mod_spec(shift, tm), _mod_spec(scale, tm), _mod_spec(gate, tm),
                  pl.BlockSpec(wr.shape, lambda bi, i, e: (0, 0)), vec(ROUTER_PAD),
                  pl.BlockSpec((1, d, 2 * D_EXPERT), lambda bi, i, e: (layer * N_EXPERTS + e, 0, 0)),
                  pl.BlockSpec((1, D_EXPERT, d), lambda bi, i, e: (layer * N_EXPERTS + e, 0, 0)),
                  vec(d)],
        out_specs=row,
        out_shape=jax.ShapeDtypeStruct((b, t, d), F32),
        scratch_shapes=[pltpu.VMEM((tm, d), BF16), pltpu.VMEM((tm, d), F32),
                        col(I32), col(I32), col(F32), col(F32)],
        compiler_params=_cparams(3),
        name="moe",
    )(x, g, shift, scale, gate, wr, br, w1, w2, gf)


REC_E0, REC_E1, REC_RANK0, REC_RANK1, REC_C0, REC_C1 = range(6)


def _moe_route_kernel(x_ref, g_ref, sh_ref, sc_ref, wr_ref, br_ref, h_ref, rec_ref, recT_ref, cnt_ref,
                      tri_ref, carry_ref):
    tm = x_ref.shape[1]

    @pl.when((pl.program_id(0) == 0) & (pl.program_id(1) == 0))
    def _():
        carry_ref[...] = jnp.zeros(carry_ref.shape, F32)
        earlier = lax.broadcasted_iota(I32, (tm, tm), 0) > lax.broadcasted_iota(I32, (tm, tm), 1)
        tri_ref[...] = jnp.where(earlier, 1.0, 0.0).astype(BF16)

    h = _modulated_norm(x_ref[0], g_ref[...], sh_ref[0], sc_ref[0])
    h_ref[0] = h
    logits = jnp.dot(h.astype(BF16), wr_ref[...], preferred_element_type=F32) + br_ref[...]
    e0, e1, c0, c1 = _route(logits)
    lane = lax.broadcasted_iota(I32, logits.shape, 1)
    pick0, pick1 = lane == e0, lane == e1
    picks = jnp.where(pick0 | pick1, 1.0, 0.0)
    before = carry_ref[...] + jnp.dot(tri_ref[...], picks.astype(BF16), preferred_element_type=F32)
    rank0 = jnp.sum(jnp.where(pick0, before, 0.0), axis=1, keepdims=True)
    rank1 = jnp.sum(jnp.where(pick1, before, 0.0), axis=1, keepdims=True)
    carry_ref[...] = carry_ref[...] + jnp.sum(picks, axis=0, keepdims=True)
    cnt_ref[...] = carry_ref[...]
    rec = jnp.zeros(logits.shape, F32)
    for pos, val in ((REC_E0, e0.astype(F32)), (REC_E1, e1.astype(F32)), (REC_RANK0, rank0),
                     (REC_RANK1, rank1), (REC_C0, c0), (REC_C1, c1)):
        rec = jnp.where(lane == pos, val, rec)
    rec_ref[0] = rec
    recT_ref[0] = rec.T[0:SUBLANES, :]


def _moe_route(x, g, shift, scale, wr, br):
    b, t, d = x.shape
    tm = ROW_TILE
    row = lambda width: pl.BlockSpec((1, tm, width), lambda bi, i: (bi, i, 0))
    vec = lambda width: pl.BlockSpec((1, width), lambda bi, i: (0, 0))
    return pl.pallas_call(
        _moe_route_kernel,
        grid=(b, t // tm),
        in_specs=[row(d), vec(d), _mod_spec(shift, tm), _mod_spec(scale, tm),
                  pl.BlockSpec(wr.shape, lambda bi, i: (0, 0)), vec(ROUTER_PAD)],
        out_specs=(row(d), row(ROUTER_PAD), pl.BlockSpec((1, SUBLANES, tm), lambda bi, i: (bi, 0, i)),
                   vec(ROUTER_PAD)),
        out_shape=(jax.ShapeDtypeStruct((b, t, d), F32), jax.ShapeDtypeStruct((b, t, ROUTER_PAD), F32),
                   jax.ShapeDtypeStruct((b, SUBLANES, t), F32), jax.ShapeDtypeStruct((1, ROUTER_PAD), F32)),
        scratch_shapes=[pltpu.VMEM((tm, tm), BF16), pltpu.VMEM((1, ROUTER_PAD), F32)],
        compiler_params=_cparams(2),
        name="moe_route",
    )(x, g, shift, scale, wr, br)


def _gather_rows_start(idx_ref, idx_base, src_hbm, dst, sem, n_rows):
    for r in range(n_rows):
        pltpu.make_async_copy(src_hbm.at[pl.ds(idx_ref[idx_base + r], 1)], dst.at[pl.ds(r, 1)], sem).start()


def _gather_rows_wait(src_hbm, dst, sem, n_rows):
    pltpu.make_async_copy(src_hbm.at[pl.ds(0, n_rows)], dst, sem).wait()


def _moe_experts_kernel(blk_e_ref, n_used_ref, row_tok_ref, h_hbm, w1_ref, w2_ref, y_ref,
                        buf_ref, w1b_ref, w2b_ref, sem):
    i = pl.program_id(0)
    blk = buf_ref.shape[1]
    n_used = n_used_ref[0]
    n_slots = buf_ref.shape[0]
    slot = i % n_slots

    def start(block):
        s = block % n_slots
        _gather_rows_start(row_tok_ref, block * blk, h_hbm, buf_ref.at[s], sem.at[s], blk)

    @pl.when(i == 0)
    def _():
        for ahead in range(n_slots - 1):
            @pl.when(ahead < n_used)
            def _():
                start(ahead)

    @pl.when(i + n_slots - 1 < n_used)
    def _():
        start(i + n_slots - 1)

    @pl.when(i < n_used)
    def _():
        @pl.when((i == 0) | (blk_e_ref[i] != blk_e_ref[jnp.maximum(i - 1, 0)]))
        def _():
            w1b_ref[...] = w1_ref[0].astype(BF16)
            w2b_ref[...] = w2_ref[0].astype(BF16)

        _gather_rows_wait(h_hbm, buf_ref.at[slot], sem.at[slot], blk)
        gu = jnp.dot(buf_ref[slot].astype(BF16), w1b_ref[...], preferred_element_type=F32)
        gp, up = gu[:, :D_EXPERT], gu[:, D_EXPERT:]
        act = ((gp * jax.nn.sigmoid(gp)) * up).astype(BF16)
        y_ref[...] = jnp.dot(act, w2b_ref[...], preferred_element_type=F32)

    @pl.when(i >= n_used)
    def _():
        y_ref[...] = jnp.zeros(y_ref.shape, F32)


def _moe_experts(blk_expert, n_used, row_tok, h2d, w1, w2, layer):
    n_blk = blk_expert.shape[0]
    blk = MOE_BLOCK_ROWS
    d = h2d.shape[1]
    grid_spec = pltpu.PrefetchScalarGridSpec(
        num_scalar_prefetch=3,
        grid=(n_blk,),
        in_specs=[pl.BlockSpec(memory_space=pl.ANY),
                  pl.BlockSpec((1, d, 2 * D_EXPERT), lambda i, be, nu, rt: (layer * N_EXPERTS + be[i], 0, 0)),
                  pl.BlockSpec((1, D_EXPERT, d), lambda i, be, nu, rt: (layer * N_EXPERTS + be[i], 0, 0))],
        out_specs=pl.BlockSpec((blk, d), lambda i, be, nu, rt: (i, 0)),
        scratch_shapes=[pltpu.VMEM((GATHER_SLOTS, blk, d), F32), pltpu.VMEM((d, 2 * D_EXPERT), BF16),
                        pltpu.VMEM((D_EXPERT, d), BF16), pltpu.SemaphoreType.DMA((GATHER_SLOTS,))],
    )
    return pl.pallas_call(
        _moe_experts_kernel,
        grid_spec=grid_spec,
        out_shape=jax.ShapeDtypeStruct((n_blk * blk, d), F32),
        compiler_params=_cparams(1),
        name="moe_experts",
    )(blk_expert, n_used, row_tok, h2d, w1, w2)


def _moe_combine_kernel(dest_ref, y_hbm, x_ref, gt_ref, rec_ref, gf_ref, o_ref, buf_ref, sem,
                        *, n_tokens, final_norm):
    tm = x_ref.shape[1]
    n_steps = pl.num_programs(0) * pl.num_programs(1)
    s = pl.program_id(0) * pl.num_programs(1) + pl.program_id(1)
    slot = s % 2

    def start(step, sl):
        for pick in range(2):
            _gather_rows_start(dest_ref, pick * n_tokens + step * tm, y_hbm, buf_ref.at[sl, pick],
                               sem.at[sl, pick], tm)

    @pl.when(s == 0)
    def _():
        start(0, 0)

    @pl.when(s + 1 < n_steps)
    def _():
        start(s + 1, 1 - slot)

    for pick in range(2):
        _gather_rows_wait(y_hbm, buf_ref.at[slot, pick], sem.at[slot, pick], tm)
    rec = rec_ref[0]
    moe = buf_ref[slot, 0] * rec[:, REC_C0:REC_C0 + 1] + buf_ref[slot, 1] * rec[:, REC_C1:REC_C1 + 1]
    out = x_ref[0] + gt_ref[0] * moe
    if final_norm:
        out = (out * lax.rsqrt(jnp.mean(out * out, axis=-1, keepdims=True) + EPS)) * gf_ref[...]
    o_ref[0] = out


def _moe_combine(dest, y, x, gate, rec, gf, final_norm):
    b, t, d = x.shape
    tm = MOE_BLOCK_ROWS
    row = lambda width: pl.BlockSpec((1, tm, width), lambda bi, i, dst: (bi, i, 0))
    grid_spec = pltpu.PrefetchScalarGridSpec(
        num_scalar_prefetch=1,
        grid=(b, t // tm),
        in_specs=[pl.BlockSpec(memory_space=pl.ANY), row(d), _mod_spec(gate, tm), row(ROUTER_PAD),
                  pl.BlockSpec((1, d), lambda bi, i, dst: (0, 0))],
        out_specs=row(d),
        scratch_shapes=[pltpu.VMEM((2, 2, tm, d), F32), pltpu.SemaphoreType.DMA((2, 2))],
    )
    return pl.pallas_call(
        functools.partial(_moe_combine_kernel, n_tokens=b * t, final_norm=final_norm),
        grid_spec=grid_spec,
        out_shape=jax.ShapeDtypeStruct((b, t, d), F32),
        compiler_params=_cparams(2),
        name="moe_combine",
    )(dest, y, x, gate, rec, gf)


def _moe_sorted(x, g, shift, scale, gate, wr, br, w1, w2, layer, gf, final_norm):
    b, t, d = x.shape
    n = b * t
    blk = MOE_BLOCK_ROWS
    n_blk = (2 * n) // blk + N_EXPERTS
    h, rec, rec_t, cnt = _moe_route(x, g, shift, scale, wr, br)
    counts = cnt[0, :N_EXPERTS].astype(I32)
    padded = (counts + blk - 1) // blk * blk
    seg_end = jnp.cumsum(padded)
    seg_start = seg_end - padded
    col = lambda c: rec_t[:, c, :].reshape(n).astype(I32)
    dest = jnp.concatenate([seg_start[col(REC_E0)] + col(REC_RANK0), seg_start[col(REC_E1)] + col(REC_RANK1)])
    tok = jnp.arange(n, dtype=I32)
    row_tok = jnp.zeros((n_blk * blk,), I32).at[dest].set(jnp.concatenate([tok, tok]), unique_indices=True)
    blk_first_row = jnp.arange(n_blk, dtype=I32) * blk
    blk_expert = jnp.minimum(jnp.sum((seg_end[None, :] <= blk_first_row[:, None]).astype(I32), axis=1),
                             N_EXPERTS - 1)
    n_used = (seg_end[-1:] // blk).astype(I32)
    y = _moe_experts(blk_expert, n_used, row_tok, h.reshape(n, d), w1, w2, layer)
    return _moe_combine(dest, y, x, gate, rec, gf, final_norm)


def kernel(x_prompt, x_sample, cache_k, cache_v, cache_kidx, state_h, state_conv, page_table,
           c_prompt, c_sample, ada_w, ada_b, norm_mix_g, norm_ffn_g, norm_final_g,
           attn_w_in, attn_w_out, rec_w_in, rec_conv_w, rec_conv_b, rec_w_a, rec_b_a,
           rec_w_x, rec_b_x, rec_lambda, rec_w_out, moe_w_group, moe_b_group,
           moe_w_expert, moe_b_expert, moe_w1, moe_w2):
    b, t, d = x_prompt.shape
    db = x_sample.shape[0]
    n_pages = page_table.shape[1]
    past = n_pages * PAGE_SIZE
    n_pool = cache_k.shape[1]

    n_cond = b + db
    cond_rows = -(-n_cond // SUBLANES) * SUBLANES
    c_all = jnp.concatenate([c_prompt, c_sample, jnp.zeros((cond_rows - n_cond, d), F32)], axis=0)
    mod = _adaln(c_all, ada_w, ada_b)

    tables_p = _rope_tables(jnp.arange(t, dtype=I32))
    tables_s = _rope_tables(jnp.full((db,), past, I32))
    cache_kT = cache_k.transpose(0, 1, 3, 4, 2).reshape(-1, KV_WIDTH, PAGE_SIZE)
    cache_vT = cache_v.transpose(0, 1, 3, 4, 2).reshape(-1, KV_WIDTH, PAGE_SIZE)
    cache_kiT = cache_kidx.transpose(0, 1, 3, 2).reshape(-1, IDX_DIM, PAGE_SIZE)
    pt_flat = page_table.reshape(-1)
    moe_w1f = moe_w1.reshape(-1, d, 2 * D_EXPERT)
    moe_w2f = moe_w2.reshape(-1, D_EXPERT, d)
    row_vec = lambda v: v.reshape(1, -1)
    head_grp = jnp.arange(N_HEADS)[:, None] // Q_PER_KV == jnp.arange(KV_WIDTH)[None, :] // HEAD_DIM

    xp = x_prompt
    xs = x_sample.reshape(1, db, d)
    outs = {name: [] for name in ("k_p", "v_p", "ki_p", "h_p", "cv_p", "k_s", "v_s", "ki_s", "h_s", "cv_s")}
    for l in range(DEPTH):
        parts = [mod[l, :, i * d:(i + 1) * d] for i in range(6)]
        sh1p, sc1p, g1p, sh2p, sc2p, g2p = [m[:b].reshape(b, 1, d) for m in parts]
        sh1s, sc1s, g1s, sh2s, sc2s, g2s = [m[b:n_cond].reshape(1, db, d) for m in parts]
        g_mix = row_vec(norm_mix_g[l])
        if l % 2 == 0:
            a = l // 2
            w_in = jnp.pad(attn_w_in[a], ((0, 0), (0, ATTN_PROJ_PAD - ATTN_PROJ))).astype(BF16)
            w_out = attn_w_out[a].astype(BF16)
            kT, vT, kiT, qT, qiT, wiT, kb, vTb, kib = _attn_proj_prompt(xp, g_mix, sh1p, sc1p, w_in, tables_p)
            op = _dsa_prompt(qT, qiT, wiT, kb, vTb, kib)
            outs["k_p"].append(kT.reshape(b, N_KV_HEADS, HEAD_DIM, t).transpose(0, 3, 1, 2))
            outs["v_p"].append(vT.reshape(b, N_KV_HEADS, HEAD_DIM, t).transpose(0, 3, 1, 2))
            outs["ki_p"].append(kiT.transpose(0, 2, 1))
            xp = _out_proj(op, w_out, xp, g1p)

            q, k, v, qi, ki, wi = _attn_proj_sample(xs, g_mix, sh1s, sc1s, w_in, tables_s)
            keys = _dsa_sample_scores(pt_flat, qi.reshape(db, IDX_HEADS, IDX_DIM).astype(BF16),
                                      wi.reshape(db, IDX_HEADS, 1), ki.reshape(db, 1, IDX_DIM),
                                      cache_kiT, a, n_pool, n_pages)
            bias = _dsa_sample_select(keys.reshape(db, -1), past + 1).reshape(keys.shape)
            q_bd = jnp.where(head_grp[None], jnp.tile(q.reshape(db, N_HEADS, HEAD_DIM), (1, 1, N_KV_HEADS)),
                             0.0).astype(BF16)
            os_ = _dsa_sample_attend(pt_flat, q_bd, bias, k.reshape(db, 1, KV_WIDTH), v.reshape(db, 1, KV_WIDTH),
                                     cache_kT, cache_vT, a, n_pool, n_pages)
            outs["k_s"].append(k.reshape(db, 1, N_KV_HEADS, HEAD_DIM))
            outs["v_s"].append(v.reshape(db, 1, N_KV_HEADS, HEAD_DIM))
            outs["ki_s"].append(ki.reshape(db, 1, IDX_DIM))
            xs = _out_proj(os_.reshape(1, db, Q_WIDTH).astype(BF16), w_out, xs, g1s)
        else:
            r = l // 2
            w_in = rec_w_in[r].astype(BF16)
            w_out = rec_w_out[r].astype(BF16)
            rec = (rec_conv_w[r], row_vec(rec_conv_b[r]), rec_w_a[r].astype(BF16), row_vec(rec_b_a[r]),
                   rec_w_x[r].astype(BF16), row_vec(rec_b_x[r]), row_vec(rec_lambda[r]))
            gate, xb = _rec_proj(xp, g_mix, sh1p, sc1p, w_in)
            y, h_t, buf = _rglru_prompt(gate, xb, *rec)
            outs["h_p"].append(h_t.reshape(b, D_RNN))
            outs["cv_p"].append(buf)
            xp = _out_proj(y, w_out, xp, g1p)

            gate, xb = _rec_proj(xs, g_mix, sh1s, sc1s, w_in)
            y, h_t, buf = _rglru_sample(gate[0], xb[0], state_conv[r].swapaxes(0, 1), state_h[r], *rec)
            outs["h_s"].append(h_t)
            outs["cv_s"].append(buf.swapaxes(0, 1))
            xs = _out_proj(y.reshape(1, db, D_RNN), w_out, xs, g1s)

        wr = jnp.concatenate([moe_w_group[l], moe_w_expert[l],
                              jnp.zeros((d, ROUTER_PAD - N_GROUPS - N_EXPERTS), F32)], axis=1).astype(BF16)
        br = jnp.concatenate([moe_b_group[l], moe_b_expert[l],
                              jnp.zeros((ROUTER_PAD - N_GROUPS - N_EXPERTS,), F32)]).reshape(1, ROUTER_PAD)
        last = l == DEPTH - 1
        moe_args = (wr, br, moe_w1f, moe_w2f, l, row_vec(norm_final_g), last)
        xp = _moe_sorted(xp, row_vec(norm_ffn_g[l]), sh2p, sc2p, g2p, *moe_args)
        xs = _moe(xs, row_vec(norm_ffn_g[l]), sh2s, sc2s, g2s, *moe_args)

    st = lambda name: jnp.stack(outs[name])
    return (xp, xs.reshape(db, 1, d), st("k_p"), st("v_p"), st("ki_p"), st("h_p"), st("cv_p"),
            st("k_s"), st("v_s"), st("ki_s"), st("h_s"), st("cv_s"))
```

```python
import functools
import math

import jax
import jax.numpy as jnp
from jax import lax
from jax.experimental import pallas as pl
from jax.experimental.pallas import tpu as pltpu

F32 = jnp.float32
BF16 = jnp.bfloat16
I32 = jnp.int32

D_MODEL = 1024
DEPTH = 4
PAGE_SIZE = 128
N_HEADS = 16
HEAD_DIM = 64
N_KV_HEADS = 4
Q_PER_KV = N_HEADS // N_KV_HEADS
IDX_HEADS = 8
IDX_DIM = 64
IDX_SCALE = (IDX_HEADS * IDX_DIM) ** -0.5
TOPK_MAX = 256
ROPE_THETA = 500000.0
ROPE_FRACTION = 4
ROT_HALF = HEAD_DIM // ROPE_FRACTION // 2
Q_WIDTH = N_HEADS * HEAD_DIM
KV_WIDTH = N_KV_HEADS * HEAD_DIM
QI_WIDTH = IDX_HEADS * IDX_DIM
ATTN_PROJ = Q_WIDTH + 2 * KV_WIDTH + QI_WIDTH + IDX_DIM + IDX_HEADS
D_RNN = D_MODEL
RG_BLOCKS = 8
RG_BLOCK_W = D_RNN // RG_BLOCKS
CONV_W = 4
RG_C = 8.0
N_GROUPS = 4
EXPERTS_PER_GROUP = 8
N_EXPERTS = N_GROUPS * EXPERTS_PER_GROUP
D_EXPERT = 256
EPS = 1e-6

LANES = 128
SUBLANES = 8
VMEM_LIMIT_BYTES = 56 * 1024 * 1024

ATTN_PROJ_PAD = -(-ATTN_PROJ // LANES) * LANES
ROUTER_PAD = LANES
Q_TILE = LANES
KEY_CHUNK = 512
ROW_TILE = 512
MOE_ROW_TILE = 1024
MOE_BLOCK_ROWS = 256
GATHER_SLOTS = 3
SCAN_CHUNK = 512
PAGE_GROUP = 8
BF16_SUBLANES = 16
HEAD_AUG = HEAD_DIM + BF16_SUBLANES
BOUND_SLACK = 1.02
DENOM_FLOOR = 2.0 ** -60
DENOM_CEIL = 2.0 ** 60
INT_MIN = -(2 ** 31)
INT_MAX = 2 ** 31 - 1
NEG = -1e30


def _cparams(n_axes):
    return pltpu.CompilerParams(dimension_semantics=("arbitrary",) * n_axes,
                                vmem_limit_bytes=VMEM_LIMIT_BYTES)


def _modulated_norm(x, g, shift, scale):
    xn = x * lax.rsqrt(jnp.mean(x * x, axis=-1, keepdims=True) + EPS)
    return (xn * g) * (1.0 + scale) + shift


def _rope_group(xs, cos, sa, sb):
    return xs * cos + pltpu.roll(xs, LANES - 8, 1) * sa + pltpu.roll(xs, 8, 1) * sb


def _float_order_key(s):
    bits = lax.bitcast_convert_type(s, I32)
    return bits ^ ((bits >> 31) & INT_MAX)


def _mod_spec(arr, tm):
    if arr.shape[1] == 1:
        return pl.BlockSpec((1, 1, arr.shape[2]), lambda b, i, *_: (b, 0, 0))
    return pl.BlockSpec((1, tm, arr.shape[2]), lambda b, i, *_: (b, i, 0))


def _adaln_kernel(c_ref, w_ref, b_ref, o_ref):
    c = c_ref[...]
    s = (c * jax.nn.sigmoid(c)).astype(BF16)
    o_ref[0] = jnp.dot(s, w_ref[0].astype(BF16), preferred_element_type=F32) + b_ref[0]


def _adaln(c_all, ada_w, ada_b):
    depth, d, n = ada_w.shape
    rows = c_all.shape[0]
    tn = 1536
    return pl.pallas_call(
        _adaln_kernel,
        grid=(depth, n // tn),
        in_specs=[pl.BlockSpec((rows, d), lambda l, j: (0, 0)),
                  pl.BlockSpec((1, d, tn), lambda l, j: (l, 0, j)),
                  pl.BlockSpec((1, 1, tn), lambda l, j: (l, 0, j))],
        out_specs=pl.BlockSpec((1, rows, tn), lambda l, j: (l, 0, j)),
        out_shape=jax.ShapeDtypeStruct((depth, rows, n), F32),
        compiler_params=_cparams(2),
        name="adaln",
    )(c_all, ada_w, ada_b.reshape(depth, 1, n))


def _rope_tables(pos):
    rot = HEAD_DIM // ROPE_FRACTION
    half = rot // 2
    inv = jnp.exp(-math.log(ROPE_THETA) * jnp.arange(half, dtype=F32) * (2.0 / rot))
    ang = pos.astype(F32)[:, None] * inv[None, :]
    cos, sin = jnp.cos(ang), jnp.sin(ang)
    r = pos.shape[0]
    z = lambda w: jnp.zeros((r, w), F32)
    cos64 = jnp.concatenate([cos, cos, jnp.ones((r, HEAD_DIM - rot), F32)], axis=1)
    sa64 = jnp.concatenate([-sin, z(HEAD_DIM - half)], axis=1)
    sb64 = jnp.concatenate([z(half), sin, z(HEAD_DIM - rot)], axis=1)
    two = lambda t: jnp.concatenate([t, t], axis=1)
    return two(cos64), two(sa64), two(sb64), cos.T, sin.T


def _attn_proj_prompt_kernel(x_ref, g_ref, sh_ref, sc_ref, w_ref, cos_ref, sa_ref, sb_ref, cosT_ref, sinT_ref,
                             kT_ref, vT_ref, kiT_ref, qT_ref, qiT_ref, wiT_ref, kb_ref, vTb_ref, kib_ref,
                             kmax_ref):
    h = _modulated_norm(x_ref[0], g_ref[...], sh_ref[0], sc_ref[0]).astype(BF16)
    acc = jnp.dot(h, w_ref[...], preferred_element_type=F32)
    cos, sa, sb = cos_ref[...], sa_ref[...], sb_ref[...]
    n_qb = acc.shape[0] // Q_TILE
    grp = lambda off, j: acc[:, off + j * LANES: off + (j + 1) * LANES]

    tm = acc.shape[0]
    one_hot_row = lambda shape, axis: jnp.where(lax.broadcasted_iota(I32, shape, axis) == 0, 1.0, 0.0).astype(BF16)

    @pl.when(pl.program_id(1) == 0)
    def _():
        kmax_ref[...] = jnp.zeros(kmax_ref.shape, F32)

    for j in range(KV_WIDTH // LANES):
        kj = _rope_group(grp(Q_WIDTH, j), cos, sa, sb)
        kjT = kj.T
        kT_ref[0, j * LANES:(j + 1) * LANES, :] = kjT
        sq = kjT * kjT
        for hh in range(2):
            n = 2 * j + hh
            norm2 = jnp.sum(sq[hh * HEAD_DIM:(hh + 1) * HEAD_DIM, :], axis=0, keepdims=True)
            kmax_ref[n] = jnp.maximum(kmax_ref[n], jnp.max(norm2, axis=1, keepdims=True))
            kb_ref[0, n, :, 0:HEAD_DIM] = kj[:, hh * HEAD_DIM:(hh + 1) * HEAD_DIM].astype(BF16)
            kb_ref[0, n, :, HEAD_DIM:HEAD_AUG] = one_hot_row((tm, BF16_SUBLANES), 1)
    cosT, sinT = cosT_ref[...], sinT_ref[...]

    def rope_rows(xT):
        parts = []
        for hh in range(2):
            lo, mid, hi = hh * HEAD_DIM, hh * HEAD_DIM + ROT_HALF, hh * HEAD_DIM + 2 * ROT_HALF
            x1, x2 = xT[lo:mid], xT[mid:hi]
            parts += [x1 * cosT - x2 * sinT, x1 * sinT + x2 * cosT, xT[hi:(hh + 1) * HEAD_DIM]]
        return jnp.concatenate(parts, axis=0)

    for j in range(Q_WIDTH // LANES):
        qjT = rope_rows(grp(0, j).T) * (HEAD_DIM ** -0.5)
        sq = qjT * qjT
        qjTb = qjT.astype(BF16)
        for hh in range(2):
            n, g = divmod(2 * j + hh, Q_PER_KV)
            norm2 = jnp.sum(sq[hh * HEAD_DIM:(hh + 1) * HEAD_DIM, :], axis=0, keepdims=True)
            shift = -BOUND_SLACK * jnp.sqrt(norm2 * kmax_ref[n])
            shift = jnp.concatenate([shift, jnp.zeros((BF16_SUBLANES - 1, tm), F32)], axis=0).astype(BF16)
            for jb in range(n_qb):
                qT_ref[0, jb, n, 0:HEAD_DIM, g * Q_TILE:(g + 1) * Q_TILE] = (
                    qjTb[hh * HEAD_DIM:(hh + 1) * HEAD_DIM, jb * Q_TILE:(jb + 1) * Q_TILE])
                qT_ref[0, jb, n, HEAD_DIM:HEAD_AUG, g * Q_TILE:(g + 1) * Q_TILE] = (
                    shift[:, jb * Q_TILE:(jb + 1) * Q_TILE])
    for j in range(KV_WIDTH // LANES):
        vjT = grp(Q_WIDTH + KV_WIDTH, j).T
        vT_ref[0, j * LANES:(j + 1) * LANES, :] = vjT
        for hh in range(2):
            n = 2 * j + hh
            vTb_ref[0, 0, n * HEAD_AUG:n * HEAD_AUG + HEAD_DIM, :] = (
                vjT[hh * HEAD_DIM:(hh + 1) * HEAD_DIM, :].astype(BF16))
            vTb_ref[0, 0, n * HEAD_AUG + HEAD_DIM:(n + 1) * HEAD_AUG, :] = one_hot_row((BF16_SUBLANES, tm), 0)
    off_qi = Q_WIDTH + 2 * KV_WIDTH
    for j in range(QI_WIDTH // LANES):
        qijT = rope_rows(grp(off_qi, j).T).astype(BF16)
        for hh in range(2):
            head = 2 * j + hh
            for jb in range(n_qb):
                qiT_ref[0, jb, :, head * Q_TILE:(head + 1) * Q_TILE] = (
                    qijT[hh * IDX_DIM:(hh + 1) * IDX_DIM, jb * Q_TILE:(jb + 1) * Q_TILE])
    last = grp(off_qi + QI_WIDTH, 0)
    kir = _rope_group(last, cos, sa, sb)
    kiT_ref[0] = kir.T[:IDX_DIM, :]
    kib_ref[0] = kir[:, :IDX_DIM].astype(BF16)
    lastT = last.T
    for jb in range(n_qb):
        wiT_ref[0, jb] = lastT[IDX_DIM:IDX_DIM + IDX_HEADS, jb * Q_TILE:(jb + 1) * Q_TILE]


def _attn_proj_prompt(x, g, shift, scale, w_pad, tables):
    b, t, d = x.shape
    tm = ROW_TILE
    nq = t // Q_TILE
    grid = (b, t // tm)
    row = lambda width: pl.BlockSpec((1, tm, width), lambda bi, i: (bi, i, 0))
    col = lambda width: pl.BlockSpec((1, width, tm), lambda bi, i: (bi, 0, i))
    tab = pl.BlockSpec((tm, LANES), lambda bi, i: (i, 0))
    tab_t = pl.BlockSpec((ROT_HALF, tm), lambda bi, i: (0, i))
    out_shape = (
        jax.ShapeDtypeStruct((b, KV_WIDTH, t), F32),
        jax.ShapeDtypeStruct((b, KV_WIDTH, t), F32),
        jax.ShapeDtypeStruct((b, IDX_DIM, t), F32),
        jax.ShapeDtypeStruct((b, nq, N_KV_HEADS, HEAD_AUG, Q_PER_KV * Q_TILE), BF16),
        jax.ShapeDtypeStruct((b, nq, IDX_DIM, IDX_HEADS * Q_TILE), BF16),
        jax.ShapeDtypeStruct((b, nq, IDX_HEADS, Q_TILE), F32),
        jax.ShapeDtypeStruct((b, N_KV_HEADS, t, HEAD_AUG), BF16),
        jax.ShapeDtypeStruct((b, t // tm, N_KV_HEADS * HEAD_AUG, tm), BF16),
        jax.ShapeDtypeStruct((b, t, IDX_DIM), BF16),
    )
    nqb = tm // Q_TILE
    out_specs = (
        col(KV_WIDTH), col(KV_WIDTH), col(IDX_DIM),
        pl.BlockSpec((1, nqb, N_KV_HEADS, HEAD_AUG, Q_PER_KV * Q_TILE), lambda bi, i: (bi, i, 0, 0, 0)),
        pl.BlockSpec((1, nqb, IDX_DIM, IDX_HEADS * Q_TILE), lambda bi, i: (bi, i, 0, 0)),
        pl.BlockSpec((1, nqb, IDX_HEADS, Q_TILE), lambda bi, i: (bi, i, 0, 0)),
        pl.BlockSpec((1, N_KV_HEADS, tm, HEAD_AUG), lambda bi, i: (bi, 0, i, 0)),
        pl.BlockSpec((1, 1, N_KV_HEADS * HEAD_AUG, tm), lambda bi, i: (bi, i, 0, 0)),
        row(IDX_DIM),
    )
    return pl.pallas_call(
        _attn_proj_prompt_kernel,
        grid=grid,
        in_specs=[row(d), pl.BlockSpec((1, d), lambda bi, i: (0, 0)),
                  _mod_spec(shift, tm), _mod_spec(scale, tm),
                  pl.BlockSpec(w_pad.shape, lambda bi, i: (0, 0)), tab, tab, tab, tab_t, tab_t],
        out_specs=out_specs,
        out_shape=out_shape,
        scratch_shapes=[pltpu.VMEM((N_KV_HEADS, 1, 1), F32)],
        compiler_params=_cparams(2),
        name="attn_proj_prompt",
    )(x, g, shift, scale, w_pad, *tables)


def _attn_proj_sample_kernel(x_ref, g_ref, sh_ref, sc_ref, w_ref, cos_ref, sa_ref, sb_ref,
                             q_ref, k_ref, v_ref, qi_ref, ki_ref, wi_ref):
    h = _modulated_norm(x_ref[0], g_ref[...], sh_ref[0], sc_ref[0]).astype(BF16)
    acc = jnp.dot(h, w_ref[...], preferred_element_type=F32)
    cos, sa, sb = cos_ref[...], sa_ref[...], sb_ref[...]
    grp = lambda off, j: acc[:, off + j * LANES: off + (j + 1) * LANES]
    for j in range(Q_WIDTH // LANES):
        q_ref[:, j * LANES:(j + 1) * LANES] = _rope_group(grp(0, j), cos, sa, sb) * (HEAD_DIM ** -0.5)
    for j in range(KV_WIDTH // LANES):
        k_ref[:, j * LANES:(j + 1) * LANES] = _rope_group(grp(Q_WIDTH, j), cos, sa, sb)
        v_ref[:, j * LANES:(j + 1) * LANES] = grp(Q_WIDTH + KV_WIDTH, j)
    off_qi = Q_WIDTH + 2 * KV_WIDTH
    for j in range(QI_WIDTH // LANES):
        qi_ref[:, j * LANES:(j + 1) * LANES] = _rope_group(grp(off_qi, j), cos, sa, sb)
    last = grp(off_qi + QI_WIDTH, 0)
    ki_ref[...] = _rope_group(last, cos, sa, sb)[:, :IDX_DIM]
    wi_ref[...] = last[:, IDX_DIM:IDX_DIM + IDX_HEADS]


def _attn_proj_sample(x, g, shift, scale, w_pad, tables):
    _, rows, d = x.shape
    full = lambda shape: pl.BlockSpec(shape, lambda i: (0,) * len(shape))
    widths = (Q_WIDTH, KV_WIDTH, KV_WIDTH, QI_WIDTH, IDX_DIM, IDX_HEADS)
    return pl.pallas_call(
        _attn_proj_sample_kernel,
        grid=(1,),
        in_specs=[full((1, rows, d)), full((1, d)), full((1, rows, d)), full((1, rows, d)),
                  full(w_pad.shape), full((rows, LANES)), full((rows, LANES)), full((rows, LANES))],
        out_specs=tuple(full((rows, w)) for w in widths),
        out_shape=tuple(jax.ShapeDtypeStruct((rows, w), F32) for w in widths),
        compiler_params=_cparams(1),
        name="attn_proj_sample",
    )(x, g, shift, scale, w_pad, *tables[:3])


def _dsa_prompt_kernel(qT_ref, qiT_ref, wiT_ref, kb_ref, vT_ref, kib_ref, o_ref,
                       keys_ref, acc_ref, acc2_ref, m_ref, l_ref, j_ref, *, topk, idx_bits):
    kc = KEY_CHUNK
    qb = pl.program_id(1)
    n_chunks = (qb * Q_TILE + Q_TILE + kc - 1) // kc
    rows = lax.broadcasted_iota(I32, (kc, Q_TILE), 0)
    q_pos = qb * Q_TILE + lax.broadcasted_iota(I32, (kc, Q_TILE), 1)

    w_heads = wiT_ref[0, 0] * IDX_SCALE
    qiT = qiT_ref[0, 0]

    def score_chunk(c, carry):
        off = pl.multiple_of(c * kc, kc)
        dots = jnp.dot(kib_ref[0, pl.ds(off, kc), :], qiT, preferred_element_type=F32)
        s = jnp.zeros((kc, Q_TILE), F32)
        for h in range(IDX_HEADS):
            s = s + jnp.maximum(dots[:, h * Q_TILE:(h + 1) * Q_TILE], 0.0) * w_heads[h:h + 1, :]
        keys_ref[pl.ds(off, kc), :] = jnp.where(rows + off <= q_pos, _float_order_key(s), INT_MIN)
        return carry

    lax.fori_loop(0, n_chunks, score_chunk, 0)

    def count(pred):
        def body(c, cnt):
            off = pl.multiple_of(c * kc, kc)
            hit = jnp.where(pred(keys_ref[pl.ds(off, kc), :], rows + off), 1, 0)
            return cnt + hit.reshape(kc // SUBLANES, SUBLANES, Q_TILE).sum(axis=0)
        cnt = lax.fori_loop(0, n_chunks, body, jnp.zeros((SUBLANES, Q_TILE), I32))
        return cnt.sum(axis=0, keepdims=True)

    def thr_bit(i, state):
        thr, n_ge = state
        cand = thr + lax.shift_left(jnp.int32(1), 31 - i)
        n_cand = count(lambda k, s: k >= cand)
        ok = n_cand >= topk
        return jnp.where(ok, cand, thr), jnp.where(ok, n_cand, n_ge)

    n_all = jnp.full((1, Q_TILE), 1, I32) * (n_chunks * kc)
    thr, n_ge = lax.fori_loop(0, 32, thr_bit, (jnp.full((1, Q_TILE), INT_MIN, I32), n_all))

    j_ref[...] = jnp.full((1, Q_TILE), INT_MAX, I32)
    tie_split = jnp.max(jnp.where((n_ge > topk) & (thr > INT_MIN), 1, 0))

    @pl.when(tie_split > 0)
    def _():
        need = topk - count(lambda k, s: k > thr)

        def idx_bit(i, j0):
            cand = j0 + lax.shift_left(jnp.int32(1), idx_bits - 1 - i)
            taken_before = count(lambda k, s: (k == thr) & (s < cand))
            return jnp.where(taken_before < need, cand, j0)
        j_ref[...] = lax.fori_loop(0, idx_bits, idx_bit, jnp.zeros((1, Q_TILE), I32))

    j_sel = j_ref[...]

    def selection_bias(c):
        off = pl.multiple_of(c * kc, kc)
        k = keys_ref[pl.ds(off, kc), :]
        s_idx = rows + off
        sel = ((k > thr) | ((k == thr) & (s_idx <= j_sel))) & (s_idx <= q_pos)
        bias = jnp.where(sel, 0.0, NEG)
        return off, jnp.concatenate([bias] * Q_PER_KV, axis=1)

    acc_ref[...] = jnp.zeros(acc_ref.shape, F32)

    def attend_chunk(c, carry):
        off, bias = selection_bias(c)
        for n in range(N_KV_HEADS):
            s = jnp.dot(kb_ref[0, n, pl.ds(off, kc), :], qT_ref[0, 0, n], preferred_element_type=F32) + bias
            acc_ref[n] += jnp.dot(vT_ref[0, c, n * HEAD_AUG:(n + 1) * HEAD_AUG, :], jnp.exp(s).astype(BF16),
                                  preferred_element_type=F32)
        return carry

    lax.fori_loop(0, n_chunks, attend_chunk, 0)
    denom = jnp.concatenate([acc_ref[n][HEAD_DIM:HEAD_DIM + 1, :] for n in range(N_KV_HEADS)], axis=0)
    healthy = (jnp.min(denom) >= DENOM_FLOOR) & (jnp.max(denom) <= DENOM_CEIL)

    @pl.when(jnp.logical_not(healthy))
    def _():
        m_ref[...] = jnp.full(m_ref.shape, NEG, F32)
        l_ref[...] = jnp.zeros(l_ref.shape, F32)
        acc2_ref[...] = jnp.zeros(acc2_ref.shape, F32)

        def attend_chunk_online(c, carry):
            off, bias = selection_bias(c)
            for n in range(N_KV_HEADS):
                s = jnp.dot(kb_ref[0, n, pl.ds(off, kc), 0:HEAD_DIM], qT_ref[0, 0, n, 0:HEAD_DIM, :],
                            preferred_element_type=F32) + bias
                m_prev = m_ref[n]
                m_new = jnp.maximum(m_prev, s.max(axis=0, keepdims=True))
                alpha = jnp.exp(m_prev - m_new)
                p = jnp.exp(s - m_new)
                l_ref[n] = alpha * l_ref[n] + p.sum(axis=0, keepdims=True)
                pv = jnp.dot(vT_ref[0, c, n * HEAD_AUG:n * HEAD_AUG + HEAD_DIM, :], p.astype(BF16),
                             preferred_element_type=F32)
                acc2_ref[n] = alpha * acc2_ref[n] + pv
                m_ref[n] = m_new
            return carry

        lax.fori_loop(0, n_chunks, attend_chunk_online, 0)
        for n in range(N_KV_HEADS):
            acc_ref[n, 0:HEAD_DIM, :] = acc2_ref[n]
            acc_ref[n, HEAD_DIM:HEAD_DIM + 1, :] = l_ref[n]

    for n in range(N_KV_HEADS):
        on = acc_ref[n, 0:HEAD_DIM, :] / acc_ref[n, HEAD_DIM:HEAD_DIM + 1, :]
        for gp in range(Q_PER_KV // 2):
            pair = jnp.concatenate([on[:, (2 * gp) * Q_TILE:(2 * gp + 1) * Q_TILE],
                                    on[:, (2 * gp + 1) * Q_TILE:(2 * gp + 2) * Q_TILE]], axis=0)
            col = (n * Q_PER_KV // 2 + gp) * LANES
            o_ref[0, :, col:col + LANES] = pair.T.astype(BF16)


def _dsa_prompt(qT, qiT, wiT, kb, vT, kib):
    b, nq = qT.shape[:2]
    t = kb.shape[2]
    topk = min(TOPK_MAX, t // 4)
    idx_bits = max(1, (t - 1).bit_length())
    per_q = lambda shape: pl.BlockSpec((1, 1) + shape, lambda bi, qi: (bi, qi) + (0,) * len(shape))
    per_b = lambda shape: pl.BlockSpec((1,) + shape, lambda bi, qi: (bi,) + (0,) * len(shape))
    return pl.pallas_call(
        functools.partial(_dsa_prompt_kernel, topk=topk, idx_bits=idx_bits),
        grid=(b, nq),
        in_specs=[per_q(qT.shape[2:]), per_q(qiT.shape[2:]), per_q(wiT.shape[2:]),
                  per_b(kb.shape[1:]), per_b(vT.shape[1:]), per_b(kib.shape[1:])],
        out_specs=pl.BlockSpec((1, Q_TILE, Q_WIDTH), lambda bi, qi: (bi, qi, 0)),
        out_shape=jax.ShapeDtypeStruct((b, t, Q_WIDTH), BF16),
        scratch_shapes=[pltpu.VMEM((t, Q_TILE), I32),
                        pltpu.VMEM((N_KV_HEADS, HEAD_AUG, Q_PER_KV * Q_TILE), F32),
                        pltpu.VMEM((N_KV_HEADS, HEAD_DIM, Q_PER_KV * Q_TILE), F32),
                        pltpu.VMEM((N_KV_HEADS, 1, Q_PER_KV * Q_TILE), F32),
                        pltpu.VMEM((N_KV_HEADS, 1, Q_PER_KV * Q_TILE), F32),
                        pltpu.VMEM((1, Q_TILE), I32)],
        compiler_params=_cparams(2),
        name="dsa_prompt",
    )(qT, qiT, wiT, kb, vT, kib)


def _fetch_pages_start(pt_ref, seq, layer_base, caches, bufs, sems, n_pages):
    def body(p, carry):
        page = layer_base + pt_ref[seq * n_pages + p]
        for cache, buf, sem in zip(caches, bufs, sems):
            pltpu.make_async_copy(cache.at[page], buf.at[p], sem).start()
        return carry
    lax.fori_loop(0, n_pages, body, 0)


def _fetch_pages_wait(caches, bufs, sems, n_pages):
    for cache, buf, sem in zip(caches, bufs, sems):
        pltpu.make_async_copy(cache.at[pl.ds(0, n_pages)], buf, sem).wait()


def _fetch_pages_pipelined(pt_ref, layer_base, caches, bufs, sems, n_pages):
    step, n_steps = pl.program_id(0), pl.num_programs(0)
    slot = step % 2
    at = lambda sl: ([b.at[sl] for b in bufs], [s.at[sl] for s in sems])

    @pl.when(step == 0)
    def _():
        _fetch_pages_start(pt_ref, 0, layer_base, caches, *at(0), n_pages)

    @pl.when(step + 1 < n_steps)
    def _():
        _fetch_pages_start(pt_ref, step + 1, layer_base, caches, *at(1 - slot), n_pages)

    _fetch_pages_wait(caches, *at(slot), n_pages)
    return slot


def _pages_bf16(buf, slot, first, count):
    return jnp.concatenate([buf[slot, first + j].astype(BF16) for j in range(count)], axis=1)


def _dsa_sample_scores_kernel(pt_ref, qi_ref, w_ref, kin_ref, kiT_hbm, keys_ref, buf_ref, sem,
                              *, n_pages, layer_base):
    slot = _fetch_pages_pipelined(pt_ref, layer_base, [kiT_hbm], [buf_ref], [sem], n_pages)
    qi = qi_ref[0]
    w = w_ref[0] * IDX_SCALE
    score = lambda dots: jnp.sum(jnp.maximum(dots, 0.0) * w, axis=0, keepdims=True)
    for g in range(n_pages // PAGE_GROUP):
        kiT = _pages_bf16(buf_ref, slot, g * PAGE_GROUP, PAGE_GROUP)
        keys = _float_order_key(score(jnp.dot(qi, kiT, preferred_element_type=F32)))
        keys_ref[0, :, g * PAGE_GROUP * PAGE_SIZE:(g + 1) * PAGE_GROUP * PAGE_SIZE] = keys
    kn = kin_ref[0].astype(BF16).astype(F32)
    sn = score(jnp.sum(qi.astype(F32) * kn, axis=1, keepdims=True))
    lane = lax.broadcasted_iota(I32, (1, LANES), 1)
    keys_ref[0, :, n_pages * PAGE_SIZE:] = jnp.where(lane == 0, _float_order_key(sn), INT_MIN)


def _dsa_sample_scores(pt_flat, qi3, w3, ki_new, cache_kiT, layer, n_pool, n_pages):
    db = qi3.shape[0]
    n_lanes = n_pages * PAGE_SIZE + LANES
    per_b = lambda shape: pl.BlockSpec((1,) + shape, lambda b, pt: (b,) + (0,) * len(shape))
    grid_spec = pltpu.PrefetchScalarGridSpec(
        num_scalar_prefetch=1,
        grid=(db,),
        in_specs=[per_b((IDX_HEADS, IDX_DIM)), per_b((IDX_HEADS, 1)), per_b((1, IDX_DIM)),
                  pl.BlockSpec(memory_space=pl.ANY)],
        out_specs=per_b((1, n_lanes)),
        scratch_shapes=[pltpu.VMEM((2, n_pages, IDX_DIM, PAGE_SIZE), F32), pltpu.SemaphoreType.DMA((2,))],
    )
    return pl.pallas_call(
        functools.partial(_dsa_sample_scores_kernel, n_pages=n_pages, layer_base=layer * n_pool),
        grid_spec=grid_spec,
        out_shape=jax.ShapeDtypeStruct((db, 1, n_lanes), I32),
        compiler_params=_cparams(1),
        name="dsa_sample_scores",
    )(pt_flat, qi3, w3, ki_new, cache_kiT)


def _dsa_sample_select_kernel(keys_ref, bias_ref, *, n_keys, topk, idx_bits):
    keys = keys_ref[...]
    idx = lax.broadcasted_iota(I32, keys.shape, 1)
    count = lambda hit: jnp.sum(jnp.where(hit, 1, 0), axis=1, keepdims=True)
    keys = jnp.where(idx < n_keys, keys, INT_MIN)

    def thr_bit(i, thr):
        cand = thr + lax.shift_left(jnp.int32(1), 31 - i)
        return jnp.where(count(keys >= cand) >= topk, cand, thr)

    thr = lax.fori_loop(0, 32, thr_bit, jnp.full((keys.shape[0], 1), INT_MIN, I32))
    need = topk - count(keys > thr)

    def idx_bit(i, j0):
        cand = j0 + lax.shift_left(jnp.int32(1), idx_bits - 1 - i)
        return jnp.where(count((keys == thr) & (idx < cand)) < need, cand, j0)

    j_sel = lax.fori_loop(0, idx_bits, idx_bit, jnp.zeros((keys.shape[0], 1), I32))
    sel = ((keys > thr) | ((keys == thr) & (idx <= j_sel))) & (idx < n_keys)
    bias_ref[...] = jnp.where(sel, 0.0, NEG)


def _dsa_sample_select(keys, n_keys):
    topk = min(TOPK_MAX, n_keys // 4)
    idx_bits = max(1, (n_keys - 1).bit_length())
    full = pl.BlockSpec(keys.shape, lambda i: (0, 0))
    return pl.pallas_call(
        functools.partial(_dsa_sample_select_kernel, n_keys=n_keys, topk=topk, idx_bits=idx_bits),
        grid=(1,),
        in_specs=[full],
        out_specs=full,
        out_shape=jax.ShapeDtypeStruct(keys.shape, F32),
        compiler_params=_cparams(1),
        name="dsa_sample_select",
    )(keys)


def _dsa_sample_attend_kernel(pt_ref, q_ref, bias_ref, kn_ref, vn_ref, kT_hbm, vT_hbm, o_ref,
                              kbuf_ref, vbuf_ref, ksem, vsem, *, n_pages, layer_base):
    slot = _fetch_pages_pipelined(pt_ref, layer_base, [kT_hbm, vT_hbm], [kbuf_ref, vbuf_ref],
                                  [ksem, vsem], n_pages)
    q = q_ref[0]
    n_past = n_pages * PAGE_SIZE
    group = PAGE_GROUP * PAGE_SIZE
    scores = [jnp.dot(q, _pages_bf16(kbuf_ref, slot, g * PAGE_GROUP, PAGE_GROUP), preferred_element_type=F32)
              + bias_ref[0, :, g * group:(g + 1) * group] for g in range(n_pages // PAGE_GROUP)]
    kn = kn_ref[0].astype(BF16).astype(F32)
    vn = vn_ref[0].astype(BF16).astype(F32)
    sn = jnp.sum(q.astype(F32) * kn, axis=1, keepdims=True) + bias_ref[0, :, n_past:n_past + 1]
    m = sn
    for s in scores:
        m = jnp.maximum(m, s.max(axis=1, keepdims=True))
    pn = jnp.exp(sn - m)
    denom = pn
    acc = pn.astype(BF16).astype(F32) * vn
    for g, s in enumerate(scores):
        p = jnp.exp(s - m)
        denom = denom + p.sum(axis=1, keepdims=True)
        acc = acc + lax.dot_general(p.astype(BF16), _pages_bf16(vbuf_ref, slot, g * PAGE_GROUP, PAGE_GROUP),
                                    (((1,), (1,)), ((), ())), preferred_element_type=F32)
    out = acc / denom
    head_grp = lax.broadcasted_iota(I32, out.shape, 0) // Q_PER_KV
    lane_grp = lax.broadcasted_iota(I32, out.shape, 1) // HEAD_DIM
    out = jnp.where(head_grp == lane_grp, out, 0.0)
    o = out[:, 0:HEAD_DIM]
    for n in range(1, N_KV_HEADS):
        o = o + out[:, n * HEAD_DIM:(n + 1) * HEAD_DIM]
    o_ref[0] = o


def _dsa_sample_attend(pt_flat, q_bd, bias, k_new, v_new, cache_kT, cache_vT, layer, n_pool, n_pages):
    db = q_bd.shape[0]
    per_b = lambda shape: pl.BlockSpec((1,) + shape, lambda b, pt: (b,) + (0,) * len(shape))
    hbm = pl.BlockSpec(memory_space=pl.ANY)
    pages = pltpu.VMEM((2, n_pages, KV_WIDTH, PAGE_SIZE), F32)
    grid_spec = pltpu.PrefetchScalarGridSpec(
        num_scalar_prefetch=1,
        grid=(db,),
        in_specs=[per_b(q_bd.shape[1:]), per_b(bias.shape[1:]), per_b((1, KV_WIDTH)), per_b((1, KV_WIDTH)),
                  hbm, hbm],
        out_specs=per_b((N_HEADS, HEAD_DIM)),
        scratch_shapes=[pages, pages, pltpu.SemaphoreType.DMA((2,)), pltpu.SemaphoreType.DMA((2,))],
    )
    return pl.pallas_call(
        functools.partial(_dsa_sample_attend_kernel, n_pages=n_pages, layer_base=layer * n_pool),
        grid_spec=grid_spec,
        out_shape=jax.ShapeDtypeStruct((db, N_HEADS, HEAD_DIM), F32),
        compiler_params=_cparams(1),
        name="dsa_sample_attend",
    )(pt_flat, q_bd, bias, k_new, v_new, cache_kT, cache_vT)


def _out_proj_kernel(a_ref, w_ref, x_ref, gt_ref, o_ref):
    y = jnp.dot(a_ref[0], w_ref[...], preferred_element_type=F32)
    o_ref[0] = x_ref[0] + gt_ref[0] * y


def _out_proj(a, w, x, gate):
    b, t, d = x.shape
    tm = min(ROW_TILE, t)
    row = lambda width: pl.BlockSpec((1, tm, width), lambda bi, i: (bi, i, 0))
    return pl.pallas_call(
        _out_proj_kernel,
        grid=(b, t // tm),
        in_specs=[row(a.shape[2]), pl.BlockSpec(w.shape, lambda bi, i: (0, 0)), row(d), _mod_spec(gate, tm)],
        out_specs=row(d),
        out_shape=jax.ShapeDtypeStruct((b, t, d), F32),
        compiler_params=_cparams(2),
        name="out_proj",
    )(a, w, x, gate)


def _gelu_tanh(x):
    return x * (0.5 * (1.0 + jnp.tanh(math.sqrt(2.0 / math.pi) * (x + 0.044715 * (x * x * x)))))


def _rec_proj_kernel(x_ref, g_ref, sh_ref, sc_ref, w_ref, gate_ref, xb_ref):
    h = _modulated_norm(x_ref[0], g_ref[...], sh_ref[0], sc_ref[0]).astype(BF16)
    acc = jnp.dot(h, w_ref[...], preferred_element_type=F32)
    gate_ref[0] = _gelu_tanh(acc[:, :D_RNN])
    xb_ref[0] = acc[:, D_RNN:]


def _rec_proj(x, g, shift, scale, w):
    b, t, d = x.shape
    tm = min(ROW_TILE, t)
    row = lambda width: pl.BlockSpec((1, tm, width), lambda bi, i: (bi, i, 0))
    return pl.pallas_call(
        _rec_proj_kernel,
        grid=(b, t // tm),
        in_specs=[row(d), pl.BlockSpec((1, d), lambda bi, i: (0, 0)), _mod_spec(shift, tm),
                  _mod_spec(scale, tm), pl.BlockSpec(w.shape, lambda bi, i: (0, 0))],
        out_specs=(row(D_RNN), row(D_RNN)),
        out_shape=(jax.ShapeDtypeStruct((b, t, D_RNN), F32),) * 2,
        compiler_params=_cparams(2),
        name="rec_proj",
    )(x, g, shift, scale, w)


def _rglru_gates(xc, wa_ref, ba, wx_ref, bx, lam):
    xcb = xc.astype(BF16)
    blk = lambda w_ref: jnp.concatenate(
        [jnp.dot(xcb[:, n * RG_BLOCK_W:(n + 1) * RG_BLOCK_W], w_ref[n], preferred_element_type=F32)
         for n in range(RG_BLOCKS)], axis=1)
    r = jax.nn.sigmoid(blk(wa_ref) + ba)
    i = jax.nn.sigmoid(blk(wx_ref) + bx)
    neg_lam = -lam
    softplus = jnp.maximum(neg_lam, 0.0) + jnp.log1p(jnp.exp(-jnp.abs(neg_lam)))
    log_a = -RG_C * r * softplus
    a = jnp.exp(log_a)
    u = jnp.sqrt(1.0 - a * a) * (i * xc)
    return a, u


def _rglru_prompt_kernel(gate_ref, xb_ref, cw_ref, cb_ref, wa_ref, ba_ref, wx_ref, bx_ref, lam_ref,
                         y_ref, h_ref, conv_ref, xcat_ref, a_ref, u_ref, hs_ref, hc_ref):
    tc = xb_ref.shape[1]
    c = pl.program_id(1)
    pad = SUBLANES

    @pl.when(c == 0)
    def _():
        xcat_ref[0:pad, :] = jnp.zeros((pad, D_RNN), F32)
        hc_ref[...] = jnp.zeros(hc_ref.shape, F32)

    @pl.when(c > 0)
    def _():
        xcat_ref[0:pad, :] = xcat_ref[tc:tc + pad, :]

    xcat_ref[pad:pad + tc, :] = xb_ref[0]
    cw = cw_ref[...]
    xc = xcat_ref[pad - 3:pad - 3 + tc, :] * cw[0:1, :]
    for j in range(1, CONV_W):
        xc = xc + xcat_ref[pad - 3 + j:pad - 3 + j + tc, :] * cw[j:j + 1, :]
    xc = cb_ref[...] + xc
    a, u = _rglru_gates(xc, wa_ref, ba_ref[...], wx_ref, bx_ref[...], lam_ref[...])
    a_ref[...] = a
    u_ref[...] = u

    def step(t, h):
        h = a_ref[pl.ds(t, 1), :] * h + u_ref[pl.ds(t, 1), :]
        hs_ref[pl.ds(t, 1), :] = h
        return h

    h = lax.fori_loop(0, tc, step, hc_ref[...], unroll=8)
    hc_ref[...] = h
    y_ref[0] = (hs_ref[...] * gate_ref[0]).astype(BF16)
    h_ref[0] = h
    conv_ref[0] = xcat_ref[pad + tc - (CONV_W - 1):pad + tc, :]


def _rglru_prompt(gate, xb, cw, cb, wa, ba, wx, bx, lam):
    b, t, d = xb.shape
    tc = min(SCAN_CHUNK, t)
    row = pl.BlockSpec((1, tc, d), lambda bi, i: (bi, i, 0))
    vec = pl.BlockSpec((1, d), lambda bi, i: (0, 0))
    full = lambda shape: pl.BlockSpec(shape, lambda bi, i: (0,) * len(shape))
    return pl.pallas_call(
        _rglru_prompt_kernel,
        grid=(b, t // tc),
        in_specs=[row, row, full(cw.shape), vec, full(wa.shape), vec, full(wx.shape), vec, vec],
        out_specs=(row, pl.BlockSpec((1, 1, d), lambda bi, i: (bi, 0, 0)),
                   pl.BlockSpec((1, CONV_W - 1, d), lambda bi, i: (bi, 0, 0))),
        out_shape=(jax.ShapeDtypeStruct((b, t, d), BF16), jax.ShapeDtypeStruct((b, 1, d), F32),
                   jax.ShapeDtypeStruct((b, CONV_W - 1, d), F32)),
        scratch_shapes=[pltpu.VMEM((tc + 2 * SUBLANES, d), F32), pltpu.VMEM((tc, d), F32),
                        pltpu.VMEM((tc, d), F32), pltpu.VMEM((tc, d), F32), pltpu.VMEM((1, d), F32)],
        compiler_params=_cparams(2),
        name="rglru_prompt",
    )(gate, xb, cw, cb, wa, ba, wx, bx, lam)


def _rglru_sample_kernel(gate_ref, xb_ref, buf_ref, h0_ref, cw_ref, cb_ref, wa_ref, ba_ref, wx_ref, bx_ref,
                         lam_ref, y_ref, h_ref, nbuf_ref):
    cw = cw_ref[...]
    xb = xb_ref[...]
    xc = buf_ref[0] * cw[0:1, :]
    for j in range(1, CONV_W - 1):
        xc = xc + buf_ref[j] * cw[j:j + 1, :]
    xc = cb_ref[...] + (xc + xb * cw[CONV_W - 1:CONV_W, :])
    a, u = _rglru_gates(xc, wa_ref, ba_ref[...], wx_ref, bx_ref[...], lam_ref[...])
    h = a * h0_ref[...] + u
    h_ref[...] = h
    y_ref[...] = (h * gate_ref[...]).astype(BF16)
    for j in range(CONV_W - 2):
        nbuf_ref[j] = buf_ref[j + 1]
    nbuf_ref[CONV_W - 2] = xb


def _rglru_sample(gate, xb, buf, h0, cw, cb, wa, ba, wx, bx, lam):
    rows, d = xb.shape
    full = lambda a: pl.BlockSpec(a.shape, lambda i: (0,) * a.ndim)
    args = (gate, xb, buf, h0, cw, cb, wa, ba, wx, bx, lam)
    return pl.pallas_call(
        _rglru_sample_kernel,
        grid=(1,),
        in_specs=[full(a) for a in args],
        out_specs=(full(xb), full(xb), full(buf)),
        out_shape=(jax.ShapeDtypeStruct((rows, d), BF16), jax.ShapeDtypeStruct((rows, d), F32),
                   jax.ShapeDtypeStruct(buf.shape, F32)),
        compiler_params=_cparams(1),
        name="rglru_sample",
    )(*args)


def _route(logits):
    lane = lax.broadcasted_iota(I32, logits.shape, 1)

    def first_max(v):
        m = jnp.max(v, axis=1, keepdims=True)
        return m, jnp.min(jnp.where(v == m, lane, LANES), axis=1, keepdims=True)

    is_grp = lane < N_GROUPS
    gm, grp = first_max(jnp.where(is_grp, logits, -jnp.inf))
    g_w = 1.0 / jnp.sum(jnp.where(is_grp, jnp.exp(logits - gm), 0.0), axis=1, keepdims=True)
    lo = N_GROUPS + EXPERTS_PER_GROUP * grp
    el = jnp.where((lane >= lo) & (lane < lo + EXPERTS_PER_GROUP), logits, -jnp.inf)
    m1, i1 = first_max(el)
    m2, i2 = first_max(jnp.where(lane == i1, -jnp.inf, el))
    e2 = jnp.exp(m2 - m1)
    den = 1.0 + e2
    return i1 - N_GROUPS, i2 - N_GROUPS, (1.0 / den) * g_w, (e2 / den) * g_w


def _moe_kernel(x_ref, g_ref, sh_ref, sc_ref, gt_ref, wr_ref, br_ref, w1_ref, w2_ref, gf_ref, o_ref,
                h_ref, acc_ref, e0_ref, e1_ref, c0_ref, c1_ref, *, final_norm):
    e = pl.program_id(2)

    @pl.when(e == 0)
    def _():
        h = _modulated_norm(x_ref[0], g_ref[...], sh_ref[0], sc_ref[0])
        h_ref[...] = h.astype(BF16)
        logits = jnp.dot(h_ref[...], wr_ref[...], preferred_element_type=F32) + br_ref[...]
        e0_ref[...], e1_ref[...], c0_ref[...], c1_ref[...] = _route(logits)
        acc_ref[...] = jnp.zeros(acc_ref.shape, F32)

    gu = jnp.dot(h_ref[...], w1_ref[0].astype(BF16), preferred_element_type=F32)
    gp, up = gu[:, :D_EXPERT], gu[:, D_EXPERT:]
    act = ((gp * jax.nn.sigmoid(gp)) * up).astype(BF16)
    y = jnp.dot(act, w2_ref[0].astype(BF16), preferred_element_type=F32)
    wt = jnp.where(e0_ref[...] == e, c0_ref[...], 0.0) + jnp.where(e1_ref[...] == e, c1_ref[...], 0.0)
    acc_ref[...] += y * wt

    @pl.when(e == N_EXPERTS - 1)
    def _():
        out = x_ref[0] + gt_ref[0] * acc_ref[...]
        if final_norm:
            out = (out * lax.rsqrt(jnp.mean(out * out, axis=-1, keepdims=True) + EPS)) * gf_ref[...]
        o_ref[0] = out


def _moe(x, g, shift, scale, gate, wr, br, w1, w2, layer, gf, final_norm):
    b, t, d = x.shape
    tm = min(MOE_ROW_TILE, t)
    row = pl.BlockSpec((1, tm, d), lambda bi, i, e: (bi, i, 0))
    vec = lambda width: pl.BlockSpec((1, width), lambda bi, i, e: (0, 0))
    col = lambda dt: pltpu.VMEM((tm, 1), dt)
    return pl.pallas_call(
        functools.partial(_moe_kernel, final_norm=final_norm),
        grid=(b, t // tm, N_EXPERTS),
        in_specs=[row, vec(d), _mod_spec(shift, tm), _mod_spec(scale, tm), _mod_spec(gate, tm),
                  pl.BlockSpec(wr.shape, lambda bi, i, e: (0, 0)), vec(ROUTER_PAD),
                  pl.BlockSpec((1, d, 2 * D_EXPERT), lambda bi, i, e: (layer * N_EXPERTS + e, 0, 0)),
                  pl.BlockSpec((1, D_EXPERT, d), lambda bi, i, e: (layer * N_EXPERTS + e, 0, 0)),
                  vec(d)],
        out_specs=row,
        out_shape=jax.ShapeDtypeStruct((b, t, d), F32),
        scratch_shapes=[pltpu.VMEM((tm, d), BF16), pltpu.VMEM((tm, d), F32),
                        col(I32), col(I32), col(F32), col(F32)],
        compiler_params=_cparams(3),
        name="moe",
    )(x, g, shift, scale, gate, wr, br, w1, w2, gf)


REC_E0, REC_E1, REC_RANK0, REC_RANK1, REC_C0, REC_C1 = range(6)


def _moe_route_kernel(x_ref, g_ref, sh_ref, sc_ref, wr_ref, br_ref, h_ref, rec_ref, recT_ref, cnt_ref,
                      tri_ref, carry_ref):
    tm = x_ref.shape[1]

    @pl.when((pl.program_id(0) == 0) & (pl.program_id(1) == 0))
    def _():
        carry_ref[...] = jnp.zeros(carry_ref.shape, F32)
        earlier = lax.broadcasted_iota(I32, (tm, tm), 0) > lax.broadcasted_iota(I32, (tm, tm), 1)
        tri_ref[...] = jnp.where(earlier, 1.0, 0.0).astype(BF16)

    h = _modulated_norm(x_ref[0], g_ref[...], sh_ref[0], sc_ref[0])
    h_ref[0] = h
    logits = jnp.dot(h.astype(BF16), wr_ref[...], preferred_element_type=F32) + br_ref[...]
    e0, e1, c0, c1 = _route(logits)
    lane = lax.broadcasted_iota(I32, logits.shape, 1)
    pick0, pick1 = lane == e0, lane == e1
    picks = jnp.where(pick0 | pick1, 1.0, 0.0)
    before = carry_ref[...] + jnp.dot(tri_ref[...], picks.astype(BF16), preferred_element_type=F32)
    rank0 = jnp.sum(jnp.where(pick0, before, 0.0), axis=1, keepdims=True)
    rank1 = jnp.sum(jnp.where(pick1, before, 0.0), axis=1, keepdims=True)
    carry_ref[...] = carry_ref[...] + jnp.sum(picks, axis=0, keepdims=True)
    cnt_ref[...] = carry_ref[...]
    rec = jnp.zeros(logits.shape, F32)
    for pos, val in ((REC_E0, e0.astype(F32)), (REC_E1, e1.astype(F32)), (REC_RANK0, rank0),
                     (REC_RANK1, rank1), (REC_C0, c0), (REC_C1, c1)):
        rec = jnp.where(lane == pos, val, rec)
    rec_ref[0] = rec
    recT_ref[0] = rec.T[0:SUBLANES, :]


def _moe_route(x, g, shift, scale, wr, br):
    b, t, d = x.shape
    tm = ROW_TILE
    row = lambda width: pl.BlockSpec((1, tm, width), lambda bi, i: (bi, i, 0))
    vec = lambda width: pl.BlockSpec((1, width), lambda bi, i: (0, 0))
    return pl.pallas_call(
        _moe_route_kernel,
        grid=(b, t // tm),
        in_specs=[row(d), vec(d), _mod_spec(shift, tm), _mod_spec(scale, tm),
                  pl.BlockSpec(wr.shape, lambda bi, i: (0, 0)), vec(ROUTER_PAD)],
        out_specs=(row(d), row(ROUTER_PAD), pl.BlockSpec((1, SUBLANES, tm), lambda bi, i: (bi, 0, i)),
                   vec(ROUTER_PAD)),
        out_shape=(jax.ShapeDtypeStruct((b, t, d), F32), jax.ShapeDtypeStruct((b, t, ROUTER_PAD), F32),
                   jax.ShapeDtypeStruct((b, SUBLANES, t), F32), jax.ShapeDtypeStruct((1, ROUTER_PAD), F32)),
        scratch_shapes=[pltpu.VMEM((tm, tm), BF16), pltpu.VMEM((1, ROUTER_PAD), F32)],
        compiler_params=_cparams(2),
        name="moe_route",
    )(x, g, shift, scale, wr, br)


def _gather_rows_start(idx_ref, idx_base, src_hbm, dst, sem, n_rows):
    for r in range(n_rows):
        pltpu.make_async_copy(src_hbm.at[pl.ds(idx_ref[idx_base + r], 1)], dst.at[pl.ds(r, 1)], sem).start()


def _gather_rows_wait(src_hbm, dst, sem, n_rows):
    pltpu.make_async_copy(src_hbm.at[pl.ds(0, n_rows)], dst, sem).wait()


def _moe_experts_kernel(blk_e_ref, n_used_ref, dest_ref, h_hbm, w1_ref, w2_ref, y_ref,
                        buf_ref, w1b_ref, w2b_ref, row_tok_ref, sem):
    i = pl.program_id(0)
    blk = buf_ref.shape[1]
    n_used = n_used_ref[0]
    n_slots = buf_ref.shape[0]
    slot = i % n_slots
    n_tok = dest_ref.shape[0] // 2

    def start(block):
        s = block % n_slots
        _gather_rows_start(row_tok_ref, block * blk, h_hbm, buf_ref.at[s], sem.at[s], blk)

    @pl.when(i == 0)
    def _():
        def clear(r, carry):
            row_tok_ref[r] = 0
            return carry

        def place(tok, carry):
            row_tok_ref[dest_ref[tok]] = tok
            row_tok_ref[dest_ref[n_tok + tok]] = tok
            return carry

        lax.fori_loop(0, row_tok_ref.shape[0], clear, 0, unroll=16)
        lax.fori_loop(0, n_tok, place, 0, unroll=8)
        for ahead in range(n_slots - 1):
            @pl.when(ahead < n_used)
            def _():
                start(ahead)

    @pl.when(i + n_slots - 1 < n_used)
    def _():
        start(i + n_slots - 1)

    @pl.when(i < n_used)
    def _():
        @pl.when((i == 0) | (blk_e_ref[i] != blk_e_ref[jnp.maximum(i - 1, 0)]))
        def _():
            w1b_ref[...] = w1_ref[0].astype(BF16)
            w2b_ref[...] = w2_ref[0].astype(BF16)

        _gather_rows_wait(h_hbm, buf_ref.at[slot], sem.at[slot], blk)
        gu = jnp.dot(buf_ref[slot].astype(BF16), w1b_ref[...], preferred_element_type=F32)
        gp, up = gu[:, :D_EXPERT], gu[:, D_EXPERT:]
        act = ((gp * jax.nn.sigmoid(gp)) * up).astype(BF16)
        y_ref[...] = jnp.dot(act, w2b_ref[...], preferred_element_type=F32)

    @pl.when(i >= n_used)
    def _():
        y_ref[...] = jnp.zeros(y_ref.shape, F32)


def _moe_experts(blk_expert, n_used, dest, h2d, w1, w2, layer):
    n_blk = blk_expert.shape[0]
    blk = MOE_BLOCK_ROWS
    d = h2d.shape[1]
    grid_spec = pltpu.PrefetchScalarGridSpec(
        num_scalar_prefetch=3,
        grid=(n_blk,),
        in_specs=[pl.BlockSpec(memory_space=pl.ANY),
                  pl.BlockSpec((1, d, 2 * D_EXPERT), lambda i, be, nu, rt: (layer * N_EXPERTS + be[i], 0, 0)),
                  pl.BlockSpec((1, D_EXPERT, d), lambda i, be, nu, rt: (layer * N_EXPERTS + be[i], 0, 0))],
        out_specs=pl.BlockSpec((blk, d), lambda i, be, nu, rt: (i, 0)),
        scratch_shapes=[pltpu.VMEM((GATHER_SLOTS, blk, d), F32), pltpu.VMEM((d, 2 * D_EXPERT), BF16),
                        pltpu.VMEM((D_EXPERT, d), BF16), pltpu.SMEM((n_blk * blk,), I32),
                        pltpu.SemaphoreType.DMA((GATHER_SLOTS,))],
    )
    return pl.pallas_call(
        _moe_experts_kernel,
        grid_spec=grid_spec,
        out_shape=jax.ShapeDtypeStruct((n_blk * blk, d), F32),
        compiler_params=_cparams(1),
        name="moe_experts",
    )(blk_expert, n_used, dest, h2d, w1, w2)


def _moe_combine_kernel(dest_ref, y_hbm, x_ref, gt_ref, rec_ref, gf_ref, o_ref, buf_ref, sem,
                        *, n_tokens, final_norm):
    tm = x_ref.shape[1]
    n_steps = pl.num_programs(0) * pl.num_programs(1)
    s = pl.program_id(0) * pl.num_programs(1) + pl.program_id(1)
    slot = s % 2

    def start(step, sl):
        for pick in range(2):
            _gather_rows_start(dest_ref, pick * n_tokens + step * tm, y_hbm, buf_ref.at[sl, pick],
                               sem.at[sl, pick], tm)

    @pl.when(s == 0)
    def _():
        start(0, 0)

    @pl.when(s + 1 < n_steps)
    def _():
        start(s + 1, 1 - slot)

    for pick in range(2):
        _gather_rows_wait(y_hbm, buf_ref.at[slot, pick], sem.at[slot, pick], tm)
    rec = rec_ref[0]
    moe = buf_ref[slot, 0] * rec[:, REC_C0:REC_C0 + 1] + buf_ref[slot, 1] * rec[:, REC_C1:REC_C1 + 1]
    out = x_ref[0] + gt_ref[0] * moe
    if final_norm:
        out = (out * lax.rsqrt(jnp.mean(out * out, axis=-1, keepdims=True) + EPS)) * gf_ref[...]
    o_ref[0] = out


def _moe_combine(dest, y, x, gate, rec, gf, final_norm):
    b, t, d = x.shape
    tm = MOE_BLOCK_ROWS
    row = lambda width: pl.BlockSpec((1, tm, width), lambda bi, i, dst: (bi, i, 0))
    grid_spec = pltpu.PrefetchScalarGridSpec(
        num_scalar_prefetch=1,
        grid=(b, t // tm),
        in_specs=[pl.BlockSpec(memory_space=pl.ANY), row(d), _mod_spec(gate, tm), row(ROUTER_PAD),
                  pl.BlockSpec((1, d), lambda bi, i, dst: (0, 0))],
        out_specs=row(d),
        scratch_shapes=[pltpu.VMEM((2, 2, tm, d), F32), pltpu.SemaphoreType.DMA((2, 2))],
    )
    return pl.pallas_call(
        functools.partial(_moe_combine_kernel, n_tokens=b * t, final_norm=final_norm),
        grid_spec=grid_spec,
        out_shape=jax.ShapeDtypeStruct((b, t, d), F32),
        compiler_params=_cparams(2),
        name="moe_combine",
    )(dest, y, x, gate, rec, gf)


def _moe_sorted(x, g, shift, scale, gate, wr, br, w1, w2, layer, gf, final_norm):
    b, t, d = x.shape
    n = b * t
    blk = MOE_BLOCK_ROWS
    n_blk = (2 * n) // blk + N_EXPERTS
    h, rec, rec_t, cnt = _moe_route(x, g, shift, scale, wr, br)
    counts = cnt[0, :N_EXPERTS].astype(I32)
    padded = (counts + blk - 1) // blk * blk
    seg_end = jnp.cumsum(padded)
    seg_start = seg_end - padded
    col = lambda c: rec_t[:, c, :].reshape(n).astype(I32)
    dest = jnp.concatenate([seg_start[col(REC_E0)] + col(REC_RANK0), seg_start[col(REC_E1)] + col(REC_RANK1)])
    blk_first_row = jnp.arange(n_blk, dtype=I32) * blk
    blk_expert = jnp.minimum(jnp.sum((seg_end[None, :] <= blk_first_row[:, None]).astype(I32), axis=1),
                             N_EXPERTS - 1)
    n_used = (seg_end[-1:] // blk).astype(I32)
    y = _moe_experts(blk_expert, n_used, dest, h.reshape(n, d), w1, w2, layer)
    return _moe_combine(dest, y, x, gate, rec, gf, final_norm)


def kernel(x_prompt, x_sample, cache_k, cache_v, cache_kidx, state_h, state_conv, page_table,
           c_prompt, c_sample, ada_w, ada_b, norm_mix_g, norm_ffn_g, norm_final_g,
           attn_w_in, attn_w_out, rec_w_in, rec_conv_w, rec_conv_b, rec_w_a, rec_b_a,
           rec_w_x, rec_b_x, rec_lambda, rec_w_out, moe_w_group, moe_b_group,
           moe_w_expert, moe_b_expert, moe_w1, moe_w2):
    b, t, d = x_prompt.shape
    db = x_sample.shape[0]
    n_pages = page_table.shape[1]
    past = n_pages * PAGE_SIZE
    n_pool = cache_k.shape[1]

    n_cond = b + db
    cond_rows = -(-n_cond // SUBLANES) * SUBLANES
    c_all = jnp.concatenate([c_prompt, c_sample, jnp.zeros((cond_rows - n_cond, d), F32)], axis=0)
    mod = _adaln(c_all, ada_w, ada_b)

    tables_p = _rope_tables(jnp.arange(t, dtype=I32))
    tables_s = _rope_tables(jnp.full((db,), past, I32))
    cache_kT = cache_k.transpose(0, 1, 3, 4, 2).reshape(-1, KV_WIDTH, PAGE_SIZE)
    cache_vT = cache_v.transpose(0, 1, 3, 4, 2).reshape(-1, KV_WIDTH, PAGE_SIZE)
    cache_kiT = cache_kidx.transpose(0, 1, 3, 2).reshape(-1, IDX_DIM, PAGE_SIZE)
    pt_flat = page_table.reshape(-1)
    moe_w1f = moe_w1.reshape(-1, d, 2 * D_EXPERT)
    moe_w2f = moe_w2.reshape(-1, D_EXPERT, d)
    row_vec = lambda v: v.reshape(1, -1)
    head_grp = jnp.arange(N_HEADS)[:, None] // Q_PER_KV == jnp.arange(KV_WIDTH)[None, :] // HEAD_DIM

    xp = x_prompt
    xs = x_sample.reshape(1, db, d)
    outs = {name: [] for name in ("k_p", "v_p", "ki_p", "h_p", "cv_p", "k_s", "v_s", "ki_s", "h_s", "cv_s")}
    for l in range(DEPTH):
        parts = [mod[l, :, i * d:(i + 1) * d] for i in range(6)]
        sh1p, sc1p, g1p, sh2p, sc2p, g2p = [m[:b].reshape(b, 1, d) for m in parts]
        sh1s, sc1s, g1s, sh2s, sc2s, g2s = [m[b:n_cond].reshape(1, db, d) for m in parts]
        g_mix = row_vec(norm_mix_g[l])
        if l % 2 == 0:
            a = l // 2
            w_in = jnp.pad(attn_w_in[a], ((0, 0), (0, ATTN_PROJ_PAD - ATTN_PROJ))).astype(BF16)
            w_out = attn_w_out[a].astype(BF16)
            kT, vT, kiT, qT, qiT, wiT, kb, vTb, kib = _attn_proj_prompt(xp, g_mix, sh1p, sc1p, w_in, tables_p)
            op = _dsa_prompt(qT, qiT, wiT, kb, vTb, kib)
            outs["k_p"].append(kT.reshape(b, N_KV_HEADS, HEAD_DIM, t).transpose(0, 3, 1, 2))
            outs["v_p"].append(vT.reshape(b, N_KV_HEADS, HEAD_DIM, t).transpose(0, 3, 1, 2))
            outs["ki_p"].append(kiT.transpose(0, 2, 1))
            xp = _out_proj(op, w_out, xp, g1p)

            q, k, v, qi, ki, wi = _attn_proj_sample(xs, g_mix, sh1s, sc1s, w_in, tables_s)
            keys = _dsa_sample_scores(pt_flat, qi.reshape(db, IDX_HEADS, IDX_DIM).astype(BF16),
                                      wi.reshape(db, IDX_HEADS, 1), ki.reshape(db, 1, IDX_DIM),
                                      cache_kiT, a, n_pool, n_pages)
            bias = _dsa_sample_select(keys.reshape(db, -1), past + 1).reshape(keys.shape)
            q_bd = jnp.where(head_grp[None], jnp.tile(q.reshape(db, N_HEADS, HEAD_DIM), (1, 1, N_KV_HEADS)),
                             0.0).astype(BF16)
            os_ = _dsa_sample_attend(pt_flat, q_bd, bias, k.reshape(db, 1, KV_WIDTH), v.reshape(db, 1, KV_WIDTH),
                                     cache_kT, cache_vT, a, n_pool, n_pages)
            outs["k_s"].append(k.reshape(db, 1, N_KV_HEADS, HEAD_DIM))
            outs["v_s"].append(v.reshape(db, 1, N_KV_HEADS, HEAD_DIM))
            outs["ki_s"].append(ki.reshape(db, 1, IDX_DIM))
            xs = _out_proj(os_.reshape(1, db, Q_WIDTH).astype(BF16), w_out, xs, g1s)
        else:
            r = l // 2
            w_in = rec_w_in[r].astype(BF16)
            w_out = rec_w_out[r].astype(BF16)
            rec = (rec_conv_w[r], row_vec(rec_conv_b[r]), rec_w_a[r].astype(BF16), row_vec(rec_b_a[r]),
                   rec_w_x[r].astype(BF16), row_vec(rec_b_x[r]), row_vec(rec_lambda[r]))
            gate, xb = _rec_proj(xp, g_mix, sh1p, sc1p, w_in)
            y, h_t, buf = _rglru_prompt(gate, xb, *rec)
            outs["h_p"].append(h_t.reshape(b, D_RNN))
            outs["cv_p"].append(buf)
            xp = _out_proj(y, w_out, xp, g1p)

            gate, xb = _rec_proj(xs, g_mix, sh1s, sc1s, w_in)
            y, h_t, buf = _rglru_sample(gate[0], xb[0], state_conv[r].swapaxes(0, 1), state_h[r], *rec)
            outs["h_s"].append(h_t)
            outs["cv_s"].append(buf.swapaxes(0, 1))
            xs = _out_proj(y.reshape(1, db, D_RNN), w_out, xs, g1s)

        wr = jnp.concatenate([moe_w_group[l], moe_w_expert[l],
                              jnp.zeros((d, ROUTER_PAD - N_GROUPS - N_EXPERTS), F32)], axis=1).astype(BF16)
        br = jnp.concatenate([moe_b_group[l], moe_b_expert[l],
                              jnp.zeros((ROUTER_PAD - N_GROUPS - N_EXPERTS,), F32)]).reshape(1, ROUTER_PAD)
        last = l == DEPTH - 1
        moe_args = (wr, br, moe_w1f, moe_w2f, l, row_vec(norm_final_g), last)
        xp = _moe_sorted(xp, row_vec(norm_ffn_g[l]), sh2p, sc2p, g2p, *moe_args)
        xs = _moe(xs, row_vec(norm_ffn_g[l]), sh2s, sc2s, g2s, *moe_args)

    st = lambda name: jnp.stack(outs[name])
    return (xp, xs.reshape(db, 1, d), st("k_p"), st("v_p"), st("ki_p"), st("h_p"), st("cv_p"),
            st("k_s"), st("v_s"), st("ki_s"), st("h_s"), st("cv_s"))
```

```python
import functools
import math

import jax
import jax.numpy as jnp
from jax import lax
from jax.experimental import pallas as pl
from jax.experimental.pallas import tpu as pltpu

F32 = jnp.float32
BF16 = jnp.bfloat16
I32 = jnp.int32

D_MODEL = 1024
DEPTH = 4
PAGE_SIZE = 128
N_HEADS = 16
HEAD_DIM = 64
N_KV_HEADS = 4
Q_PER_KV = N_HEADS // N_KV_HEADS
IDX_HEADS = 8
IDX_DIM = 64
IDX_SCALE = (IDX_HEADS * IDX_DIM) ** -0.5
TOPK_MAX = 256
ROPE_THETA = 500000.0
ROPE_FRACTION = 4
ROT_HALF = HEAD_DIM // ROPE_FRACTION // 2
Q_WIDTH = N_HEADS * HEAD_DIM
KV_WIDTH = N_KV_HEADS * HEAD_DIM
QI_WIDTH = IDX_HEADS * IDX_DIM
ATTN_PROJ = Q_WIDTH + 2 * KV_WIDTH + QI_WIDTH + IDX_DIM + IDX_HEADS
D_RNN = D_MODEL
RG_BLOCKS = 8
RG_BLOCK_W = D_RNN // RG_BLOCKS
CONV_W = 4
RG_C = 8.0
N_GROUPS = 4
EXPERTS_PER_GROUP = 8
N_EXPERTS = N_GROUPS * EXPERTS_PER_GROUP
D_EXPERT = 256
EPS = 1e-6

LANES = 128
SUBLANES = 8
VMEM_LIMIT_BYTES = 56 * 1024 * 1024

ATTN_PROJ_PAD = -(-ATTN_PROJ // LANES) * LANES
ROUTER_PAD = LANES
Q_TILE = LANES
KEY_CHUNK = 512
ROW_TILE = 512
MOE_ROW_TILE = 1024
MOE_BLOCK_ROWS = 256
GATHER_SLOTS = 3
SCAN_CHUNK = 512
PAGE_GROUP = 8
BF16_SUBLANES = 16
HEAD_AUG = HEAD_DIM + BF16_SUBLANES
BOUND_SLACK = 1.02
DENOM_FLOOR = 2.0 ** -60
DENOM_CEIL = 2.0 ** 60
INT_MIN = -(2 ** 31)
INT_MAX = 2 ** 31 - 1
NEG = -1e30


def _cparams(n_axes):
    return pltpu.CompilerParams(dimension_semantics=("arbitrary",) * n_axes,
                                vmem_limit_bytes=VMEM_LIMIT_BYTES)


def _modulated_norm(x, g, shift, scale):
    xn = x * lax.rsqrt(jnp.mean(x * x, axis=-1, keepdims=True) + EPS)
    return (xn * g) * (1.0 + scale) + shift


def _rope_group(xs, cos, sa, sb):
    return xs * cos + pltpu.roll(xs, LANES - 8, 1) * sa + pltpu.roll(xs, 8, 1) * sb


def _float_order_key(s):
    bits = lax.bitcast_convert_type(s, I32)
    return bits ^ ((bits >> 31) & INT_MAX)


def _mod_spec(arr, tm):
    if arr.shape[1] == 1:
        return pl.BlockSpec((1, 1, arr.shape[2]), lambda b, i, *_: (b, 0, 0))
    return pl.BlockSpec((1, tm, arr.shape[2]), lambda b, i, *_: (b, i, 0))


def _adaln_kernel(c_ref, w_ref, b_ref, o_ref):
    c = c_ref[...]
    s = (c * jax.nn.sigmoid(c)).astype(BF16)
    o_ref[0] = jnp.dot(s, w_ref[0].astype(BF16), preferred_element_type=F32) + b_ref[0]


def _adaln(c_all, ada_w, ada_b):
    depth, d, n = ada_w.shape
    rows = c_all.shape[0]
    tn = 1536
    return pl.pallas_call(
        _adaln_kernel,
        grid=(depth, n // tn),
        in_specs=[pl.BlockSpec((rows, d), lambda l, j: (0, 0)),
                  pl.BlockSpec((1, d, tn), lambda l, j: (l, 0, j)),
                  pl.BlockSpec((1, 1, tn), lambda l, j: (l, 0, j))],
        out_specs=pl.BlockSpec((1, rows, tn), lambda l, j: (l, 0, j)),
        out_shape=jax.ShapeDtypeStruct((depth, rows, n), F32),
        compiler_params=_cparams(2),
        name="adaln",
    )(c_all, ada_w, ada_b.reshape(depth, 1, n))


def _rope_tables(pos):
    rot = HEAD_DIM // ROPE_FRACTION
    half = rot // 2
    inv = jnp.exp(-math.log(ROPE_THETA) * jnp.arange(half, dtype=F32) * (2.0 / rot))
    ang = pos.astype(F32)[:, None] * inv[None, :]
    cos, sin = jnp.cos(ang), jnp.sin(ang)
    r = pos.shape[0]
    z = lambda w: jnp.zeros((r, w), F32)
    cos64 = jnp.concatenate([cos, cos, jnp.ones((r, HEAD_DIM - rot), F32)], axis=1)
    sa64 = jnp.concatenate([-sin, z(HEAD_DIM - half)], axis=1)
    sb64 = jnp.concatenate([z(half), sin, z(HEAD_DIM - rot)], axis=1)
    two = lambda t: jnp.concatenate([t, t], axis=1)
    return two(cos64), two(sa64), two(sb64), cos.T, sin.T


def _attn_proj_prompt_kernel(x_ref, g_ref, sh_ref, sc_ref, w_ref, cos_ref, sa_ref, sb_ref, cosT_ref, sinT_ref,
                             kT_ref, vT_ref, kiT_ref, qT_ref, qiT_ref, wiT_ref, kb_ref, vTb_ref, kib_ref,
                             kmax_ref):
    h = _modulated_norm(x_ref[0], g_ref[...], sh_ref[0], sc_ref[0]).astype(BF16)
    acc = jnp.dot(h, w_ref[...], preferred_element_type=F32)
    cos, sa, sb = cos_ref[...], sa_ref[...], sb_ref[...]
    n_qb = acc.shape[0] // Q_TILE
    grp = lambda off, j: acc[:, off + j * LANES: off + (j + 1) * LANES]

    tm = acc.shape[0]
    one_hot_row = lambda shape, axis: jnp.where(lax.broadcasted_iota(I32, shape, axis) == 0, 1.0, 0.0).astype(BF16)

    @pl.when(pl.program_id(1) == 0)
    def _():
        kmax_ref[...] = jnp.zeros(kmax_ref.shape, F32)

    for j in range(KV_WIDTH // LANES):
        kj = _rope_group(grp(Q_WIDTH, j), cos, sa, sb)
        kjT = kj.T
        kT_ref[0, j * LANES:(j + 1) * LANES, :] = kjT
        sq = kjT * kjT
        for hh in range(2):
            n = 2 * j + hh
            norm2 = jnp.sum(sq[hh * HEAD_DIM:(hh + 1) * HEAD_DIM, :], axis=0, keepdims=True)
            kmax_ref[n] = jnp.maximum(kmax_ref[n], jnp.max(norm2, axis=1, keepdims=True))
            kb_ref[0, n, :, 0:HEAD_DIM] = kj[:, hh * HEAD_DIM:(hh + 1) * HEAD_DIM].astype(BF16)
            kb_ref[0, n, :, HEAD_DIM:HEAD_AUG] = one_hot_row((tm, BF16_SUBLANES), 1)
    cosT, sinT = cosT_ref[...], sinT_ref[...]

    def rope_rows(xT):
        parts = []
        for hh in range(2):
            lo, mid, hi = hh * HEAD_DIM, hh * HEAD_DIM + ROT_HALF, hh * HEAD_DIM + 2 * ROT_HALF
            x1, x2 = xT[lo:mid], xT[mid:hi]
            parts += [x1 * cosT - x2 * sinT, x1 * sinT + x2 * cosT, xT[hi:(hh + 1) * HEAD_DIM]]
        return jnp.concatenate(parts, axis=0)

    for j in range(Q_WIDTH // LANES):
        qjT = rope_rows(grp(0, j).T) * (HEAD_DIM ** -0.5)
        sq = qjT * qjT
        qjTb = qjT.astype(BF16)
        for hh in range(2):
            n, g = divmod(2 * j + hh, Q_PER_KV)
            norm2 = jnp.sum(sq[hh * HEAD_DIM:(hh + 1) * HEAD_DIM, :], axis=0, keepdims=True)
            shift = -BOUND_SLACK * jnp.sqrt(norm2 * kmax_ref[n])
            shift = jnp.concatenate([shift, jnp.zeros((BF16_SUBLANES - 1, tm), F32)], axis=0).astype(BF16)
            for jb in range(n_qb):
                qT_ref[0, jb, n, 0:HEAD_DIM, g * Q_TILE:(g + 1) * Q_TILE] = (
                    qjTb[hh * HEAD_DIM:(hh + 1) * HEAD_DIM, jb * Q_TILE:(jb + 1) * Q_TILE])
                qT_ref[0, jb, n, HEAD_DIM:HEAD_AUG, g * Q_TILE:(g + 1) * Q_TILE] = (
                    shift[:, jb * Q_TILE:(jb + 1) * Q_TILE])
    for j in range(KV_WIDTH // LANES):
        vjT = grp(Q_WIDTH + KV_WIDTH, j).T
        vT_ref[0, j * LANES:(j + 1) * LANES, :] = vjT
        for hh in range(2):
            n = 2 * j + hh
            vTb_ref[0, 0, n * HEAD_AUG:n * HEAD_AUG + HEAD_DIM, :] = (
                vjT[hh * HEAD_DIM:(hh + 1) * HEAD_DIM, :].astype(BF16))
            vTb_ref[0, 0, n * HEAD_AUG + HEAD_DIM:(n + 1) * HEAD_AUG, :] = one_hot_row((BF16_SUBLANES, tm), 0)
    off_qi = Q_WIDTH + 2 * KV_WIDTH
    for j in range(QI_WIDTH // LANES):
        qijT = rope_rows(grp(off_qi, j).T).astype(BF16)
        for hh in range(2):
            head = 2 * j + hh
            for jb in range(n_qb):
                qiT_ref[0, jb, :, head * Q_TILE:(head + 1) * Q_TILE] = (
                    qijT[hh * IDX_DIM:(hh + 1) * IDX_DIM, jb * Q_TILE:(jb + 1) * Q_TILE])
    last = grp(off_qi + QI_WIDTH, 0)
    kir = _rope_group(last, cos, sa, sb)
    kiT_ref[0] = kir.T[:IDX_DIM, :]
    kib_ref[0] = kir[:, :IDX_DIM].astype(BF16)
    lastT = last.T
    for jb in range(n_qb):
        wiT_ref[0, jb] = lastT[IDX_DIM:IDX_DIM + IDX_HEADS, jb * Q_TILE:(jb + 1) * Q_TILE]


def _attn_proj_prompt(x, g, shift, scale, w_pad, tables):
    b, t, d = x.shape
    tm = ROW_TILE
    nq = t // Q_TILE
    grid = (b, t // tm)
    row = lambda width: pl.BlockSpec((1, tm, width), lambda bi, i: (bi, i, 0))
    col = lambda width: pl.BlockSpec((1, width, tm), lambda bi, i: (bi, 0, i))
    tab = pl.BlockSpec((tm, LANES), lambda bi, i: (i, 0))
    tab_t = pl.BlockSpec((ROT_HALF, tm), lambda bi, i: (0, i))
    out_shape = (
        jax.ShapeDtypeStruct((b, KV_WIDTH, t), F32),
        jax.ShapeDtypeStruct((b, KV_WIDTH, t), F32),
        jax.ShapeDtypeStruct((b, IDX_DIM, t), F32),
        jax.ShapeDtypeStruct((b, nq, N_KV_HEADS, HEAD_AUG, Q_PER_KV * Q_TILE), BF16),
        jax.ShapeDtypeStruct((b, nq, IDX_DIM, IDX_HEADS * Q_TILE), BF16),
        jax.ShapeDtypeStruct((b, nq, IDX_HEADS, Q_TILE), F32),
        jax.ShapeDtypeStruct((b, N_KV_HEADS, t, HEAD_AUG), BF16),
        jax.ShapeDtypeStruct((b, t // tm, N_KV_HEADS * HEAD_AUG, tm), BF16),
        jax.ShapeDtypeStruct((b, t, IDX_DIM), BF16),
    )
    nqb = tm // Q_TILE
    out_specs = (
        col(KV_WIDTH), col(KV_WIDTH), col(IDX_DIM),
        pl.BlockSpec((1, nqb, N_KV_HEADS, HEAD_AUG, Q_PER_KV * Q_TILE), lambda bi, i: (bi, i, 0, 0, 0)),
        pl.BlockSpec((1, nqb, IDX_DIM, IDX_HEADS * Q_TILE), lambda bi, i: (bi, i, 0, 0)),
        pl.BlockSpec((1, nqb, IDX_HEADS, Q_TILE), lambda bi, i: (bi, i, 0, 0)),
        pl.BlockSpec((1, N_KV_HEADS, tm, HEAD_AUG), lambda bi, i: (bi, 0, i, 0)),
        pl.BlockSpec((1, 1, N_KV_HEADS * HEAD_AUG, tm), lambda bi, i: (bi, i, 0, 0)),
        row(IDX_DIM),
    )
    return pl.pallas_call(
        _attn_proj_prompt_kernel,
        grid=grid,
        in_specs=[row(d), pl.BlockSpec((1, d), lambda bi, i: (0, 0)),
                  _mod_spec(shift, tm), _mod_spec(scale, tm),
                  pl.BlockSpec(w_pad.shape, lambda bi, i: (0, 0)), tab, tab, tab, tab_t, tab_t],
        out_specs=out_specs,
        out_shape=out_shape,
        scratch_shapes=[pltpu.VMEM((N_KV_HEADS, 1, 1), F32)],
        compiler_params=_cparams(2),
        name="attn_proj_prompt",
    )(x, g, shift, scale, w_pad, *tables)


def _attn_proj_sample_kernel(x_ref, g_ref, sh_ref, sc_ref, w_ref, cos_ref, sa_ref, sb_ref,
                             q_ref, k_ref, v_ref, qi_ref, ki_ref, wi_ref):
    h = _modulated_norm(x_ref[0], g_ref[...], sh_ref[0], sc_ref[0]).astype(BF16)
    acc = jnp.dot(h, w_ref[...], preferred_element_type=F32)
    cos, sa, sb = cos_ref[...], sa_ref[...], sb_ref[...]
    grp = lambda off, j: acc[:, off + j * LANES: off + (j + 1) * LANES]
    for j in range(Q_WIDTH // LANES):
        q_ref[:, j * LANES:(j + 1) * LANES] = _rope_group(grp(0, j), cos, sa, sb) * (HEAD_DIM ** -0.5)
    for j in range(KV_WIDTH // LANES):
        k_ref[:, j * LANES:(j + 1) * LANES] = _rope_group(grp(Q_WIDTH, j), cos, sa, sb)
        v_ref[:, j * LANES:(j + 1) * LANES] = grp(Q_WIDTH + KV_WIDTH, j)
    off_qi = Q_WIDTH + 2 * KV_WIDTH
    for j in range(QI_WIDTH // LANES):
        qi_ref[:, j * LANES:(j + 1) * LANES] = _rope_group(grp(off_qi, j), cos, sa, sb)
    last = grp(off_qi + QI_WIDTH, 0)
    ki_ref[...] = _rope_group(last, cos, sa, sb)[:, :IDX_DIM]
    wi_ref[...] = last[:, IDX_DIM:IDX_DIM + IDX_HEADS]


def _attn_proj_sample(x, g, shift, scale, w_pad, tables):
    _, rows, d = x.shape
    full = lambda shape: pl.BlockSpec(shape, lambda i: (0,) * len(shape))
    widths = (Q_WIDTH, KV_WIDTH, KV_WIDTH, QI_WIDTH, IDX_DIM, IDX_HEADS)
    return pl.pallas_call(
        _attn_proj_sample_kernel,
        grid=(1,),
        in_specs=[full((1, rows, d)), full((1, d)), full((1, rows, d)), full((1, rows, d)),
                  full(w_pad.shape), full((rows, LANES)), full((rows, LANES)), full((rows, LANES))],
        out_specs=tuple(full((rows, w)) for w in widths),
        out_shape=tuple(jax.ShapeDtypeStruct((rows, w), F32) for w in widths),
        compiler_params=_cparams(1),
        name="attn_proj_sample",
    )(x, g, shift, scale, w_pad, *tables[:3])


def _dsa_prompt_kernel(qT_ref, qiT_ref, wiT_ref, kb_ref, vT_ref, kib_ref, o_ref,
                       keys_ref, acc_ref, acc2_ref, m_ref, l_ref, j_ref, *, topk, idx_bits):
    kc = KEY_CHUNK
    qb = pl.program_id(1)
    n_chunks = (qb * Q_TILE + Q_TILE + kc - 1) // kc
    rows = lax.broadcasted_iota(I32, (kc, Q_TILE), 0)
    q_pos = qb * Q_TILE + lax.broadcasted_iota(I32, (kc, Q_TILE), 1)

    w_heads = wiT_ref[0, 0] * IDX_SCALE
    qiT = qiT_ref[0, 0]

    def score_chunk(c, carry):
        off = pl.multiple_of(c * kc, kc)
        dots = jnp.dot(kib_ref[0, pl.ds(off, kc), :], qiT, preferred_element_type=F32)
        s = jnp.zeros((kc, Q_TILE), F32)
        for h in range(IDX_HEADS):
            s = s + jnp.maximum(dots[:, h * Q_TILE:(h + 1) * Q_TILE], 0.0) * w_heads[h:h + 1, :]
        keys_ref[pl.ds(off, kc), :] = jnp.where(rows + off <= q_pos, _float_order_key(s), INT_MIN)
        return carry

    lax.fori_loop(0, n_chunks, score_chunk, 0)

    def count(pred):
        def body(c, cnt):
            off = pl.multiple_of(c * kc, kc)
            hit = jnp.where(pred(keys_ref[pl.ds(off, kc), :], rows + off), 1, 0)
            return cnt + hit.reshape(kc // SUBLANES, SUBLANES, Q_TILE).sum(axis=0)
        cnt = lax.fori_loop(0, n_chunks, body, jnp.zeros((SUBLANES, Q_TILE), I32))
        return cnt.sum(axis=0, keepdims=True)

    def thr_bit(i, state):
        thr, n_ge = state
        cand = thr + lax.shift_left(jnp.int32(1), 31 - i)
        n_cand = count(lambda k, s: k >= cand)
        ok = n_cand >= topk
        return jnp.where(ok, cand, thr), jnp.where(ok, n_cand, n_ge)

    n_all = jnp.full((1, Q_TILE), 1, I32) * (n_chunks * kc)
    thr, n_ge = lax.fori_loop(0, 32, thr_bit, (jnp.full((1, Q_TILE), INT_MIN, I32), n_all))

    j_ref[...] = jnp.full((1, Q_TILE), INT_MAX, I32)
    tie_split = jnp.max(jnp.where((n_ge > topk) & (thr > INT_MIN), 1, 0))

    @pl.when(tie_split > 0)
    def _():
        need = topk - count(lambda k, s: k > thr)

        def idx_bit(i, j0):
            cand = j0 + lax.shift_left(jnp.int32(1), idx_bits - 1 - i)
            taken_before = count(lambda k, s: (k == thr) & (s < cand))
            return jnp.where(taken_before < need, cand, j0)
        j_ref[...] = lax.fori_loop(0, idx_bits, idx_bit, jnp.zeros((1, Q_TILE), I32))

    j_sel = j_ref[...]

    def selection_bias(c):
        off = pl.multiple_of(c * kc, kc)
        k = keys_ref[pl.ds(off, kc), :]
        s_idx = rows + off
        sel = ((k > thr) | ((k == thr) & (s_idx <= j_sel))) & (s_idx <= q_pos)
        bias = jnp.where(sel, 0.0, NEG)
        return off, jnp.concatenate([bias] * Q_PER_KV, axis=1)

    acc_ref[...] = jnp.zeros(acc_ref.shape, F32)

    def attend_chunk(c, carry):
        off, bias = selection_bias(c)
        for n in range(N_KV_HEADS):
            s = jnp.dot(kb_ref[0, n, pl.ds(off, kc), :], qT_ref[0, 0, n], preferred_element_type=F32) + bias
            acc_ref[n] += jnp.dot(vT_ref[0, c, n * HEAD_AUG:(n + 1) * HEAD_AUG, :], jnp.exp(s).astype(BF16),
                                  preferred_element_type=F32)
        return carry

    lax.fori_loop(0, n_chunks, attend_chunk, 0)
    denom = jnp.concatenate([acc_ref[n][HEAD_DIM:HEAD_DIM + 1, :] for n in range(N_KV_HEADS)], axis=0)
    healthy = (jnp.min(denom) >= DENOM_FLOOR) & (jnp.max(denom) <= DENOM_CEIL)

    @pl.when(jnp.logical_not(healthy))
    def _():
        m_ref[...] = jnp.full(m_ref.shape, NEG, F32)
        l_ref[...] = jnp.zeros(l_ref.shape, F32)
        acc2_ref[...] = jnp.zeros(acc2_ref.shape, F32)

        def attend_chunk_online(c, carry):
            off, bias = selection_bias(c)
            for n in range(N_KV_HEADS):
                s = jnp.dot(kb_ref[0, n, pl.ds(off, kc), 0:HEAD_DIM], qT_ref[0, 0, n, 0:HEAD_DIM, :],
                            preferred_element_type=F32) + bias
                m_prev = m_ref[n]
                m_new = jnp.maximum(m_prev, s.max(axis=0, keepdims=True))
                alpha = jnp.exp(m_prev - m_new)
                p = jnp.exp(s - m_new)
                l_ref[n] = alpha * l_ref[n] + p.sum(axis=0, keepdims=True)
                pv = jnp.dot(vT_ref[0, c, n * HEAD_AUG:n * HEAD_AUG + HEAD_DIM, :], p.astype(BF16),
                             preferred_element_type=F32)
                acc2_ref[n] = alpha * acc2_ref[n] + pv
                m_ref[n] = m_new
            return carry

        lax.fori_loop(0, n_chunks, attend_chunk_online, 0)
        for n in range(N_KV_HEADS):
            acc_ref[n, 0:HEAD_DIM, :] = acc2_ref[n]
            acc_ref[n, HEAD_DIM:HEAD_DIM + 1, :] = l_ref[n]

    for n in range(N_KV_HEADS):
        on = acc_ref[n, 0:HEAD_DIM, :] / acc_ref[n, HEAD_DIM:HEAD_DIM + 1, :]
        for gp in range(Q_PER_KV // 2):
            pair = jnp.concatenate([on[:, (2 * gp) * Q_TILE:(2 * gp + 1) * Q_TILE],
                                    on[:, (2 * gp + 1) * Q_TILE:(2 * gp + 2) * Q_TILE]], axis=0)
            col = (n * Q_PER_KV // 2 + gp) * LANES
            o_ref[0, :, col:col + LANES] = pair.T.astype(BF16)


def _dsa_prompt(qT, qiT, wiT, kb, vT, kib):
    b, nq = qT.shape[:2]
    t = kb.shape[2]
    topk = min(TOPK_MAX, t // 4)
    idx_bits = max(1, (t - 1).bit_length())
    per_q = lambda shape: pl.BlockSpec((1, 1) + shape, lambda bi, qi: (bi, qi) + (0,) * len(shape))
    per_b = lambda shape: pl.BlockSpec((1,) + shape, lambda bi, qi: (bi,) + (0,) * len(shape))
    return pl.pallas_call(
        functools.partial(_dsa_prompt_kernel, topk=topk, idx_bits=idx_bits),
        grid=(b, nq),
        in_specs=[per_q(qT.shape[2:]), per_q(qiT.shape[2:]), per_q(wiT.shape[2:]),
                  per_b(kb.shape[1:]), per_b(vT.shape[1:]), per_b(kib.shape[1:])],
        out_specs=pl.BlockSpec((1, Q_TILE, Q_WIDTH), lambda bi, qi: (bi, qi, 0)),
        out_shape=jax.ShapeDtypeStruct((b, t, Q_WIDTH), BF16),
        scratch_shapes=[pltpu.VMEM((t, Q_TILE), I32),
                        pltpu.VMEM((N_KV_HEADS, HEAD_AUG, Q_PER_KV * Q_TILE), F32),
                        pltpu.VMEM((N_KV_HEADS, HEAD_DIM, Q_PER_KV * Q_TILE), F32),
                        pltpu.VMEM((N_KV_HEADS, 1, Q_PER_KV * Q_TILE), F32),
                        pltpu.VMEM((N_KV_HEADS, 1, Q_PER_KV * Q_TILE), F32),
                        pltpu.VMEM((1, Q_TILE), I32)],
        compiler_params=_cparams(2),
        name="dsa_prompt",
    )(qT, qiT, wiT, kb, vT, kib)


def _fetch_pages_start(pt_ref, seq, layer_base, caches, bufs, sems, n_pages):
    def body(p, carry):
        page = layer_base + pt_ref[seq * n_pages + p]
        for cache, buf, sem in zip(caches, bufs, sems):
            pltpu.make_async_copy(cache.at[page], buf.at[p], sem).start()
        return carry
    lax.fori_loop(0, n_pages, body, 0)


def _fetch_pages_wait(caches, bufs, sems, n_pages):
    for cache, buf, sem in zip(caches, bufs, sems):
        pltpu.make_async_copy(cache.at[pl.ds(0, n_pages)], buf, sem).wait()


def _fetch_pages_pipelined(pt_ref, layer_base, caches, bufs, sems, n_pages):
    step, n_steps = pl.program_id(0), pl.num_programs(0)
    slot = step % 2
    at = lambda sl: ([b.at[sl] for b in bufs], [s.at[sl] for s in sems])

    @pl.when(step == 0)
    def _():
        _fetch_pages_start(pt_ref, 0, layer_base, caches, *at(0), n_pages)

    @pl.when(step + 1 < n_steps)
    def _():
        _fetch_pages_start(pt_ref, step + 1, layer_base, caches, *at(1 - slot), n_pages)

    _fetch_pages_wait(caches, *at(slot), n_pages)
    return slot


def _pages_bf16(buf, slot, first, count):
    return jnp.concatenate([buf[slot, first + j].astype(BF16) for j in range(count)], axis=1)


def _dsa_sample_scores_kernel(pt_ref, qi_ref, w_ref, kin_ref, kiT_hbm, keys_ref, buf_ref, sem,
                              *, n_pages, layer_base):
    slot = _fetch_pages_pipelined(pt_ref, layer_base, [kiT_hbm], [buf_ref], [sem], n_pages)
    qi = qi_ref[0]
    w = w_ref[0] * IDX_SCALE
    score = lambda dots: jnp.sum(jnp.maximum(dots, 0.0) * w, axis=0, keepdims=True)
    for g in range(n_pages // PAGE_GROUP):
        kiT = _pages_bf16(buf_ref, slot, g * PAGE_GROUP, PAGE_GROUP)
        keys = _float_order_key(score(jnp.dot(qi, kiT, preferred_element_type=F32)))
        keys_ref[0, :, g * PAGE_GROUP * PAGE_SIZE:(g + 1) * PAGE_GROUP * PAGE_SIZE] = keys
    kn = kin_ref[0].astype(BF16).astype(F32)
    sn = score(jnp.sum(qi.astype(F32) * kn, axis=1, keepdims=True))
    lane = lax.broadcasted_iota(I32, (1, LANES), 1)
    keys_ref[0, :, n_pages * PAGE_SIZE:] = jnp.where(lane == 0, _float_order_key(sn), INT_MIN)


def _dsa_sample_scores(pt_flat, qi3, w3, ki_new, cache_kiT, layer, n_pool, n_pages):
    db = qi3.shape[0]
    n_lanes = n_pages * PAGE_SIZE + LANES
    per_b = lambda shape: pl.BlockSpec((1,) + shape, lambda b, pt: (b,) + (0,) * len(shape))
    grid_spec = pltpu.PrefetchScalarGridSpec(
        num_scalar_prefetch=1,
        grid=(db,),
        in_specs=[per_b((IDX_HEADS, IDX_DIM)), per_b((IDX_HEADS, 1)), per_b((1, IDX_DIM)),
                  pl.BlockSpec(memory_space=pl.ANY)],
        out_specs=per_b((1, n_lanes)),
        scratch_shapes=[pltpu.VMEM((2, n_pages, IDX_DIM, PAGE_SIZE), F32), pltpu.SemaphoreType.DMA((2,))],
    )
    return pl.pallas_call(
        functools.partial(_dsa_sample_scores_kernel, n_pages=n_pages, layer_base=layer * n_pool),
        grid_spec=grid_spec,
        out_shape=jax.ShapeDtypeStruct((db, 1, n_lanes), I32),
        compiler_params=_cparams(1),
        name="dsa_sample_scores",
    )(pt_flat, qi3, w3, ki_new, cache_kiT)


def _dsa_sample_select_kernel(keys_ref, bias_ref, *, n_keys, topk, idx_bits):
    keys = keys_ref[...]
    idx = lax.broadcasted_iota(I32, keys.shape, 1)
    count = lambda hit: jnp.sum(jnp.where(hit, 1, 0), axis=1, keepdims=True)
    keys = jnp.where(idx < n_keys, keys, INT_MIN)

    def thr_bit(i, thr):
        cand = thr + lax.shift_left(jnp.int32(1), 31 - i)
        return jnp.where(count(keys >= cand) >= topk, cand, thr)

    thr = lax.fori_loop(0, 32, thr_bit, jnp.full((keys.shape[0], 1), INT_MIN, I32))
    need = topk - count(keys > thr)

    def idx_bit(i, j0):
        cand = j0 + lax.shift_left(jnp.int32(1), idx_bits - 1 - i)
        return jnp.where(count((keys == thr) & (idx < cand)) < need, cand, j0)

    j_sel = lax.fori_loop(0, idx_bits, idx_bit, jnp.zeros((keys.shape[0], 1), I32))
    sel = ((keys > thr) | ((keys == thr) & (idx <= j_sel))) & (idx < n_keys)
    bias_ref[...] = jnp.where(sel, 0.0, NEG)


def _dsa_sample_select(keys, n_keys):
    topk = min(TOPK_MAX, n_keys // 4)
    idx_bits = max(1, (n_keys - 1).bit_length())
    full = pl.BlockSpec(keys.shape, lambda i: (0, 0))
    return pl.pallas_call(
        functools.partial(_dsa_sample_select_kernel, n_keys=n_keys, topk=topk, idx_bits=idx_bits),
        grid=(1,),
        in_specs=[full],
        out_specs=full,
        out_shape=jax.ShapeDtypeStruct(keys.shape, F32),
        compiler_params=_cparams(1),
        name="dsa_sample_select",
    )(keys)


def _dsa_sample_attend_kernel(pt_ref, q_ref, bias_ref, kn_ref, vn_ref, kT_hbm, vT_hbm, o_ref,
                              kbuf_ref, vbuf_ref, ksem, vsem, *, n_pages, layer_base):
    slot = _fetch_pages_pipelined(pt_ref, layer_base, [kT_hbm, vT_hbm], [kbuf_ref, vbuf_ref],
                                  [ksem, vsem], n_pages)
    q = q_ref[0]
    n_past = n_pages * PAGE_SIZE
    group = PAGE_GROUP * PAGE_SIZE
    scores = [jnp.dot(q, _pages_bf16(kbuf_ref, slot, g * PAGE_GROUP, PAGE_GROUP), preferred_element_type=F32)
              + bias_ref[0, :, g * group:(g + 1) * group] for g in range(n_pages // PAGE_GROUP)]
    kn = kn_ref[0].astype(BF16).astype(F32)
    vn = vn_ref[0].astype(BF16).astype(F32)
    sn = jnp.sum(q.astype(F32) * kn, axis=1, keepdims=True) + bias_ref[0, :, n_past:n_past + 1]
    m = sn
    for s in scores:
        m = jnp.maximum(m, s.max(axis=1, keepdims=True))
    pn = jnp.exp(sn - m)
    denom = pn
    acc = pn.astype(BF16).astype(F32) * vn
    for g, s in enumerate(scores):
        p = jnp.exp(s - m)
        denom = denom + p.sum(axis=1, keepdims=True)
        acc = acc + lax.dot_general(p.astype(BF16), _pages_bf16(vbuf_ref, slot, g * PAGE_GROUP, PAGE_GROUP),
                                    (((1,), (1,)), ((), ())), preferred_element_type=F32)
    out = acc / denom
    head_grp = lax.broadcasted_iota(I32, out.shape, 0) // Q_PER_KV
    lane_grp = lax.broadcasted_iota(I32, out.shape, 1) // HEAD_DIM
    out = jnp.where(head_grp == lane_grp, out, 0.0)
    o = out[:, 0:HEAD_DIM]
    for n in range(1, N_KV_HEADS):
        o = o + out[:, n * HEAD_DIM:(n + 1) * HEAD_DIM]
    o_ref[0] = o


def _dsa_sample_attend(pt_flat, q_bd, bias, k_new, v_new, cache_kT, cache_vT, layer, n_pool, n_pages):
    db = q_bd.shape[0]
    per_b = lambda shape: pl.BlockSpec((1,) + shape, lambda b, pt: (b,) + (0,) * len(shape))
    hbm = pl.BlockSpec(memory_space=pl.ANY)
    pages = pltpu.VMEM((2, n_pages, KV_WIDTH, PAGE_SIZE), F32)
    grid_spec = pltpu.PrefetchScalarGridSpec(
        num_scalar_prefetch=1,
        grid=(db,),
        in_specs=[per_b(q_bd.shape[1:]), per_b(bias.shape[1:]), per_b((1, KV_WIDTH)), per_b((1, KV_WIDTH)),
                  hbm, hbm],
        out_specs=per_b((N_HEADS, HEAD_DIM)),
        scratch_shapes=[pages, pages, pltpu.SemaphoreType.DMA((2,)), pltpu.SemaphoreType.DMA((2,))],
    )
    return pl.pallas_call(
        functools.partial(_dsa_sample_attend_kernel, n_pages=n_pages, layer_base=layer * n_pool),
        grid_spec=grid_spec,
        out_shape=jax.ShapeDtypeStruct((db, N_HEADS, HEAD_DIM), F32),
        compiler_params=_cparams(1),
        name="dsa_sample_attend",
    )(pt_flat, q_bd, bias, k_new, v_new, cache_kT, cache_vT)


def _out_proj_kernel(a_ref, w_ref, x_ref, gt_ref, o_ref):
    y = jnp.dot(a_ref[0], w_ref[...], preferred_element_type=F32)
    o_ref[0] = x_ref[0] + gt_ref[0] * y


def _out_proj(a, w, x, gate):
    b, t, d = x.shape
    tm = min(ROW_TILE, t)
    row = lambda width: pl.BlockSpec((1, tm, width), lambda bi, i: (bi, i, 0))
    return pl.pallas_call(
        _out_proj_kernel,
        grid=(b, t // tm),
        in_specs=[row(a.shape[2]), pl.BlockSpec(w.shape, lambda bi, i: (0, 0)), row(d), _mod_spec(gate, tm)],
        out_specs=row(d),
        out_shape=jax.ShapeDtypeStruct((b, t, d), F32),
        compiler_params=_cparams(2),
        name="out_proj",
    )(a, w, x, gate)


def _gelu_tanh(x):
    return x * (0.5 * (1.0 + jnp.tanh(math.sqrt(2.0 / math.pi) * (x + 0.044715 * (x * x * x)))))


def _rec_proj_kernel(x_ref, g_ref, sh_ref, sc_ref, w_ref, gate_ref, xb_ref):
    h = _modulated_norm(x_ref[0], g_ref[...], sh_ref[0], sc_ref[0]).astype(BF16)
    acc = jnp.dot(h, w_ref[...], preferred_element_type=F32)
    gate_ref[0] = _gelu_tanh(acc[:, :D_RNN])
    xb_ref[0] = acc[:, D_RNN:]


def _rec_proj(x, g, shift, scale, w):
    b, t, d = x.shape
    tm = min(ROW_TILE, t)
    row = lambda width: pl.BlockSpec((1, tm, width), lambda bi, i: (bi, i, 0))
    return pl.pallas_call(
        _rec_proj_kernel,
        grid=(b, t // tm),
        in_specs=[row(d), pl.BlockSpec((1, d), lambda bi, i: (0, 0)), _mod_spec(shift, tm),
                  _mod_spec(scale, tm), pl.BlockSpec(w.shape, lambda bi, i: (0, 0))],
        out_specs=(row(D_RNN), row(D_RNN)),
        out_shape=(jax.ShapeDtypeStruct((b, t, D_RNN), F32),) * 2,
        compiler_params=_cparams(2),
        name="rec_proj",
    )(x, g, shift, scale, w)


def _rglru_gates(xc, wa_ref, ba, wx_ref, bx, lam):
    xcb = xc.astype(BF16)
    blk = lambda w_ref: jnp.concatenate(
        [jnp.dot(xcb[:, n * RG_BLOCK_W:(n + 1) * RG_BLOCK_W], w_ref[n], preferred_element_type=F32)
         for n in range(RG_BLOCKS)], axis=1)
    r = jax.nn.sigmoid(blk(wa_ref) + ba)
    i = jax.nn.sigmoid(blk(wx_ref) + bx)
    neg_lam = -lam
    softplus = jnp.maximum(neg_lam, 0.0) + jnp.log1p(jnp.exp(-jnp.abs(neg_lam)))
    log_a = -RG_C * r * softplus
    a = jnp.exp(log_a)
    u = jnp.sqrt(1.0 - a * a) * (i * xc)
    return a, u


def _rglru_prompt_kernel(gate_ref, xb_ref, cw_ref, cb_ref, wa_ref, ba_ref, wx_ref, bx_ref, lam_ref,
                         y_ref, h_ref, conv_ref, xcat_ref, a_ref, u_ref, hs_ref, hc_ref):
    tc = xb_ref.shape[1]
    c = pl.program_id(1)
    pad = SUBLANES

    @pl.when(c == 0)
    def _():
        xcat_ref[0:pad, :] = jnp.zeros((pad, D_RNN), F32)
        hc_ref[...] = jnp.zeros(hc_ref.shape, F32)

    @pl.when(c > 0)
    def _():
        xcat_ref[0:pad, :] = xcat_ref[tc:tc + pad, :]

    xcat_ref[pad:pad + tc, :] = xb_ref[0]
    cw = cw_ref[...]
    xc = xcat_ref[pad - 3:pad - 3 + tc, :] * cw[0:1, :]
    for j in range(1, CONV_W):
        xc = xc + xcat_ref[pad - 3 + j:pad - 3 + j + tc, :] * cw[j:j + 1, :]
    xc = cb_ref[...] + xc
    a, u = _rglru_gates(xc, wa_ref, ba_ref[...], wx_ref, bx_ref[...], lam_ref[...])
    a_ref[...] = a
    u_ref[...] = u

    def step(t, h):
        h = a_ref[pl.ds(t, 1), :] * h + u_ref[pl.ds(t, 1), :]
        hs_ref[pl.ds(t, 1), :] = h
        return h

    h = lax.fori_loop(0, tc, step, hc_ref[...], unroll=8)
    hc_ref[...] = h
    y_ref[0] = (hs_ref[...] * gate_ref[0]).astype(BF16)
    h_ref[0] = h
    conv_ref[0] = xcat_ref[pad + tc - (CONV_W - 1):pad + tc, :]


def _rglru_prompt(gate, xb, cw, cb, wa, ba, wx, bx, lam):
    b, t, d = xb.shape
    tc = min(SCAN_CHUNK, t)
    row = pl.BlockSpec((1, tc, d), lambda bi, i: (bi, i, 0))
    vec = pl.BlockSpec((1, d), lambda bi, i: (0, 0))
    full = lambda shape: pl.BlockSpec(shape, lambda bi, i: (0,) * len(shape))
    return pl.pallas_call(
        _rglru_prompt_kernel,
        grid=(b, t // tc),
        in_specs=[row, row, full(cw.shape), vec, full(wa.shape), vec, full(wx.shape), vec, vec],
        out_specs=(row, pl.BlockSpec((1, 1, d), lambda bi, i: (bi, 0, 0)),
                   pl.BlockSpec((1, CONV_W - 1, d), lambda bi, i: (bi, 0, 0))),
        out_shape=(jax.ShapeDtypeStruct((b, t, d), BF16), jax.ShapeDtypeStruct((b, 1, d), F32),
                   jax.ShapeDtypeStruct((b, CONV_W - 1, d), F32)),
        scratch_shapes=[pltpu.VMEM((tc + 2 * SUBLANES, d), F32), pltpu.VMEM((tc, d), F32),
                        pltpu.VMEM((tc, d), F32), pltpu.VMEM((tc, d), F32), pltpu.VMEM((1, d), F32)],
        compiler_params=_cparams(2),
        name="rglru_prompt",
    )(gate, xb, cw, cb, wa, ba, wx, bx, lam)


def _rglru_sample_kernel(gate_ref, xb_ref, buf_ref, h0_ref, cw_ref, cb_ref, wa_ref, ba_ref, wx_ref, bx_ref,
                         lam_ref, y_ref, h_ref, nbuf_ref):
    cw = cw_ref[...]
    xb = xb_ref[...]
    xc = buf_ref[0] * cw[0:1, :]
    for j in range(1, CONV_W - 1):
        xc = xc + buf_ref[j] * cw[j:j + 1, :]
    xc = cb_ref[...] + (xc + xb * cw[CONV_W - 1:CONV_W, :])
    a, u = _rglru_gates(xc, wa_ref, ba_ref[...], wx_ref, bx_ref[...], lam_ref[...])
    h = a * h0_ref[...] + u
    h_ref[...] = h
    y_ref[...] = (h * gate_ref[...]).astype(BF16)
    for j in range(CONV_W - 2):
        nbuf_ref[j] = buf_ref[j + 1]
    nbuf_ref[CONV_W - 2] = xb


def _rglru_sample(gate, xb, buf, h0, cw, cb, wa, ba, wx, bx, lam):
    rows, d = xb.shape
    full = lambda a: pl.BlockSpec(a.shape, lambda i: (0,) * a.ndim)
    args = (gate, xb, buf, h0, cw, cb, wa, ba, wx, bx, lam)
    return pl.pallas_call(
        _rglru_sample_kernel,
        grid=(1,),
        in_specs=[full(a) for a in args],
        out_specs=(full(xb), full(xb), full(buf)),
        out_shape=(jax.ShapeDtypeStruct((rows, d), BF16), jax.ShapeDtypeStruct((rows, d), F32),
                   jax.ShapeDtypeStruct(buf.shape, F32)),
        compiler_params=_cparams(1),
        name="rglru_sample",
    )(*args)


def _route(logits):
    lane = lax.broadcasted_iota(I32, logits.shape, 1)

    def first_max(v):
        m = jnp.max(v, axis=1, keepdims=True)
        return m, jnp.min(jnp.where(v == m, lane, LANES), axis=1, keepdims=True)

    is_grp = lane < N_GROUPS
    gm, grp = first_max(jnp.where(is_grp, logits, -jnp.inf))
    g_w = 1.0 / jnp.sum(jnp.where(is_grp, jnp.exp(logits - gm), 0.0), axis=1, keepdims=True)
    lo = N_GROUPS + EXPERTS_PER_GROUP * grp
    el = jnp.where((lane >= lo) & (lane < lo + EXPERTS_PER_GROUP), logits, -jnp.inf)
    m1, i1 = first_max(el)
    m2, i2 = first_max(jnp.where(lane == i1, -jnp.inf, el))
    e2 = jnp.exp(m2 - m1)
    den = 1.0 + e2
    return i1 - N_GROUPS, i2 - N_GROUPS, (1.0 / den) * g_w, (e2 / den) * g_w


def _moe_kernel(x_ref, g_ref, sh_ref, sc_ref, gt_ref, wr_ref, br_ref, w1_ref, w2_ref, gf_ref, o_ref,
                h_ref, acc_ref, e0_ref, e1_ref, c0_ref, c1_ref, *, final_norm):
    e = pl.program_id(2)

    @pl.when(e == 0)
    def _():
        h = _modulated_norm(x_ref[0], g_ref[...], sh_ref[0], sc_ref[0])
        h_ref[...] = h.astype(BF16)
        logits = jnp.dot(h_ref[...], wr_ref[...], preferred_element_type=F32) + br_ref[...]
        e0_ref[...], e1_ref[...], c0_ref[...], c1_ref[...] = _route(logits)
        acc_ref[...] = jnp.zeros(acc_ref.shape, F32)

    gu = jnp.dot(h_ref[...], w1_ref[0].astype(BF16), preferred_element_type=F32)
    gp, up = gu[:, :D_EXPERT], gu[:, D_EXPERT:]
    act = ((gp * jax.nn.sigmoid(gp)) * up).astype(BF16)
    y = jnp.dot(act, w2_ref[0].astype(BF16), preferred_element_type=F32)
    wt = jnp.where(e0_ref[...] == e, c0_ref[...], 0.0) + jnp.where(e1_ref[...] == e, c1_ref[...], 0.0)
    acc_ref[...] += y * wt

    @pl.when(e == N_EXPERTS - 1)
    def _():
        out = x_ref[0] + gt_ref[0] * acc_ref[...]
        if final_norm:
            out = (out * lax.rsqrt(jnp.mean(out * out, axis=-1, keepdims=True) + EPS)) * gf_ref[...]
        o_ref[0] = out


def _moe(x, g, shift, scale, gate, wr, br, w1, w2, layer, gf, final_norm):
    b, t, d = x.shape
    tm = min(MOE_ROW_TILE, t)
    row = pl.BlockSpec((1, tm, d), lambda bi, i, e: (bi, i, 0))
    vec = lambda width: pl.BlockSpec((1, width), lambda bi, i, e: (0, 0))
    col = lambda dt: pltpu.VMEM((tm, 1), dt)
    return pl.pallas_call(
        functools.partial(_moe_kernel, final_norm=final_norm),
        grid=(b, t // tm, N_EXPERTS),
        in_specs=[row, vec(d), _mod_spec(shift, tm), _mod_spec(scale, tm), _mod_spec(gate, tm),
                  pl.BlockSpec(wr.shape, lambda bi, i, e: (0, 0)), vec(ROUTER_PAD),
                  pl.BlockSpec((1, d, 2 * D_EXPERT), lambda bi, i, e: (layer * N_EXPERTS + e, 0, 0)),
                  pl.BlockSpec((1, D_EXPERT, d), lambda bi, i, e: (layer * N_EXPERTS + e, 0, 0)),
                  vec(d)],
        out_specs=row,
        out_shape=jax.ShapeDtypeStruct((b, t, d), F32),
        scratch_shapes=[pltpu.VMEM((tm, d), BF16), pltpu.VMEM((tm, d), F32),
                        col(I32), col(I32), col(F32), col(F32)],
        compiler_params=_cparams(3),
        name="moe",
    )(x, g, shift, scale, gate, wr, br, w1, w2, gf)


REC_E0, REC_E1, REC_RANK0, REC_RANK1, REC_C0, REC_C1 = range(6)


def _moe_route_kernel(x_ref, g_ref, sh_ref, sc_ref, wr_ref, br_ref, h_ref, rec_ref, recT_ref, cnt_ref,
                      tri_ref, carry_ref):
    tm = x_ref.shape[1]

    @pl.when((pl.program_id(0) == 0) & (pl.program_id(1) == 0))
    def _():
        carry_ref[...] = jnp.zeros(carry_ref.shape, F32)
        earlier = lax.broadcasted_iota(I32, (tm, tm), 0) > lax.broadcasted_iota(I32, (tm, tm), 1)
        tri_ref[...] = jnp.where(earlier, 1.0, 0.0).astype(BF16)

    h = _modulated_norm(x_ref[0], g_ref[...], sh_ref[0], sc_ref[0])
    h_ref[0] = h
    logits = jnp.dot(h.astype(BF16), wr_ref[...], preferred_element_type=F32) + br_ref[...]
    e0, e1, c0, c1 = _route(logits)
    lane = lax.broadcasted_iota(I32, logits.shape, 1)
    pick0, pick1 = lane == e0, lane == e1
    picks = jnp.where(pick0 | pick1, 1.0, 0.0)
    before = carry_ref[...] + jnp.dot(tri_ref[...], picks.astype(BF16), preferred_element_type=F32)
    rank0 = jnp.sum(jnp.where(pick0, before, 0.0), axis=1, keepdims=True)
    rank1 = jnp.sum(jnp.where(pick1, before, 0.0), axis=1, keepdims=True)
    carry_ref[...] = carry_ref[...] + jnp.sum(picks, axis=0, keepdims=True)
    cnt_ref[...] = carry_ref[...]
    rec = jnp.zeros(logits.shape, F32)
    for pos, val in ((REC_E0, e0.astype(F32)), (REC_E1, e1.astype(F32)), (REC_RANK0, rank0),
                     (REC_RANK1, rank1), (REC_C0, c0), (REC_C1, c1)):
        rec = jnp.where(lane == pos, val, rec)
    rec_ref[0] = rec
    recT_ref[0] = rec.T[0:SUBLANES, :]


def _moe_route(x, g, shift, scale, wr, br):
    b, t, d = x.shape
    tm = ROW_TILE
    row = lambda width: pl.BlockSpec((1, tm, width), lambda bi, i: (bi, i, 0))
    vec = lambda width: pl.BlockSpec((1, width), lambda bi, i: (0, 0))
    return pl.pallas_call(
        _moe_route_kernel,
        grid=(b, t // tm),
        in_specs=[row(d), vec(d), _mod_spec(shift, tm), _mod_spec(scale, tm),
                  pl.BlockSpec(wr.shape, lambda bi, i: (0, 0)), vec(ROUTER_PAD)],
        out_specs=(row(d), row(ROUTER_PAD), pl.BlockSpec((1, SUBLANES, tm), lambda bi, i: (bi, 0, i)),
                   vec(ROUTER_PAD)),
        out_shape=(jax.ShapeDtypeStruct((b, t, d), F32), jax.ShapeDtypeStruct((b, t, ROUTER_PAD), F32),
                   jax.ShapeDtypeStruct((b, SUBLANES, t), F32), jax.ShapeDtypeStruct((1, ROUTER_PAD), F32)),
        scratch_shapes=[pltpu.VMEM((tm, tm), BF16), pltpu.VMEM((1, ROUTER_PAD), F32)],
        compiler_params=_cparams(2),
        name="moe_route",
    )(x, g, shift, scale, wr, br)


def _gather_rows_start(idx_ref, idx_base, src_hbm, dst, sem, n_rows):
    for r in range(n_rows):
        pltpu.make_async_copy(src_hbm.at[pl.ds(idx_ref[idx_base + r], 1)], dst.at[pl.ds(r, 1)], sem).start()


def _gather_rows_wait(src_hbm, dst, sem, n_rows):
    pltpu.make_async_copy(src_hbm.at[pl.ds(0, n_rows)], dst, sem).wait()


def _moe_experts_kernel(blk_e_ref, n_used_ref, dest_ref, pad_ref, h_hbm, w1_ref, w2_ref, y_ref,
                        buf_ref, w1b_ref, w2b_ref, row_tok_ref, sem):
    i = pl.program_id(0)
    blk = buf_ref.shape[1]
    n_used = n_used_ref[0]
    n_slots = buf_ref.shape[0]
    slot = i % n_slots
    n_tok = dest_ref.shape[0] // 2

    def start(block):
        s = block % n_slots
        _gather_rows_start(row_tok_ref, block * blk, h_hbm, buf_ref.at[s], sem.at[s], blk)

    @pl.when(i == 0)
    def _():
        def clear(r, carry):
            row_tok_ref[r] = 0
            return carry

        def place(tok, carry):
            row_tok_ref[dest_ref[tok]] = tok
            row_tok_ref[dest_ref[n_tok + tok]] = tok
            return carry

        for e in range(N_EXPERTS):
            lax.fori_loop(pad_ref[e], pad_ref[N_EXPERTS + e], clear, 0)
        lax.fori_loop(0, n_tok, place, 0, unroll=8)
        for ahead in range(n_slots - 1):
            @pl.when(ahead < n_used)
            def _():
                start(ahead)

    @pl.when(i + n_slots - 1 < n_used)
    def _():
        start(i + n_slots - 1)

    @pl.when(i < n_used)
    def _():
        @pl.when((i == 0) | (blk_e_ref[i] != blk_e_ref[jnp.maximum(i - 1, 0)]))
        def _():
            w1b_ref[...] = w1_ref[0].astype(BF16)
            w2b_ref[...] = w2_ref[0].astype(BF16)

        _gather_rows_wait(h_hbm, buf_ref.at[slot], sem.at[slot], blk)
        gu = jnp.dot(buf_ref[slot].astype(BF16), w1b_ref[...], preferred_element_type=F32)
        gp, up = gu[:, :D_EXPERT], gu[:, D_EXPERT:]
        act = ((gp * jax.nn.sigmoid(gp)) * up).astype(BF16)
        y_ref[...] = jnp.dot(act, w2b_ref[...], preferred_element_type=F32)

    @pl.when(i >= n_used)
    def _():
        y_ref[...] = jnp.zeros(y_ref.shape, F32)


def _moe_experts(blk_expert, n_used, dest, pad_bounds, h2d, w1, w2, layer):
    n_blk = blk_expert.shape[0]
    blk = MOE_BLOCK_ROWS
    d = h2d.shape[1]
    grid_spec = pltpu.PrefetchScalarGridSpec(
        num_scalar_prefetch=4,
        grid=(n_blk,),
        in_specs=[pl.BlockSpec(memory_space=pl.ANY),
                  pl.BlockSpec((1, d, 2 * D_EXPERT), lambda i, be, *_: (layer * N_EXPERTS + be[i], 0, 0)),
                  pl.BlockSpec((1, D_EXPERT, d), lambda i, be, *_: (layer * N_EXPERTS + be[i], 0, 0))],
        out_specs=pl.BlockSpec((blk, d), lambda i, *_: (i, 0)),
        scratch_shapes=[pltpu.VMEM((GATHER_SLOTS, blk, d), F32), pltpu.VMEM((d, 2 * D_EXPERT), BF16),
                        pltpu.VMEM((D_EXPERT, d), BF16), pltpu.SMEM((n_blk * blk,), I32),
                        pltpu.SemaphoreType.DMA((GATHER_SLOTS,))],
    )
    return pl.pallas_call(
        _moe_experts_kernel,
        grid_spec=grid_spec,
        out_shape=jax.ShapeDtypeStruct((n_blk * blk, d), F32),
        compiler_params=_cparams(1),
        name="moe_experts",
    )(blk_expert, n_used, dest, pad_bounds, h2d, w1, w2)


def _moe_combine_kernel(dest_ref, y_hbm, x_ref, gt_ref, rec_ref, gf_ref, o_ref, buf_ref, sem,
                        *, n_tokens, final_norm):
    tm = x_ref.shape[1]
    n_steps = pl.num_programs(0) * pl.num_programs(1)
    s = pl.program_id(0) * pl.num_programs(1) + pl.program_id(1)
    slot = s % 2

    def start(step, sl):
        for pick in range(2):
            _gather_rows_start(dest_ref, pick * n_tokens + step * tm, y_hbm, buf_ref.at[sl, pick],
                               sem.at[sl, pick], tm)

    @pl.when(s == 0)
    def _():
        start(0, 0)

    @pl.when(s + 1 < n_steps)
    def _():
        start(s + 1, 1 - slot)

    for pick in range(2):
        _gather_rows_wait(y_hbm, buf_ref.at[slot, pick], sem.at[slot, pick], tm)
    rec = rec_ref[0]
    moe = buf_ref[slot, 0] * rec[:, REC_C0:REC_C0 + 1] + buf_ref[slot, 1] * rec[:, REC_C1:REC_C1 + 1]
    out = x_ref[0] + gt_ref[0] * moe
    if final_norm:
        out = (out * lax.rsqrt(jnp.mean(out * out, axis=-1, keepdims=True) + EPS)) * gf_ref[...]
    o_ref[0] = out


def _moe_combine(dest, y, x, gate, rec, gf, final_norm):
    b, t, d = x.shape
    tm = MOE_BLOCK_ROWS
    row = lambda width: pl.BlockSpec((1, tm, width), lambda bi, i, dst: (bi, i, 0))
    grid_spec = pltpu.PrefetchScalarGridSpec(
        num_scalar_prefetch=1,
        grid=(b, t // tm),
        in_specs=[pl.BlockSpec(memory_space=pl.ANY), row(d), _mod_spec(gate, tm), row(ROUTER_PAD),
                  pl.BlockSpec((1, d), lambda bi, i, dst: (0, 0))],
        out_specs=row(d),
        scratch_shapes=[pltpu.VMEM((2, 2, tm, d), F32), pltpu.SemaphoreType.DMA((2, 2))],
    )
    return pl.pallas_call(
        functools.partial(_moe_combine_kernel, n_tokens=b * t, final_norm=final_norm),
        grid_spec=grid_spec,
        out_shape=jax.ShapeDtypeStruct((b, t, d), F32),
        compiler_params=_cparams(2),
        name="moe_combine",
    )(dest, y, x, gate, rec, gf)


def _moe_sorted(x, g, shift, scale, gate, wr, br, w1, w2, layer, gf, final_norm):
    b, t, d = x.shape
    n = b * t
    blk = MOE_BLOCK_ROWS
    n_blk = (2 * n) // blk + N_EXPERTS
    h, rec, rec_t, cnt = _moe_route(x, g, shift, scale, wr, br)
    counts = cnt[0, :N_EXPERTS].astype(I32)
    padded = (counts + blk - 1) // blk * blk
    seg_end = jnp.cumsum(padded)
    seg_start = seg_end - padded
    col = lambda c: rec_t[:, c, :].reshape(n).astype(I32)
    dest = jnp.concatenate([seg_start[col(REC_E0)] + col(REC_RANK0), seg_start[col(REC_E1)] + col(REC_RANK1)])
    blk_first_row = jnp.arange(n_blk, dtype=I32) * blk
    blk_expert = jnp.minimum(jnp.sum((seg_end[None, :] <= blk_first_row[:, None]).astype(I32), axis=1),
                             N_EXPERTS - 1)
    n_used = (seg_end[-1:] // blk).astype(I32)
    pad_bounds = jnp.concatenate([seg_start + counts, seg_end]).astype(I32)
    y = _moe_experts(blk_expert, n_used, dest, pad_bounds, h.reshape(n, d), w1, w2, layer)
    return _moe_combine(dest, y, x, gate, rec, gf, final_norm)


def kernel(x_prompt, x_sample, cache_k, cache_v, cache_kidx, state_h, state_conv, page_table,
           c_prompt, c_sample, ada_w, ada_b, norm_mix_g, norm_ffn_g, norm_final_g,
           attn_w_in, attn_w_out, rec_w_in, rec_conv_w, rec_conv_b, rec_w_a, rec_b_a,
           rec_w_x, rec_b_x, rec_lambda, rec_w_out, moe_w_group, moe_b_group,
           moe_w_expert, moe_b_expert, moe_w1, moe_w2):
    b, t, d = x_prompt.shape
    db = x_sample.shape[0]
    n_pages = page_table.shape[1]
    past = n_pages * PAGE_SIZE
    n_pool = cache_k.shape[1]

    n_cond = b + db
    cond_rows = -(-n_cond // SUBLANES) * SUBLANES
    c_all = jnp.concatenate([c_prompt, c_sample, jnp.zeros((cond_rows - n_cond, d), F32)], axis=0)
    mod = _adaln(c_all, ada_w, ada_b)

    tables_p = _rope_tables(jnp.arange(t, dtype=I32))
    tables_s = _rope_tables(jnp.full((db,), past, I32))
    cache_kT = cache_k.transpose(0, 1, 3, 4, 2).reshape(-1, KV_WIDTH, PAGE_SIZE)
    cache_vT = cache_v.transpose(0, 1, 3, 4, 2).reshape(-1, KV_WIDTH, PAGE_SIZE)
    cache_kiT = cache_kidx.transpose(0, 1, 3, 2).reshape(-1, IDX_DIM, PAGE_SIZE)
    pt_flat = page_table.reshape(-1)
    moe_w1f = moe_w1.reshape(-1, d, 2 * D_EXPERT)
    moe_w2f = moe_w2.reshape(-1, D_EXPERT, d)
    row_vec = lambda v: v.reshape(1, -1)
    head_grp = jnp.arange(N_HEADS)[:, None] // Q_PER_KV == jnp.arange(KV_WIDTH)[None, :] // HEAD_DIM

    xp = x_prompt
    xs = x_sample.reshape(1, db, d)
    outs = {name: [] for name in ("k_p", "v_p", "ki_p", "h_p", "cv_p", "k_s", "v_s", "ki_s", "h_s", "cv_s")}
    for l in range(DEPTH):
        parts = [mod[l, :, i * d:(i + 1) * d] for i in range(6)]
        sh1p, sc1p, g1p, sh2p, sc2p, g2p = [m[:b].reshape(b, 1, d) for m in parts]
        sh1s, sc1s, g1s, sh2s, sc2s, g2s = [m[b:n_cond].reshape(1, db, d) for m in parts]
        g_mix = row_vec(norm_mix_g[l])
        if l % 2 == 0:
            a = l // 2
            w_in = jnp.pad(attn_w_in[a], ((0, 0), (0, ATTN_PROJ_PAD - ATTN_PROJ))).astype(BF16)
            w_out = attn_w_out[a].astype(BF16)
            kT, vT, kiT, qT, qiT, wiT, kb, vTb, kib = _attn_proj_prompt(xp, g_mix, sh1p, sc1p, w_in, tables_p)
            op = _dsa_prompt(qT, qiT, wiT, kb, vTb, kib)
            outs["k_p"].append(kT.reshape(b, N_KV_HEADS, HEAD_DIM, t).transpose(0, 3, 1, 2))
            outs["v_p"].append(vT.reshape(b, N_KV_HEADS, HEAD_DIM, t).transpose(0, 3, 1, 2))
            outs["ki_p"].append(kiT.transpose(0, 2, 1))
            xp = _out_proj(op, w_out, xp, g1p)

            q, k, v, qi, ki, wi = _attn_proj_sample(xs, g_mix, sh1s, sc1s, w_in, tables_s)
            keys = _dsa_sample_scores(pt_flat, qi.reshape(db, IDX_HEADS, IDX_DIM).astype(BF16),
                                      wi.reshape(db, IDX_HEADS, 1), ki.reshape(db, 1, IDX_DIM),
                                      cache_kiT, a, n_pool, n_pages)
            bias = _dsa_sample_select(keys.reshape(db, -1), past + 1).reshape(keys.shape)
            q_bd = jnp.where(head_grp[None], jnp.tile(q.reshape(db, N_HEADS, HEAD_DIM), (1, 1, N_KV_HEADS)),
                             0.0).astype(BF16)
            os_ = _dsa_sample_attend(pt_flat, q_bd, bias, k.reshape(db, 1, KV_WIDTH), v.reshape(db, 1, KV_WIDTH),
                                     cache_kT, cache_vT, a, n_pool, n_pages)
            outs["k_s"].append(k.reshape(db, 1, N_KV_HEADS, HEAD_DIM))
            outs["v_s"].append(v.reshape(db, 1, N_KV_HEADS, HEAD_DIM))
            outs["ki_s"].append(ki.reshape(db, 1, IDX_DIM))
            xs = _out_proj(os_.reshape(1, db, Q_WIDTH).astype(BF16), w_out, xs, g1s)
        else:
            r = l // 2
            w_in = rec_w_in[r].astype(BF16)
            w_out = rec_w_out[r].astype(BF16)
            rec = (rec_conv_w[r], row_vec(rec_conv_b[r]), rec_w_a[r].astype(BF16), row_vec(rec_b_a[r]),
                   rec_w_x[r].astype(BF16), row_vec(rec_b_x[r]), row_vec(rec_lambda[r]))
            gate, xb = _rec_proj(xp, g_mix, sh1p, sc1p, w_in)
            y, h_t, buf = _rglru_prompt(gate, xb, *rec)
            outs["h_p"].append(h_t.reshape(b, D_RNN))
            outs["cv_p"].append(buf)
            xp = _out_proj(y, w_out, xp, g1p)

            gate, xb = _rec_proj(xs, g_mix, sh1s, sc1s, w_in)
            y, h_t, buf = _rglru_sample(gate[0], xb[0], state_conv[r].swapaxes(0, 1), state_h[r], *rec)
            outs["h_s"].append(h_t)
            outs["cv_s"].append(buf.swapaxes(0, 1))
            xs = _out_proj(y.reshape(1, db, D_RNN), w_out, xs, g1s)

        wr = jnp.concatenate([moe_w_group[l], moe_w_expert[l],
                              jnp.zeros((d, ROUTER_PAD - N_GROUPS - N_EXPERTS), F32)], axis=1).astype(BF16)
        br = jnp.concatenate([moe_b_group[l], moe_b_expert[l],
                              jnp.zeros((ROUTER_PAD - N_GROUPS - N_EXPERTS,), F32)]).reshape(1, ROUTER_PAD)
        last = l == DEPTH - 1
        moe_args = (wr, br, moe_w1f, moe_w2f, l, row_vec(norm_final_g), last)
        xp = _moe_sorted(xp, row_vec(norm_ffn_g[l]), sh2p, sc2p, g2p, *moe_args)
        xs = _moe(xs, row_vec(norm_ffn_g[l]), sh2s, sc2s, g2s, *moe_args)

    st = lambda name: jnp.stack(outs[name])
    return (xp, xs.reshape(db, 1, d), st("k_p"), st("v_p"), st("ki_p"), st("h_p"), st("cv_p"),
            st("k_s"), st("v_s"), st("ki_s"), st("h_s"), st("cv_s"))
```
